```python
import math
import jax, jax.numpy as jnp
from jax import lax
import numpy as np

D_MODEL = 2048
BATCH = 32
SEQ = 256
DEPTH = 1
DEC_BATCH = 2
DEC_SEQ = 1024
PAST_LEN = 256

GRID_W = 64
D_SSM = 1024
SSM_CH = 16
SSM_GROUPS = D_SSM // SSM_CH
SSM_STATE = 64
N_HEADS = 8
N_KV_HEADS = 2
HEAD_DIM = 128
Q_PER_KV = N_HEADS // N_KV_HEADS
D_ATTN = N_HEADS * HEAD_DIM
D_KV = N_KV_HEADS * HEAD_DIM
WINDOW = 128
BLOCK = 128
D_FF = 5632
CONV_WIDTH = 3
ROPE_THETA = 10000.0
EPS = 1e-6
NEG_INF = -1e30
IN_COLS = D_SSM + D_ATTN + 2 * D_KV + 2 * D_MODEL

kernel_name = 'hybrid_s5_swa_prefix_dit_step'


def rmsnorm(x, g):
    xf = x.astype(jnp.float32)
    y = xf * lax.rsqrt(jnp.mean(xf * xf, axis=-1, keepdims=True) + EPS)
    return (y * g.astype(jnp.float32)).astype(x.dtype)


def adaln(cond, w, b):
    m = jax.nn.silu(cond) @ w + b
    if m.ndim == 2:
        m = m[:, None, :]
    return jnp.split(m, 6, axis=-1)


def axial_rope_tables(T):
    rows = T // GRID_W
    row = jnp.repeat(jnp.arange(rows, dtype=jnp.float32), GRID_W)
    col = jnp.tile(jnp.arange(GRID_W, dtype=jnp.float32), rows)
    n_freq = HEAD_DIM // 4
    inv = ROPE_THETA ** (-jnp.arange(n_freq, dtype=jnp.float32) / n_freq)
    ang_r = row[:, None] * inv[None, :]
    ang_c = col[:, None] * inv[None, :]
    return (jnp.cos(ang_r), jnp.sin(ang_r), jnp.cos(ang_c), jnp.sin(ang_c))


def rotate_half(x, cos, sin):
    x1, x2 = jnp.split(x, 2, axis=-1)
    return jnp.concatenate([x1 * cos - x2 * sin, x1 * sin + x2 * cos], axis=-1)


def apply_axial_rope(x, tables):
    cos_r, sin_r, cos_c, sin_c = tables
    shp = (cos_r.shape[0],) + (1,) * (x.ndim - 3) + (cos_r.shape[1],)
    xr, xc = jnp.split(x.astype(jnp.float32), 2, axis=-1)
    out = jnp.concatenate([rotate_half(xr, cos_r.reshape(shp), sin_r.reshape(shp)),
                           rotate_half(xc, cos_c.reshape(shp), sin_c.reshape(shp))], axis=-1)
    return out.astype(x.dtype)


def ssm_discretize(lam_re, lam_im, log_dt, b_re, b_im):
    lam = lax.complex(lam_re.astype(jnp.float32), lam_im.astype(jnp.float32))
    dt = jnp.exp(log_dt.astype(jnp.float32))[:, None]
    lam_bar = jnp.exp(lam * dt)
    bmat = lax.complex(b_re.astype(jnp.float32), b_im.astype(jnp.float32))
    b_bar = ((lam_bar - 1.0) / lam)[..., None] * bmat
    return lam_bar, b_bar


def scan_combine(e1, e2):
    a1, b1 = e1
    a2, b2 = e2
    return a1 * a2, a2 * b1 + b2


def ssm_scan(u_g, lam_bar, b_bar, h0, reverse):
    bu = jnp.einsum('gpc,btgc->btgp', b_bar, u_g.astype(jnp.complex64))
    edge = -1 if reverse else 0
    bu = bu.at[:, edge].add(lam_bar[None] * h0)
    a = jnp.broadcast_to(lam_bar, bu.shape)
    _, h = lax.associative_scan(scan_combine, (a, bu), axis=1, reverse=reverse)
    return h


def ssm_branch(u, p, h0_re, h0_im):
    bsz, T, _ = u.shape
    uf = u.astype(jnp.float32)
    ug = uf.reshape(bsz, T, SSM_GROUPS, SSM_CH)
    h0 = lax.complex(h0_re.astype(jnp.float32), h0_im.astype(jnp.float32))
    y = uf * p['ssm_d'].astype(jnp.float32)
    finals = []
    for d in range(2):
        lam_bar, b_bar = ssm_discretize(p['lam_re'][d], p['lam_im'][d], p['log_dt'][d],
                                        p['b_re'][d], p['b_im'][d])
        h = ssm_scan(ug, lam_bar, b_bar, h0[:, d], reverse=(d == 1))
        cmat = lax.complex(p['c_re'][d].astype(jnp.float32), p['c_im'][d].astype(jnp.float32))
        y = y + jnp.real(jnp.einsum('gcp,btgp->btgc', cmat, h)).reshape(bsz, T, D_SSM)
        finals.append(h[:, -1] if d == 0 else h[:, 0])
    hf = jnp.stack(finals, axis=1)
    z = jax.nn.gelu(y).astype(u.dtype)
    z = z * jax.nn.sigmoid(z @ p['w_glu'])
    return z, jnp.real(hf), jnp.imag(hf)


def sink_column(sink, s):
    sk = sink.astype(jnp.float32).reshape(N_KV_HEADS, Q_PER_KV, 1, 1)
    return jnp.broadcast_to(sk, s.shape[:-1] + (1,))


def attention_context(q, k, v, sink):
    bsz, S = q.shape[:2]
    nqb = S // BLOCK
    qb = jnp.moveaxis(q.reshape(bsz, nqb, BLOCK, N_KV_HEADS, Q_PER_KV, HEAD_DIM), 1, 0)

    def one_block(qblk):
        s = jnp.einsum('bqkgd,bskd->bkgqs', qblk, k).astype(jnp.float32)
        pr = jax.nn.softmax(jnp.concatenate([s, sink_column(sink, s)], axis=-1), axis=-1)[..., :-1]
        return jnp.einsum('bkgqs,bskd->bqkgd', pr.astype(v.dtype), v)

    o = lax.map(one_block, qb)
    return jnp.moveaxis(o, 0, 1).reshape(bsz, S, D_ATTN)


def attention_latent(q, k, v, ck, cv, sink):
    bsz, T = q.shape[:2]
    nb = T // BLOCK
    pad = ((0, 0), (BLOCK, BLOCK), (0, 0), (0, 0))
    kp = jnp.pad(k, pad)
    vp = jnp.pad(v, pad)
    ar_q = jnp.arange(BLOCK)
    ar_k = jnp.arange(3 * BLOCK)

    def one_block(bi):
        start = bi * BLOCK
        qblk = lax.dynamic_slice_in_dim(q, start, BLOCK, axis=1)
        kblk = lax.dynamic_slice_in_dim(kp, start, 3 * BLOCK, axis=1)
        vblk = lax.dynamic_slice_in_dim(vp, start, 3 * BLOCK, axis=1)
        qpos = start + ar_q
        kpos = start - BLOCK + ar_k
        valid = ((jnp.abs(qpos[:, None] - kpos[None, :]) <= WINDOW)
                 & (kpos >= 0)[None, :] & (kpos < T)[None, :])
        s_loc = jnp.einsum('bqkgd,bskd->bkgqs', qblk, kblk).astype(jnp.float32)
        s_loc = jnp.where(valid, s_loc, NEG_INF)
        s_ctx = jnp.einsum('bqkgd,bskd->bkgqs', qblk, ck).astype(jnp.float32)
        pr = jax.nn.softmax(jnp.concatenate([s_loc, s_ctx, sink_column(sink, s_loc)], axis=-1), axis=-1)
        p_loc = pr[..., :3 * BLOCK].astype(v.dtype)
        p_ctx = pr[..., 3 * BLOCK:-1].astype(v.dtype)
        return (jnp.einsum('bkgqs,bskd->bqkgd', p_loc, vblk)
                + jnp.einsum('bkgqs,bskd->bqkgd', p_ctx, cv))

    o = lax.map(one_block, jnp.arange(nb))
    return jnp.moveaxis(o, 0, 1).reshape(bsz, T, D_ATTN)


def mixer(xm, p, h0_re, h0_im, rope, ctx_kv):
    bsz, T, _ = xm.shape
    proj = xm @ p['w_in']
    o1 = D_SSM
    o2 = o1 + D_ATTN
    o3 = o2 + D_KV
    o4 = o3 + D_KV
    o5 = o4 + D_MODEL
    u, q, k, v, g_s, g_a = jnp.split(proj, [o1, o2, o3, o4, o5], axis=-1)
    y_ssm, h_re, h_im = ssm_branch(u, p, h0_re, h0_im)
    q = q.reshape(bsz, T, N_KV_HEADS, Q_PER_KV, HEAD_DIM)
    k = k.reshape(bsz, T, N_KV_HEADS, HEAD_DIM)
    v = v.reshape(bsz, T, N_KV_HEADS, HEAD_DIM)
    if rope is not None:
        q = apply_axial_rope(q, rope)
        k = apply_axial_rope(k, rope)
    q = q * (HEAD_DIM ** -0.5)
    if ctx_kv is None:
        o = attention_context(q, k, v, p['sink'])
    else:
        o = attention_latent(q, k, v, ctx_kv[0], ctx_kv[1], p['sink'])
    merged = jax.nn.sigmoid(g_s) * (y_ssm @ p['w_ssm_o']) + jax.nn.sigmoid(g_a) * (o @ p['w_attn_o'])
    return merged @ p['w_out'], k, v, h_re, h_im


def conv_ffn(xm, p):
    h = xm @ p['w_up']
    T = h.shape[1]
    hp = jnp.pad(h, ((0, 0), (1, 1), (0, 0)))
    w = p['conv_w']
    h = hp[:, :T] * w[0] + hp[:, 1:T + 1] * w[1] + hp[:, 2:] * w[2] + p['conv_b']
    a, b = jnp.split(h, 2, axis=-1)
    return (jax.nn.silu(a) * b) @ p['w_down']


def layer_forward(x, mod, p, h0_re, h0_im, rope, ctx_kv):
    sh1, sc1, g1, sh2, sc2, g2 = mod
    xm = rmsnorm(x, p['norm_mix_g']) * (1.0 + sc1) + sh1
    out, k, v, h_re, h_im = mixer(xm, p, h0_re, h0_im, rope, ctx_kv)
    x = x + g1 * out
    xm2 = rmsnorm(x, p['norm_ffn_g']) * (1.0 + sc2) + sh2
    x = x + g2 * conv_ffn(xm2, p)
    return x, k, v, h_re, h_im


def setup_inputs(seed: int = 0) -> dict:
    key = jax.random.key(seed)
    ks = jax.random.split(key, 32)
    nrm = jax.random.normal
    f32 = jnp.float32
    L = DEPTH
    lam_im_base = math.pi * jnp.arange(SSM_STATE, dtype=f32)
    return {
        'x_prompt': nrm(ks[0], (BATCH, SEQ, D_MODEL), f32),
        'x_sample': nrm(ks[1], (DEC_BATCH, DEC_SEQ, D_MODEL), f32),
        'cache_k': nrm(ks[2], (DEC_BATCH, L, PAST_LEN, N_KV_HEADS, HEAD_DIM), f32),
        'cache_v': nrm(ks[3], (DEC_BATCH, L, PAST_LEN, N_KV_HEADS, HEAD_DIM), f32),
        'state_ssm_re': 0.1 * nrm(ks[4], (DEC_BATCH, L, 2, SSM_GROUPS, SSM_STATE), f32),
        'state_ssm_im': 0.1 * nrm(ks[5], (DEC_BATCH, L, 2, SSM_GROUPS, SSM_STATE), f32),
        'c': nrm(ks[6], (DEC_BATCH, D_MODEL), f32),
        'c_ctx': nrm(ks[7], (D_MODEL,), f32),
        'norm_mix_g': 1.0 + 0.01 * nrm(ks[8], (L, D_MODEL), f32),
        'norm_ffn_g': 1.0 + 0.01 * nrm(ks[9], (L, D_MODEL), f32),
        'w_mod': 0.5 * D_MODEL ** -0.5 * nrm(ks[10], (L, D_MODEL, 6 * D_MODEL), f32),
        'b_mod': 0.01 * nrm(ks[11], (L, 6 * D_MODEL), f32),
        'w_in': D_MODEL ** -0.5 * nrm(ks[12], (L, D_MODEL, IN_COLS), f32),
        'ssm_lambda_re': -0.5 + 0.01 * nrm(ks[13], (L, 2, SSM_GROUPS, SSM_STATE), f32),
        'ssm_lambda_im': lam_im_base + 0.01 * nrm(ks[14], (L, 2, SSM_GROUPS, SSM_STATE), f32),
        'ssm_log_dt': jax.random.uniform(ks[15], (L, 2, SSM_GROUPS), f32,
                                         minval=math.log(1e-3), maxval=math.log(1e-1)),
        'ssm_b_re': (2 * SSM_CH) ** -0.5 * nrm(ks[16], (L, 2, SSM_GROUPS, SSM_STATE, SSM_CH), f32),
        'ssm_b_im': (2 * SSM_CH) ** -0.5 * nrm(ks[17], (L, 2, SSM_GROUPS, SSM_STATE, SSM_CH), f32),
        'ssm_c_re': SSM_STATE ** -0.5 * nrm(ks[18], (L, 2, SSM_GROUPS, SSM_CH, SSM_STATE), f32),
        'ssm_c_im': SSM_STATE ** -0.5 * nrm(ks[19], (L, 2, SSM_GROUPS, SSM_CH, SSM_STATE), f32),
        'ssm_d': nrm(ks[20], (L, D_SSM), f32),
        'w_glu': D_SSM ** -0.5 * nrm(ks[21], (L, D_SSM, D_SSM), f32),
        'attn_sink': 0.5 * nrm(ks[22], (L, N_HEADS), f32),
        'w_ssm_o': D_SSM ** -0.5 * nrm(ks[23], (L, D_SSM, D_MODEL), f32),
        'w_attn_o': D_ATTN ** -0.5 * nrm(ks[24], (L, D_ATTN, D_MODEL), f32),
        'w_out': D_MODEL ** -0.5 * nrm(ks[25], (L, D_MODEL, D_MODEL), f32),
        'w_up': D_MODEL ** -0.5 * nrm(ks[26], (L, D_MODEL, 2 * D_FF), f32),
        'conv_w': CONV_WIDTH ** -0.5 * nrm(ks[27], (L, CONV_WIDTH, 2 * D_FF), f32),
        'conv_b': 0.01 * nrm(ks[28], (L, 2 * D_FF), f32),
        'w_down': D_FF ** -0.5 * nrm(ks[29], (L, D_FF, D_MODEL), f32),
        'final_norm_g': 1.0 + 0.01 * nrm(ks[30], (D_MODEL,), f32),
    }


def reference(x_prompt, x_sample, cache_k, cache_v, state_ssm_re, state_ssm_im, c, c_ctx,
              norm_mix_g, norm_ffn_g, w_mod, b_mod, w_in, ssm_lambda_re, ssm_lambda_im,
              ssm_log_dt, ssm_b_re, ssm_b_im, ssm_c_re, ssm_c_im, ssm_d, w_glu, attn_sink,
              w_ssm_o, w_attn_o, w_out, w_up, conv_w, conv_b, w_down, final_norm_g):
    rope = axial_rope_tables(x_sample.shape[1])
    zero_h = jnp.zeros((x_prompt.shape[0], 2, SSM_GROUPS, SSM_STATE), jnp.float32)
    xp = x_prompt
    xs = x_sample
    ks_out, vs_out, hre_out, him_out = [], [], [], []
    for l in range(DEPTH):
        p = {
            'norm_mix_g': norm_mix_g[l], 'norm_ffn_g': norm_ffn_g[l], 'w_in': w_in[l],
            'lam_re': ssm_lambda_re[l], 'lam_im': ssm_lambda_im[l], 'log_dt': ssm_log_dt[l],
            'b_re': ssm_b_re[l], 'b_im': ssm_b_im[l], 'c_re': ssm_c_re[l], 'c_im': ssm_c_im[l],
            'ssm_d': ssm_d[l], 'w_glu': w_glu[l], 'sink': attn_sink[l],
            'w_ssm_o': w_ssm_o[l], 'w_attn_o': w_attn_o[l], 'w_out': w_out[l],
            'w_up': w_up[l], 'conv_w': conv_w[l], 'conv_b': conv_b[l], 'w_down': w_down[l],
        }
        mod_ctx = adaln(c_ctx, w_mod[l], b_mod[l])
        xp, k_l, v_l, h_re, h_im = layer_forward(xp, mod_ctx, p, zero_h, zero_h, None, None)
        ks_out.append(k_l)
        vs_out.append(v_l)
        hre_out.append(h_re)
        him_out.append(h_im)
        mod_lat = adaln(c, w_mod[l], b_mod[l])
        xs, _, _, _, _ = layer_forward(xs, mod_lat, p, state_ssm_re[:, l], state_ssm_im[:, l], rope,
                                       (cache_k[:, l], cache_v[:, l]))
    y_prompt = rmsnorm(xp, final_norm_g)
    y_sample = rmsnorm(xs, final_norm_g)
    new_cache_k = jnp.stack(ks_out, axis=1)
    new_cache_v = jnp.stack(vs_out, axis=1)
    new_state_ssm_re = jnp.stack(hre_out, axis=1)
    new_state_ssm_im = jnp.stack(him_out, axis=1)
    return (y_prompt, y_sample, new_cache_k, new_cache_v, new_state_ssm_re, new_state_ssm_im)
```

```python
import functools
import math

import jax
import jax.numpy as jnp
from jax import lax
from jax.experimental import pallas as pl
from jax.experimental.pallas import tpu as pltpu

F32 = jnp.float32
BF16 = jnp.bfloat16

GRID_W = 64
SSM_CH = 16
SSM_STATE = 64
N_HEADS = 8
N_KV_HEADS = 2
HEAD_DIM = 128
Q_PER_KV = N_HEADS // N_KV_HEADS
WINDOW = 128
BLOCK = 128
ROPE_THETA = 10000.0
EPS = 1e-6
NEG_INF = -1e30

SSM_CHUNK = 16
SSM_LANES = SSM_CHUNK * SSM_CH
LATENT_SEQ_PAD = 8

V7X_VMEM_LIMIT_BYTES = 56 * 1024 * 1024


def _cparams(semantics):
    return pltpu.CompilerParams(dimension_semantics=semantics, vmem_limit_bytes=V7X_VMEM_LIMIT_BYTES)


def _sigmoid(x):
    return 1.0 / (1.0 + jnp.exp(-x))


def _rms_modulate(x, g, scale, shift):
    ms = jnp.mean(x * x, axis=-1, keepdims=True)
    return (x * lax.rsqrt(ms + EPS) * g) * (1.0 + scale) + shift


def _mod_kernel(c_ref, w_ref, b_ref, o_ref):
    c = c_ref[...]
    a = (c * _sigmoid(c)).astype(BF16)
    o_ref[...] = jnp.dot(a, w_ref[...].astype(BF16), preferred_element_type=F32) + b_ref[...]


def _modulation(cond, w_mod, b_mod):
    rows, d = cond.shape
    n = w_mod.shape[1]
    tn = 1024
    return pl.pallas_call(
        _mod_kernel,
        grid=(n // tn,),
        in_specs=[pl.BlockSpec((rows, d), lambda j: (0, 0)),
                  pl.BlockSpec((d, tn), lambda j: (0, j)),
                  pl.BlockSpec((1, tn), lambda j: (0, j))],
        out_specs=pl.BlockSpec((rows, tn), lambda j: (0, j)),
        out_shape=jax.ShapeDtypeStruct((rows, n), F32),
        compiler_params=_cparams(("arbitrary",)),
        name="mod",
    )(cond, w_mod, b_mod.reshape(1, n))


PROJ_TN = 512


def _swap32(x):
    n = x.shape[-1]
    lane = lax.broadcasted_iota(jnp.int32, x.shape, x.ndim - 1)
    return jnp.where((lane & 63) < 32, pltpu.roll(x, n - 32, x.ndim - 1), pltpu.roll(x, 32, x.ndim - 1))


def _proj_kernel(*refs, rope, q_tiles, kv_tile, d_kv):
    if rope:
        x_ref, mod_ref, g_ref, w_ref, cos_ref, sin_ref, o_ref, xm_scr = refs
    else:
        x_ref, mod_ref, g_ref, w_ref, o_ref, kv_ref, xm_scr = refs
    j = pl.program_id(1)

    @pl.when(j == 0)
    def _():
        xm = _rms_modulate(x_ref[...], g_ref[...], mod_ref[1:2, :], mod_ref[0:1, :])
        xm_scr[...] = xm.astype(BF16)

    acc = jnp.dot(xm_scr[...], w_ref[...], preferred_element_type=F32)
    q_scale = HEAD_DIM ** -0.5
    is_q = (j >= q_tiles[0]) & (j < q_tiles[1])
    is_kv = j == kv_tile

    if rope:
        reps = acc.shape[1] // HEAD_DIM

        def roped(a):
            cos = jnp.concatenate([cos_ref[...]] * reps, axis=1)
            sin = jnp.concatenate([sin_ref[...]] * reps, axis=1)
            return a * cos + _swap32(a) * sin

        @pl.when(is_q)
        def _():
            o_ref[...] = (roped(acc) * q_scale).astype(BF16)

        @pl.when(is_kv)
        def _():
            lane = lax.broadcasted_iota(jnp.int32, acc.shape, 1)
            o_ref[...] = jnp.where(lane < d_kv, roped(acc), acc).astype(BF16)

        @pl.when(jnp.logical_not(is_q | is_kv))
        def _():
            o_ref[...] = acc.astype(BF16)
    else:
        @pl.when(is_q)
        def _():
            o_ref[...] = (acc * q_scale).astype(BF16)

        @pl.when(jnp.logical_not(is_q))
        def _():
            o_ref[...] = acc.astype(BF16)

        @pl.when(is_kv)
        def _():
            kv_ref[...] = acc


def _proj(x, mod, norm_g, w_in_b, *, tm, tiles_per_seq, rope_tables, col):
    n, d = x.shape
    cols = w_in_b.shape[1]
    tn = PROJ_TN
    q_tiles = (col["q"] // tn, (col["q"] + col["d_attn"]) // tn)
    kv_tile = col["k"] // tn
    rope = rope_tables is not None
    seq_of = (lambda i: i // tiles_per_seq) if mod.shape[0] > 1 else (lambda i: 0)
    in_specs = [pl.BlockSpec((tm, d), lambda i, j: (i, 0)),
                pl.BlockSpec((None, 6, d), lambda i, j: (seq_of(i), 0, 0)),
                pl.BlockSpec((1, d), lambda i, j: (0, 0)),
                pl.BlockSpec((d, tn), lambda i, j: (0, j))]
    args = [x, mod, norm_g.reshape(1, d), w_in_b]
    out_specs = [pl.BlockSpec((tm, tn), lambda i, j: (i, j))]
    out_shape = [jax.ShapeDtypeStruct((n, cols), BF16)]
    if rope:
        cos, sin = rope_tables
        in_specs += [pl.BlockSpec((tm, HEAD_DIM), lambda i, j: (i % tiles_per_seq, 0))] * 2
        args += [cos, sin]
    else:
        out_specs.append(pl.BlockSpec((tm, tn), lambda i, j: (i, 0)))
        out_shape.append(jax.ShapeDtypeStruct((n, tn), F32))
    outs = pl.pallas_call(
        functools.partial(_proj_kernel, rope=rope, q_tiles=q_tiles, kv_tile=kv_tile, d_kv=col["d_kv"]),
        grid=(n // tm, cols // tn),
        in_specs=in_specs,
        out_specs=out_specs,
        out_shape=out_shape,
        scratch_shapes=[pltpu.VMEM((tm, d), BF16)],
        compiler_params=_cparams(("arbitrary", "arbitrary")),
        name="proj_rope" if rope else "proj",
    )(*args)
    return outs if not rope else (outs[0], None)


def _ssm_operators(lam_re, lam_im, log_dt, b_re, b_im, c_re, c_im, ssm_d):
    two, g, p = lam_re.shape
    ch = b_re.shape[-1]
    L = SSM_CHUNK
    dt = jnp.exp(log_dt.astype(F32))[..., None]
    a = lam_re * dt
    th = lam_im * dt
    tau = jnp.arange(L + 1, dtype=F32)
    mag = jnp.exp(a[..., None] * tau)
    ang = th[..., None] * tau
    e_re = mag * jnp.cos(ang)
    e_im = mag * jnp.sin(ang)
    n_re = e_re[..., 1] - 1.0
    n_im = e_im[..., 1]
    den = lam_re * lam_re + lam_im * lam_im
    q_re = (n_re * lam_re + n_im * lam_im) / den
    q_im = (n_im * lam_re - n_re * lam_im) / den
    bb_re = q_re[..., None] * b_re - q_im[..., None] * b_im
    bb_im = q_re[..., None] * b_im + q_im[..., None] * b_re
    ce_re = c_re[..., None] * e_re[:, :, None] - c_im[..., None] * e_im[:, :, None]
    ce_im = c_re[..., None] * e_im[:, :, None] + c_im[..., None] * e_re[:, :, None]
    k = (jnp.einsum("dgkpt,dgpc->dgtkc", ce_re[..., :L], bb_re)
         - jnp.einsum("dgkpt,dgpc->dgtkc", ce_im[..., :L], bb_im))
    s_idx = jnp.arange(L)[:, None]
    t_idx = jnp.arange(L)[None, :]
    lag_f = jnp.clip(t_idx - s_idx, 0, L - 1)
    lag_b = jnp.clip(s_idx - t_idx, 0, L - 1)
    kf = jnp.where((t_idx >= s_idx)[None, :, :, None, None], k[0][:, lag_f], 0.0)
    kb = jnp.where((s_idx >= t_idx)[None, :, :, None, None], k[1][:, lag_b], 0.0)
    skip = (jnp.eye(L)[None, :, :, None, None] * jnp.eye(ch)[None, None, None]
            * ssm_d.reshape(g, 1, 1, 1, ch))
    m_op = (kf + kb + skip).transpose(0, 1, 4, 2, 3).reshape(g, L * ch, L * ch)

    def state_in(d, e_r, e_i):
        re = e_r[:, :, :, None] * bb_re[d][:, :, None, :] - e_i[:, :, :, None] * bb_im[d][:, :, None, :]
        im = e_r[:, :, :, None] * bb_im[d][:, :, None, :] + e_i[:, :, :, None] * bb_re[d][:, :, None, :]
        return (re.transpose(0, 2, 3, 1).reshape(g, L * ch, p),
                im.transpose(0, 2, 3, 1).reshape(g, L * ch, p))
    wf_re, wf_im = state_in(0, e_re[0][..., L - 1::-1][..., :L], e_im[0][..., L - 1::-1][..., :L])
    wb_re, wb_im = state_in(1, e_re[1][..., :L], e_im[1][..., :L])
    w_op = jnp.concatenate([wf_re, wb_re, wf_im, wb_im], axis=-1)

    def state_out(d, taus):
        re = ce_re[d][..., taus].transpose(0, 2, 3, 1).reshape(g, p, L * ch)
        im = ce_im[d][..., taus].transpose(0, 2, 3, 1).reshape(g, p, L * ch)
        return re, -im
    vf_re, vf_im = state_out(0, jnp.arange(1, L + 1))
    vb_re, vb_im = state_out(1, jnp.arange(L, 0, -1))
    v_op = jnp.concatenate([vf_re, vb_re, vf_im, vb_im], axis=1)
    a_op = jnp.concatenate([e_re[0][..., L], e_re[1][..., L], e_im[0][..., L], e_im[1][..., L]],
                           axis=-1).reshape(g, 1, 4 * p)
    return m_op.astype(BF16), w_op.astype(BF16), v_op.astype(BF16), a_op


def _ssm_kernel(u_ref, m_ref, w_ref, v_ref, a_ref, h0_ref, y_ref, hfin_ref, s_scr, hs_scr, hin_scr, *, n_chunks, ns):
    half = 2 * SSM_STATE
    u = u_ref[...]
    s_scr[...] = jnp.dot(u, w_ref[...], preferred_element_type=F32)
    lane = lax.broadcasted_iota(jnp.int32, (1, half), 1)
    is_fwd = lane < SSM_STATE
    a_re = a_ref[:, :half]
    a_im = a_ref[:, half:]
    h_re = h0_ref[:, :half]
    h_im = h0_ref[:, half:]
    for k in range(n_chunks):
        kb = n_chunks - 1 - k
        hs_scr[k * ns:(k + 1) * ns, :half] = h_re
        hs_scr[k * ns:(k + 1) * ns, half:] = h_im
        s_re = jnp.where(is_fwd, s_scr[k * ns:(k + 1) * ns, :half], s_scr[kb * ns:(kb + 1) * ns, :half])
        s_im = jnp.where(is_fwd, s_scr[k * ns:(k + 1) * ns, half:], s_scr[kb * ns:(kb + 1) * ns, half:])
        h_re, h_im = a_re * h_re - a_im * h_im + s_re, a_re * h_im + a_im * h_re + s_im
    hfin_ref[:, :half] = h_re
    hfin_ref[:, half:] = h_im
    lane2 = lax.broadcasted_iota(jnp.int32, (1, 2 * half), 1)
    is_fwd2 = (lane2 & (half - 1)) < SSM_STATE
    for c in range(n_chunks):
        cb = n_chunks - 1 - c
        hin_scr[c * ns:(c + 1) * ns, :] = jnp.where(
            is_fwd2, hs_scr[c * ns:(c + 1) * ns, :], hs_scr[cb * ns:(cb + 1) * ns, :]).astype(BF16)
    y = jnp.dot(u, m_ref[...], preferred_element_type=F32)
    y = y + jnp.dot(hin_scr[...], v_ref[...], preferred_element_type=F32)
    y_ref[...] = y.astype(BF16)


def _ssm(u_chunks, ops, h0, *, n_chunks, ns):
    m_op, w_op, v_op, a_op = ops
    g, rows, lanes = u_chunks.shape
    mat = pl.BlockSpec((None, lanes, lanes), lambda i: (i, 0, 0))
    return pl.pallas_call(
        functools.partial(_ssm_kernel, n_chunks=n_chunks, ns=ns),
        grid=(g,),
        in_specs=[pl.BlockSpec((None, rows, lanes), lambda i: (i, 0, 0)), mat, mat, mat,
                  pl.BlockSpec((None, 1, lanes), lambda i: (i, 0, 0)),
                  pl.BlockSpec((None, ns, lanes), lambda i: (i, 0, 0))],
        out_specs=[pl.BlockSpec((None, rows, lanes), lambda i: (i, 0, 0)),
                   pl.BlockSpec((None, ns, lanes), lambda i: (i, 0, 0))],
        out_shape=[jax.ShapeDtypeStruct((g, rows, lanes), BF16),
                   jax.ShapeDtypeStruct((g, ns, lanes), F32)],
        scratch_shapes=[pltpu.VMEM((rows, lanes), F32), pltpu.VMEM((rows, lanes), F32),
                        pltpu.VMEM((rows, lanes), BF16)],
        compiler_params=_cparams(("arbitrary",)),
        name="ssm",
    )(u_chunks, m_op, w_op, v_op, a_op, h0)


def _to_chunks(u, n_seq, seq_len, ns):
    g = u.shape[1] // SSM_CH
    n_chunks = seq_len // SSM_CHUNK
    x = u.reshape(n_seq, n_chunks, SSM_CHUNK, g, SSM_CH).transpose(3, 1, 0, 2, 4)
    if ns > n_seq:
        x = jnp.pad(x, ((0, 0), (0, 0), (0, ns - n_seq), (0, 0), (0, 0)))
    return x.reshape(g, n_chunks * ns, SSM_LANES)


def _from_chunks(y, n_seq, seq_len, ns):
    g = y.shape[0]
    n_chunks = seq_len // SSM_CHUNK
    x = y.reshape(g, n_chunks, ns, SSM_CHUNK, SSM_CH)[:, :, :n_seq]
    return x.transpose(2, 1, 3, 0, 4).reshape(n_seq * seq_len, g * SSM_CH)


def _softmax_pv(pieces, sink):
    m = sink
    for s, _ in pieces:
        m = jnp.maximum(m, jnp.max(s, axis=-1, keepdims=True))
    den = jnp.exp(sink - m)
    out = None
    for s, v in pieces:
        p = jnp.exp(s - m)
        den = den + jnp.sum(p, axis=-1, keepdims=True)
        pv = jnp.dot(p.astype(BF16), v, preferred_element_type=F32)
        out = pv if out is None else out + pv
    return out / den


def _qk(q, k):
    return lax.dot_general(q, k, (((1,), (1,)), ((), ())), preferred_element_type=F32)


def _attn_ctx_kernel(sink_ref, q_ref, kv_ref, o_ref):
    d_kv = N_KV_HEADS * HEAD_DIM
    for h in range(N_HEADS):
        kh = h // Q_PER_KV
        q = q_ref[:, h * HEAD_DIM:(h + 1) * HEAD_DIM]
        k = kv_ref[:, kh * HEAD_DIM:(kh + 1) * HEAD_DIM]
        v = kv_ref[:, d_kv + kh * HEAD_DIM:d_kv + (kh + 1) * HEAD_DIM]
        o = _softmax_pv([(_qk(q, k), v)], sink_ref[h])
        o_ref[:, h * HEAD_DIM:(h + 1) * HEAD_DIM] = o.astype(BF16)


def _attn_ctx(proj, sink, *, n_seq, seq_len, col):
    d_attn = col["d_attn"]
    kv_w = 2 * col["d_kv"]
    return pl.pallas_call(
        _attn_ctx_kernel,
        grid=(n_seq,),
        in_specs=[pl.BlockSpec(memory_space=pltpu.SMEM),
                  pl.BlockSpec((seq_len, d_attn), lambda b: (b, col["q"] // d_attn)),
                  pl.BlockSpec((seq_len, kv_w), lambda b: (b, col["k"] // kv_w))],
        out_specs=pl.BlockSpec((seq_len, d_attn), lambda b: (b, 0)),
        out_shape=jax.ShapeDtypeStruct((n_seq * seq_len, d_attn), BF16),
        compiler_params=_cparams(("arbitrary",)),
        name="attn_ctx",
    )(sink, proj, proj)


def _attn_lat_kernel(sink_ref, q_ref, kvp_ref, kvc_ref, kvn_ref, ck_ref, cv_ref, o_ref, *, n_blocks):
    i = pl.program_id(1)
    d_kv = N_KV_HEADS * HEAD_DIM
    r = lax.broadcasted_iota(jnp.int32, (BLOCK, 3 * BLOCK), 0)
    c = lax.broadcasted_iota(jnp.int32, (BLOCK, 3 * BLOCK), 1)
    cc = c & (BLOCK - 1)
    valid = (((c < BLOCK) & (cc >= r) & (i > 0)) | ((c >= BLOCK) & (c < 2 * BLOCK))
             | ((c >= 2 * BLOCK) & (cc <= r) & (i < n_blocks - 1)))
    for h in range(N_HEADS):
        kh = h // Q_PER_KV
        ks = slice(kh * HEAD_DIM, (kh + 1) * HEAD_DIM)
        vs = slice(d_kv + kh * HEAD_DIM, d_kv + (kh + 1) * HEAD_DIM)
        q = q_ref[:, h * HEAD_DIM:(h + 1) * HEAD_DIM]
        k_loc = jnp.concatenate([kvp_ref[:, ks], kvc_ref[:, ks], kvn_ref[:, ks]], axis=0)
        v_loc = jnp.concatenate([kvp_ref[:, vs], kvc_ref[:, vs], kvn_ref[:, vs]], axis=0)
        s_loc = jnp.where(valid, _qk(q, k_loc), NEG_INF)
        s_ctx = _qk(q, ck_ref[:, ks].astype(BF16))
        o = _softmax_pv([(s_loc, v_loc), (s_ctx, cv_ref[:, ks].astype(BF16))], sink_ref[h])
        o_ref[:, h * HEAD_DIM:(h + 1) * HEAD_DIM] = o.astype(BF16)


def _attn_lat(proj, cache_k, cache_v, sink, *, n_seq, seq_len, col):
    d_attn = col["d_attn"]
    kv_w = 2 * col["d_kv"]
    nb = seq_len // BLOCK
    kv_col = col["k"] // kv_w
    past, d_kv = cache_k.shape[1], cache_k.shape[2]

    def kv_spec(off):
        return pl.BlockSpec((BLOCK, kv_w), lambda b, i: (b * nb + jnp.clip(i + off, 0, nb - 1), kv_col))
    cache_spec = pl.BlockSpec((None, past, d_kv), lambda b, i: (b, 0, 0))
    return pl.pallas_call(
        functools.partial(_attn_lat_kernel, n_blocks=nb),
        grid=(n_seq, nb),
        in_specs=[pl.BlockSpec(memory_space=pltpu.SMEM),
                  pl.BlockSpec((BLOCK, d_attn), lambda b, i: (b * nb + i, col["q"] // d_attn)),
                  kv_spec(-1), kv_spec(0), kv_spec(1), cache_spec, cache_spec],
        out_specs=pl.BlockSpec((BLOCK, d_attn), lambda b, i: (b * nb + i, 0)),
        out_shape=jax.ShapeDtypeStruct((n_seq * seq_len, d_attn), BF16),
        compiler_params=_cparams(("arbitrary", "arbitrary")),
        name="attn_lat",
    )(sink, proj, proj, proj, proj, cache_k, cache_v)


def _gelu_tanh(x):
    return 0.5 * x * (1.0 + jnp.tanh(math.sqrt(2.0 / math.pi) * (x + 0.044715 * (x * x * x))))


def _mixer_out_kernel(x_ref, y_ref, o_ref, gs_ref, ga_ref, mod_ref, g_ref, wglu_ref, wso_ref, wao_ref, wout_ref,
                      x1_ref, xm2_ref):
    z = _gelu_tanh(y_ref[...].astype(F32))
    z = z * _sigmoid(jnp.dot(z.astype(BF16), wglu_ref[...], preferred_element_type=F32))
    s_br = jnp.dot(z.astype(BF16), wso_ref[...], preferred_element_type=F32)
    a_br = jnp.dot(o_ref[...], wao_ref[...], preferred_element_type=F32)
    merged = _sigmoid(gs_ref[...].astype(F32)) * s_br + _sigmoid(ga_ref[...].astype(F32)) * a_br
    out = jnp.dot(merged.astype(BF16), wout_ref[...], preferred_element_type=F32)
    x1 = x_ref[...] + mod_ref[2:3, :] * out
    x1_ref[...] = x1
    xm2_ref[...] = _rms_modulate(x1, g_ref[...], mod_ref[4:5, :], mod_ref[3:4, :]).astype(BF16)


def _mixer_out(x, y, o, proj, mod, norm_g, w_glu, w_ssm_o, w_attn_o, w_out, *, tm, tiles_per_seq, col):
    n, d = x.shape
    d_ssm = y.shape[1]
    d_attn = o.shape[1]
    seq_of = (lambda i: i // tiles_per_seq) if mod.shape[0] > 1 else (lambda i: 0)

    def resident(shape):
        return pl.BlockSpec(shape, lambda i: (0, 0), pipeline_mode=pl.Buffered(1))
    return pl.pallas_call(
        _mixer_out_kernel,
        grid=(n // tm,),
        in_specs=[pl.BlockSpec((tm, d), lambda i: (i, 0)),
                  pl.BlockSpec((tm, d_ssm), lambda i: (i, 0)),
                  pl.BlockSpec((tm, d_attn), lambda i: (i, 0)),
                  pl.BlockSpec((tm, d), lambda i: (i, col["gs"] // d)),
                  pl.BlockSpec((tm, d), lambda i: (i, col["ga"] // d)),
                  pl.BlockSpec((None, 6, d), lambda i: (seq_of(i), 0, 0)),
                  pl.BlockSpec((1, d), lambda i: (0, 0)),
                  resident(w_glu.shape), resident(w_ssm_o.shape), resident(w_attn_o.shape), resident(w_out.shape)],
        out_specs=[pl.BlockSpec((tm, d), lambda i: (i, 0)), pl.BlockSpec((tm, d), lambda i: (i, 0))],
        out_shape=[jax.ShapeDtypeStruct((n, d), F32), jax.ShapeDtypeStruct((n, d), BF16)],
        compiler_params=_cparams(("arbitrary",)),
        name="mixer_out",
    )(x, y, o, proj, proj, mod, norm_g.reshape(1, d), w_glu, w_ssm_o, w_attn_o, w_out)


FFN_TF = 512
HALO_ROWS = 16


def _ffn_kernel(xm_ref, halo_ref, wa_ref, wb_ref, cwa_ref, cwb_ref, cba_ref, cbb_ref, wd_ref, x1_ref, mod_ref,
                g_ref, o_ref, acc_scr, *, seq_len):
    i = pl.program_id(0)
    j = pl.program_id(1)
    tm = xm_ref.shape[0]
    xm = xm_ref[...]
    halo = halo_ref[...]
    row = lax.broadcasted_iota(jnp.int32, (tm, 1), 0)
    pos = (i * tm + row) & (seq_len - 1)

    def conv(w_ref, cw_ref, cb_ref):
        h = jnp.dot(xm, w_ref[...], preferred_element_type=F32)
        hh = jnp.dot(halo, w_ref[...], preferred_element_type=F32)
        h_prev = jnp.where(row == 0, hh[0:1, :], pltpu.roll(h, 1, 0))
        h_prev = jnp.where(pos == 0, 0.0, h_prev)
        h_next = jnp.where(row == tm - 1, hh[1:2, :], pltpu.roll(h, tm - 1, 0))
        h_next = jnp.where(pos == seq_len - 1, 0.0, h_next)
        return cw_ref[0:1, :] * h_prev + cw_ref[1:2, :] * h + cw_ref[2:3, :] * h_next + cb_ref[...]

    a = conv(wa_ref, cwa_ref, cba_ref)
    b = conv(wb_ref, cwb_ref, cbb_ref)
    act = ((a * _sigmoid(a)) * b).astype(BF16)
    part = jnp.dot(act, wd_ref[...], preferred_element_type=F32)

    @pl.when(j == 0)
    def _():
        acc_scr[...] = part

    @pl.when(j > 0)
    def _():
        acc_scr[...] += part

    @pl.when(j == pl.num_programs(1) - 1)
    def _():
        x2 = x1_ref[...] + mod_ref[5:6, :] * acc_scr[...]
        ms = jnp.mean(x2 * x2, axis=-1, keepdims=True)
        o_ref[...] = x2 * lax.rsqrt(ms + EPS) * g_ref[...]


def _ffn(x1, xm2, mod, w_up, conv_w, conv_b, w_down, final_g, *, tm, seq_len):
    n, d = x1.shape
    d_ff = w_down.shape[0]
    tf = FFN_TF
    nf = d_ff // tf
    n_tiles = n // tm
    tiles_per_seq = max(seq_len // tm, 1)
    seq_of = (lambda i: i // tiles_per_seq) if mod.shape[0] > 1 else (lambda i: 0)
    xt = xm2.reshape(n_tiles, tm, d)
    zero = jnp.zeros((1, d), xm2.dtype)
    prev = jnp.concatenate([zero, xt[:-1, tm - 1]], axis=0)
    nxt = jnp.concatenate([xt[1:, 0], zero], axis=0)
    halo = jnp.concatenate([prev[:, None], nxt[:, None], jnp.zeros((n_tiles, HALO_ROWS - 2, d), xm2.dtype)], axis=1)
    cb = conv_b.reshape(1, 2 * d_ff)
    return pl.pallas_call(
        functools.partial(_ffn_kernel, seq_len=seq_len),
        grid=(n_tiles, nf),
        in_specs=[pl.BlockSpec((tm, d), lambda i, j: (i, 0)),
                  pl.BlockSpec((None, HALO_ROWS, d), lambda i, j: (i, 0, 0)),
                  pl.BlockSpec((d, tf), lambda i, j: (0, j)),
                  pl.BlockSpec((d, tf), lambda i, j: (0, nf + j)),
                  pl.BlockSpec((3, tf), lambda i, j: (0, j)),
                  pl.BlockSpec((3, tf), lambda i, j: (0, nf + j)),
                  pl.BlockSpec((1, tf), lambda i, j: (0, j)),
                  pl.BlockSpec((1, tf), lambda i, j: (0, nf + j)),
                  pl.BlockSpec((tf, d), lambda i, j: (j, 0)),
                  pl.BlockSpec((tm, d), lambda i, j: (i, 0)),
                  pl.BlockSpec((None, 6, d), lambda i, j: (seq_of(i), 0, 0)),
                  pl.BlockSpec((1, d), lambda i, j: (0, 0))],
        out_specs=pl.BlockSpec((tm, d), lambda i, j: (i, 0)),
        out_shape=jax.ShapeDtypeStruct((n, d), F32),
        scratch_shapes=[pltpu.VMEM((tm, d), F32)],
        compiler_params=_cparams(("arbitrary", "arbitrary")),
        name="ffn",
    )(xm2, halo, w_up, w_up, conv_w, conv_w, cb, cb, w_down, x1, mod, final_g.reshape(1, d))


def _rope_tables(seq_len):
    rows = seq_len // GRID_W
    row = jnp.repeat(jnp.arange(rows, dtype=F32), GRID_W)
    colp = jnp.tile(jnp.arange(GRID_W, dtype=F32), rows)
    n_freq = HEAD_DIM // 4
    inv = ROPE_THETA ** (-jnp.arange(n_freq, dtype=F32) / n_freq)
    ang_r = row[:, None] * inv[None, :]
    ang_c = colp[:, None] * inv[None, :]
    cos = jnp.concatenate([jnp.cos(ang_r), jnp.cos(ang_r), jnp.cos(ang_c), jnp.cos(ang_c)], axis=1)
    sin = jnp.concatenate([-jnp.sin(ang_r), jnp.sin(ang_r), -jnp.sin(ang_c), jnp.sin(ang_c)], axis=1)
    return cos, sin


def _layer(x, mod, w, ssm_ops, h0, *, n_seq, seq_len, latent, cache=None):
    col = w["col"]
    tm_proj = 1024
    proj, kv_raw = _proj(x, mod, w["norm_mix_g"], w["w_in"], tm=tm_proj, tiles_per_seq=max(seq_len // tm_proj, 1),
                         rope_tables=_rope_tables(seq_len) if latent else None, col=col)
    d_ssm = col["d_ssm"]
    ns = LATENT_SEQ_PAD if latent else n_seq
    n_chunks = seq_len // SSM_CHUNK
    u_chunks = _to_chunks(proj[:, col["u"]:col["u"] + d_ssm], n_seq, seq_len, ns)
    y_chunks, h_fin = _ssm(u_chunks, ssm_ops, h0, n_chunks=n_chunks, ns=ns)
    y_ssm = _from_chunks(y_chunks, n_seq, seq_len, ns)
    if latent:
        o = _attn_lat(proj, cache[0], cache[1], w["sink"], n_seq=n_seq, seq_len=seq_len, col=col)
    else:
        o = _attn_ctx(proj, w["sink"], n_seq=n_seq, seq_len=seq_len, col=col)
    tm = 512
    x1, xm2 = _mixer_out(x, y_ssm, o, proj, mod, w["norm_ffn_g"], w["w_glu"], w["w_ssm_o"], w["w_attn_o"],
                         w["w_out"], tm=tm, tiles_per_seq=max(seq_len // tm, 1), col=col)
    y = _ffn(x1, xm2, mod, w["w_up"], w["conv_w"], w["conv_b"], w["w_down"], w["final_norm_g"], tm=tm,
             seq_len=seq_len)
    return y, kv_raw, h_fin


def kernel(x_prompt, x_sample, cache_k, cache_v, state_ssm_re, state_ssm_im, c, c_ctx, norm_mix_g, norm_ffn_g,
           w_mod, b_mod, w_in, ssm_lambda_re, ssm_lambda_im, ssm_log_dt, ssm_b_re, ssm_b_im, ssm_c_re, ssm_c_im,
           ssm_d, w_glu, attn_sink, w_ssm_o, w_attn_o, w_out, w_up, conv_w, conv_b, w_down, final_norm_g):
    batch, seq, d = x_prompt.shape
    dec_batch, dec_seq, _ = x_sample.shape
    depth = w_in.shape[0]
    assert depth == 1, "final norm is fused into the (single) layer's ffn kernel"
    d_ssm = w_glu.shape[1]
    d_attn = N_HEADS * HEAD_DIM
    d_kv = N_KV_HEADS * HEAD_DIM
    groups = d_ssm // SSM_CH
    assert w_in.shape[2] == d_ssm + d_attn + 2 * d_kv + 2 * d
    assert ssm_lambda_re.shape[2:] == (groups, SSM_STATE) and dec_batch <= LATENT_SEQ_PAD
    l = 0

    o1, o2, o3, o4 = d_ssm, d_ssm + d_attn, d_ssm + d_attn + d_kv, d_ssm + d_attn + 2 * d_kv
    w_in_l = w_in[l]
    w_in_b = jnp.concatenate([w_in_l[:, o4:], w_in_l[:, :o4]], axis=1).astype(BF16)
    col = {"gs": 0, "ga": d, "u": 2 * d, "q": 2 * d + d_ssm, "k": 2 * d + d_ssm + d_attn,
           "d_ssm": d_ssm, "d_attn": d_attn, "d_kv": d_kv}
    w = {"col": col, "w_in": w_in_b, "norm_mix_g": norm_mix_g[l], "norm_ffn_g": norm_ffn_g[l],
         "w_glu": w_glu[l].astype(BF16), "w_ssm_o": w_ssm_o[l].astype(BF16), "w_attn_o": w_attn_o[l].astype(BF16),
         "w_out": w_out[l].astype(BF16), "w_up": w_up[l].astype(BF16), "conv_w": conv_w[l], "conv_b": conv_b[l],
         "w_down": w_down[l].astype(BF16), "sink": attn_sink[l], "final_norm_g": final_norm_g}

    cond = jnp.concatenate([c_ctx[None], c, jnp.zeros((8 - 1 - dec_batch, d), F32)], axis=0)
    mod = _modulation(cond, w_mod[l], b_mod[l]).reshape(8, 6, d)
    mod_ctx, mod_lat = mod[0:1], mod[1:1 + dec_batch]

    ssm_ops = _ssm_operators(ssm_lambda_re[l], ssm_lambda_im[l], ssm_log_dt[l], ssm_b_re[l], ssm_b_im[l],
                             ssm_c_re[l], ssm_c_im[l], ssm_d[l])

    h0_ctx = jnp.zeros((groups, batch, 4 * SSM_STATE), F32)
    y_p, kv_raw, h_fin = _layer(x_prompt.reshape(batch * seq, d), mod_ctx, w, ssm_ops, h0_ctx,
                                n_seq=batch, seq_len=seq, latent=False)

    def lanes(s):
        return s.transpose(2, 0, 1, 3).reshape(groups, dec_batch, 2 * SSM_STATE)
    h0_lat = jnp.concatenate([lanes(state_ssm_re[:, l]), lanes(state_ssm_im[:, l])], axis=-1)
    h0_lat = jnp.pad(h0_lat, ((0, 0), (0, LATENT_SEQ_PAD - dec_batch), (0, 0)))
    cache = (cache_k[:, l].reshape(dec_batch, -1, d_kv), cache_v[:, l].reshape(dec_batch, -1, d_kv))
    y_s, _, _ = _layer(x_sample.reshape(dec_batch * dec_seq, d), mod_lat, w, ssm_ops, h0_lat,
                       n_seq=dec_batch, seq_len=dec_seq, latent=True, cache=cache)

    new_k = kv_raw[:, :d_kv].reshape(batch, 1, seq, N_KV_HEADS, HEAD_DIM)
    new_v = kv_raw[:, d_kv:].reshape(batch, 1, seq, N_KV_HEADS, HEAD_DIM)

    def unlanes(hl):
        return hl.reshape(groups, batch, 2, SSM_STATE).transpose(1, 2, 0, 3)[:, None]
    new_re = unlanes(h_fin[:, :, :2 * SSM_STATE])
    new_im = unlanes(h_fin[:, :, 2 * SSM_STATE:])
    return (y_p.reshape(batch, seq, d), y_s.reshape(dec_batch, dec_seq, d), new_k, new_v, new_re, new_im)
```

```python
import functools
import math

import jax
import jax.numpy as jnp
from jax import lax
from jax.experimental import pallas as pl
from jax.experimental.pallas import tpu as pltpu

F32 = jnp.float32
BF16 = jnp.bfloat16

GRID_W = 64
SSM_CH = 16
SSM_STATE = 64
N_HEADS = 8
N_KV_HEADS = 2
HEAD_DIM = 128
Q_PER_KV = N_HEADS // N_KV_HEADS
WINDOW = 128
BLOCK = 128
ROPE_THETA = 10000.0
EPS = 1e-6
NEG_INF = -1e30

SSM_CHUNK = 16
SSM_LANES = SSM_CHUNK * SSM_CH
SSM_SEQ_BLOCK = 16

V7X_VMEM_LIMIT_BYTES = 56 * 1024 * 1024


def _cparams(semantics):
    return pltpu.CompilerParams(dimension_semantics=semantics, vmem_limit_bytes=V7X_VMEM_LIMIT_BYTES)


def _sigmoid(x):
    return 1.0 / (1.0 + jnp.exp(-x))


def _rms_modulate(x, g, scale, shift):
    ms = jnp.mean(x * x, axis=-1, keepdims=True)
    return (x * lax.rsqrt(ms + EPS) * g) * (1.0 + scale) + shift


def _mod_kernel(c_ref, w_ref, b_ref, o_ref):
    c = c_ref[...]
    a = (c * _sigmoid(c)).astype(BF16)
    o_ref[...] = jnp.dot(a, w_ref[...].astype(BF16), preferred_element_type=F32) + b_ref[...]


def _modulation(cond, w_mod, b_mod):
    rows, d = cond.shape
    n = w_mod.shape[1]
    tn = 1024
    return pl.pallas_call(
        _mod_kernel,
        grid=(n // tn,),
        in_specs=[pl.BlockSpec((rows, d), lambda j: (0, 0)),
                  pl.BlockSpec((d, tn), lambda j: (0, j)),
                  pl.BlockSpec((1, tn), lambda j: (0, j))],
        out_specs=pl.BlockSpec((rows, tn), lambda j: (0, j)),
        out_shape=jax.ShapeDtypeStruct((rows, n), F32),
        compiler_params=_cparams(("arbitrary",)),
        name="mod",
    )(cond, w_mod, b_mod.reshape(1, n))


PROJ_TN = 512


def _swap32(x):
    n = x.shape[-1]
    lane = lax.broadcasted_iota(jnp.int32, x.shape, x.ndim - 1)
    return jnp.where((lane & 63) < 32, pltpu.roll(x, n - 32, x.ndim - 1), pltpu.roll(x, 32, x.ndim - 1))


def _proj_kernel(*refs, rope, u_tiles, q_tiles, kv_tile, d_kv):
    if rope:
        x_ref, mod_ref, g_ref, w_ref, cos_ref, sin_ref, o_ref, u_ref, xm_scr = refs
    else:
        x_ref, mod_ref, g_ref, w_ref, o_ref, u_ref, kv_ref, xm_scr = refs
    j = pl.program_id(1)

    @pl.when(j == 0)
    def _():
        xm = _rms_modulate(x_ref[...], g_ref[...], mod_ref[1:2, :], mod_ref[0:1, :])
        xm_scr[...] = xm.astype(BF16)

    acc = jnp.dot(xm_scr[...], w_ref[...], preferred_element_type=F32)
    q_scale = HEAD_DIM ** -0.5
    is_q = (j >= q_tiles[0]) & (j < q_tiles[1])
    is_kv = j == kv_tile

    @pl.when((j >= u_tiles[0]) & (j < u_tiles[1]))
    def _():
        u_ref[...] = acc

    if rope:
        reps = acc.shape[1] // HEAD_DIM

        def roped(a):
            cos = jnp.concatenate([cos_ref[...]] * reps, axis=1)
            sin = jnp.concatenate([sin_ref[...]] * reps, axis=1)
            return a * cos + _swap32(a) * sin

        @pl.when(is_q)
        def _():
            o_ref[...] = (roped(acc) * q_scale).astype(BF16)

        @pl.when(is_kv)
        def _():
            lane = lax.broadcasted_iota(jnp.int32, acc.shape, 1)
            o_ref[...] = jnp.where(lane < d_kv, roped(acc), acc).astype(BF16)

        @pl.when(jnp.logical_not(is_q | is_kv))
        def _():
            o_ref[...] = acc.astype(BF16)
    else:
        @pl.when(is_q)
        def _():
            o_ref[...] = (acc * q_scale).astype(BF16)

        @pl.when(jnp.logical_not(is_q))
        def _():
            o_ref[...] = acc.astype(BF16)

        @pl.when(is_kv)
        def _():
            kv_ref[...] = acc


def _proj(x, mod, norm_g, w_in_b, *, tm, tiles_per_seq, rope_tables, col):
    n, d = x.shape
    cols = w_in_b.shape[1]
    tn = PROJ_TN
    u_tiles = (col["u"] // tn, (col["u"] + col["d_ssm"]) // tn)
    q_tiles = (col["q"] // tn, (col["q"] + col["d_attn"]) // tn)
    kv_tile = col["k"] // tn
    assert 2 * col["d_kv"] == tn
    rope = rope_tables is not None
    seq_of = (lambda i: i // tiles_per_seq) if mod.shape[0] > 1 else (lambda i: 0)
    in_specs = [pl.BlockSpec((tm, d), lambda i, j: (i, 0)),
                pl.BlockSpec((None, 6, d), lambda i, j: (seq_of(i), 0, 0)),
                pl.BlockSpec((1, d), lambda i, j: (0, 0)),
                pl.BlockSpec((d, tn), lambda i, j: (0, j))]
    args = [x, mod, norm_g.reshape(1, d), w_in_b]
    n_u = u_tiles[1] - u_tiles[0]
    out_specs = [pl.BlockSpec((tm, tn), lambda i, j: (i, j)),
                 pl.BlockSpec((tm, tn), lambda i, j: (i, jnp.clip(j - u_tiles[0], 0, n_u - 1)))]
    out_shape = [jax.ShapeDtypeStruct((n, cols), BF16), jax.ShapeDtypeStruct((n, col["d_ssm"]), F32)]
    if rope:
        cos, sin = rope_tables
        in_specs += [pl.BlockSpec((tm, HEAD_DIM), lambda i, j: (i % tiles_per_seq, 0))] * 2
        args += [cos, sin]
    else:
        out_specs.append(pl.BlockSpec((tm, tn), lambda i, j: (i, 0)))
        out_shape.append(jax.ShapeDtypeStruct((n, tn), F32))
    outs = pl.pallas_call(
        functools.partial(_proj_kernel, rope=rope, u_tiles=u_tiles, q_tiles=q_tiles, kv_tile=kv_tile,
                          d_kv=col["d_kv"]),
        grid=(n // tm, cols // tn),
        in_specs=in_specs,
        out_specs=out_specs,
        out_shape=out_shape,
        scratch_shapes=[pltpu.VMEM((tm, d), BF16)],
        compiler_params=_cparams(("arbitrary", "arbitrary")),
        name="proj_rope" if rope else "proj",
    )(*args)
    return outs if not rope else (outs[0], outs[1], None)


def _ssm_operators(lam_re, lam_im, log_dt, b_re, b_im, c_re, c_im, ssm_d):
    two, g, p = lam_re.shape
    ch = b_re.shape[-1]
    L = SSM_CHUNK
    dt = jnp.exp(log_dt.astype(F32))[..., None]
    a = lam_re * dt
    th = lam_im * dt
    tau = jnp.arange(L + 1, dtype=F32)
    mag = jnp.exp(a[..., None] * tau)
    ang = th[..., None] * tau
    e_re = mag * jnp.cos(ang)
    e_im = mag * jnp.sin(ang)
    n_re = e_re[..., 1] - 1.0
    n_im = e_im[..., 1]
    den = lam_re * lam_re + lam_im * lam_im
    q_re = (n_re * lam_re + n_im * lam_im) / den
    q_im = (n_im * lam_re - n_re * lam_im) / den
    bb_re = q_re[..., None] * b_re - q_im[..., None] * b_im
    bb_im = q_re[..., None] * b_im + q_im[..., None] * b_re
    ce_re = c_re[..., None] * e_re[:, :, None] - c_im[..., None] * e_im[:, :, None]
    ce_im = c_re[..., None] * e_im[:, :, None] + c_im[..., None] * e_re[:, :, None]
    k = (jnp.einsum("dgkpt,dgpc->dgtkc", ce_re[..., :L], bb_re)
         - jnp.einsum("dgkpt,dgpc->dgtkc", ce_im[..., :L], bb_im))
    s_idx = jnp.arange(L)[:, None]
    t_idx = jnp.arange(L)[None, :]
    lag_f = jnp.clip(t_idx - s_idx, 0, L - 1)
    lag_b = jnp.clip(s_idx - t_idx, 0, L - 1)
    kf = jnp.where((t_idx >= s_idx)[None, :, :, None, None], k[0][:, lag_f], 0.0)
    kb = jnp.where((s_idx >= t_idx)[None, :, :, None, None], k[1][:, lag_b], 0.0)
    skip = (jnp.eye(L)[None, :, :, None, None] * jnp.eye(ch)[None, None, None]
            * ssm_d.reshape(g, 1, 1, 1, ch))
    m_op = (kf + kb + skip).transpose(0, 1, 4, 2, 3).reshape(g, L * ch, L * ch)

    def state_in(d, e_r, e_i):
        re = e_r[:, :, :, None] * bb_re[d][:, :, None, :] - e_i[:, :, :, None] * bb_im[d][:, :, None, :]
        im = e_r[:, :, :, None] * bb_im[d][:, :, None, :] + e_i[:, :, :, None] * bb_re[d][:, :, None, :]
        return (re.transpose(0, 2, 3, 1).reshape(g, L * ch, p),
                im.transpose(0, 2, 3, 1).reshape(g, L * ch, p))
    wf_re, wf_im = state_in(0, e_re[0][..., L - 1::-1][..., :L], e_im[0][..., L - 1::-1][..., :L])
    wb_re, wb_im = state_in(1, e_re[1][..., :L], e_im[1][..., :L])
    w_op = jnp.concatenate([wf_re, wb_re, wf_im, wb_im], axis=-1)

    def state_out(d, taus):
        re = ce_re[d][..., taus].transpose(0, 2, 3, 1).reshape(g, p, L * ch)
        im = ce_im[d][..., taus].transpose(0, 2, 3, 1).reshape(g, p, L * ch)
        return re, -im
    vf_re, vf_im = state_out(0, jnp.arange(1, L + 1))
    vb_re, vb_im = state_out(1, jnp.arange(L, 0, -1))
    v_op = jnp.concatenate([vf_re, vb_re, vf_im, vb_im], axis=1)
    a_op = jnp.concatenate([e_re[0][..., L], e_re[1][..., L], e_im[0][..., L], e_im[1][..., L]],
                           axis=-1).reshape(g, 1, 4 * p)
    return m_op.astype(BF16), w_op.astype(BF16), v_op.astype(BF16), a_op


SSM_SLAB_GROUPS = 128 // SSM_CH


def _chunk_transpose(xs, chunk_id):
    for k in (4, 2, 1):
        keep = (chunk_id & k) == 0
        new = list(xs)
        for i in range(len(xs)):
            if i & k == 0:
                a, b = xs[i], xs[i + k]
                new[i] = jnp.where(keep, a, pltpu.roll(b, SSM_CH * k, 1))
                new[i + k] = jnp.where(keep, pltpu.roll(a, 128 - SSM_CH * k, 1), b)
        xs = new
    return xs


def _ssm_kernel(u_ref, m_ref, w_ref, v_ref, a_ref, h0_ref, y_ref, hfin_ref, ug_scr, s_scr, hs_scr, hin_scr, yg_scr,
                *, n_chunks, ns):
    gps = SSM_SLAB_GROUPS
    nc = ns * n_chunks
    half = 2 * SSM_STATE
    chunk_id = lax.broadcasted_iota(jnp.int32, (nc, 128), 1) // SSM_CH
    for hf in range(2):
        xs = [u_ref[pl.ds(hf * gps + i, nc, stride=SSM_CHUNK), :] for i in range(gps)]
        xs = _chunk_transpose(xs, chunk_id)
        for g in range(gps):
            ug_scr[g, :, hf * 128:(hf + 1) * 128] = xs[g].astype(BF16)
    for g in range(gps):
        s = jnp.dot(ug_scr[g], w_ref[g], preferred_element_type=F32)
        s_scr[g] = s[:, :half]
        s_scr[gps + g] = s[:, half:]
    is_fwd = lax.broadcasted_iota(jnp.int32, (1, half), 1) < SSM_STATE
    blocks = range(2 * gps)
    a = [a_ref[:, b * half:(b + 1) * half] for b in blocks]
    h = [h0_ref[:, b * half:(b + 1) * half] for b in blocks]

    def rows(c):
        return pl.ds(c, ns, stride=n_chunks)

    for k in range(n_chunks):
        kb = n_chunks - 1 - k
        new_h = list(h)
        for b in blocks:
            hs_scr[b, rows(k), :] = h[b]
        for g in range(gps):
            s_re = jnp.where(is_fwd, s_scr[g, rows(k), :], s_scr[g, rows(kb), :])
            s_im = jnp.where(is_fwd, s_scr[gps + g, rows(k), :], s_scr[gps + g, rows(kb), :])
            new_h[g] = a[g] * h[g] - a[gps + g] * h[gps + g] + s_re
            new_h[gps + g] = a[g] * h[gps + g] + a[gps + g] * h[g] + s_im
        h = new_h
    for b in blocks:
        hfin_ref[:, b * half:(b + 1) * half] = h[b]
    for c in range(n_chunks):
        for b in blocks:
            hin_scr[b, rows(c), :] = jnp.where(is_fwd, hs_scr[b, rows(c), :], hs_scr[b, rows(n_chunks - 1 - c), :])
    for g in range(gps):
        hin = jnp.concatenate([hin_scr[g], hin_scr[gps + g]], axis=1).astype(BF16)
        yg_scr[g] = (jnp.dot(ug_scr[g], m_ref[g], preferred_element_type=F32)
                     + jnp.dot(hin, v_ref[g], preferred_element_type=F32))
    for hf in range(2):
        xs = [yg_scr[g, :, hf * 128:(hf + 1) * 128] for g in range(gps)]
        xs = _chunk_transpose(xs, chunk_id)
        for i in range(gps):
            y_ref[pl.ds(hf * gps + i, nc, stride=SSM_CHUNK), :] = xs[i]


def _slab_lanes(x):
    lead = x.shape[:-2]
    g = x.shape[-2]
    gps = SSM_SLAB_GROUPS
    x = x.reshape(lead + (g // gps, gps, 2, 2 * SSM_STATE))
    x = jnp.swapaxes(x, -3, -2)
    return x.reshape(lead + (g * 4 * SSM_STATE,))


def _unslab_lanes(x, g):
    lead = x.shape[:-1]
    gps = SSM_SLAB_GROUPS
    x = x.reshape(lead + (g // gps, 2, gps, 2 * SSM_STATE))
    x = jnp.swapaxes(x, -3, -2)
    return x.reshape(lead + (g, 4 * SSM_STATE))


def _ssm(u, ops, h0, *, n_seq, seq_len, seq_block):
    m_op, w_op, v_op, a_op = ops
    n, d_ssm = u.shape
    g = m_op.shape[0]
    gps = SSM_SLAB_GROUPS
    lanes = SSM_LANES
    n_chunks = seq_len // SSM_CHUNK
    nc = seq_block * n_chunks
    rows = seq_block * seq_len
    wide2 = gps * 4 * SSM_STATE
    mat = pl.BlockSpec((gps, lanes, lanes), lambda j, b: (j, 0, 0))
    return pl.pallas_call(
        functools.partial(_ssm_kernel, n_chunks=n_chunks, ns=seq_block),
        grid=(g // gps, n_seq // seq_block),
        in_specs=[pl.BlockSpec((rows, 128), lambda j, b: (b, j)), mat, mat, mat,
                  pl.BlockSpec((1, wide2), lambda j, b: (0, j)),
                  pl.BlockSpec((seq_block, wide2), lambda j, b: (b, j))],
        out_specs=[pl.BlockSpec((rows, 128), lambda j, b: (b, j)),
                   pl.BlockSpec((seq_block, wide2), lambda j, b: (b, j))],
        out_shape=[jax.ShapeDtypeStruct((n, d_ssm), F32),
                   jax.ShapeDtypeStruct((n_seq, g * 4 * SSM_STATE), F32)],
        scratch_shapes=[pltpu.VMEM((gps, nc, lanes), BF16), pltpu.VMEM((2 * gps, nc, 128), F32),
                        pltpu.VMEM((2 * gps, nc, 128), F32), pltpu.VMEM((2 * gps, nc, 128), F32),
                        pltpu.VMEM((gps, nc, lanes), F32)],
        compiler_params=_cparams(("arbitrary", "arbitrary")),
        name="ssm",
    )(u, m_op, w_op, v_op, _slab_lanes(a_op[:, 0])[None], h0)


def _softmax_pv(pieces, sink):
    m = sink
    for s, _ in pieces:
        m = jnp.maximum(m, jnp.max(s, axis=-1, keepdims=True))
    den = jnp.exp(sink - m)
    out = None
    for s, v in pieces:
        p = jnp.exp(s - m)
        den = den + jnp.sum(p, axis=-1, keepdims=True)
        pv = jnp.dot(p.astype(BF16), v, preferred_element_type=F32)
        out = pv if out is None else out + pv
    return out / den


def _qk(q, k):
    return lax.dot_general(q, k, (((1,), (1,)), ((), ())), preferred_element_type=F32)


def _attn_ctx_kernel(sink_ref, q_ref, kv_ref, o_ref):
    d_kv = N_KV_HEADS * HEAD_DIM
    for h in range(N_HEADS):
        kh = h // Q_PER_KV
        q = q_ref[:, h * HEAD_DIM:(h + 1) * HEAD_DIM]
        k = kv_ref[:, kh * HEAD_DIM:(kh + 1) * HEAD_DIM]
        v = kv_ref[:, d_kv + kh * HEAD_DIM:d_kv + (kh + 1) * HEAD_DIM]
        o = _softmax_pv([(_qk(q, k), v)], sink_ref[h])
        o_ref[:, h * HEAD_DIM:(h + 1) * HEAD_DIM] = o.astype(BF16)


def _attn_ctx(proj, sink, *, n_seq, seq_len, col):
    d_attn = col["d_attn"]
    kv_w = 2 * col["d_kv"]
    return pl.pallas_call(
        _attn_ctx_kernel,
        grid=(n_seq,),
        in_specs=[pl.BlockSpec(memory_space=pltpu.SMEM),
                  pl.BlockSpec((seq_len, d_attn), lambda b: (b, col["q"] // d_attn)),
                  pl.BlockSpec((seq_len, kv_w), lambda b: (b, col["k"] // kv_w))],
        out_specs=pl.BlockSpec((seq_len, d_attn), lambda b: (b, 0)),
        out_shape=jax.ShapeDtypeStruct((n_seq * seq_len, d_attn), BF16),
        compiler_params=_cparams(("arbitrary",)),
        name="attn_ctx",
    )(sink, proj, proj)


def _attn_lat_kernel(sink_ref, q_ref, kvp_ref, kvc_ref, kvn_ref, ck_ref, cv_ref, o_ref, *, n_blocks):
    i = pl.program_id(1)
    d_kv = N_KV_HEADS * HEAD_DIM
    r = lax.broadcasted_iota(jnp.int32, (BLOCK, 3 * BLOCK), 0)
    c = lax.broadcasted_iota(jnp.int32, (BLOCK, 3 * BLOCK), 1)
    cc = c & (BLOCK - 1)
    valid = (((c < BLOCK) & (cc >= r) & (i > 0)) | ((c >= BLOCK) & (c < 2 * BLOCK))
             | ((c >= 2 * BLOCK) & (cc <= r) & (i < n_blocks - 1)))
    for h in range(N_HEADS):
        kh = h // Q_PER_KV
        ks = slice(kh * HEAD_DIM, (kh + 1) * HEAD_DIM)
        vs = slice(d_kv + kh * HEAD_DIM, d_kv + (kh + 1) * HEAD_DIM)
        q = q_ref[:, h * HEAD_DIM:(h + 1) * HEAD_DIM]
        k_loc = jnp.concatenate([kvp_ref[:, ks], kvc_ref[:, ks], kvn_ref[:, ks]], axis=0)
        v_loc = jnp.concatenate([kvp_ref[:, vs], kvc_ref[:, vs], kvn_ref[:, vs]], axis=0)
        s_loc = jnp.where(valid, _qk(q, k_loc), NEG_INF)
        s_ctx = _qk(q, ck_ref[:, ks].astype(BF16))
        o = _softmax_pv([(s_loc, v_loc), (s_ctx, cv_ref[:, ks].astype(BF16))], sink_ref[h])
        o_ref[:, h * HEAD_DIM:(h + 1) * HEAD_DIM] = o.astype(BF16)


def _attn_lat(proj, cache_k, cache_v, sink, *, n_seq, seq_len, col):
    d_attn = col["d_attn"]
    kv_w = 2 * col["d_kv"]
    nb = seq_len // BLOCK
    kv_col = col["k"] // kv_w
    past, d_kv = cache_k.shape[1], cache_k.shape[2]

    def kv_spec(off):
        return pl.BlockSpec((BLOCK, kv_w), lambda b, i: (b * nb + jnp.clip(i + off, 0, nb - 1), kv_col))
    cache_spec = pl.BlockSpec((None, past, d_kv), lambda b, i: (b, 0, 0))
    return pl.pallas_call(
        functools.partial(_attn_lat_kernel, n_blocks=nb),
        grid=(n_seq, nb),
        in_specs=[pl.BlockSpec(memory_space=pltpu.SMEM),
                  pl.BlockSpec((BLOCK, d_attn), lambda b, i: (b * nb + i, col["q"] // d_attn)),
                  kv_spec(-1), kv_spec(0), kv_spec(1), cache_spec, cache_spec],
        out_specs=pl.BlockSpec((BLOCK, d_attn), lambda b, i: (b * nb + i, 0)),
        out_shape=jax.ShapeDtypeStruct((n_seq * seq_len, d_attn), BF16),
        compiler_params=_cparams(("arbitrary", "arbitrary")),
        name="attn_lat",
    )(sink, proj, proj, proj, proj, cache_k, cache_v)


def _gelu_tanh(x):
    return 0.5 * x * (1.0 + jnp.tanh(math.sqrt(2.0 / math.pi) * (x + 0.044715 * (x * x * x))))


def _mixer_out_kernel(x_ref, y_ref, o_ref, gs_ref, ga_ref, mod_ref, g_ref, wglu_ref, wso_ref, wao_ref, wout_ref,
                      x1_ref, xm2_ref):
    z = _gelu_tanh(y_ref[...].astype(F32))
    z = z * _sigmoid(jnp.dot(z.astype(BF16), wglu_ref[...], preferred_element_type=F32))
    s_br = jnp.dot(z.astype(BF16), wso_ref[...], preferred_element_type=F32)
    a_br = jnp.dot(o_ref[...], wao_ref[...], preferred_element_type=F32)
    merged = _sigmoid(gs_ref[...].astype(F32)) * s_br + _sigmoid(ga_ref[...].astype(F32)) * a_br
    out = jnp.dot(merged.astype(BF16), wout_ref[...], preferred_element_type=F32)
    x1 = x_ref[...] + mod_ref[2:3, :] * out
    x1_ref[...] = x1
    xm2_ref[...] = _rms_modulate(x1, g_ref[...], mod_ref[4:5, :], mod_ref[3:4, :]).astype(BF16)


def _mixer_out(x, y, o, proj, mod, norm_g, w_glu, w_ssm_o, w_attn_o, w_out, *, tm, tiles_per_seq, col):
    n, d = x.shape
    d_ssm = y.shape[1]
    d_attn = o.shape[1]
    seq_of = (lambda i: i // tiles_per_seq) if mod.shape[0] > 1 else (lambda i: 0)

    def resident(shape):
        return pl.BlockSpec(shape, lambda i: (0, 0), pipeline_mode=pl.Buffered(1))
    return pl.pallas_call(
        _mixer_out_kernel,
        grid=(n // tm,),
        in_specs=[pl.BlockSpec((tm, d), lambda i: (i, 0)),
                  pl.BlockSpec((tm, d_ssm), lambda i: (i, 0)),
                  pl.BlockSpec((tm, d_attn), lambda i: (i, 0)),
                  pl.BlockSpec((tm, d), lambda i: (i, col["gs"] // d)),
                  pl.BlockSpec((tm, d), lambda i: (i, col["ga"] // d)),
                  pl.BlockSpec((None, 6, d), lambda i: (seq_of(i), 0, 0)),
                  pl.BlockSpec((1, d), lambda i: (0, 0)),
                  resident(w_glu.shape), resident(w_ssm_o.shape), resident(w_attn_o.shape), resident(w_out.shape)],
        out_specs=[pl.BlockSpec((tm, d), lambda i: (i, 0)), pl.BlockSpec((tm, d), lambda i: (i, 0))],
        out_shape=[jax.ShapeDtypeStruct((n, d), F32), jax.ShapeDtypeStruct((n, d), BF16)],
        compiler_params=_cparams(("arbitrary",)),
        name="mixer_out",
    )(x, y, o, proj, proj, mod, norm_g.reshape(1, d), w_glu, w_ssm_o, w_attn_o, w_out)


FFN_TM = 1024
FFN_TF = 256
HALO_ROWS = 16


def _ffn_kernel(xm_ref, halo_ref, wa_ref, wb_ref, cwa_ref, cwb_ref, cba_ref, cbb_ref, wd_ref, x1_ref, mod_ref,
                g_ref, o_ref, *, seq_len):
    i = pl.program_id(0)
    j = pl.program_id(1)
    tm = xm_ref.shape[0]
    xm = xm_ref[...]
    halo = halo_ref[...]
    row = lax.broadcasted_iota(jnp.int32, (tm, 1), 0)
    pos = (i * tm + row) & (seq_len - 1)

    def conv(w_ref, cw_ref, cb_ref):
        h = jnp.dot(xm, w_ref[...], preferred_element_type=F32)
        hh = jnp.dot(halo, w_ref[...], preferred_element_type=F32)
        h_prev = jnp.where(row == 0, hh[0:1, :], pltpu.roll(h, 1, 0))
        h_prev = jnp.where(pos == 0, 0.0, h_prev)
        h_next = jnp.where(row == tm - 1, hh[1:2, :], pltpu.roll(h, tm - 1, 0))
        h_next = jnp.where(pos == seq_len - 1, 0.0, h_next)
        return cw_ref[0:1, :] * h_prev + cw_ref[1:2, :] * h + cw_ref[2:3, :] * h_next + cb_ref[...]

    a = conv(wa_ref, cwa_ref, cba_ref)
    b = conv(wb_ref, cwb_ref, cbb_ref)
    act = ((a * _sigmoid(a)) * b).astype(BF16)
    part = jnp.dot(act, wd_ref[...], preferred_element_type=F32)

    @pl.when(j == 0)
    def _():
        o_ref[...] = part

    @pl.when(j > 0)
    def _():
        o_ref[...] += part

    @pl.when(j == pl.num_programs(1) - 1)
    def _():
        x2 = x1_ref[...] + mod_ref[5:6, :] * o_ref[...]
        ms = jnp.mean(x2 * x2, axis=-1, keepdims=True)
        o_ref[...] = x2 * lax.rsqrt(ms + EPS) * g_ref[...]


def _ffn(x1, xm2, mod, w_up, conv_w, conv_b, w_down, final_g, *, tm, seq_len):
    n, d = x1.shape
    d_ff = w_down.shape[0]
    tf = FFN_TF
    nf = d_ff // tf
    n_tiles = n // tm
    tiles_per_seq = max(seq_len // tm, 1)
    seq_of = (lambda i: i // tiles_per_seq) if mod.shape[0] > 1 else (lambda i: 0)
    xt = xm2.reshape(n_tiles, tm, d)
    zero = jnp.zeros((1, d), xm2.dtype)
    prev = jnp.concatenate([zero, xt[:-1, tm - 1]], axis=0)
    nxt = jnp.concatenate([xt[1:, 0], zero], axis=0)
    halo = jnp.concatenate([prev[:, None], nxt[:, None], jnp.zeros((n_tiles, HALO_ROWS - 2, d), xm2.dtype)], axis=1)
    cb = conv_b.reshape(1, 2 * d_ff)
    return pl.pallas_call(
        functools.partial(_ffn_kernel, seq_len=seq_len),
        grid=(n_tiles, nf),
        in_specs=[pl.BlockSpec((tm, d), lambda i, j: (i, 0)),
                  pl.BlockSpec((None, HALO_ROWS, d), lambda i, j: (i, 0, 0)),
                  pl.BlockSpec((d, tf), lambda i, j: (0, j)),
                  pl.BlockSpec((d, tf), lambda i, j: (0, nf + j)),
                  pl.BlockSpec((3, tf), lambda i, j: (0, j)),
                  pl.BlockSpec((3, tf), lambda i, j: (0, nf + j)),
                  pl.BlockSpec((1, tf), lambda i, j: (0, j)),
                  pl.BlockSpec((1, tf), lambda i, j: (0, nf + j)),
                  pl.BlockSpec((tf, d), lambda i, j: (j, 0)),
                  pl.BlockSpec((tm, d), lambda i, j: (i, 0), pipeline_mode=pl.Buffered(1)),
                  pl.BlockSpec((None, 6, d), lambda i, j: (seq_of(i), 0, 0)),
                  pl.BlockSpec((1, d), lambda i, j: (0, 0))],
        out_specs=pl.BlockSpec((tm, d), lambda i, j: (i, 0)),
        out_shape=jax.ShapeDtypeStruct((n, d), F32),
        compiler_params=_cparams(("arbitrary", "arbitrary")),
        name="ffn",
    )(xm2, halo, w_up, w_up, conv_w, conv_w, cb, cb, w_down, x1, mod, final_g.reshape(1, d))


def _rope_tables(seq_len):
    rows = seq_len // GRID_W
    row = jnp.repeat(jnp.arange(rows, dtype=F32), GRID_W)
    colp = jnp.tile(jnp.arange(GRID_W, dtype=F32), rows)
    n_freq = HEAD_DIM // 4
    inv = ROPE_THETA ** (-jnp.arange(n_freq, dtype=F32) / n_freq)
    ang_r = row[:, None] * inv[None, :]
    ang_c = colp[:, None] * inv[None, :]
    cos = jnp.concatenate([jnp.cos(ang_r), jnp.cos(ang_r), jnp.cos(ang_c), jnp.cos(ang_c)], axis=1)
    sin = jnp.concatenate([-jnp.sin(ang_r), jnp.sin(ang_r), -jnp.sin(ang_c), jnp.sin(ang_c)], axis=1)
    return cos, sin


def _layer(x, mod, w, ssm_ops, h0, *, n_seq, seq_len, latent, cache=None):
    col = w["col"]
    tm_proj = 1024
    proj, u, kv_raw = _proj(x, mod, w["norm_mix_g"], w["w_in"], tm=tm_proj, tiles_per_seq=max(seq_len // tm_proj, 1),
                            rope_tables=_rope_tables(seq_len) if latent else None, col=col)
    y_ssm, h_fin = _ssm(u, ssm_ops, h0, n_seq=n_seq, seq_len=seq_len, seq_block=min(n_seq, SSM_SEQ_BLOCK))
    if latent:
        o = _attn_lat(proj, cache[0], cache[1], w["sink"], n_seq=n_seq, seq_len=seq_len, col=col)
    else:
        o = _attn_ctx(proj, w["sink"], n_seq=n_seq, seq_len=seq_len, col=col)
    tm = 256
    x1, xm2 = _mixer_out(x, y_ssm, o, proj, mod, w["norm_ffn_g"], w["w_glu"], w["w_ssm_o"], w["w_attn_o"],
                         w["w_out"], tm=tm, tiles_per_seq=max(seq_len // tm, 1), col=col)
    y = _ffn(x1, xm2, mod, w["w_up"], w["conv_w"], w["conv_b"], w["w_down"], w["final_norm_g"], tm=FFN_TM,
             seq_len=seq_len)
    return y, kv_raw, h_fin


def kernel(x_prompt, x_sample, cache_k, cache_v, state_ssm_re, state_ssm_im, c, c_ctx, norm_mix_g, norm_ffn_g,
           w_mod, b_mod, w_in, ssm_lambda_re, ssm_lambda_im, ssm_log_dt, ssm_b_re, ssm_b_im, ssm_c_re, ssm_c_im,
           ssm_d, w_glu, attn_sink, w_ssm_o, w_attn_o, w_out, w_up, conv_w, conv_b, w_down, final_norm_g):
    batch, seq, d = x_prompt.shape
    dec_batch, dec_seq, _ = x_sample.shape
    depth = w_in.shape[0]
    assert depth == 1, "final norm is fused into the (single) layer's ffn kernel"
    d_ssm = w_glu.shape[1]
    d_attn = N_HEADS * HEAD_DIM
    d_kv = N_KV_HEADS * HEAD_DIM
    groups = d_ssm // SSM_CH
    assert w_in.shape[2] == d_ssm + d_attn + 2 * d_kv + 2 * d
    assert ssm_lambda_re.shape[2:] == (groups, SSM_STATE) and dec_batch <= 8 - 1
    l = 0

    o1, o2, o3, o4 = d_ssm, d_ssm + d_attn, d_ssm + d_attn + d_kv, d_ssm + d_attn + 2 * d_kv
    w_in_l = w_in[l]
    w_in_b = jnp.concatenate([w_in_l[:, o4:], w_in_l[:, :o4]], axis=1).astype(BF16)
    col = {"gs": 0, "ga": d, "u": 2 * d, "q": 2 * d + d_ssm, "k": 2 * d + d_ssm + d_attn,
           "d_ssm": d_ssm, "d_attn": d_attn, "d_kv": d_kv}
    w = {"col": col, "w_in": w_in_b, "norm_mix_g": norm_mix_g[l], "norm_ffn_g": norm_ffn_g[l],
         "w_glu": w_glu[l].astype(BF16), "w_ssm_o": w_ssm_o[l].astype(BF16), "w_attn_o": w_attn_o[l].astype(BF16),
         "w_out": w_out[l].astype(BF16), "w_up": w_up[l].astype(BF16), "conv_w": conv_w[l], "conv_b": conv_b[l],
         "w_down": w_down[l].astype(BF16), "sink": attn_sink[l], "final_norm_g": final_norm_g}

    cond = jnp.concatenate([c_ctx[None], c, jnp.zeros((8 - 1 - dec_batch, d), F32)], axis=0)
    mod = _modulation(cond, w_mod[l], b_mod[l]).reshape(8, 6, d)
    mod_ctx, mod_lat = mod[0:1], mod[1:1 + dec_batch]

    ssm_ops = _ssm_operators(ssm_lambda_re[l], ssm_lambda_im[l], ssm_log_dt[l], ssm_b_re[l], ssm_b_im[l],
                             ssm_c_re[l], ssm_c_im[l], ssm_d[l])

    h0_ctx = jnp.zeros((batch, groups * 4 * SSM_STATE), F32)
    y_p, kv_raw, h_fin = _layer(x_prompt.reshape(batch * seq, d), mod_ctx, w, ssm_ops, h0_ctx,
                                n_seq=batch, seq_len=seq, latent=False)

    def lanes(s):
        return s.transpose(0, 2, 1, 3).reshape(dec_batch, groups, 2 * SSM_STATE)
    h0_lat = _slab_lanes(jnp.concatenate([lanes(state_ssm_re[:, l]), lanes(state_ssm_im[:, l])], axis=-1))
    cache = (cache_k[:, l].reshape(dec_batch, -1, d_kv), cache_v[:, l].reshape(dec_batch, -1, d_kv))
    y_s, _, _ = _layer(x_sample.reshape(dec_batch * dec_seq, d), mod_lat, w, ssm_ops, h0_lat,
                       n_seq=dec_batch, seq_len=dec_seq, latent=True, cache=cache)

    new_k = kv_raw[:, :d_kv].reshape(batch, 1, seq, N_KV_HEADS, HEAD_DIM)
    new_v = kv_raw[:, d_kv:].reshape(batch, 1, seq, N_KV_HEADS, HEAD_DIM)

    def unlanes(hl):
        return hl.reshape(batch, groups, 2, SSM_STATE).transpose(0, 2, 1, 3)[:, None]
    h_fin = _unslab_lanes(h_fin, groups)
    new_re = unlanes(h_fin[:, :, :2 * SSM_STATE])
    new_im = unlanes(h_fin[:, :, 2 * SSM_STATE:])
    return (y_p.reshape(batch, seq, d), y_s.reshape(dec_batch, dec_seq, d), new_k, new_v, new_re, new_im)
```

```python
import functools
import math

import jax
import jax.numpy as jnp
from jax import lax
from jax.experimental import pallas as pl
from jax.experimental.pallas import tpu as pltpu

F32 = jnp.float32
BF16 = jnp.bfloat16

GRID_W = 64
SSM_CH = 16
SSM_STATE = 64
N_HEADS = 8
N_KV_HEADS = 2
HEAD_DIM = 128
Q_PER_KV = N_HEADS // N_KV_HEADS
WINDOW = 128
BLOCK = 128
ROPE_THETA = 10000.0
EPS = 1e-6
NEG_INF = -1e30

SSM_CHUNK = 16
SSM_LANES = SSM_CHUNK * SSM_CH
SSM_SEQ_BLOCK = 16

V7X_VMEM_LIMIT_BYTES = 56 * 1024 * 1024


def _cparams(semantics):
    return pltpu.CompilerParams(dimension_semantics=semantics, vmem_limit_bytes=V7X_VMEM_LIMIT_BYTES)


def _sigmoid(x):
    return 1.0 / (1.0 + jnp.exp(-x))


def _rms_modulate(x, g, scale, shift):
    ms = jnp.mean(x * x, axis=-1, keepdims=True)
    return (x * lax.rsqrt(ms + EPS) * g) * (1.0 + scale) + shift


def _mod_kernel(c_ref, w_ref, b_ref, o_ref):
    c = c_ref[...]
    a = (c * _sigmoid(c)).astype(BF16)
    o_ref[...] = jnp.dot(a, w_ref[...].astype(BF16), preferred_element_type=F32) + b_ref[...]


def _modulation(cond, w_mod, b_mod):
    rows, d = cond.shape
    n = w_mod.shape[1]
    tn = 1024
    return pl.pallas_call(
        _mod_kernel,
        grid=(n // tn,),
        in_specs=[pl.BlockSpec((rows, d), lambda j: (0, 0)),
                  pl.BlockSpec((d, tn), lambda j: (0, j)),
                  pl.BlockSpec((1, tn), lambda j: (0, j))],
        out_specs=pl.BlockSpec((rows, tn), lambda j: (0, j)),
        out_shape=jax.ShapeDtypeStruct((rows, n), F32),
        compiler_params=_cparams(("arbitrary",)),
        name="mod",
    )(cond, w_mod, b_mod.reshape(1, n))


PROJ_TN = 512


def _swap32(x):
    n = x.shape[-1]
    lane = lax.broadcasted_iota(jnp.int32, x.shape, x.ndim - 1)
    return jnp.where((lane & 63) < 32, pltpu.roll(x, n - 32, x.ndim - 1), pltpu.roll(x, 32, x.ndim - 1))


def _proj_kernel(x_ref, mod_ref, g_ref, w_ref, o_ref, tail_ref, xm_scr, *, q_tiles):
    j = pl.program_id(1)

    @pl.when(j == 0)
    def _():
        xm = _rms_modulate(x_ref[...], g_ref[...], mod_ref[1:2, :], mod_ref[0:1, :])
        xm_scr[...] = xm.astype(BF16)

    acc = jnp.dot(xm_scr[...], w_ref[...].astype(BF16), preferred_element_type=F32)
    scale = jnp.where((j >= q_tiles[0]) & (j < q_tiles[1]), HEAD_DIM ** -0.5, 1.0)
    o_ref[...] = (acc * scale).astype(BF16)
    tail_ref[...] = acc


def _proj(x, mod, norm_g, w_in, *, tm, tiles_per_seq, col):
    n, d = x.shape
    cols = w_in.shape[1]
    tn = PROJ_TN
    n_tiles = cols // tn
    q_tiles = (col["q"] // tn, (col["q"] + col["d_attn"]) // tn)
    tail0 = col["k"] // tn
    n_tail = n_tiles - tail0
    assert 2 * col["d_kv"] == tn and col["u"] == col["k"] + tn
    n_gate = col["q"] // tn
    n_u = col["d_ssm"] // tn

    def src_tile(j):
        return jnp.where(j < n_gate, j + (n_tiles - n_gate), jnp.where(j < n_tiles - n_u, j - n_gate + n_u, j - (n_tiles - n_u)))
    seq_of = (lambda i: i // tiles_per_seq) if mod.shape[0] > 1 else (lambda i: 0)
    return pl.pallas_call(
        functools.partial(_proj_kernel, q_tiles=q_tiles),
        grid=(n // tm, n_tiles),
        in_specs=[pl.BlockSpec((tm, d), lambda i, j: (i, 0)),
                  pl.BlockSpec((None, 6, d), lambda i, j: (seq_of(i), 0, 0)),
                  pl.BlockSpec((1, d), lambda i, j: (0, 0)),
                  pl.BlockSpec((d, tn), lambda i, j: (0, src_tile(j)))],
        out_specs=[pl.BlockSpec((tm, tn), lambda i, j: (i, j)),
                   pl.BlockSpec((tm, tn), lambda i, j: (i, jnp.clip(j - tail0, 0, n_tail - 1)))],
        out_shape=[jax.ShapeDtypeStruct((n, cols), BF16), jax.ShapeDtypeStruct((n, n_tail * tn), F32)],
        scratch_shapes=[pltpu.VMEM((tm, d), BF16)],
        compiler_params=_cparams(("arbitrary", "arbitrary")),
        name="proj",
    )(x, mod, norm_g.reshape(1, d), w_in)


def _ssm_operators(lam_re, lam_im, log_dt, b_re, b_im, c_re, c_im, ssm_d):
    two, g, p = lam_re.shape
    ch = b_re.shape[-1]
    L = SSM_CHUNK
    dt = jnp.exp(log_dt.astype(F32))[..., None]
    a = lam_re * dt
    th = lam_im * dt
    tau = jnp.arange(L + 1, dtype=F32)
    mag = jnp.exp(a[..., None] * tau)
    ang = th[..., None] * tau
    e_re = mag * jnp.cos(ang)
    e_im = mag * jnp.sin(ang)
    n_re = e_re[..., 1] - 1.0
    n_im = e_im[..., 1]
    den = lam_re * lam_re + lam_im * lam_im
    q_re = (n_re * lam_re + n_im * lam_im) / den
    q_im = (n_im * lam_re - n_re * lam_im) / den
    bb_re = q_re[..., None] * b_re - q_im[..., None] * b_im
    bb_im = q_re[..., None] * b_im + q_im[..., None] * b_re
    ce_re = c_re[..., None] * e_re[:, :, None] - c_im[..., None] * e_im[:, :, None]
    ce_im = c_re[..., None] * e_im[:, :, None] + c_im[..., None] * e_re[:, :, None]
    k = (jnp.einsum("dgkpt,dgpc->dgtkc", ce_re[..., :L], bb_re)
         - jnp.einsum("dgkpt,dgpc->dgtkc", ce_im[..., :L], bb_im))
    s_idx = jnp.arange(L)[:, None]
    t_idx = jnp.arange(L)[None, :]
    lag_f = jnp.clip(t_idx - s_idx, 0, L - 1)
    lag_b = jnp.clip(s_idx - t_idx, 0, L - 1)
    kf = jnp.where((t_idx >= s_idx)[None, :, :, None, None], k[0][:, lag_f], 0.0)
    kb = jnp.where((s_idx >= t_idx)[None, :, :, None, None], k[1][:, lag_b], 0.0)
    skip = (jnp.eye(L)[None, :, :, None, None] * jnp.eye(ch)[None, None, None]
            * ssm_d.reshape(g, 1, 1, 1, ch))
    m_op = (kf + kb + skip).transpose(0, 1, 4, 2, 3).reshape(g, L * ch, L * ch)

    def state_in(d, e_r, e_i):
        re = e_r[:, :, :, None] * bb_re[d][:, :, None, :] - e_i[:, :, :, None] * bb_im[d][:, :, None, :]
        im = e_r[:, :, :, None] * bb_im[d][:, :, None, :] + e_i[:, :, :, None] * bb_re[d][:, :, None, :]
        return (re.transpose(0, 2, 3, 1).reshape(g, L * ch, p),
                im.transpose(0, 2, 3, 1).reshape(g, L * ch, p))
    wf_re, wf_im = state_in(0, e_re[0][..., L - 1::-1][..., :L], e_im[0][..., L - 1::-1][..., :L])
    wb_re, wb_im = state_in(1, e_re[1][..., :L], e_im[1][..., :L])
    w_op = jnp.concatenate([wf_re, wb_re, wf_im, wb_im], axis=-1)

    def state_out(d, taus):
        re = ce_re[d][..., taus].transpose(0, 2, 3, 1).reshape(g, p, L * ch)
        im = ce_im[d][..., taus].transpose(0, 2, 3, 1).reshape(g, p, L * ch)
        return re, -im
    vf_re, vf_im = state_out(0, jnp.arange(1, L + 1))
    vb_re, vb_im = state_out(1, jnp.arange(L, 0, -1))
    v_op = jnp.concatenate([vf_re, vb_re, vf_im, vb_im], axis=1)
    a_op = jnp.concatenate([e_re[0][..., L], e_re[1][..., L], e_im[0][..., L], e_im[1][..., L]],
                           axis=-1).reshape(g, 1, 4 * p)
    return m_op.astype(BF16), w_op.astype(BF16), v_op.astype(BF16), a_op


SSM_SLAB_GROUPS = 128 // SSM_CH


def _chunk_transpose(xs, chunk_id):
    for k in (4, 2, 1):
        keep = (chunk_id & k) == 0
        new = list(xs)
        for i in range(len(xs)):
            if i & k == 0:
                a, b = xs[i], xs[i + k]
                new[i] = jnp.where(keep, a, pltpu.roll(b, SSM_CH * k, 1))
                new[i + k] = jnp.where(keep, pltpu.roll(a, 128 - SSM_CH * k, 1), b)
        xs = new
    return xs


def _ssm_kernel(u_ref, m_ref, w_ref, v_ref, a_ref, h0_ref, y_ref, hfin_ref, ug_scr, s_scr, hs_scr, hin_scr, yg_scr,
                *, n_chunks, ns):
    gps = SSM_SLAB_GROUPS
    nc = ns * n_chunks
    half = 2 * SSM_STATE
    chunk_id = lax.broadcasted_iota(jnp.int32, (nc, 128), 1) // SSM_CH
    for hf in range(2):
        xs = [u_ref[pl.ds(hf * gps + i, nc, stride=SSM_CHUNK), :] for i in range(gps)]
        xs = _chunk_transpose(xs, chunk_id)
        for g in range(gps):
            ug_scr[g, :, hf * 128:(hf + 1) * 128] = xs[g].astype(BF16)
    for g in range(gps):
        s = jnp.dot(ug_scr[g], w_ref[g], preferred_element_type=F32)
        s_scr[g] = s[:, :half]
        s_scr[gps + g] = s[:, half:]
    is_fwd = lax.broadcasted_iota(jnp.int32, (1, half), 1) < SSM_STATE
    blocks = range(2 * gps)
    a = [a_ref[:, b * half:(b + 1) * half] for b in blocks]
    h = [h0_ref[:, b * half:(b + 1) * half] for b in blocks]

    def rows(c):
        return pl.ds(c, ns, stride=n_chunks)

    for k in range(n_chunks):
        kb = n_chunks - 1 - k
        new_h = list(h)
        for b in blocks:
            hs_scr[b, rows(k), :] = h[b]
        for g in range(gps):
            s_re = jnp.where(is_fwd, s_scr[g, rows(k), :], s_scr[g, rows(kb), :])
            s_im = jnp.where(is_fwd, s_scr[gps + g, rows(k), :], s_scr[gps + g, rows(kb), :])
            new_h[g] = a[g] * h[g] - a[gps + g] * h[gps + g] + s_re
            new_h[gps + g] = a[g] * h[gps + g] + a[gps + g] * h[g] + s_im
        h = new_h
    for b in blocks:
        hfin_ref[:, b * half:(b + 1) * half] = h[b]
    for c in range(n_chunks):
        for b in blocks:
            hin_scr[b, rows(c), :] = jnp.where(is_fwd, hs_scr[b, rows(c), :], hs_scr[b, rows(n_chunks - 1 - c), :])
    for g in range(gps):
        hin = jnp.concatenate([hin_scr[g], hin_scr[gps + g]], axis=1).astype(BF16)
        yg_scr[g] = (jnp.dot(ug_scr[g], m_ref[g], preferred_element_type=F32)
                     + jnp.dot(hin, v_ref[g], preferred_element_type=F32))
    for hf in range(2):
        xs = [yg_scr[g, :, hf * 128:(hf + 1) * 128] for g in range(gps)]
        xs = _chunk_transpose(xs, chunk_id)
        for i in range(gps):
            y_ref[pl.ds(hf * gps + i, nc, stride=SSM_CHUNK), :] = xs[i]


def _slab_lanes(x):
    lead = x.shape[:-2]
    g = x.shape[-2]
    gps = SSM_SLAB_GROUPS
    x = x.reshape(lead + (g // gps, gps, 2, 2 * SSM_STATE))
    x = jnp.swapaxes(x, -3, -2)
    return x.reshape(lead + (g * 4 * SSM_STATE,))


def _unslab_lanes(x, g):
    lead = x.shape[:-1]
    gps = SSM_SLAB_GROUPS
    x = x.reshape(lead + (g // gps, 2, gps, 2 * SSM_STATE))
    x = jnp.swapaxes(x, -3, -2)
    return x.reshape(lead + (g, 4 * SSM_STATE))


def _ssm(u, u_col, ops, h0, *, n_seq, seq_len, seq_block):
    m_op, w_op, v_op, a_op = ops
    n = u.shape[0]
    g = m_op.shape[0]
    d_ssm = g * SSM_CH
    slab0 = u_col // 128
    gps = SSM_SLAB_GROUPS
    lanes = SSM_LANES
    n_chunks = seq_len // SSM_CHUNK
    nc = seq_block * n_chunks
    rows = seq_block * seq_len
    wide2 = gps * 4 * SSM_STATE
    mat = pl.BlockSpec((gps, lanes, lanes), lambda j, b: (j, 0, 0))
    return pl.pallas_call(
        functools.partial(_ssm_kernel, n_chunks=n_chunks, ns=seq_block),
        grid=(g // gps, n_seq // seq_block),
        in_specs=[pl.BlockSpec((rows, 128), lambda j, b: (b, slab0 + j)), mat, mat, mat,
                  pl.BlockSpec((1, wide2), lambda j, b: (0, j)),
                  pl.BlockSpec((seq_block, wide2), lambda j, b: (b, j))],
        out_specs=[pl.BlockSpec((rows, 128), lambda j, b: (b, j)),
                   pl.BlockSpec((seq_block, wide2), lambda j, b: (b, j))],
        out_shape=[jax.ShapeDtypeStruct((n, d_ssm), F32),
                   jax.ShapeDtypeStruct((n_seq, g * 4 * SSM_STATE), F32)],
        scratch_shapes=[pltpu.VMEM((gps, nc, lanes), BF16), pltpu.VMEM((2 * gps, nc, 128), F32),
                        pltpu.VMEM((2 * gps, nc, 128), F32), pltpu.VMEM((2 * gps, nc, 128), F32),
                        pltpu.VMEM((gps, nc, lanes), F32)],
        compiler_params=_cparams(("arbitrary", "arbitrary")),
        name="ssm",
    )(u, m_op, w_op, v_op, _slab_lanes(a_op[:, 0])[None], h0)


def _softmax_pv(pieces, sink):
    m = sink
    for s, _ in pieces:
        m = jnp.maximum(m, jnp.max(s, axis=-1, keepdims=True))
    den = jnp.exp(sink - m)
    out = None
    for s, v in pieces:
        p = jnp.exp(s - m)
        den = den + jnp.sum(p, axis=-1, keepdims=True)
        pv = jnp.dot(p.astype(BF16), v, preferred_element_type=F32)
        out = pv if out is None else out + pv
    return out / den


def _qk(q, k):
    return lax.dot_general(q, k, (((1,), (1,)), ((), ())), preferred_element_type=F32)


def _attn_ctx_kernel(sink_ref, q_ref, kv_ref, o_ref):
    d_kv = N_KV_HEADS * HEAD_DIM
    for h in range(N_HEADS):
        kh = h // Q_PER_KV
        q = q_ref[:, h * HEAD_DIM:(h + 1) * HEAD_DIM]
        k = kv_ref[:, kh * HEAD_DIM:(kh + 1) * HEAD_DIM]
        v = kv_ref[:, d_kv + kh * HEAD_DIM:d_kv + (kh + 1) * HEAD_DIM]
        o = _softmax_pv([(_qk(q, k), v)], sink_ref[h])
        o_ref[:, h * HEAD_DIM:(h + 1) * HEAD_DIM] = o.astype(BF16)


def _attn_ctx(proj, sink, *, n_seq, seq_len, col):
    d_attn = col["d_attn"]
    kv_w = 2 * col["d_kv"]
    return pl.pallas_call(
        _attn_ctx_kernel,
        grid=(n_seq,),
        in_specs=[pl.BlockSpec(memory_space=pltpu.SMEM),
                  pl.BlockSpec((seq_len, d_attn), lambda b: (b, col["q"] // d_attn)),
                  pl.BlockSpec((seq_len, kv_w), lambda b: (b, col["k"] // kv_w))],
        out_specs=pl.BlockSpec((seq_len, d_attn), lambda b: (b, 0)),
        out_shape=jax.ShapeDtypeStruct((n_seq * seq_len, d_attn), BF16),
        compiler_params=_cparams(("arbitrary",)),
        name="attn_ctx",
    )(sink, proj, proj)


def _rope(x, cos, sin):
    reps = x.shape[1] // HEAD_DIM
    xf = x.astype(F32)
    return (xf * jnp.concatenate([cos] * reps, axis=1) + _swap32(xf) * jnp.concatenate([sin] * reps, axis=1)).astype(BF16)


def _attn_lat_kernel(sink_ref, q_ref, kvp_ref, kvc_ref, kvn_ref, ck_ref, cv_ref, cos_ref, sin_ref, o_ref, *, n_blocks):
    i = pl.program_id(1)
    d_kv = N_KV_HEADS * HEAD_DIM

    def tables(blk):
        rows = pl.ds(pl.multiple_of(blk * BLOCK, BLOCK), BLOCK)
        return cos_ref[rows, :], sin_ref[rows, :]

    q_all = _rope(q_ref[...], *tables(i))
    k_all = jnp.concatenate([_rope(kvp_ref[:, :d_kv], *tables(jnp.maximum(i - 1, 0))),
                             _rope(kvc_ref[:, :d_kv], *tables(i)),
                             _rope(kvn_ref[:, :d_kv], *tables(jnp.minimum(i + 1, n_blocks - 1)))], axis=0)
    r = lax.broadcasted_iota(jnp.int32, (BLOCK, 3 * BLOCK), 0)
    c = lax.broadcasted_iota(jnp.int32, (BLOCK, 3 * BLOCK), 1)
    cc = c & (BLOCK - 1)
    valid = (((c < BLOCK) & (cc >= r) & (i > 0)) | ((c >= BLOCK) & (c < 2 * BLOCK))
             | ((c >= 2 * BLOCK) & (cc <= r) & (i < n_blocks - 1)))
    for h in range(N_HEADS):
        kh = h // Q_PER_KV
        ks = slice(kh * HEAD_DIM, (kh + 1) * HEAD_DIM)
        vs = slice(d_kv + kh * HEAD_DIM, d_kv + (kh + 1) * HEAD_DIM)
        q = q_all[:, h * HEAD_DIM:(h + 1) * HEAD_DIM]
        k_loc = k_all[:, ks]
        v_loc = jnp.concatenate([kvp_ref[:, vs], kvc_ref[:, vs], kvn_ref[:, vs]], axis=0)
        s_loc = jnp.where(valid, _qk(q, k_loc), NEG_INF)
        s_ctx = _qk(q, ck_ref[:, ks].astype(BF16))
        o = _softmax_pv([(s_loc, v_loc), (s_ctx, cv_ref[:, ks].astype(BF16))], sink_ref[h])
        o_ref[:, h * HEAD_DIM:(h + 1) * HEAD_DIM] = o.astype(BF16)


def _attn_lat(proj, cache_k, cache_v, sink, *, n_seq, seq_len, col):
    d_attn = col["d_attn"]
    kv_w = 2 * col["d_kv"]
    nb = seq_len // BLOCK
    kv_col = col["k"] // kv_w
    past, d_kv = cache_k.shape[1], cache_k.shape[2]

    def kv_spec(off):
        return pl.BlockSpec((BLOCK, kv_w), lambda b, i: (b * nb + jnp.clip(i + off, 0, nb - 1), kv_col))
    cache_spec = pl.BlockSpec((None, past, d_kv), lambda b, i: (b, 0, 0))
    table_spec = pl.BlockSpec((seq_len, HEAD_DIM), lambda b, i: (0, 0))
    cos, sin = _rope_tables(seq_len)
    return pl.pallas_call(
        functools.partial(_attn_lat_kernel, n_blocks=nb),
        grid=(n_seq, nb),
        in_specs=[pl.BlockSpec(memory_space=pltpu.SMEM),
                  pl.BlockSpec((BLOCK, d_attn), lambda b, i: (b * nb + i, col["q"] // d_attn)),
                  kv_spec(-1), kv_spec(0), kv_spec(1), cache_spec, cache_spec, table_spec, table_spec],
        out_specs=pl.BlockSpec((BLOCK, d_attn), lambda b, i: (b * nb + i, 0)),
        out_shape=jax.ShapeDtypeStruct((n_seq * seq_len, d_attn), BF16),
        compiler_params=_cparams(("arbitrary", "arbitrary")),
        name="attn_lat",
    )(sink, proj, proj, proj, proj, cache_k, cache_v, cos, sin)


def _gelu_tanh(x):
    return 0.5 * x * (1.0 + jnp.tanh(math.sqrt(2.0 / math.pi) * (x + 0.044715 * (x * x * x))))


def _mixer_out_kernel(x_ref, y_ref, o_ref, gs_ref, ga_ref, mod_ref, g_ref, wglu_ref, wso_ref, wao_ref, wout_ref,
                      x1_ref, xm2_ref):
    z = _gelu_tanh(y_ref[...].astype(F32))
    z = z * _sigmoid(jnp.dot(z.astype(BF16), wglu_ref[...], preferred_element_type=F32))
    s_br = jnp.dot(z.astype(BF16), wso_ref[...], preferred_element_type=F32)
    a_br = jnp.dot(o_ref[...], wao_ref[...], preferred_element_type=F32)
    merged = _sigmoid(gs_ref[...].astype(F32)) * s_br + _sigmoid(ga_ref[...].astype(F32)) * a_br
    out = jnp.dot(merged.astype(BF16), wout_ref[...], preferred_element_type=F32)
    x1 = x_ref[...] + mod_ref[2:3, :] * out
    x1_ref[...] = x1
    xm2_ref[...] = _rms_modulate(x1, g_ref[...], mod_ref[4:5, :], mod_ref[3:4, :]).astype(BF16)


def _mixer_out(x, y, o, proj, mod, norm_g, w_glu, w_ssm_o, w_attn_o, w_out, *, tm, tiles_per_seq, col):
    n, d = x.shape
    d_ssm = y.shape[1]
    d_attn = o.shape[1]
    seq_of = (lambda i: i // tiles_per_seq) if mod.shape[0] > 1 else (lambda i: 0)

    def resident(shape):
        return pl.BlockSpec(shape, lambda i: (0, 0), pipeline_mode=pl.Buffered(1))
    return pl.pallas_call(
        _mixer_out_kernel,
        grid=(n // tm,),
        in_specs=[pl.BlockSpec((tm, d), lambda i: (i, 0)),
                  pl.BlockSpec((tm, d_ssm), lambda i: (i, 0)),
                  pl.BlockSpec((tm, d_attn), lambda i: (i, 0)),
                  pl.BlockSpec((tm, d), lambda i: (i, col["gs"] // d)),
                  pl.BlockSpec((tm, d), lambda i: (i, col["ga"] // d)),
                  pl.BlockSpec((None, 6, d), lambda i: (seq_of(i), 0, 0)),
                  pl.BlockSpec((1, d), lambda i: (0, 0)),
                  resident(w_glu.shape), resident(w_ssm_o.shape), resident(w_attn_o.shape), resident(w_out.shape)],
        out_specs=[pl.BlockSpec((tm, d), lambda i: (i, 0)), pl.BlockSpec((tm, d), lambda i: (i, 0))],
        out_shape=[jax.ShapeDtypeStruct((n, d), F32), jax.ShapeDtypeStruct((n, d), BF16)],
        compiler_params=_cparams(("arbitrary",)),
        name="mixer_out",
    )(x, y, o, proj, proj, mod, norm_g.reshape(1, d), w_glu, w_ssm_o, w_attn_o, w_out)


FFN_TM = 1024
FFN_TF = 512
FFN_TN = 256
FFN_CHUNK = 256
FFN_MIN_ROWS = 256


def _ffn_kernel(xm_ref, wa_ref, wb_ref, cwa_ref, cwb_ref, cba_ref, cbb_ref, wd_ref, x1_ref, mod_ref, g_ref, o_ref,
                act_scr, *, seq_len, nf, nn, rows_per_piece):
    j = pl.program_id(1)
    tm = xm_ref.shape[0]
    tf = wa_ref.shape[1]
    tn = wd_ref.shape[1]

    @pl.when(j < nf)
    def _():
        rb = rows_per_piece
        pos = lax.broadcasted_iota(jnp.int32, (rb, 1), 0) & (seq_len - 1)
        has_prev = pos != 0
        has_next = pos != seq_len - 1

        def conv(xm, w_ref, cw_ref, cb_ref, lo):
            h = jnp.dot(xm, w_ref[:, lo:lo + FFN_CHUNK], preferred_element_type=F32)
            h_prev = jnp.where(has_prev, pltpu.roll(h, 1, 0), 0.0)
            h_next = jnp.where(has_next, pltpu.roll(h, rb - 1, 0), 0.0)
            cw = cw_ref[:, lo:lo + FFN_CHUNK]
            return cw[0:1] * h_prev + cw[1:2] * h + cw[2:3] * h_next + cb_ref[:, lo:lo + FFN_CHUNK]

        for r0 in range(0, tm, rb):
            xm = xm_ref[r0:r0 + rb, :]
            for lo in range(0, tf, FFN_CHUNK):
                a = conv(xm, wa_ref, cwa_ref, cba_ref, lo)
                b = conv(xm, wb_ref, cwb_ref, cbb_ref, lo)
                act = ((a * _sigmoid(a)) * b).astype(BF16)
                act_scr[r0:r0 + rb, pl.ds(pl.multiple_of(j * tf + lo, FFN_CHUNK), FFN_CHUNK)] = act

    @pl.when(j >= nf)
    def _():
        cols = pl.ds(pl.multiple_of((j - nf) * tn, tn), tn)
        ffn = jnp.dot(act_scr[...], wd_ref[...], preferred_element_type=F32)
        o_ref[:, cols] = x1_ref[...] + mod_ref[5:6, cols] * ffn

    @pl.when(j == nf + nn - 1)
    def _():
        x2 = o_ref[...]
        ms = jnp.mean(x2 * x2, axis=-1, keepdims=True)
        o_ref[...] = x2 * lax.rsqrt(ms + EPS) * g_ref[...]


def _ffn(x1, xm2, mod, w_up, conv_w, conv_b, w_down, final_g, *, tm, seq_len):
    n, d = x1.shape
    d_ff = w_down.shape[0]
    tf, tn = FFN_TF, FFN_TN
    nf, nn = d_ff // tf, d // tn
    assert tm % seq_len == 0 and seq_len & (seq_len - 1) == 0 and d_ff % tf == 0 and tf % FFN_CHUNK == 0
    seqs_per_tile = tm // seq_len
    seq_of = (lambda i: i * seqs_per_tile) if mod.shape[0] > 1 else (lambda i: 0)
    assert mod.shape[0] == 1 or seqs_per_tile == 1
    cb = conv_b.reshape(1, 2 * d_ff)

    def up(j):
        return jnp.minimum(j, nf - 1)

    def down(j):
        return jnp.maximum(j - nf, 0)
    return pl.pallas_call(
        functools.partial(_ffn_kernel, seq_len=seq_len, nf=nf, nn=nn, rows_per_piece=max(seq_len, FFN_MIN_ROWS)),
        grid=(n // tm, nf + nn),
        in_specs=[pl.BlockSpec((tm, d), lambda i, j: (i, 0), pipeline_mode=pl.Buffered(1)),
                  pl.BlockSpec((d, tf), lambda i, j: (0, up(j))),
                  pl.BlockSpec((d, tf), lambda i, j: (0, nf + up(j))),
                  pl.BlockSpec((3, tf), lambda i, j: (0, up(j))),
                  pl.BlockSpec((3, tf), lambda i, j: (0, nf + up(j))),
                  pl.BlockSpec((1, tf), lambda i, j: (0, up(j))),
                  pl.BlockSpec((1, tf), lambda i, j: (0, nf + up(j))),
                  pl.BlockSpec((d_ff, tn), lambda i, j: (0, down(j))),
                  pl.BlockSpec((tm, tn), lambda i, j: (i, down(j))),
                  pl.BlockSpec((None, 6, d), lambda i, j: (seq_of(i), 0, 0)),
                  pl.BlockSpec((1, d), lambda i, j: (0, 0))],
        out_specs=pl.BlockSpec((tm, d), lambda i, j: (i, 0)),
        out_shape=jax.ShapeDtypeStruct((n, d), F32),
        scratch_shapes=[pltpu.VMEM((tm, d_ff), BF16)],
        compiler_params=_cparams(("arbitrary", "arbitrary")),
        name="ffn",
    )(xm2, w_up, w_up, conv_w, conv_w, cb, cb, w_down, x1, mod, final_g.reshape(1, d))


def _rope_tables(seq_len):
    rows = seq_len // GRID_W
    row = jnp.repeat(jnp.arange(rows, dtype=F32), GRID_W)
    colp = jnp.tile(jnp.arange(GRID_W, dtype=F32), rows)
    n_freq = HEAD_DIM // 4
    inv = ROPE_THETA ** (-jnp.arange(n_freq, dtype=F32) / n_freq)
    ang_r = row[:, None] * inv[None, :]
    ang_c = colp[:, None] * inv[None, :]
    cos = jnp.concatenate([jnp.cos(ang_r), jnp.cos(ang_r), jnp.cos(ang_c), jnp.cos(ang_c)], axis=1)
    sin = jnp.concatenate([-jnp.sin(ang_r), jnp.sin(ang_r), -jnp.sin(ang_c), jnp.sin(ang_c)], axis=1)
    return cos, sin


def _layer(x, mod, w, ssm_ops, h0, *, n_seq, seq_len, latent, cache=None):
    col = w["col"]
    tm_proj = 1024
    proj, tail = _proj(x, mod, w["norm_mix_g"], w["w_in"], tm=tm_proj, tiles_per_seq=max(seq_len // tm_proj, 1),
                       col=col)
    y_ssm, h_fin = _ssm(tail, 2 * col["d_kv"], ssm_ops, h0, n_seq=n_seq, seq_len=seq_len,
                        seq_block=min(n_seq, SSM_SEQ_BLOCK))
    if latent:
        o = _attn_lat(proj, cache[0], cache[1], w["sink"], n_seq=n_seq, seq_len=seq_len, col=col)
    else:
        o = _attn_ctx(proj, w["sink"], n_seq=n_seq, seq_len=seq_len, col=col)
    tm = 256
    x1, xm2 = _mixer_out(x, y_ssm, o, proj, mod, w["norm_ffn_g"], w["w_glu"], w["w_ssm_o"], w["w_attn_o"],
                         w["w_out"], tm=tm, tiles_per_seq=max(seq_len // tm, 1), col=col)
    y = _ffn(x1, xm2, mod, w["w_up"], w["conv_w"], w["conv_b"], w["w_down"], w["final_norm_g"], tm=FFN_TM,
             seq_len=seq_len)
    return y, tail, h_fin


def kernel(x_prompt, x_sample, cache_k, cache_v, state_ssm_re, state_ssm_im, c, c_ctx, norm_mix_g, norm_ffn_g,
           w_mod, b_mod, w_in, ssm_lambda_re, ssm_lambda_im, ssm_log_dt, ssm_b_re, ssm_b_im, ssm_c_re, ssm_c_im,
           ssm_d, w_glu, attn_sink, w_ssm_o, w_attn_o, w_out, w_up, conv_w, conv_b, w_down, final_norm_g):
    batch, seq, d = x_prompt.shape
    dec_batch, dec_seq, _ = x_sample.shape
    depth = w_in.shape[0]
    assert depth == 1, "final norm is fused into the (single) layer's ffn kernel"
    d_ssm = w_glu.shape[1]
    d_attn = N_HEADS * HEAD_DIM
    d_kv = N_KV_HEADS * HEAD_DIM
    groups = d_ssm // SSM_CH
    assert w_in.shape[2] == d_ssm + d_attn + 2 * d_kv + 2 * d
    assert ssm_lambda_re.shape[2:] == (groups, SSM_STATE) and dec_batch <= 8 - 1
    l = 0

    col = {"gs": 0, "ga": d, "q": 2 * d, "k": 2 * d + d_attn, "u": 2 * d + d_attn + 2 * d_kv,
           "d_ssm": d_ssm, "d_attn": d_attn, "d_kv": d_kv}
    w = {"col": col, "w_in": w_in[l], "norm_mix_g": norm_mix_g[l], "norm_ffn_g": norm_ffn_g[l],
         "w_glu": w_glu[l].astype(BF16), "w_ssm_o": w_ssm_o[l].astype(BF16), "w_attn_o": w_attn_o[l].astype(BF16),
         "w_out": w_out[l].astype(BF16), "w_up": w_up[l].astype(BF16), "conv_w": conv_w[l], "conv_b": conv_b[l],
         "w_down": w_down[l].astype(BF16), "sink": attn_sink[l], "final_norm_g": final_norm_g}

    cond = jnp.concatenate([c_ctx[None], c, jnp.zeros((8 - 1 - dec_batch, d), F32)], axis=0)
    mod = _modulation(cond, w_mod[l], b_mod[l]).reshape(8, 6, d)
    mod_ctx, mod_lat = mod[0:1], mod[1:1 + dec_batch]

    ssm_ops = _ssm_operators(ssm_lambda_re[l], ssm_lambda_im[l], ssm_log_dt[l], ssm_b_re[l], ssm_b_im[l],
                             ssm_c_re[l], ssm_c_im[l], ssm_d[l])

    h0_ctx = jnp.zeros((batch, groups * 4 * SSM_STATE), F32)
    y_p, kv_raw, h_fin = _layer(x_prompt.reshape(batch * seq, d), mod_ctx, w, ssm_ops, h0_ctx,
                                n_seq=batch, seq_len=seq, latent=False)

    def lanes(s):
        return s.transpose(0, 2, 1, 3).reshape(dec_batch, groups, 2 * SSM_STATE)
    h0_lat = _slab_lanes(jnp.concatenate([lanes(state_ssm_re[:, l]), lanes(state_ssm_im[:, l])], axis=-1))
    cache = (cache_k[:, l].reshape(dec_batch, -1, d_kv), cache_v[:, l].reshape(dec_batch, -1, d_kv))
    y_s, _, _ = _layer(x_sample.reshape(dec_batch * dec_seq, d), mod_lat, w, ssm_ops, h0_lat,
                       n_seq=dec_batch, seq_len=dec_seq, latent=True, cache=cache)

    new_k = kv_raw[:, :d_kv].reshape(batch, 1, seq, N_KV_HEADS, HEAD_DIM)
    new_v = kv_raw[:, d_kv:2 * d_kv].reshape(batch, 1, seq, N_KV_HEADS, HEAD_DIM)

    def unlanes(hl):
        return hl.reshape(batch, groups, 2, SSM_STATE).transpose(0, 2, 1, 3)[:, None]
    h_fin = _unslab_lanes(h_fin, groups)
    new_re = unlanes(h_fin[:, :, :2 * SSM_STATE])
    new_im = unlanes(h_fin[:, :, 2 * SSM_STATE:])
    return (y_p.reshape(batch, seq, d), y_s.reshape(dec_batch, dec_seq, d), new_k, new_v, new_re, new_im)
```

```python
import functools
import math

import jax
import jax.numpy as jnp
from jax import lax
from jax.experimental import pallas as pl
from jax.experimental.pallas import tpu as pltpu

F32 = jnp.float32
BF16 = jnp.bfloat16

GRID_W = 64
SSM_CH = 16
SSM_STATE = 64
N_HEADS = 8
N_KV_HEADS = 2
HEAD_DIM = 128
Q_PER_KV = N_HEADS // N_KV_HEADS
WINDOW = 128
BLOCK = 128
ROPE_THETA = 10000.0
EPS = 1e-6
NEG_INF = -1e30

SSM_CHUNK = 16
SSM_LANES = SSM_CHUNK * SSM_CH
SSM_SEQ_BLOCK = 16

V7X_VMEM_LIMIT_BYTES = 56 * 1024 * 1024


def _cparams(semantics):
    return pltpu.CompilerParams(dimension_semantics=semantics, vmem_limit_bytes=V7X_VMEM_LIMIT_BYTES)


def _sigmoid(x):
    return 1.0 / (1.0 + jnp.exp(-x))


def _rms_modulate(x, g, scale, shift):
    ms = jnp.mean(x * x, axis=-1, keepdims=True)
    return (x * lax.rsqrt(ms + EPS) * g) * (1.0 + scale) + shift


def _mod_kernel(c_ref, w_ref, b_ref, o_ref):
    c = c_ref[...]
    a = (c * _sigmoid(c)).astype(BF16)
    o_ref[...] = jnp.dot(a, w_ref[...].astype(BF16), preferred_element_type=F32) + b_ref[...]


def _modulation(cond, w_mod, b_mod):
    rows, d = cond.shape
    n = w_mod.shape[1]
    tn = 1024
    return pl.pallas_call(
        _mod_kernel,
        grid=(n // tn,),
        in_specs=[pl.BlockSpec((rows, d), lambda j: (0, 0)),
                  pl.BlockSpec((d, tn), lambda j: (0, j)),
                  pl.BlockSpec((1, tn), lambda j: (0, j))],
        out_specs=pl.BlockSpec((rows, tn), lambda j: (0, j)),
        out_shape=jax.ShapeDtypeStruct((rows, n), F32),
        compiler_params=_cparams(("arbitrary",)),
        name="mod",
    )(cond, w_mod, b_mod.reshape(1, n))


PROJ_TN = 512


def _swap32(x):
    n = x.shape[-1]
    lane = lax.broadcasted_iota(jnp.int32, x.shape, x.ndim - 1)
    return jnp.where((lane & 63) < 32, pltpu.roll(x, n - 32, x.ndim - 1), pltpu.roll(x, 32, x.ndim - 1))


def _proj_kernel(x_ref, mod_ref, g_ref, w_ref, o_ref, tail_ref, xm_scr, *, q_tiles):
    j = pl.program_id(1)

    @pl.when(j == 0)
    def _():
        xm = _rms_modulate(x_ref[...], g_ref[...], mod_ref[1:2, :], mod_ref[0:1, :])
        xm_scr[...] = xm.astype(BF16)

    acc = jnp.dot(xm_scr[...], w_ref[...].astype(BF16), preferred_element_type=F32)
    scale = jnp.where((j >= q_tiles[0]) & (j < q_tiles[1]), HEAD_DIM ** -0.5, 1.0)
    o_ref[...] = (acc * scale).astype(BF16)
    tail_ref[...] = acc


def _proj(x, mod, norm_g, w_in, *, tm, tiles_per_seq, col):
    n, d = x.shape
    cols = w_in.shape[1]
    tn = PROJ_TN
    n_tiles = cols // tn
    q_tiles = (col["q"] // tn, (col["q"] + col["d_attn"]) // tn)
    tail0 = col["k"] // tn
    n_tail = n_tiles - tail0
    assert 2 * col["d_kv"] == tn and col["u"] == col["k"] + tn
    n_gate = col["q"] // tn
    n_u = col["d_ssm"] // tn

    def src_tile(j):
        return jnp.where(j < n_gate, j + (n_tiles - n_gate), jnp.where(j < n_tiles - n_u, j - n_gate + n_u, j - (n_tiles - n_u)))
    seq_of = (lambda i: i // tiles_per_seq) if mod.shape[0] > 1 else (lambda i: 0)
    return pl.pallas_call(
        functools.partial(_proj_kernel, q_tiles=q_tiles),
        grid=(n // tm, n_tiles),
        in_specs=[pl.BlockSpec((tm, d), lambda i, j: (i, 0)),
                  pl.BlockSpec((None, 6, d), lambda i, j: (seq_of(i), 0, 0)),
                  pl.BlockSpec((1, d), lambda i, j: (0, 0)),
                  pl.BlockSpec((d, tn), lambda i, j: (0, src_tile(j)))],
        out_specs=[pl.BlockSpec((tm, tn), lambda i, j: (i, j)),
                   pl.BlockSpec((tm, tn), lambda i, j: (i, jnp.clip(j - tail0, 0, n_tail - 1)))],
        out_shape=[jax.ShapeDtypeStruct((n, cols), BF16), jax.ShapeDtypeStruct((n, n_tail * tn), F32)],
        scratch_shapes=[pltpu.VMEM((tm, d), BF16)],
        compiler_params=_cparams(("arbitrary", "arbitrary")),
        name="proj",
    )(x, mod, norm_g.reshape(1, d), w_in)


def _cmul(ar, ai, br, bi):
    return ar * br - ai * bi, ar * bi + ai * br


def _cpow(base_r, base_i, n, bits):
    res_r = jnp.ones_like(base_r)
    res_i = jnp.zeros_like(base_r)
    for bit in range(bits):
        nr, ni = _cmul(res_r, res_i, base_r, base_i)
        take = ((n >> bit) & 1) == 1
        res_r = jnp.where(take, nr, res_r)
        res_i = jnp.where(take, ni, res_i)
        if bit + 1 < bits:
            base_r, base_i = _cmul(base_r, base_i, base_r, base_i)
    return res_r, res_i


def _ssm_disc_kernel(lre_ref, lim_ref, ldt_ref, ere_ref, eim_ref, qre_ref, qim_ref):
    lre = lre_ref[...]
    lim = lim_ref[...]
    dt = jnp.exp(ldt_ref[...])
    mag = jnp.exp(lre * dt)
    e_re = mag * jnp.cos(lim * dt)
    e_im = mag * jnp.sin(lim * dt)
    n_re = e_re - 1.0
    den = lre * lre + lim * lim
    ere_ref[...] = e_re
    eim_ref[...] = e_im
    qre_ref[...] = (n_re * lre + e_im * lim) / den
    qim_ref[...] = (e_im * lre - n_re * lim) / den


def _ssm_disc(lam_re, lam_im, log_dt):
    two, g, p = lam_re.shape
    shp = jax.ShapeDtypeStruct((two * g, p), F32)
    outs = pl.pallas_call(_ssm_disc_kernel, out_shape=[shp] * 4, name="ssm_disc")(
        lam_re.reshape(two * g, p), lam_im.reshape(two * g, p), log_dt.reshape(two * g, 1))
    return [o.reshape(two, g, p) for o in outs]


def _ssm_ops_kernel(ecol_ref, rows_ref, b_ref, ct_ref, d_ref, m_ref, w_ref, v_ref, a_ref):
    L, ch, p = SSM_CHUNK, SSM_CH, SSM_STATE
    lanes = SSM_LANES
    half = 2 * p
    lane_c = lax.broadcasted_iota(jnp.int32, (p, lanes), 1)
    lag = lane_c // ch

    def col(k):
        return jnp.broadcast_to(ecol_ref[k], (p, lanes))

    f0 = _cpow(col(0), col(1), lag, 4)
    b0 = _cpow(col(2), col(3), (L - 1) - lag, 4)
    f1 = _cmul(f0[0], f0[1], col(0), col(1))
    b1 = _cmul(b0[0], b0[1], col(2), col(3))
    cf = (ct_ref[0], ct_ref[1])
    cb = (ct_ref[2], ct_ref[3])
    yf0 = _cmul(*f0, *cf)
    yb0 = _cmul(*b0, *cb)
    yf1 = _cmul(*f1, *cf)
    yb1 = _cmul(*b1, *cb)
    v_ref[...] = jnp.concatenate([yf1[0], yb1[0], -yf1[1], -yb1[1]], axis=0).astype(BF16)

    lane_r = lax.broadcasted_iota(jnp.int32, (ch, lanes), 1)
    sgn = jnp.where(lane_r < half, -1.0, 1.0)
    b1raw = b_ref[...]
    b2raw = pltpu.roll(b1raw, half, 1) * sgn
    bb1 = rows_ref[2:3, :] * b1raw + rows_ref[3:4, :] * b2raw
    bb2 = pltpu.roll(bb1, half, 1) * sgn

    zero = jnp.zeros((p, lanes), F32)
    hi = lax.Precision.HIGHEST
    r0f = jnp.dot(bb1, jnp.concatenate([yf0[0], zero, -yf0[1], zero], axis=0), precision=hi,
                  preferred_element_type=F32)
    r0b = jnp.dot(bb1, jnp.concatenate([zero, yb0[0], zero, -yb0[1]], axis=0), precision=hi,
                  preferred_element_type=F32)
    row_r = lax.broadcasted_iota(jnp.int32, (ch, lanes), 0)
    d_col = d_ref[...]
    for s in range(L):
        blk = jnp.where(lane_r == ch * s + row_r, d_col, 0.0)
        fwd = r0f if s == 0 else jnp.where(lane_r >= ch * s, pltpu.roll(r0f, ch * s, 1), 0.0)
        k = ch * (L - 1 - s)
        bwd = r0b if k == 0 else jnp.where(lane_r < lanes - k, pltpu.roll(r0b, lanes - k, 1), 0.0)
        m_ref[s * ch:(s + 1) * ch, :] = (blk + fwd + bwd).astype(BF16)

    is_f = (lane_r & (half - 1)) < p
    n_row = jnp.where(is_f, (L - 1) - row_r, row_r)
    er, ei = _cpow(jnp.broadcast_to(rows_ref[0:1, :], (L, lanes)), jnp.broadcast_to(rows_ref[1:2, :], (L, lanes)),
                   n_row, 4)
    for s in range(L):
        w_ref[s * ch:(s + 1) * ch, :] = (er[s:s + 1, :] * bb1 + ei[s:s + 1, :] * bb2).astype(BF16)

    ar, ai = rows_ref[0:1, :], rows_ref[1:2, :]
    for _ in range(4):
        ar, ai = _cmul(ar, ai, ar, ai)
    lane_1 = lax.broadcasted_iota(jnp.int32, (1, lanes), 1)
    a_ref[...] = jnp.where(lane_1 < half, ar, ai)


def _ssm_operators(lam_re, lam_im, log_dt, b_re, b_im, c_re, c_im, ssm_d):
    two, g, p = lam_re.shape
    ch = b_re.shape[-1]
    lanes = SSM_LANES
    assert (two, p, ch, SSM_CHUNK * ch) == (2, SSM_STATE, SSM_CH, lanes)
    e_re, e_im, q_re, q_im = _ssm_disc(lam_re, lam_im, log_dt)
    ecol = jnp.stack([e_re[0], e_im[0], e_re[1], e_im[1]], axis=1)[..., None]

    def fbfb(x):
        return jnp.concatenate([x[0], x[1], x[0], x[1]], axis=-1)
    rows = jnp.stack([fbfb(e_re), fbfb(e_im), fbfb(q_re), fbfb(q_im)], axis=1)

    def bt(x):
        return jnp.swapaxes(x, -1, -2)
    b_rows = jnp.concatenate([bt(b_re[0]), bt(b_re[1]), bt(b_im[0]), bt(b_im[1])], axis=-1)

    def ctile(x):
        return jnp.tile(jnp.swapaxes(x, -1, -2), (1, 1, SSM_CHUNK))
    ct = jnp.stack([ctile(c_re[0]), ctile(c_im[0]), ctile(c_re[1]), ctile(c_im[1])], axis=1)
    d_col = ssm_d.reshape(g, ch, 1)
    mat = pl.BlockSpec((None, lanes, lanes), lambda i: (i, 0, 0))
    return pl.pallas_call(
        _ssm_ops_kernel,
        grid=(g,),
        in_specs=[pl.BlockSpec((None, 4, p, 1), lambda i: (i, 0, 0, 0)),
                  pl.BlockSpec((None, 4, lanes), lambda i: (i, 0, 0)),
                  pl.BlockSpec((None, ch, lanes), lambda i: (i, 0, 0)),
                  pl.BlockSpec((None, 4, p, lanes), lambda i: (i, 0, 0, 0)),
                  pl.BlockSpec((None, ch, 1), lambda i: (i, 0, 0))],
        out_specs=[mat, mat, mat, pl.BlockSpec((None, 1, lanes), lambda i: (i, 0, 0))],
        out_shape=[jax.ShapeDtypeStruct((g, lanes, lanes), BF16)] * 3 + [jax.ShapeDtypeStruct((g, 1, lanes), F32)],
        compiler_params=_cparams(("arbitrary",)),
        name="ssm_ops",
    )(ecol, rows, b_rows, ct, d_col)


SSM_SLAB_GROUPS = 128 // SSM_CH


def _chunk_transpose(xs, chunk_id):
    for k in (4, 2, 1):
        keep = (chunk_id & k) == 0
        new = list(xs)
        for i in range(len(xs)):
            if i & k == 0:
                a, b = xs[i], xs[i + k]
                new[i] = jnp.where(keep, a, pltpu.roll(b, SSM_CH * k, 1))
                new[i + k] = jnp.where(keep, pltpu.roll(a, 128 - SSM_CH * k, 1), b)
        xs = new
    return xs


def _ssm_kernel(u_ref, m_ref, w_ref, v_ref, a_ref, h0_ref, y_ref, hfin_ref, ug_scr, s_scr, hf_scr, hb_scr, yg_scr,
                *, n_chunks, ns):
    gps = SSM_SLAB_GROUPS
    nc = ns * n_chunks
    half = 2 * SSM_STATE
    chunk_id = lax.broadcasted_iota(jnp.int32, (nc, 128), 1) // SSM_CH
    for hf in range(2):
        xs = [u_ref[pl.ds(hf * gps + i, nc, stride=SSM_CHUNK), :] for i in range(gps)]
        xs = _chunk_transpose(xs, chunk_id)
        for g in range(gps):
            ug_scr[g, :, hf * 128:(hf + 1) * 128] = xs[g].astype(BF16)
    pitch = n_chunks + 8
    for g in range(gps):
        s = jnp.dot(ug_scr[g], w_ref[g], preferred_element_type=F32)
        for q in range(ns):
            s_scr[g, q * pitch:q * pitch + n_chunks, :] = s[q * n_chunks:(q + 1) * n_chunks, :half]
            s_scr[gps + g, q * pitch:q * pitch + n_chunks, :] = s[q * n_chunks:(q + 1) * n_chunks, half:]
    is_fwd = lax.broadcasted_iota(jnp.int32, (1, half), 1) < SSM_STATE
    blocks = range(2 * gps)
    a = [a_ref[:, b * half:(b + 1) * half] for b in blocks]
    h = [h0_ref[:, b * half:(b + 1) * half] for b in blocks]

    def rows(c):
        return pl.ds(c, ns, stride=pitch)

    for k in range(n_chunks):
        kb = n_chunks - 1 - k
        new_h = list(h)
        for b in blocks:
            hf_scr[b, rows(k), :] = h[b]
            hb_scr[b, rows(kb), :] = h[b]
        for g in range(gps):
            s_re = jnp.where(is_fwd, s_scr[g, rows(k), :], s_scr[g, rows(kb), :])
            s_im = jnp.where(is_fwd, s_scr[gps + g, rows(k), :], s_scr[gps + g, rows(kb), :])
            new_h[g] = a[g] * h[g] - a[gps + g] * h[gps + g] + s_re
            new_h[gps + g] = a[g] * h[gps + g] + a[gps + g] * h[g] + s_im
        h = new_h
    for b in blocks:
        hfin_ref[:, b * half:(b + 1) * half] = h[b]

    def entering(b):
        return jnp.concatenate(
            [jnp.where(is_fwd, hf_scr[b, q * pitch:q * pitch + n_chunks, :], hb_scr[b, q * pitch:q * pitch + n_chunks, :])
             for q in range(ns)], axis=0)

    for g in range(gps):
        hin = jnp.concatenate([entering(g), entering(gps + g)], axis=1).astype(BF16)
        yg_scr[g] = (jnp.dot(ug_scr[g], m_ref[g], preferred_element_type=F32)
                     + jnp.dot(hin, v_ref[g], preferred_element_type=F32))
    for hf in range(2):
        xs = [yg_scr[g, :, hf * 128:(hf + 1) * 128] for g in range(gps)]
        xs = _chunk_transpose(xs, chunk_id)
        for i in range(gps):
            y_ref[pl.ds(hf * gps + i, nc, stride=SSM_CHUNK), :] = xs[i]


def _slab_lanes(x):
    lead = x.shape[:-2]
    g = x.shape[-2]
    gps = SSM_SLAB_GROUPS
    x = x.reshape(lead + (g // gps, gps, 2, 2 * SSM_STATE))
    x = jnp.swapaxes(x, -3, -2)
    return x.reshape(lead + (g * 4 * SSM_STATE,))


def _unslab_lanes(x, g):
    lead = x.shape[:-1]
    gps = SSM_SLAB_GROUPS
    x = x.reshape(lead + (g // gps, 2, gps, 2 * SSM_STATE))
    x = jnp.swapaxes(x, -3, -2)
    return x.reshape(lead + (g, 4 * SSM_STATE))


def _ssm(u, u_col, ops, h0, *, n_seq, seq_len, seq_block):
    m_op, w_op, v_op, a_op = ops
    n = u.shape[0]
    g = m_op.shape[0]
    d_ssm = g * SSM_CH
    slab0 = u_col // 128
    gps = SSM_SLAB_GROUPS
    lanes = SSM_LANES
    n_chunks = seq_len // SSM_CHUNK
    nc = seq_block * n_chunks
    padded = seq_block * (n_chunks + 8)
    rows = seq_block * seq_len
    wide2 = gps * 4 * SSM_STATE
    mat = pl.BlockSpec((gps, lanes, lanes), lambda j, b: (j, 0, 0))
    return pl.pallas_call(
        functools.partial(_ssm_kernel, n_chunks=n_chunks, ns=seq_block),
        grid=(g // gps, n_seq // seq_block),
        in_specs=[pl.BlockSpec((rows, 128), lambda j, b: (b, slab0 + j)), mat, mat, mat,
                  pl.BlockSpec((1, wide2), lambda j, b: (0, j)),
                  pl.BlockSpec((seq_block, wide2), lambda j, b: (b, j))],
        out_specs=[pl.BlockSpec((rows, 128), lambda j, b: (b, j)),
                   pl.BlockSpec((seq_block, wide2), lambda j, b: (b, j))],
        out_shape=[jax.ShapeDtypeStruct((n, d_ssm), F32),
                   jax.ShapeDtypeStruct((n_seq, g * 4 * SSM_STATE), F32)],
        scratch_shapes=[pltpu.VMEM((gps, nc, lanes), BF16), pltpu.VMEM((2 * gps, padded, 128), F32),
                        pltpu.VMEM((2 * gps, padded, 128), F32), pltpu.VMEM((2 * gps, padded, 128), F32),
                        pltpu.VMEM((gps, nc, lanes), F32)],
        compiler_params=_cparams(("arbitrary", "arbitrary")),
        name="ssm",
    )(u, m_op, w_op, v_op, _slab_lanes(a_op[:, 0])[None], h0)


def _softmax_pv(pieces, sink):
    m = sink
    for s, _ in pieces:
        m = jnp.maximum(m, jnp.max(s, axis=-1, keepdims=True))
    den = jnp.exp(sink - m)
    out = None
    for s, v in pieces:
        p = jnp.exp(s - m)
        den = den + jnp.sum(p, axis=-1, keepdims=True)
        pv = jnp.dot(p.astype(BF16), v, preferred_element_type=F32)
        out = pv if out is None else out + pv
    return out / den


def _qk(q, k):
    return lax.dot_general(q, k, (((1,), (1,)), ((), ())), preferred_element_type=F32)


def _attn_ctx_kernel(sink_ref, q_ref, kv_ref, o_ref):
    d_kv = N_KV_HEADS * HEAD_DIM
    for h in range(N_HEADS):
        kh = h // Q_PER_KV
        q = q_ref[:, h * HEAD_DIM:(h + 1) * HEAD_DIM]
        k = kv_ref[:, kh * HEAD_DIM:(kh + 1) * HEAD_DIM]
        v = kv_ref[:, d_kv + kh * HEAD_DIM:d_kv + (kh + 1) * HEAD_DIM]
        o = _softmax_pv([(_qk(q, k), v)], sink_ref[h])
        o_ref[:, h * HEAD_DIM:(h + 1) * HEAD_DIM] = o.astype(BF16)


def _attn_ctx(proj, sink, *, n_seq, seq_len, col):
    d_attn = col["d_attn"]
    kv_w = 2 * col["d_kv"]
    return pl.pallas_call(
        _attn_ctx_kernel,
        grid=(n_seq,),
        in_specs=[pl.BlockSpec(memory_space=pltpu.SMEM),
                  pl.BlockSpec((seq_len, d_attn), lambda b: (b, col["q"] // d_attn)),
                  pl.BlockSpec((seq_len, kv_w), lambda b: (b, col["k"] // kv_w))],
        out_specs=pl.BlockSpec((seq_len, d_attn), lambda b: (b, 0)),
        out_shape=jax.ShapeDtypeStruct((n_seq * seq_len, d_attn), BF16),
        compiler_params=_cparams(("arbitrary",)),
        name="attn_ctx",
    )(sink, proj, proj)


def _rope(x, cos, sin):
    reps = x.shape[1] // HEAD_DIM
    xf = x.astype(F32)
    return (xf * jnp.concatenate([cos] * reps, axis=1) + _swap32(xf) * jnp.concatenate([sin] * reps, axis=1)).astype(BF16)


def _attn_lat_kernel(sink_ref, q_ref, kvp_ref, kvc_ref, kvn_ref, ck_ref, cv_ref, cos_ref, sin_ref, o_ref, *, n_blocks):
    i = pl.program_id(1)
    d_kv = N_KV_HEADS * HEAD_DIM

    def tables(blk):
        rows = pl.ds(pl.multiple_of(blk * BLOCK, BLOCK), BLOCK)
        return cos_ref[rows, :], sin_ref[rows, :]

    q_all = _rope(q_ref[...], *tables(i))
    k_all = jnp.concatenate([_rope(kvp_ref[:, :d_kv], *tables(jnp.maximum(i - 1, 0))),
                             _rope(kvc_ref[:, :d_kv], *tables(i)),
                             _rope(kvn_ref[:, :d_kv], *tables(jnp.minimum(i + 1, n_blocks - 1)))], axis=0)
    r = lax.broadcasted_iota(jnp.int32, (BLOCK, 3 * BLOCK), 0)
    c = lax.broadcasted_iota(jnp.int32, (BLOCK, 3 * BLOCK), 1)
    cc = c & (BLOCK - 1)
    valid = (((c < BLOCK) & (cc >= r) & (i > 0)) | ((c >= BLOCK) & (c < 2 * BLOCK))
             | ((c >= 2 * BLOCK) & (cc <= r) & (i < n_blocks - 1)))
    for h in range(N_HEADS):
        kh = h // Q_PER_KV
        ks = slice(kh * HEAD_DIM, (kh + 1) * HEAD_DIM)
        vs = slice(d_kv + kh * HEAD_DIM, d_kv + (kh + 1) * HEAD_DIM)
        q = q_all[:, h * HEAD_DIM:(h + 1) * HEAD_DIM]
        k_loc = k_all[:, ks]
        v_loc = jnp.concatenate([kvp_ref[:, vs], kvc_ref[:, vs], kvn_ref[:, vs]], axis=0)
        s_loc = jnp.where(valid, _qk(q, k_loc), NEG_INF)
        s_ctx = _qk(q, ck_ref[:, ks].astype(BF16))
        o = _softmax_pv([(s_loc, v_loc), (s_ctx, cv_ref[:, ks].astype(BF16))], sink_ref[h])
        o_ref[:, h * HEAD_DIM:(h + 1) * HEAD_DIM] = o.astype(BF16)


def _attn_lat(proj, cache_k, cache_v, sink, *, n_seq, seq_len, col):
    d_attn = col["d_attn"]
    kv_w = 2 * col["d_kv"]
    nb = seq_len // BLOCK
    kv_col = col["k"] // kv_w
    past, d_kv = cache_k.shape[1], cache_k.shape[2]

    def kv_spec(off):
        return pl.BlockSpec((BLOCK, kv_w), lambda b, i: (b * nb + jnp.clip(i + off, 0, nb - 1), kv_col))
    cache_spec = pl.BlockSpec((None, past, d_kv), lambda b, i: (b, 0, 0))
    table_spec = pl.BlockSpec((seq_len, HEAD_DIM), lambda b, i: (0, 0))
    cos, sin = _rope_tables(seq_len)
    return pl.pallas_call(
        functools.partial(_attn_lat_kernel, n_blocks=nb),
        grid=(n_seq, nb),
        in_specs=[pl.BlockSpec(memory_space=pltpu.SMEM),
                  pl.BlockSpec((BLOCK, d_attn), lambda b, i: (b * nb + i, col["q"] // d_attn)),
                  kv_spec(-1), kv_spec(0), kv_spec(1), cache_spec, cache_spec, table_spec, table_spec],
        out_specs=pl.BlockSpec((BLOCK, d_attn), lambda b, i: (b * nb + i, 0)),
        out_shape=jax.ShapeDtypeStruct((n_seq * seq_len, d_attn), BF16),
        compiler_params=_cparams(("arbitrary", "arbitrary")),
        name="attn_lat",
    )(sink, proj, proj, proj, proj, cache_k, cache_v, cos, sin)


def _gelu_tanh(x):
    return 0.5 * x * (1.0 + jnp.tanh(math.sqrt(2.0 / math.pi) * (x + 0.044715 * (x * x * x))))


def _mixer_out_kernel(x_ref, y_ref, o_ref, gs_ref, ga_ref, mod_ref, g_ref, wglu_ref, wso_ref, wao_ref, wout_ref,
                      x1_ref, xm2_ref):
    z = _gelu_tanh(y_ref[...].astype(F32))
    z = z * _sigmoid(jnp.dot(z.astype(BF16), wglu_ref[...], preferred_element_type=F32))
    s_br = jnp.dot(z.astype(BF16), wso_ref[...], preferred_element_type=F32)
    a_br = jnp.dot(o_ref[...], wao_ref[...], preferred_element_type=F32)
    merged = _sigmoid(gs_ref[...].astype(F32)) * s_br + _sigmoid(ga_ref[...].astype(F32)) * a_br
    out = jnp.dot(merged.astype(BF16), wout_ref[...], preferred_element_type=F32)
    x1 = x_ref[...] + mod_ref[2:3, :] * out
    x1_ref[...] = x1
    xm2_ref[...] = _rms_modulate(x1, g_ref[...], mod_ref[4:5, :], mod_ref[3:4, :]).astype(BF16)


def _mixer_out(x, y, o, proj, mod, norm_g, w_glu, w_ssm_o, w_attn_o, w_out, *, tm, tiles_per_seq, col):
    n, d = x.shape
    d_ssm = y.shape[1]
    d_attn = o.shape[1]
    seq_of = (lambda i: i // tiles_per_seq) if mod.shape[0] > 1 else (lambda i: 0)

    def resident(shape):
        return pl.BlockSpec(shape, lambda i: (0, 0), pipeline_mode=pl.Buffered(1))
    return pl.pallas_call(
        _mixer_out_kernel,
        grid=(n // tm,),
        in_specs=[pl.BlockSpec((tm, d), lambda i: (i, 0)),
                  pl.BlockSpec((tm, d_ssm), lambda i: (i, 0)),
                  pl.BlockSpec((tm, d_attn), lambda i: (i, 0)),
                  pl.BlockSpec((tm, d), lambda i: (i, col["gs"] // d)),
                  pl.BlockSpec((tm, d), lambda i: (i, col["ga"] // d)),
                  pl.BlockSpec((None, 6, d), lambda i: (seq_of(i), 0, 0)),
                  pl.BlockSpec((1, d), lambda i: (0, 0)),
                  resident(w_glu.shape), resident(w_ssm_o.shape), resident(w_attn_o.shape), resident(w_out.shape)],
        out_specs=[pl.BlockSpec((tm, d), lambda i: (i, 0)), pl.BlockSpec((tm, d), lambda i: (i, 0))],
        out_shape=[jax.ShapeDtypeStruct((n, d), F32), jax.ShapeDtypeStruct((n, d), BF16)],
        compiler_params=_cparams(("arbitrary",)),
        name="mixer_out",
    )(x, y, o, proj, proj, mod, norm_g.reshape(1, d), w_glu, w_ssm_o, w_attn_o, w_out)


FFN_TM = 1024
FFN_TF = 512
FFN_TN = 256
FFN_CHUNK = 256
FFN_MIN_ROWS = 256


def _ffn_kernel(xm_ref, wa_ref, wb_ref, cwa_ref, cwb_ref, cba_ref, cbb_ref, wd_ref, x1_ref, mod_ref, g_ref, o_ref,
                act_scr, *, seq_len, nf, nn, rows_per_piece):
    j = pl.program_id(1)
    tm = xm_ref.shape[0]
    tf = wa_ref.shape[1]
    tn = wd_ref.shape[1]

    @pl.when(j < nf)
    def _():
        rb = rows_per_piece
        pos = lax.broadcasted_iota(jnp.int32, (rb, 1), 0) & (seq_len - 1)
        has_prev = pos != 0
        has_next = pos != seq_len - 1

        def conv(xm, w_ref, cw_ref, cb_ref, lo):
            h = jnp.dot(xm, w_ref[:, lo:lo + FFN_CHUNK], preferred_element_type=F32)
            h_prev = jnp.where(has_prev, pltpu.roll(h, 1, 0), 0.0)
            h_next = jnp.where(has_next, pltpu.roll(h, rb - 1, 0), 0.0)
            cw = cw_ref[:, lo:lo + FFN_CHUNK]
            return cw[0:1] * h_prev + cw[1:2] * h + cw[2:3] * h_next + cb_ref[:, lo:lo + FFN_CHUNK]

        for r0 in range(0, tm, rb):
            xm = xm_ref[r0:r0 + rb, :]
            for lo in range(0, tf, FFN_CHUNK):
                a = conv(xm, wa_ref, cwa_ref, cba_ref, lo)
                b = conv(xm, wb_ref, cwb_ref, cbb_ref, lo)
                act = ((a * _sigmoid(a)) * b).astype(BF16)
                act_scr[r0:r0 + rb, pl.ds(pl.multiple_of(j * tf + lo, FFN_CHUNK), FFN_CHUNK)] = act

    @pl.when(j >= nf)
    def _():
        cols = pl.ds(pl.multiple_of((j - nf) * tn, tn), tn)
        ffn = jnp.dot(act_scr[...], wd_ref[...], preferred_element_type=F32)
        o_ref[:, cols] = x1_ref[...] + mod_ref[5:6, cols] * ffn

    @pl.when(j == nf + nn - 1)
    def _():
        x2 = o_ref[...]
        ms = jnp.mean(x2 * x2, axis=-1, keepdims=True)
        o_ref[...] = x2 * lax.rsqrt(ms + EPS) * g_ref[...]


def _ffn(x1, xm2, mod, w_up, conv_w, conv_b, w_down, final_g, *, tm, seq_len):
    n, d = x1.shape
    d_ff = w_down.shape[0]
    tf, tn = FFN_TF, FFN_TN
    nf, nn = d_ff // tf, d // tn
    assert tm % seq_len == 0 and seq_len & (seq_len - 1) == 0 and d_ff % tf == 0 and tf % FFN_CHUNK == 0
    seqs_per_tile = tm // seq_len
    seq_of = (lambda i: i * seqs_per_tile) if mod.shape[0] > 1 else (lambda i: 0)
    assert mod.shape[0] == 1 or seqs_per_tile == 1
    cb = conv_b.reshape(1, 2 * d_ff)

    def up(j):
        return jnp.minimum(j, nf - 1)

    def down(j):
        return jnp.maximum(j - nf, 0)
    return pl.pallas_call(
        functools.partial(_ffn_kernel, seq_len=seq_len, nf=nf, nn=nn, rows_per_piece=max(seq_len, FFN_MIN_ROWS)),
        grid=(n // tm, nf + nn),
        in_specs=[pl.BlockSpec((tm, d), lambda i, j: (i, 0), pipeline_mode=pl.Buffered(1)),
                  pl.BlockSpec((d, tf), lambda i, j: (0, up(j))),
                  pl.BlockSpec((d, tf), lambda i, j: (0, nf + up(j))),
                  pl.BlockSpec((3, tf), lambda i, j: (0, up(j))),
                  pl.BlockSpec((3, tf), lambda i, j: (0, nf + up(j))),
                  pl.BlockSpec((1, tf), lambda i, j: (0, up(j))),
                  pl.BlockSpec((1, tf), lambda i, j: (0, nf + up(j))),
                  pl.BlockSpec((d_ff, tn), lambda i, j: (0, down(j))),
                  pl.BlockSpec((tm, tn), lambda i, j: (i, down(j))),
                  pl.BlockSpec((None, 6, d), lambda i, j: (seq_of(i), 0, 0)),
                  pl.BlockSpec((1, d), lambda i, j: (0, 0))],
        out_specs=pl.BlockSpec((tm, d), lambda i, j: (i, 0)),
        out_shape=jax.ShapeDtypeStruct((n, d), F32),
        scratch_shapes=[pltpu.VMEM((tm, d_ff), BF16)],
        compiler_params=_cparams(("arbitrary", "arbitrary")),
        name="ffn",
    )(xm2, w_up, w_up, conv_w, conv_w, cb, cb, w_down, x1, mod, final_g.reshape(1, d))


def _rope_tables(seq_len):
    rows = seq_len // GRID_W
    row = jnp.repeat(jnp.arange(rows, dtype=F32), GRID_W)
    colp = jnp.tile(jnp.arange(GRID_W, dtype=F32), rows)
    n_freq = HEAD_DIM // 4
    inv = ROPE_THETA ** (-jnp.arange(n_freq, dtype=F32) / n_freq)
    ang_r = row[:, None] * inv[None, :]
    ang_c = colp[:, None] * inv[None, :]
    cos = jnp.concatenate([jnp.cos(ang_r), jnp.cos(ang_r), jnp.cos(ang_c), jnp.cos(ang_c)], axis=1)
    sin = jnp.concatenate([-jnp.sin(ang_r), jnp.sin(ang_r), -jnp.sin(ang_c), jnp.sin(ang_c)], axis=1)
    return cos, sin


def _layer(x, mod, w, ssm_ops, h0, *, n_seq, seq_len, latent, cache=None):
    col = w["col"]
    tm_proj = 1024
    proj, tail = _proj(x, mod, w["norm_mix_g"], w["w_in"], tm=tm_proj, tiles_per_seq=max(seq_len // tm_proj, 1),
                       col=col)
    y_ssm, h_fin = _ssm(tail, 2 * col["d_kv"], ssm_ops, h0, n_seq=n_seq, seq_len=seq_len,
                        seq_block=min(n_seq, SSM_SEQ_BLOCK))
    if latent:
        o = _attn_lat(proj, cache[0], cache[1], w["sink"], n_seq=n_seq, seq_len=seq_len, col=col)
    else:
        o = _attn_ctx(proj, w["sink"], n_seq=n_seq, seq_len=seq_len, col=col)
    tm = 256
    x1, xm2 = _mixer_out(x, y_ssm, o, proj, mod, w["norm_ffn_g"], w["w_glu"], w["w_ssm_o"], w["w_attn_o"],
                         w["w_out"], tm=tm, tiles_per_seq=max(seq_len // tm, 1), col=col)
    y = _ffn(x1, xm2, mod, w["w_up"], w["conv_w"], w["conv_b"], w["w_down"], w["final_norm_g"], tm=FFN_TM,
             seq_len=seq_len)
    return y, tail, h_fin


def kernel(x_prompt, x_sample, cache_k, cache_v, state_ssm_re, state_ssm_im, c, c_ctx, norm_mix_g, norm_ffn_g,
           w_mod, b_mod, w_in, ssm_lambda_re, ssm_lambda_im, ssm_log_dt, ssm_b_re, ssm_b_im, ssm_c_re, ssm_c_im,
           ssm_d, w_glu, attn_sink, w_ssm_o, w_attn_o, w_out, w_up, conv_w, conv_b, w_down, final_norm_g):
    batch, seq, d = x_prompt.shape
    dec_batch, dec_seq, _ = x_sample.shape
    depth = w_in.shape[0]
    assert depth == 1, "final norm is fused into the (single) layer's ffn kernel"
    d_ssm = w_glu.shape[1]
    d_attn = N_HEADS * HEAD_DIM
    d_kv = N_KV_HEADS * HEAD_DIM
    groups = d_ssm // SSM_CH
    assert w_in.shape[2] == d_ssm + d_attn + 2 * d_kv + 2 * d
    assert ssm_lambda_re.shape[2:] == (groups, SSM_STATE) and dec_batch <= 8 - 1
    l = 0

    col = {"gs": 0, "ga": d, "q": 2 * d, "k": 2 * d + d_attn, "u": 2 * d + d_attn + 2 * d_kv,
           "d_ssm": d_ssm, "d_attn": d_attn, "d_kv": d_kv}
    w = {"col": col, "w_in": w_in[l], "norm_mix_g": norm_mix_g[l], "norm_ffn_g": norm_ffn_g[l],
         "w_glu": w_glu[l].astype(BF16), "w_ssm_o": w_ssm_o[l].astype(BF16), "w_attn_o": w_attn_o[l].astype(BF16),
         "w_out": w_out[l].astype(BF16), "w_up": w_up[l].astype(BF16), "conv_w": conv_w[l], "conv_b": conv_b[l],
         "w_down": w_down[l].astype(BF16), "sink": attn_sink[l], "final_norm_g": final_norm_g}

    cond = jnp.concatenate([c_ctx[None], c, jnp.zeros((8 - 1 - dec_batch, d), F32)], axis=0)
    mod = _modulation(cond, w_mod[l], b_mod[l]).reshape(8, 6, d)
    mod_ctx, mod_lat = mod[0:1], mod[1:1 + dec_batch]

    ssm_ops = _ssm_operators(ssm_lambda_re[l], ssm_lambda_im[l], ssm_log_dt[l], ssm_b_re[l], ssm_b_im[l],
                             ssm_c_re[l], ssm_c_im[l], ssm_d[l])

    h0_ctx = jnp.zeros((batch, groups * 4 * SSM_STATE), F32)
    y_p, kv_raw, h_fin = _layer(x_prompt.reshape(batch * seq, d), mod_ctx, w, ssm_ops, h0_ctx,
                                n_seq=batch, seq_len=seq, latent=False)

    def lanes(s):
        return s.transpose(0, 2, 1, 3).reshape(dec_batch, groups, 2 * SSM_STATE)
    h0_lat = _slab_lanes(jnp.concatenate([lanes(state_ssm_re[:, l]), lanes(state_ssm_im[:, l])], axis=-1))
    cache = (cache_k[:, l].reshape(dec_batch, -1, d_kv), cache_v[:, l].reshape(dec_batch, -1, d_kv))
    y_s, _, _ = _layer(x_sample.reshape(dec_batch * dec_seq, d), mod_lat, w, ssm_ops, h0_lat,
                       n_seq=dec_batch, seq_len=dec_seq, latent=True, cache=cache)

    new_k = kv_raw[:, :d_kv].reshape(batch, 1, seq, N_KV_HEADS, HEAD_DIM)
    new_v = kv_raw[:, d_kv:2 * d_kv].reshape(batch, 1, seq, N_KV_HEADS, HEAD_DIM)

    def unlanes(hl):
        return hl.reshape(batch, groups, 2, SSM_STATE).transpose(0, 2, 1, 3)[:, None]
    h_fin = _unslab_lanes(h_fin, groups)
    new_re = unlanes(h_fin[:, :, :2 * SSM_STATE])
    new_im = unlanes(h_fin[:, :, 2 * SSM_STATE:])
    return (y_p.reshape(batch, seq, d), y_s.reshape(dec_batch, dec_seq, d), new_k, new_v, new_re, new_im)
```

```python
import functools
import math

import jax
import jax.numpy as jnp
from jax import lax
from jax.experimental import pallas as pl
from jax.experimental.pallas import tpu as pltpu

F32 = jnp.float32
BF16 = jnp.bfloat16

GRID_W = 64
SSM_CH = 16
SSM_STATE = 64
N_HEADS = 8
N_KV_HEADS = 2
HEAD_DIM = 128
Q_PER_KV = N_HEADS // N_KV_HEADS
WINDOW = 128
BLOCK = 128
ROPE_THETA = 10000.0
EPS = 1e-6
NEG_INF = -1e30

SSM_CHUNK = 16
SSM_LANES = SSM_CHUNK * SSM_CH
SSM_SEQ_BLOCK = 16

V7X_VMEM_LIMIT_BYTES = 56 * 1024 * 1024
V7X_MXU_WIDTH = 256


def _cparams(semantics):
    return pltpu.CompilerParams(dimension_semantics=semantics, vmem_limit_bytes=V7X_VMEM_LIMIT_BYTES)


def _sigmoid(x):
    return 1.0 / (1.0 + jnp.exp(-x))


def _rms_modulate(x, g, scale, shift):
    ms = jnp.mean(x * x, axis=-1, keepdims=True)
    return (x * lax.rsqrt(ms + EPS) * g) * (1.0 + scale) + shift


def _mod_kernel(c_ref, w_ref, b_ref, o_ref):
    c = c_ref[...]
    a = (c * _sigmoid(c)).astype(BF16)
    o_ref[...] = jnp.dot(a, w_ref[...].astype(BF16), preferred_element_type=F32) + b_ref[...]


def _modulation(cond, w_mod, b_mod):
    rows, d = cond.shape
    n = w_mod.shape[1]
    tn = 1024
    return pl.pallas_call(
        _mod_kernel,
        grid=(n // tn,),
        in_specs=[pl.BlockSpec((rows, d), lambda j: (0, 0)),
                  pl.BlockSpec((d, tn), lambda j: (0, j)),
                  pl.BlockSpec((1, tn), lambda j: (0, j))],
        out_specs=pl.BlockSpec((rows, tn), lambda j: (0, j)),
        out_shape=jax.ShapeDtypeStruct((rows, n), F32),
        compiler_params=_cparams(("arbitrary",)),
        name="mod",
    )(cond, w_mod, b_mod.reshape(1, n))


PROJ_TN = 512
PROJ_NORM_ROWS = 16


def _swap32(x):
    n = x.shape[-1]
    lane = lax.broadcasted_iota(jnp.int32, x.shape, x.ndim - 1)
    return jnp.where((lane & 63) < 32, pltpu.roll(x, n - 32, x.ndim - 1), pltpu.roll(x, 32, x.ndim - 1))


def _proj_kernel(x_ref, mod_ref, g_ref, w_ref, o_ref, tail_ref, k_ref, v_ref, xm_scr, *, q_tiles):
    j = pl.program_id(1)

    @pl.when(j == 0)
    def _():
        gain = g_ref[...] * (1.0 + mod_ref[1:2, :])
        shift = mod_ref[0:1, :]

        def block(c, carry):
            rows = pl.ds(pl.multiple_of(c * PROJ_NORM_ROWS, PROJ_NORM_ROWS), PROJ_NORM_ROWS)
            x = x_ref[rows, :]
            ms = jnp.mean(x * x, axis=-1, keepdims=True)
            xm_scr[rows, :] = (x * lax.rsqrt(ms + EPS) * gain + shift).astype(BF16)
            return carry

        lax.fori_loop(0, x_ref.shape[0] // PROJ_NORM_ROWS, block, 0, unroll=8)

    acc = jnp.dot(xm_scr[...], w_ref[...].astype(BF16), preferred_element_type=F32)
    scale = jnp.where((j >= q_tiles[0]) & (j < q_tiles[1]), HEAD_DIM ** -0.5, 1.0)
    o_ref[...] = (acc * scale).astype(BF16)
    tail_ref[...] = acc
    d_kv = k_ref.shape[1]
    k_ref[...] = acc[:, :d_kv]
    v_ref[...] = acc[:, d_kv:]


def _proj(x, mod, norm_g, w_in, *, tm, tiles_per_seq, col):
    n, d = x.shape
    cols = w_in.shape[1]
    tn = PROJ_TN
    n_tiles = cols // tn
    d_kv = col["d_kv"]
    n_gate, n_q, n_u = col["q"] // tn, col["d_attn"] // tn, col["d_ssm"] // tn
    q_tiles = (n_gate, n_gate + n_q)
    tail0 = col["u"] // tn
    n_tail = n_tiles - tail0
    assert 2 * d_kv == tn and col["u"] == col["q"] + col["d_attn"] and col["k"] == cols - tn

    def src_tile(j):
        return jnp.where(j < n_gate, j + (n_u + n_q + 1),
                         jnp.where(j < n_gate + n_q, j - n_gate + n_u,
                                   jnp.where(j < n_gate + n_q + n_u, j - (n_gate + n_q), n_u + n_q)))
    seq_of = (lambda i: i // tiles_per_seq) if mod.shape[0] > 1 else (lambda i: 0)
    return pl.pallas_call(
        functools.partial(_proj_kernel, q_tiles=q_tiles),
        grid=(n // tm, n_tiles),
        in_specs=[pl.BlockSpec((tm, d), lambda i, j: (i, 0)),
                  pl.BlockSpec((None, 6, d), lambda i, j: (seq_of(i), 0, 0)),
                  pl.BlockSpec((1, d), lambda i, j: (0, 0)),
                  pl.BlockSpec((d, tn), lambda i, j: (0, src_tile(j)))],
        out_specs=[pl.BlockSpec((tm, tn), lambda i, j: (i, j)),
                   pl.BlockSpec((tm, tn), lambda i, j: (i, jnp.clip(j - tail0, 0, n_tail - 1))),
                   pl.BlockSpec((tm, d_kv), lambda i, j: (i, 0)),
                   pl.BlockSpec((tm, d_kv), lambda i, j: (i, 0))],
        out_shape=[jax.ShapeDtypeStruct((n, cols), BF16), jax.ShapeDtypeStruct((n, n_tail * tn), F32),
                   jax.ShapeDtypeStruct((n, d_kv), F32), jax.ShapeDtypeStruct((n, d_kv), F32)],
        scratch_shapes=[pltpu.VMEM((tm, d), BF16)],
        compiler_params=_cparams(("arbitrary", "arbitrary")),
        name="proj",
    )(x, mod, norm_g.reshape(1, d), w_in)


def _cmul(ar, ai, br, bi):
    return ar * br - ai * bi, ar * bi + ai * br


def _cpow(base_r, base_i, n, bits):
    res_r = jnp.ones_like(base_r)
    res_i = jnp.zeros_like(base_r)
    for bit in range(bits):
        nr, ni = _cmul(res_r, res_i, base_r, base_i)
        take = ((n >> bit) & 1) == 1
        res_r = jnp.where(take, nr, res_r)
        res_i = jnp.where(take, ni, res_i)
        if bit + 1 < bits:
            base_r, base_i = _cmul(base_r, base_i, base_r, base_i)
    return res_r, res_i


def _ssm_disc_kernel(lre_ref, lim_ref, ldt_ref, ere_ref, eim_ref, qre_ref, qim_ref):
    lre = lre_ref[...]
    lim = lim_ref[...]
    dt = jnp.exp(ldt_ref[...])
    mag = jnp.exp(lre * dt)
    e_re = mag * jnp.cos(lim * dt)
    e_im = mag * jnp.sin(lim * dt)
    n_re = e_re - 1.0
    den = lre * lre + lim * lim
    ere_ref[...] = e_re
    eim_ref[...] = e_im
    qre_ref[...] = (n_re * lre + e_im * lim) / den
    qim_ref[...] = (e_im * lre - n_re * lim) / den


def _ssm_disc(lam_re, lam_im, log_dt):
    two, g, p = lam_re.shape
    shp = jax.ShapeDtypeStruct((two * g, p), F32)
    outs = pl.pallas_call(_ssm_disc_kernel, out_shape=[shp] * 4, name="ssm_disc")(
        lam_re.reshape(two * g, p), lam_im.reshape(two * g, p), log_dt.reshape(two * g, 1))
    return [o.reshape(two, g, p) for o in outs]


def _ssm_ops_kernel(ecol_ref, rows_ref, b_ref, ct_ref, d_ref, m_ref, w_ref, v_ref, a_ref):
    L, ch, p = SSM_CHUNK, SSM_CH, SSM_STATE
    lanes = SSM_LANES
    half = 2 * p
    lane_c = lax.broadcasted_iota(jnp.int32, (p, lanes), 1)
    lag = lane_c // ch

    def col(k):
        return jnp.broadcast_to(ecol_ref[k], (p, lanes))

    f0 = _cpow(col(0), col(1), lag, 4)
    b0 = _cpow(col(2), col(3), (L - 1) - lag, 4)
    f1 = _cmul(f0[0], f0[1], col(0), col(1))
    b1 = _cmul(b0[0], b0[1], col(2), col(3))
    cf = (ct_ref[0], ct_ref[1])
    cb = (ct_ref[2], ct_ref[3])
    yf0 = _cmul(*f0, *cf)
    yb0 = _cmul(*b0, *cb)
    yf1 = _cmul(*f1, *cf)
    yb1 = _cmul(*b1, *cb)
    v_ref[...] = jnp.concatenate([yf1[0], yb1[0], -yf1[1], -yb1[1]], axis=0).astype(BF16)

    lane_r = lax.broadcasted_iota(jnp.int32, (ch, lanes), 1)
    sgn = jnp.where(lane_r < half, -1.0, 1.0)
    b1raw = b_ref[...]
    b2raw = pltpu.roll(b1raw, half, 1) * sgn
    bb1 = rows_ref[2:3, :] * b1raw + rows_ref[3:4, :] * b2raw
    bb2 = pltpu.roll(bb1, half, 1) * sgn

    zero = jnp.zeros((p, lanes), F32)
    hi = lax.Precision.HIGHEST
    r0f = jnp.dot(bb1, jnp.concatenate([yf0[0], zero, -yf0[1], zero], axis=0), precision=hi,
                  preferred_element_type=F32)
    r0b = jnp.dot(bb1, jnp.concatenate([zero, yb0[0], zero, -yb0[1]], axis=0), precision=hi,
                  preferred_element_type=F32)
    row_r = lax.broadcasted_iota(jnp.int32, (ch, lanes), 0)
    d_col = d_ref[...]
    for s in range(L):
        blk = jnp.where(lane_r == ch * s + row_r, d_col, 0.0)
        fwd = r0f if s == 0 else jnp.where(lane_r >= ch * s, pltpu.roll(r0f, ch * s, 1), 0.0)
        k = ch * (L - 1 - s)
        bwd = r0b if k == 0 else jnp.where(lane_r < lanes - k, pltpu.roll(r0b, lanes - k, 1), 0.0)
        m_ref[s * ch:(s + 1) * ch, :] = (blk + fwd + bwd).astype(BF16)

    is_f = (lane_r & (half - 1)) < p
    n_row = jnp.where(is_f, (L - 1) - row_r, row_r)
    er, ei = _cpow(jnp.broadcast_to(rows_ref[0:1, :], (L, lanes)), jnp.broadcast_to(rows_ref[1:2, :], (L, lanes)),
                   n_row, 4)
    for s in range(L):
        w_ref[s * ch:(s + 1) * ch, :] = (er[s:s + 1, :] * bb1 + ei[s:s + 1, :] * bb2).astype(BF16)

    ar, ai = rows_ref[0:1, :], rows_ref[1:2, :]
    for _ in range(4):
        ar, ai = _cmul(ar, ai, ar, ai)
    lane_1 = lax.broadcasted_iota(jnp.int32, (1, lanes), 1)
    a_ref[...] = jnp.where(lane_1 < half, ar, ai)


def _ssm_operators(lam_re, lam_im, log_dt, b_re, b_im, c_re, c_im, ssm_d):
    two, g, p = lam_re.shape
    ch = b_re.shape[-1]
    lanes = SSM_LANES
    assert (two, p, ch, SSM_CHUNK * ch) == (2, SSM_STATE, SSM_CH, lanes)
    e_re, e_im, q_re, q_im = _ssm_disc(lam_re, lam_im, log_dt)
    ecol = jnp.stack([e_re[0], e_im[0], e_re[1], e_im[1]], axis=1)[..., None]

    def fbfb(x):
        return jnp.concatenate([x[0], x[1], x[0], x[1]], axis=-1)
    rows = jnp.stack([fbfb(e_re), fbfb(e_im), fbfb(q_re), fbfb(q_im)], axis=1)

    def bt(x):
        return jnp.swapaxes(x, -1, -2)
    b_rows = jnp.concatenate([bt(b_re[0]), bt(b_re[1]), bt(b_im[0]), bt(b_im[1])], axis=-1)

    def ctile(x):
        return jnp.tile(jnp.swapaxes(x, -1, -2), (1, 1, SSM_CHUNK))
    ct = jnp.stack([ctile(c_re[0]), ctile(c_im[0]), ctile(c_re[1]), ctile(c_im[1])], axis=1)
    d_col = ssm_d.reshape(g, ch, 1)
    mat = pl.BlockSpec((None, lanes, lanes), lambda i: (i, 0, 0))
    return pl.pallas_call(
        _ssm_ops_kernel,
        grid=(g,),
        in_specs=[pl.BlockSpec((None, 4, p, 1), lambda i: (i, 0, 0, 0)),
                  pl.BlockSpec((None, 4, lanes), lambda i: (i, 0, 0)),
                  pl.BlockSpec((None, ch, lanes), lambda i: (i, 0, 0)),
                  pl.BlockSpec((None, 4, p, lanes), lambda i: (i, 0, 0, 0)),
                  pl.BlockSpec((None, ch, 1), lambda i: (i, 0, 0))],
        out_specs=[mat, mat, mat, pl.BlockSpec((None, 1, lanes), lambda i: (i, 0, 0))],
        out_shape=[jax.ShapeDtypeStruct((g, lanes, lanes), BF16)] * 3 + [jax.ShapeDtypeStruct((g, 1, lanes), F32)],
        compiler_params=_cparams(("arbitrary",)),
        name="ssm_ops",
    )(ecol, rows, b_rows, ct, d_col)


SSM_SLAB_GROUPS = 128 // SSM_CH


def _chunk_transpose(xs, chunk_id):
    for k in (4, 2, 1):
        keep = (chunk_id & k) == 0
        new = list(xs)
        for i in range(len(xs)):
            if i & k == 0:
                a, b = xs[i], xs[i + k]
                new[i] = jnp.where(keep, a, pltpu.roll(b, SSM_CH * k, 1))
                new[i + k] = jnp.where(keep, pltpu.roll(a, 128 - SSM_CH * k, 1), b)
        xs = new
    return xs


def _ssm_kernel(u_ref, m_ref, w_ref, v_ref, a_ref, h0_ref, y_ref, hfin_ref, ug_scr, s_scr, hf_scr, hb_scr, yg_scr,
                *, n_chunks, ns):
    gps = SSM_SLAB_GROUPS
    nc = ns * n_chunks
    half = 2 * SSM_STATE
    chunk_id = lax.broadcasted_iota(jnp.int32, (nc, 128), 1) // SSM_CH
    for hf in range(2):
        xs = [u_ref[pl.ds(hf * gps + i, nc, stride=SSM_CHUNK), :] for i in range(gps)]
        xs = _chunk_transpose(xs, chunk_id)
        for g in range(gps):
            ug_scr[g, :, hf * 128:(hf + 1) * 128] = xs[g].astype(BF16)
    pitch = n_chunks + 8
    for g in range(gps):
        s = jnp.dot(ug_scr[g], w_ref[g], preferred_element_type=F32)
        for q in range(ns):
            s_scr[g, q * pitch:q * pitch + n_chunks, :] = s[q * n_chunks:(q + 1) * n_chunks, :half]
            s_scr[gps + g, q * pitch:q * pitch + n_chunks, :] = s[q * n_chunks:(q + 1) * n_chunks, half:]
    is_fwd = lax.broadcasted_iota(jnp.int32, (1, half), 1) < SSM_STATE
    blocks = range(2 * gps)
    a = [a_ref[:, b * half:(b + 1) * half] for b in blocks]
    h = [h0_ref[:, b * half:(b + 1) * half] for b in blocks]

    def rows(c):
        return pl.ds(c, ns, stride=pitch)

    for k in range(n_chunks):
        kb = n_chunks - 1 - k
        new_h = list(h)
        for b in blocks:
            hf_scr[b, rows(k), :] = h[b]
            hb_scr[b, rows(kb), :] = h[b]
        for g in range(gps):
            s_re = jnp.where(is_fwd, s_scr[g, rows(k), :], s_scr[g, rows(kb), :])
            s_im = jnp.where(is_fwd, s_scr[gps + g, rows(k), :], s_scr[gps + g, rows(kb), :])
            new_h[g] = a[g] * h[g] - a[gps + g] * h[gps + g] + s_re
            new_h[gps + g] = a[g] * h[gps + g] + a[gps + g] * h[g] + s_im
        h = new_h
    for b in blocks:
        hfin_ref[:, b * half:(b + 1) * half] = h[b]

    def entering(b):
        return jnp.concatenate(
            [jnp.where(is_fwd, hf_scr[b, q * pitch:q * pitch + n_chunks, :], hb_scr[b, q * pitch:q * pitch + n_chunks, :])
             for q in range(ns)], axis=0)

    for g in range(gps):
        hin = jnp.concatenate([entering(g), entering(gps + g)], axis=1).astype(BF16)
        yg_scr[g] = (jnp.dot(ug_scr[g], m_ref[g], preferred_element_type=F32)
                     + jnp.dot(hin, v_ref[g], preferred_element_type=F32))
    for hf in range(2):
        xs = [yg_scr[g, :, hf * 128:(hf + 1) * 128] for g in range(gps)]
        xs = _chunk_transpose(xs, chunk_id)
        for i in range(gps):
            y_ref[pl.ds(hf * gps + i, nc, stride=SSM_CHUNK), :] = xs[i]


def _slab_lanes(x):
    lead = x.shape[:-2]
    g = x.shape[-2]
    gps = SSM_SLAB_GROUPS
    x = x.reshape(lead + (g // gps, gps, 2, 2 * SSM_STATE))
    x = jnp.swapaxes(x, -3, -2)
    return x.reshape(lead + (g * 4 * SSM_STATE,))


def _unslab_lanes(x, g):
    lead = x.shape[:-1]
    gps = SSM_SLAB_GROUPS
    x = x.reshape(lead + (g // gps, 2, gps, 2 * SSM_STATE))
    x = jnp.swapaxes(x, -3, -2)
    return x.reshape(lead + (g, 4 * SSM_STATE))


def _ssm(u, u_col, ops, h0, *, n_seq, seq_len, seq_block):
    m_op, w_op, v_op, a_op = ops
    n = u.shape[0]
    g = m_op.shape[0]
    d_ssm = g * SSM_CH
    slab0 = u_col // 128
    gps = SSM_SLAB_GROUPS
    lanes = SSM_LANES
    n_chunks = seq_len // SSM_CHUNK
    nc = seq_block * n_chunks
    padded = seq_block * (n_chunks + 8)
    rows = seq_block * seq_len
    wide2 = gps * 4 * SSM_STATE
    mat = pl.BlockSpec((gps, lanes, lanes), lambda j, b: (j, 0, 0))
    return pl.pallas_call(
        functools.partial(_ssm_kernel, n_chunks=n_chunks, ns=seq_block),
        grid=(g // gps, n_seq // seq_block),
        in_specs=[pl.BlockSpec((rows, 128), lambda j, b: (b, slab0 + j)), mat, mat, mat,
                  pl.BlockSpec((1, wide2), lambda j, b: (0, j)),
                  pl.BlockSpec((seq_block, wide2), lambda j, b: (b, j))],
        out_specs=[pl.BlockSpec((rows, 128), lambda j, b: (b, j)),
                   pl.BlockSpec((seq_block, wide2), lambda j, b: (b, j))],
        out_shape=[jax.ShapeDtypeStruct((n, d_ssm), F32),
                   jax.ShapeDtypeStruct((n_seq, g * 4 * SSM_STATE), F32)],
        scratch_shapes=[pltpu.VMEM((gps, nc, lanes), BF16), pltpu.VMEM((2 * gps, padded, 128), F32),
                        pltpu.VMEM((2 * gps, padded, 128), F32), pltpu.VMEM((2 * gps, padded, 128), F32),
                        pltpu.VMEM((gps, nc, lanes), F32)],
        compiler_params=_cparams(("arbitrary", "arbitrary")),
        name="ssm",
    )(u, m_op, w_op, v_op, _slab_lanes(a_op[:, 0])[None], h0)


def _softmax_pv(pieces, sink):
    m = sink
    for s, _ in pieces:
        m = jnp.maximum(m, jnp.max(s, axis=-1, keepdims=True))
    den = jnp.exp(sink - m)
    out = None
    for s, v in pieces:
        p = jnp.exp(s - m)
        den = den + jnp.sum(p, axis=-1, keepdims=True)
        pv = jnp.dot(p.astype(BF16), v, preferred_element_type=F32)
        out = pv if out is None else out + pv
    return out / den


def _qk(q, k):
    return lax.dot_general(q, k, (((1,), (1,)), ((), ())), preferred_element_type=F32)


def _attn_ctx_kernel(sink_ref, q_ref, kv_ref, o_ref):
    d_kv = N_KV_HEADS * HEAD_DIM
    for h in range(N_HEADS):
        kh = h // Q_PER_KV
        q = q_ref[:, h * HEAD_DIM:(h + 1) * HEAD_DIM]
        k = kv_ref[:, kh * HEAD_DIM:(kh + 1) * HEAD_DIM]
        v = kv_ref[:, d_kv + kh * HEAD_DIM:d_kv + (kh + 1) * HEAD_DIM]
        o = _softmax_pv([(_qk(q, k), v)], sink_ref[h])
        o_ref[:, h * HEAD_DIM:(h + 1) * HEAD_DIM] = o.astype(BF16)


def _attn_ctx(proj, sink, *, n_seq, seq_len, col):
    d_attn = col["d_attn"]
    kv_w = 2 * col["d_kv"]
    return pl.pallas_call(
        _attn_ctx_kernel,
        grid=(n_seq,),
        in_specs=[pl.BlockSpec(memory_space=pltpu.SMEM),
                  pl.BlockSpec((seq_len, d_attn), lambda b: (b, col["q"] // d_attn)),
                  pl.BlockSpec((seq_len, kv_w), lambda b: (b, col["k"] // kv_w))],
        out_specs=pl.BlockSpec((seq_len, d_attn), lambda b: (b, 0)),
        out_shape=jax.ShapeDtypeStruct((n_seq * seq_len, d_attn), BF16),
        compiler_params=_cparams(("arbitrary",)),
        name="attn_ctx",
    )(sink, proj, proj)


def _rope(x, cos, sin):
    reps = x.shape[1] // HEAD_DIM
    xf = x.astype(F32)
    return (xf * jnp.concatenate([cos] * reps, axis=1) + _swap32(xf) * jnp.concatenate([sin] * reps, axis=1)).astype(BF16)


def _attn_lat_kernel(sink_ref, q_ref, kvp_ref, kvc_ref, kvn_ref, ck_ref, cv_ref, cos_ref, sin_ref, o_ref, *, n_blocks):
    i = pl.program_id(1)
    d_kv = N_KV_HEADS * HEAD_DIM

    def tables(blk):
        rows = pl.ds(pl.multiple_of(blk * BLOCK, BLOCK), BLOCK)
        return cos_ref[rows, :], sin_ref[rows, :]

    q_all = _rope(q_ref[...], *tables(i))
    k_all = jnp.concatenate([_rope(kvp_ref[:, :d_kv], *tables(jnp.maximum(i - 1, 0))),
                             _rope(kvc_ref[:, :d_kv], *tables(i)),
                             _rope(kvn_ref[:, :d_kv], *tables(jnp.minimum(i + 1, n_blocks - 1)))], axis=0)
    r = lax.broadcasted_iota(jnp.int32, (BLOCK, 3 * BLOCK), 0)
    c = lax.broadcasted_iota(jnp.int32, (BLOCK, 3 * BLOCK), 1)
    cc = c & (BLOCK - 1)
    valid = (((c < BLOCK) & (cc >= r) & (i > 0)) | ((c >= BLOCK) & (c < 2 * BLOCK))
             | ((c >= 2 * BLOCK) & (cc <= r) & (i < n_blocks - 1)))
    for h in range(N_HEADS):
        kh = h // Q_PER_KV
        ks = slice(kh * HEAD_DIM, (kh + 1) * HEAD_DIM)
        vs = slice(d_kv + kh * HEAD_DIM, d_kv + (kh + 1) * HEAD_DIM)
        q = q_all[:, h * HEAD_DIM:(h + 1) * HEAD_DIM]
        k_loc = k_all[:, ks]
        v_loc = jnp.concatenate([kvp_ref[:, vs], kvc_ref[:, vs], kvn_ref[:, vs]], axis=0)
        s_loc = jnp.where(valid, _qk(q, k_loc), NEG_INF)
        s_ctx = _qk(q, ck_ref[:, ks].astype(BF16))
        o = _softmax_pv([(s_loc, v_loc), (s_ctx, cv_ref[:, ks].astype(BF16))], sink_ref[h])
        o_ref[:, h * HEAD_DIM:(h + 1) * HEAD_DIM] = o.astype(BF16)


def _attn_lat(proj, cache_k, cache_v, sink, *, n_seq, seq_len, col):
    d_attn = col["d_attn"]
    kv_w = 2 * col["d_kv"]
    nb = seq_len // BLOCK
    kv_col = col["k"] // kv_w
    past, d_kv = cache_k.shape[1], cache_k.shape[2]

    def kv_spec(off):
        return pl.BlockSpec((BLOCK, kv_w), lambda b, i: (b * nb + jnp.clip(i + off, 0, nb - 1), kv_col))
    cache_spec = pl.BlockSpec((None, past, d_kv), lambda b, i: (b, 0, 0))
    table_spec = pl.BlockSpec((seq_len, HEAD_DIM), lambda b, i: (0, 0))
    cos, sin = _rope_tables(seq_len)
    return pl.pallas_call(
        functools.partial(_attn_lat_kernel, n_blocks=nb),
        grid=(n_seq, nb),
        in_specs=[pl.BlockSpec(memory_space=pltpu.SMEM),
                  pl.BlockSpec((BLOCK, d_attn), lambda b, i: (b * nb + i, col["q"] // d_attn)),
                  kv_spec(-1), kv_spec(0), kv_spec(1), cache_spec, cache_spec, table_spec, table_spec],
        out_specs=pl.BlockSpec((BLOCK, d_attn), lambda b, i: (b * nb + i, 0)),
        out_shape=jax.ShapeDtypeStruct((n_seq * seq_len, d_attn), BF16),
        compiler_params=_cparams(("arbitrary", "arbitrary")),
        name="attn_lat",
    )(sink, proj, proj, proj, proj, cache_k, cache_v, cos, sin)


def _gelu_tanh(x):
    return 0.5 * x * (1.0 + jnp.tanh(math.sqrt(2.0 / math.pi) * (x + 0.044715 * (x * x * x))))


def _mixer_out_kernel(x_ref, y_ref, o_ref, gs_ref, ga_ref, mod_ref, g_ref, wglu_ref, wso_ref, wao_ref, wout_ref,
                      x1_ref, xm2_ref):
    z = _gelu_tanh(y_ref[...].astype(F32))
    z = z * _sigmoid(jnp.dot(z.astype(BF16), wglu_ref[...], preferred_element_type=F32))
    s_br = jnp.dot(z.astype(BF16), wso_ref[...], preferred_element_type=F32)
    a_br = jnp.dot(o_ref[...], wao_ref[...], preferred_element_type=F32)
    merged = _sigmoid(gs_ref[...].astype(F32)) * s_br + _sigmoid(ga_ref[...].astype(F32)) * a_br
    out = jnp.dot(merged.astype(BF16), wout_ref[...], preferred_element_type=F32)
    x1 = x_ref[...] + mod_ref[2:3, :] * out
    x1_ref[...] = x1
    xm2_ref[...] = _rms_modulate(x1, g_ref[...], mod_ref[4:5, :], mod_ref[3:4, :]).astype(BF16)


def _mixer_out(x, y, o, proj, mod, norm_g, w_glu, w_ssm_o, w_attn_o, w_out, *, tm, tiles_per_seq, col):
    n, d = x.shape
    d_ssm = y.shape[1]
    d_attn = o.shape[1]
    seq_of = (lambda i: i // tiles_per_seq) if mod.shape[0] > 1 else (lambda i: 0)

    def resident(shape):
        return pl.BlockSpec(shape, lambda i: (0, 0), pipeline_mode=pl.Buffered(1))
    return pl.pallas_call(
        _mixer_out_kernel,
        grid=(n // tm,),
        in_specs=[pl.BlockSpec((tm, d), lambda i: (i, 0)),
                  pl.BlockSpec((tm, d_ssm), lambda i: (i, 0)),
                  pl.BlockSpec((tm, d_attn), lambda i: (i, 0)),
                  pl.BlockSpec((tm, d), lambda i: (i, col["gs"] // d)),
                  pl.BlockSpec((tm, d), lambda i: (i, col["ga"] // d)),
                  pl.BlockSpec((None, 6, d), lambda i: (seq_of(i), 0, 0)),
                  pl.BlockSpec((1, d), lambda i: (0, 0)),
                  resident(w_glu.shape), resident(w_ssm_o.shape), resident(w_attn_o.shape), resident(w_out.shape)],
        out_specs=[pl.BlockSpec((tm, d), lambda i: (i, 0)), pl.BlockSpec((tm, d), lambda i: (i, 0))],
        out_shape=[jax.ShapeDtypeStruct((n, d), F32), jax.ShapeDtypeStruct((n, d), BF16)],
        compiler_params=_cparams(("arbitrary",)),
        name="mixer_out",
    )(x, y, o, proj, proj, mod, norm_g.reshape(1, d), w_glu, w_ssm_o, w_attn_o, w_out)


FFN_TM = 1024
FFN_TF = 512
FFN_TN = 256


def _ffn_kernel(xm_ref, wa_ref, wb_ref, cw_ref, cb_ref, wd_ref, x1_ref, mod_ref, g_ref, o_ref, act_scr, raw_scr,
                *, seq_len, nf, nn, d_ff):
    j = pl.program_id(1)
    tm = xm_ref.shape[0]
    tf = wa_ref.shape[1]
    tn = wd_ref.shape[1]
    ch = tf // 2
    pos = lax.broadcasted_iota(jnp.int32, (tm, 1), 0) & (seq_len - 1)
    has_prev = pos != 0
    has_next = pos != seq_len - 1

    def lanes(off):
        return slice(off, off + ch) if isinstance(off, int) else pl.ds(pl.multiple_of(off, ch), ch)

    def gate(ha, hb, col0):
        def conv(h, off):
            cols = lanes(off)
            cw = cw_ref[:, cols]
            h_prev = jnp.where(has_prev, pltpu.roll(h, 1, 0), 0.0)
            h_next = jnp.where(has_next, pltpu.roll(h, tm - 1, 0), 0.0)
            return cw[0:1] * h_prev + cw[1:2] * h + cw[2:3] * h_next + cb_ref[:, cols]

        a = conv(ha, col0)
        b = conv(hb, d_ff + col0)
        return ((a * _sigmoid(a)) * b).astype(BF16)

    def act_cols(col0):
        return lanes(tf + col0)

    @pl.when(j == 0)
    def _():
        raw_scr[...] = jnp.zeros_like(raw_scr)

    @pl.when(j < nf)
    def _():
        xm = xm_ref[...]
        ha0 = jnp.dot(xm, wa_ref[:, :ch], preferred_element_type=F32)
        hb0 = jnp.dot(xm, wb_ref[:, :ch], preferred_element_type=F32)
        carried = (j - 1) * tf + ch
        act_scr[:, act_cols(carried)] = gate(raw_scr[0], raw_scr[1], jnp.maximum(carried, 0))
        ha1 = jnp.dot(xm, wa_ref[:, ch:], preferred_element_type=F32)
        hb1 = jnp.dot(xm, wb_ref[:, ch:], preferred_element_type=F32)
        act_scr[:, act_cols(j * tf)] = gate(ha0, hb0, j * tf)
        raw_scr[0] = ha1
        raw_scr[1] = hb1

    @pl.when(j == nf)
    def _():
        last = (nf - 1) * tf + ch
        act_scr[:, act_cols(last)] = gate(raw_scr[0], raw_scr[1], last)

    @pl.when(j >= nf)
    def _():
        cols = pl.ds(pl.multiple_of((j - nf) * tn, tn), tn)
        ffn = jnp.dot(act_scr[:, tf:], wd_ref[...], preferred_element_type=F32)
        o_ref[:, cols] = x1_ref[...] + mod_ref[5:6, cols] * ffn

    @pl.when(j == nf + nn - 1)
    def _():
        x2 = o_ref[...]
        ms = jnp.mean(x2 * x2, axis=-1, keepdims=True)
        o_ref[...] = x2 * lax.rsqrt(ms + EPS) * g_ref[...]


def _ffn(x1, xm2, mod, w_up, conv_w, conv_b, w_down, final_g, *, tm, seq_len):
    n, d = x1.shape
    d_ff = w_down.shape[0]
    tf, tn = FFN_TF, FFN_TN
    nf, nn = d_ff // tf, d // tn
    assert tm % seq_len == 0 and seq_len & (seq_len - 1) == 0 and d_ff % tf == 0 and (tf // 2) % V7X_MXU_WIDTH == 0
    seqs_per_tile = tm // seq_len
    seq_of = (lambda i: i * seqs_per_tile) if mod.shape[0] > 1 else (lambda i: 0)
    assert mod.shape[0] == 1 or seqs_per_tile == 1
    cb = conv_b.reshape(1, 2 * d_ff)

    def up(j):
        return jnp.minimum(j, nf - 1)

    def down(j):
        return jnp.maximum(j - nf, 0)
    return pl.pallas_call(
        functools.partial(_ffn_kernel, seq_len=seq_len, nf=nf, nn=nn, d_ff=d_ff),
        grid=(n // tm, nf + nn),
        in_specs=[pl.BlockSpec((tm, d), lambda i, j: (i, 0), pipeline_mode=pl.Buffered(1)),
                  pl.BlockSpec((d, tf), lambda i, j: (0, up(j))),
                  pl.BlockSpec((d, tf), lambda i, j: (0, nf + up(j))),
                  pl.BlockSpec((3, 2 * d_ff), lambda i, j: (0, 0)),
                  pl.BlockSpec((1, 2 * d_ff), lambda i, j: (0, 0)),
                  pl.BlockSpec((d_ff, tn), lambda i, j: (0, down(j))),
                  pl.BlockSpec((tm, tn), lambda i, j: (i, down(j))),
                  pl.BlockSpec((None, 6, d), lambda i, j: (seq_of(i), 0, 0)),
                  pl.BlockSpec((1, d), lambda i, j: (0, 0))],
        out_specs=pl.BlockSpec((tm, d), lambda i, j: (i, 0)),
        out_shape=jax.ShapeDtypeStruct((n, d), F32),
        scratch_shapes=[pltpu.VMEM((tm, tf + d_ff), BF16), pltpu.VMEM((2, tm, tf // 2), F32)],
        compiler_params=_cparams(("arbitrary", "arbitrary")),
        name="ffn",
    )(xm2, w_up, w_up, conv_w, cb, w_down, x1, mod, final_g.reshape(1, d))


def _rope_tables(seq_len):
    rows = seq_len // GRID_W
    row = jnp.repeat(jnp.arange(rows, dtype=F32), GRID_W)
    colp = jnp.tile(jnp.arange(GRID_W, dtype=F32), rows)
    n_freq = HEAD_DIM // 4
    inv = ROPE_THETA ** (-jnp.arange(n_freq, dtype=F32) / n_freq)
    ang_r = row[:, None] * inv[None, :]
    ang_c = colp[:, None] * inv[None, :]
    cos = jnp.concatenate([jnp.cos(ang_r), jnp.cos(ang_r), jnp.cos(ang_c), jnp.cos(ang_c)], axis=1)
    sin = jnp.concatenate([-jnp.sin(ang_r), jnp.sin(ang_r), -jnp.sin(ang_c), jnp.sin(ang_c)], axis=1)
    return cos, sin


def _layer(x, mod, w, ssm_ops, h0, *, n_seq, seq_len, latent, cache=None):
    col = w["col"]
    tm_proj = 1024
    proj, tail, k_raw, v_raw = _proj(x, mod, w["norm_mix_g"], w["w_in"], tm=tm_proj,
                                     tiles_per_seq=max(seq_len // tm_proj, 1), col=col)
    y_ssm, h_fin = _ssm(tail, 0, ssm_ops, h0, n_seq=n_seq, seq_len=seq_len, seq_block=min(n_seq, SSM_SEQ_BLOCK))
    if latent:
        o = _attn_lat(proj, cache[0], cache[1], w["sink"], n_seq=n_seq, seq_len=seq_len, col=col)
    else:
        o = _attn_ctx(proj, w["sink"], n_seq=n_seq, seq_len=seq_len, col=col)
    tm = 256
    x1, xm2 = _mixer_out(x, y_ssm, o, proj, mod, w["norm_ffn_g"], w["w_glu"], w["w_ssm_o"], w["w_attn_o"],
                         w["w_out"], tm=tm, tiles_per_seq=max(seq_len // tm, 1), col=col)
    y = _ffn(x1, xm2, mod, w["w_up"], w["conv_w"], w["conv_b"], w["w_down"], w["final_norm_g"], tm=FFN_TM,
             seq_len=seq_len)
    return y, (k_raw, v_raw), h_fin


def kernel(x_prompt, x_sample, cache_k, cache_v, state_ssm_re, state_ssm_im, c, c_ctx, norm_mix_g, norm_ffn_g,
           w_mod, b_mod, w_in, ssm_lambda_re, ssm_lambda_im, ssm_log_dt, ssm_b_re, ssm_b_im, ssm_c_re, ssm_c_im,
           ssm_d, w_glu, attn_sink, w_ssm_o, w_attn_o, w_out, w_up, conv_w, conv_b, w_down, final_norm_g):
    batch, seq, d = x_prompt.shape
    dec_batch, dec_seq, _ = x_sample.shape
    depth = w_in.shape[0]
    assert depth == 1, "final norm is fused into the (single) layer's ffn kernel"
    d_ssm = w_glu.shape[1]
    d_attn = N_HEADS * HEAD_DIM
    d_kv = N_KV_HEADS * HEAD_DIM
    groups = d_ssm // SSM_CH
    assert w_in.shape[2] == d_ssm + d_attn + 2 * d_kv + 2 * d
    assert ssm_lambda_re.shape[2:] == (groups, SSM_STATE) and dec_batch <= 8 - 1
    l = 0

    col = {"gs": 0, "ga": d, "q": 2 * d, "u": 2 * d + d_attn, "k": 2 * d + d_attn + d_ssm,
           "d_ssm": d_ssm, "d_attn": d_attn, "d_kv": d_kv}
    w = {"col": col, "w_in": w_in[l], "norm_mix_g": norm_mix_g[l], "norm_ffn_g": norm_ffn_g[l],
         "w_glu": w_glu[l].astype(BF16), "w_ssm_o": w_ssm_o[l].astype(BF16), "w_attn_o": w_attn_o[l].astype(BF16),
         "w_out": w_out[l].astype(BF16), "w_up": w_up[l].astype(BF16), "conv_w": conv_w[l], "conv_b": conv_b[l],
         "w_down": w_down[l].astype(BF16), "sink": attn_sink[l], "final_norm_g": final_norm_g}

    cond = jnp.concatenate([c_ctx[None], c, jnp.zeros((8 - 1 - dec_batch, d), F32)], axis=0)
    mod = _modulation(cond, w_mod[l], b_mod[l]).reshape(8, 6, d)
    mod_ctx, mod_lat = mod[0:1], mod[1:1 + dec_batch]

    ssm_ops = _ssm_operators(ssm_lambda_re[l], ssm_lambda_im[l], ssm_log_dt[l], ssm_b_re[l], ssm_b_im[l],
                             ssm_c_re[l], ssm_c_im[l], ssm_d[l])

    h0_ctx = jnp.zeros((batch, groups * 4 * SSM_STATE), F32)
    y_p, kv_raw, h_fin = _layer(x_prompt.reshape(batch * seq, d), mod_ctx, w, ssm_ops, h0_ctx,
                                n_seq=batch, seq_len=seq, latent=False)

    def lanes(s):
        return s.transpose(0, 2, 1, 3).reshape(dec_batch, groups, 2 * SSM_STATE)
    h0_lat = _slab_lanes(jnp.concatenate([lanes(state_ssm_re[:, l]), lanes(state_ssm_im[:, l])], axis=-1))
    cache = (cache_k[:, l].reshape(dec_batch, -1, d_kv), cache_v[:, l].reshape(dec_batch, -1, d_kv))
    y_s, _, _ = _layer(x_sample.reshape(dec_batch * dec_seq, d), mod_lat, w, ssm_ops, h0_lat,
                       n_seq=dec_batch, seq_len=dec_seq, latent=True, cache=cache)

    new_k = kv_raw[0].reshape(batch, 1, seq, N_KV_HEADS, HEAD_DIM)
    new_v = kv_raw[1].reshape(batch, 1, seq, N_KV_HEADS, HEAD_DIM)

    def unlanes(hl):
        return hl.reshape(batch, groups, 2, SSM_STATE).transpose(0, 2, 1, 3)[:, None]
    h_fin = _unslab_lanes(h_fin, groups)
    new_re = unlanes(h_fin[:, :, :2 * SSM_STATE])
    new_im = unlanes(h_fin[:, :, 2 * SSM_STATE:])
    return (y_p.reshape(batch, seq, d), y_s.reshape(dec_batch, dec_seq, d), new_k, new_v, new_re, new_im)
```

```python
import functools
import math

import jax
import jax.numpy as jnp
from jax import lax
from jax.experimental import pallas as pl
from jax.experimental.pallas import tpu as pltpu

F32 = jnp.float32
BF16 = jnp.bfloat16

GRID_W = 64
SSM_CH = 16
SSM_STATE = 64
N_HEADS = 8
N_KV_HEADS = 2
HEAD_DIM = 128
Q_PER_KV = N_HEADS // N_KV_HEADS
WINDOW = 128
BLOCK = 128
ROPE_THETA = 10000.0
EPS = 1e-6
NEG_INF = -1e30

SSM_CHUNK = 16
SSM_LANES = SSM_CHUNK * SSM_CH
SSM_SEQ_BLOCK = 16

V7X_VMEM_LIMIT_BYTES = 56 * 1024 * 1024
V7X_MXU_WIDTH = 256


def _cparams(semantics):
    return pltpu.CompilerParams(dimension_semantics=semantics, vmem_limit_bytes=V7X_VMEM_LIMIT_BYTES)


def _sigmoid(x):
    return 1.0 / (1.0 + jnp.exp(-x))


def _rms_modulate(x, g, scale, shift):
    ms = jnp.mean(x * x, axis=-1, keepdims=True)
    return (x * lax.rsqrt(ms + EPS) * g) * (1.0 + scale) + shift


def _mod_kernel(c_ref, w_ref, b_ref, o_ref):
    c = c_ref[...]
    a = (c * _sigmoid(c)).astype(BF16)
    o_ref[...] = jnp.dot(a, w_ref[...].astype(BF16), preferred_element_type=F32) + b_ref[...]


def _modulation(cond, w_mod, b_mod):
    rows, d = cond.shape
    n = w_mod.shape[1]
    tn = 1024
    return pl.pallas_call(
        _mod_kernel,
        grid=(n // tn,),
        in_specs=[pl.BlockSpec((rows, d), lambda j: (0, 0)),
                  pl.BlockSpec((d, tn), lambda j: (0, j)),
                  pl.BlockSpec((1, tn), lambda j: (0, j))],
        out_specs=pl.BlockSpec((rows, tn), lambda j: (0, j)),
        out_shape=jax.ShapeDtypeStruct((rows, n), F32),
        compiler_params=_cparams(("arbitrary",)),
        name="mod",
    )(cond, w_mod, b_mod.reshape(1, n))


PROJ_TN = 512
PROJ_NORM_ROWS = 16


def _swap32(x):
    n = x.shape[-1]
    lane = lax.broadcasted_iota(jnp.int32, x.shape, x.ndim - 1)
    return jnp.where((lane & 63) < 32, pltpu.roll(x, n - 32, x.ndim - 1), pltpu.roll(x, 32, x.ndim - 1))


def _proj_kernel(x_ref, mod_ref, g_ref, w_ref, o_ref, tail_ref, k_ref, v_ref, xm_scr, *, q_tiles):
    j = pl.program_id(1)

    @pl.when(j == 0)
    def _():
        gain = g_ref[...] * (1.0 + mod_ref[1:2, :])
        shift = mod_ref[0:1, :]

        def block(c, carry):
            rows = pl.ds(pl.multiple_of(c * PROJ_NORM_ROWS, PROJ_NORM_ROWS), PROJ_NORM_ROWS)
            x = x_ref[rows, :]
            ms = jnp.mean(x * x, axis=-1, keepdims=True)
            xm_scr[rows, :] = (x * lax.rsqrt(ms + EPS) * gain + shift).astype(BF16)
            return carry

        lax.fori_loop(0, x_ref.shape[0] // PROJ_NORM_ROWS, block, 0, unroll=8)

    acc = jnp.dot(xm_scr[...], w_ref[...], preferred_element_type=F32)
    scale = jnp.where((j >= q_tiles[0]) & (j < q_tiles[1]), HEAD_DIM ** -0.5, 1.0)
    o_ref[...] = (acc * scale).astype(BF16)
    tail_ref[...] = acc
    d_kv = k_ref.shape[1]
    k_ref[...] = acc[:, :d_kv]
    v_ref[...] = acc[:, d_kv:]


def _proj(x, mod, norm_g, w_in, *, tm, tiles_per_seq, col):
    n, d = x.shape
    cols = w_in.shape[1]
    tn = PROJ_TN
    n_tiles = cols // tn
    d_kv = col["d_kv"]
    n_gate, n_q, n_u = col["q"] // tn, col["d_attn"] // tn, col["d_ssm"] // tn
    q_tiles = (n_gate, n_gate + n_q)
    tail0 = col["u"] // tn
    n_tail = n_tiles - tail0
    assert 2 * d_kv == tn and col["u"] == col["q"] + col["d_attn"] and col["k"] == cols - tn

    def src_tile(j):
        return jnp.where(j < n_gate, j + (n_u + n_q + 1),
                         jnp.where(j < n_gate + n_q, j - n_gate + n_u,
                                   jnp.where(j < n_gate + n_q + n_u, j - (n_gate + n_q), n_u + n_q)))
    seq_of = (lambda i: i // tiles_per_seq) if mod.shape[0] > 1 else (lambda i: 0)
    return pl.pallas_call(
        functools.partial(_proj_kernel, q_tiles=q_tiles),
        grid=(n // tm, n_tiles),
        in_specs=[pl.BlockSpec((tm, d), lambda i, j: (i, 0)),
                  pl.BlockSpec((None, 6, d), lambda i, j: (seq_of(i), 0, 0)),
                  pl.BlockSpec((1, d), lambda i, j: (0, 0)),
                  pl.BlockSpec((d, tn), lambda i, j: (0, src_tile(j)))],
        out_specs=[pl.BlockSpec((tm, tn), lambda i, j: (i, j)),
                   pl.BlockSpec((tm, tn), lambda i, j: (i, jnp.clip(j - tail0, 0, n_tail - 1))),
                   pl.BlockSpec((tm, d_kv), lambda i, j: (i, 0)),
                   pl.BlockSpec((tm, d_kv), lambda i, j: (i, 0))],
        out_shape=[jax.ShapeDtypeStruct((n, cols), BF16), jax.ShapeDtypeStruct((n, n_tail * tn), F32),
                   jax.ShapeDtypeStruct((n, d_kv), F32), jax.ShapeDtypeStruct((n, d_kv), F32)],
        scratch_shapes=[pltpu.VMEM((tm, d), BF16)],
        compiler_params=_cparams(("arbitrary", "arbitrary")),
        name="proj",
    )(x, mod, norm_g.reshape(1, d), w_in)


def _cmul(ar, ai, br, bi):
    return ar * br - ai * bi, ar * bi + ai * br


def _cpow(base_r, base_i, n, bits):
    res_r = jnp.ones_like(base_r)
    res_i = jnp.zeros_like(base_r)
    for bit in range(bits):
        nr, ni = _cmul(res_r, res_i, base_r, base_i)
        take = ((n >> bit) & 1) == 1
        res_r = jnp.where(take, nr, res_r)
        res_i = jnp.where(take, ni, res_i)
        if bit + 1 < bits:
            base_r, base_i = _cmul(base_r, base_i, base_r, base_i)
    return res_r, res_i


def _ssm_disc_kernel(lre_ref, lim_ref, ldt_ref, ere_ref, eim_ref, qre_ref, qim_ref):
    lre = lre_ref[...]
    lim = lim_ref[...]
    dt = jnp.exp(ldt_ref[...])
    mag = jnp.exp(lre * dt)
    e_re = mag * jnp.cos(lim * dt)
    e_im = mag * jnp.sin(lim * dt)
    n_re = e_re - 1.0
    den = lre * lre + lim * lim
    ere_ref[...] = e_re
    eim_ref[...] = e_im
    qre_ref[...] = (n_re * lre + e_im * lim) / den
    qim_ref[...] = (e_im * lre - n_re * lim) / den


def _ssm_disc(lam_re, lam_im, log_dt):
    two, g, p = lam_re.shape
    shp = jax.ShapeDtypeStruct((two * g, p), F32)
    outs = pl.pallas_call(_ssm_disc_kernel, out_shape=[shp] * 4, name="ssm_disc")(
        lam_re.reshape(two * g, p), lam_im.reshape(two * g, p), log_dt.reshape(two * g, 1))
    return [o.reshape(two, g, p) for o in outs]


SSM_OPS_GROUPS = 8


def _ssm_ops_kernel(ecol_ref, rows_ref, b_ref, ct_ref, d_ref, m_ref, w_ref, v_ref, a_ref):
    def one(gi, carry):
        _ssm_ops_group(ecol_ref.at[gi], rows_ref.at[gi], b_ref.at[gi], ct_ref.at[gi], d_ref.at[gi],
                       m_ref.at[gi], w_ref.at[gi], v_ref.at[gi], a_ref.at[gi])
        return carry

    lax.fori_loop(0, ecol_ref.shape[0], one, 0)


def _ssm_ops_group(ecol_ref, rows_ref, b_ref, ct_ref, d_ref, m_ref, w_ref, v_ref, a_ref):
    L, ch, p = SSM_CHUNK, SSM_CH, SSM_STATE
    lanes = SSM_LANES
    half = 2 * p
    lane_c = lax.broadcasted_iota(jnp.int32, (p, lanes), 1)
    lag = lane_c // ch

    def col(k):
        return jnp.broadcast_to(ecol_ref[k], (p, lanes))

    f0 = _cpow(col(0), col(1), lag, 4)
    b0 = _cpow(col(2), col(3), (L - 1) - lag, 4)
    f1 = _cmul(f0[0], f0[1], col(0), col(1))
    b1 = _cmul(b0[0], b0[1], col(2), col(3))
    def ct(k):
        return jnp.tile(ct_ref[k], (1, L))

    cf = (ct(0), ct(1))
    cb = (ct(2), ct(3))
    yf0 = _cmul(*f0, *cf)
    yb0 = _cmul(*b0, *cb)
    yf1 = _cmul(*f1, *cf)
    yb1 = _cmul(*b1, *cb)
    v_ref[...] = jnp.concatenate([yf1[0], yb1[0], -yf1[1], -yb1[1]], axis=0).astype(BF16)

    lane_r = lax.broadcasted_iota(jnp.int32, (ch, lanes), 1)
    sgn = jnp.where(lane_r < half, -1.0, 1.0)
    b1raw = b_ref[...]
    b2raw = pltpu.roll(b1raw, half, 1) * sgn
    bb1 = rows_ref[2:3, :] * b1raw + rows_ref[3:4, :] * b2raw
    bb2 = pltpu.roll(bb1, half, 1) * sgn

    zero = jnp.zeros((p, lanes), F32)
    hi = lax.Precision.HIGHEST
    r0f = jnp.dot(bb1, jnp.concatenate([yf0[0], zero, -yf0[1], zero], axis=0), precision=hi,
                  preferred_element_type=F32)
    r0b = jnp.dot(bb1, jnp.concatenate([zero, yb0[0], zero, -yb0[1]], axis=0), precision=hi,
                  preferred_element_type=F32)
    row_r = lax.broadcasted_iota(jnp.int32, (ch, lanes), 0)
    d_col = d_ref[...]
    for s in range(L):
        blk = jnp.where(lane_r == ch * s + row_r, d_col, 0.0)
        fwd = r0f if s == 0 else jnp.where(lane_r >= ch * s, pltpu.roll(r0f, ch * s, 1), 0.0)
        k = ch * (L - 1 - s)
        bwd = r0b if k == 0 else jnp.where(lane_r < lanes - k, pltpu.roll(r0b, lanes - k, 1), 0.0)
        m_ref[s * ch:(s + 1) * ch, :] = (blk + fwd + bwd).astype(BF16)

    is_f = (lane_r & (half - 1)) < p
    n_row = jnp.where(is_f, (L - 1) - row_r, row_r)
    er, ei = _cpow(jnp.broadcast_to(rows_ref[0:1, :], (L, lanes)), jnp.broadcast_to(rows_ref[1:2, :], (L, lanes)),
                   n_row, 4)
    for s in range(L):
        w_ref[s * ch:(s + 1) * ch, :] = (er[s:s + 1, :] * bb1 + ei[s:s + 1, :] * bb2).astype(BF16)

    ar, ai = rows_ref[0:1, :], rows_ref[1:2, :]
    for _ in range(4):
        ar, ai = _cmul(ar, ai, ar, ai)
    lane_1 = lax.broadcasted_iota(jnp.int32, (1, lanes), 1)
    a_ref[...] = jnp.where(lane_1 < half, ar, ai)


def _ssm_operators(lam_re, lam_im, log_dt, b_re, b_im, c_re, c_im, ssm_d):
    two, g, p = lam_re.shape
    ch = b_re.shape[-1]
    lanes = SSM_LANES
    assert (two, p, ch, SSM_CHUNK * ch) == (2, SSM_STATE, SSM_CH, lanes)
    e_re, e_im, q_re, q_im = _ssm_disc(lam_re, lam_im, log_dt)
    ecol = jnp.stack([e_re[0], e_im[0], e_re[1], e_im[1]], axis=1)[..., None]

    def fbfb(x):
        return jnp.concatenate([x[0], x[1], x[0], x[1]], axis=-1)
    rows = jnp.stack([fbfb(e_re), fbfb(e_im), fbfb(q_re), fbfb(q_im)], axis=1)

    def bt(x):
        return jnp.swapaxes(x, -1, -2)
    b_rows = jnp.concatenate([bt(b_re[0]), bt(b_re[1]), bt(b_im[0]), bt(b_im[1])], axis=-1)

    ct = jnp.stack([bt(c_re[0]), bt(c_im[0]), bt(c_re[1]), bt(c_im[1])], axis=1)
    d_col = ssm_d.reshape(g, ch, 1)
    gb = SSM_OPS_GROUPS
    mat = pl.BlockSpec((gb, lanes, lanes), lambda i: (i, 0, 0))
    return pl.pallas_call(
        _ssm_ops_kernel,
        grid=(g // gb,),
        in_specs=[pl.BlockSpec((gb, 4, p, 1), lambda i: (i, 0, 0, 0)),
                  pl.BlockSpec((gb, 4, lanes), lambda i: (i, 0, 0)),
                  pl.BlockSpec((gb, ch, lanes), lambda i: (i, 0, 0)),
                  pl.BlockSpec((gb, 4, p, ch), lambda i: (i, 0, 0, 0)),
                  pl.BlockSpec((gb, ch, 1), lambda i: (i, 0, 0))],
        out_specs=[mat, mat, mat, pl.BlockSpec((gb, 1, lanes), lambda i: (i, 0, 0))],
        out_shape=[jax.ShapeDtypeStruct((g, lanes, lanes), BF16)] * 3 + [jax.ShapeDtypeStruct((g, 1, lanes), F32)],
        compiler_params=_cparams(("arbitrary",)),
        name="ssm_ops",
    )(ecol, rows, b_rows, ct, d_col)


SSM_SLAB_GROUPS = 128 // SSM_CH


def _chunk_transpose(xs, chunk_id):
    for k in (4, 2, 1):
        keep = (chunk_id & k) == 0
        new = list(xs)
        for i in range(len(xs)):
            if i & k == 0:
                a, b = xs[i], xs[i + k]
                new[i] = jnp.where(keep, a, pltpu.roll(b, SSM_CH * k, 1))
                new[i + k] = jnp.where(keep, pltpu.roll(a, 128 - SSM_CH * k, 1), b)
        xs = new
    return xs


def _ssm_kernel(u_ref, m_ref, w_ref, v_ref, a_ref, h0_ref, y_ref, hfin_ref, ug_scr, s_scr, hf_scr, hb_scr, yg_scr,
                *, n_chunks, ns):
    gps = SSM_SLAB_GROUPS
    nc = ns * n_chunks
    half = 2 * SSM_STATE
    chunk_id = lax.broadcasted_iota(jnp.int32, (nc, 128), 1) // SSM_CH
    for hf in range(2):
        xs = [u_ref[pl.ds(hf * gps + i, nc, stride=SSM_CHUNK), :] for i in range(gps)]
        xs = _chunk_transpose(xs, chunk_id)
        for g in range(gps):
            ug_scr[g, :, hf * 128:(hf + 1) * 128] = xs[g].astype(BF16)
    pitch = n_chunks + 8
    for g in range(gps):
        s = jnp.dot(ug_scr[g], w_ref[g], preferred_element_type=F32)
        for q in range(ns):
            s_scr[g, q * pitch:q * pitch + n_chunks, :] = s[q * n_chunks:(q + 1) * n_chunks, :half]
            s_scr[gps + g, q * pitch:q * pitch + n_chunks, :] = s[q * n_chunks:(q + 1) * n_chunks, half:]
    is_fwd = lax.broadcasted_iota(jnp.int32, (1, half), 1) < SSM_STATE
    blocks = range(2 * gps)
    a = [a_ref[:, b * half:(b + 1) * half] for b in blocks]
    h = [h0_ref[:, b * half:(b + 1) * half] for b in blocks]

    def rows(c):
        return pl.ds(c, ns, stride=pitch)

    for k in range(n_chunks):
        kb = n_chunks - 1 - k
        new_h = list(h)
        for b in blocks:
            hf_scr[b, rows(k), :] = h[b]
            hb_scr[b, rows(kb), :] = h[b]
        for g in range(gps):
            s_re = jnp.where(is_fwd, s_scr[g, rows(k), :], s_scr[g, rows(kb), :])
            s_im = jnp.where(is_fwd, s_scr[gps + g, rows(k), :], s_scr[gps + g, rows(kb), :])
            new_h[g] = a[g] * h[g] - a[gps + g] * h[gps + g] + s_re
            new_h[gps + g] = a[g] * h[gps + g] + a[gps + g] * h[g] + s_im
        h = new_h
    for b in blocks:
        hfin_ref[:, b * half:(b + 1) * half] = h[b]

    def entering(b):
        return jnp.concatenate(
            [jnp.where(is_fwd, hf_scr[b, q * pitch:q * pitch + n_chunks, :], hb_scr[b, q * pitch:q * pitch + n_chunks, :])
             for q in range(ns)], axis=0)

    for g in range(gps):
        hin = jnp.concatenate([entering(g), entering(gps + g)], axis=1).astype(BF16)
        yg_scr[g] = (jnp.dot(ug_scr[g], m_ref[g], preferred_element_type=F32)
                     + jnp.dot(hin, v_ref[g], preferred_element_type=F32))
    for hf in range(2):
        xs = [yg_scr[g, :, hf * 128:(hf + 1) * 128] for g in range(gps)]
        xs = _chunk_transpose(xs, chunk_id)
        for i in range(gps):
            y_ref[pl.ds(hf * gps + i, nc, stride=SSM_CHUNK), :] = xs[i]


def _slab_lanes(x):
    lead = x.shape[:-2]
    g = x.shape[-2]
    gps = SSM_SLAB_GROUPS
    x = x.reshape(lead + (g // gps, gps, 2, 2 * SSM_STATE))
    x = jnp.swapaxes(x, -3, -2)
    return x.reshape(lead + (g * 4 * SSM_STATE,))


def _unslab_lanes(x, g):
    lead = x.shape[:-1]
    gps = SSM_SLAB_GROUPS
    x = x.reshape(lead + (g // gps, 2, gps, 2 * SSM_STATE))
    x = jnp.swapaxes(x, -3, -2)
    return x.reshape(lead + (g, 4 * SSM_STATE))


def _ssm(u, u_col, ops, h0, *, n_seq, seq_len, seq_block):
    m_op, w_op, v_op, a_op = ops
    n = u.shape[0]
    g = m_op.shape[0]
    d_ssm = g * SSM_CH
    slab0 = u_col // 128
    gps = SSM_SLAB_GROUPS
    lanes = SSM_LANES
    n_chunks = seq_len // SSM_CHUNK
    nc = seq_block * n_chunks
    padded = seq_block * (n_chunks + 8)
    rows = seq_block * seq_len
    wide2 = gps * 4 * SSM_STATE
    mat = pl.BlockSpec((gps, lanes, lanes), lambda j, b: (j, 0, 0))
    return pl.pallas_call(
        functools.partial(_ssm_kernel, n_chunks=n_chunks, ns=seq_block),
        grid=(g // gps, n_seq // seq_block),
        in_specs=[pl.BlockSpec((rows, 128), lambda j, b: (b, slab0 + j)), mat, mat, mat,
                  pl.BlockSpec((1, wide2), lambda j, b: (0, j)),
                  pl.BlockSpec((seq_block, wide2), lambda j, b: (b, j))],
        out_specs=[pl.BlockSpec((rows, 128), lambda j, b: (b, j)),
                   pl.BlockSpec((seq_block, wide2), lambda j, b: (b, j))],
        out_shape=[jax.ShapeDtypeStruct((n, d_ssm), F32),
                   jax.ShapeDtypeStruct((n_seq, g * 4 * SSM_STATE), F32)],
        scratch_shapes=[pltpu.VMEM((gps, nc, lanes), BF16), pltpu.VMEM((2 * gps, padded, 128), F32),
                        pltpu.VMEM((2 * gps, padded, 128), F32), pltpu.VMEM((2 * gps, padded, 128), F32),
                        pltpu.VMEM((gps, nc, lanes), F32)],
        compiler_params=_cparams(("arbitrary", "arbitrary")),
        name="ssm",
    )(u, m_op, w_op, v_op, _slab_lanes(a_op[:, 0])[None], h0)


def _softmax_pv(pieces, sink):
    m = sink
    for s, _ in pieces:
        m = jnp.maximum(m, jnp.max(s, axis=-1, keepdims=True))
    out = None
    for s, v in pieces:
        pv = jnp.dot(jnp.exp(s - m).astype(BF16), v, preferred_element_type=F32)
        out = pv if out is None else out + pv
    return out[:, :HEAD_DIM] / (out[:, HEAD_DIM:] + jnp.exp(sink - m))


def _qk(q, k):
    return lax.dot_general(q, k, (((1,), (1,)), ((), ())), preferred_element_type=F32)


def _with_ones(v):
    return jnp.concatenate([v, jnp.ones_like(v)], axis=1)


def _attn_ctx_kernel(sink_ref, q_ref, kv_ref, o_ref):
    d_kv = N_KV_HEADS * HEAD_DIM
    for kh in range(N_KV_HEADS):
        k = kv_ref[:, kh * HEAD_DIM:(kh + 1) * HEAD_DIM]
        v = _with_ones(kv_ref[:, d_kv + kh * HEAD_DIM:d_kv + (kh + 1) * HEAD_DIM])
        for h in range(kh * Q_PER_KV, (kh + 1) * Q_PER_KV):
            q = q_ref[:, h * HEAD_DIM:(h + 1) * HEAD_DIM]
            o = _softmax_pv([(_qk(q, k), v)], sink_ref[h])
            o_ref[:, h * HEAD_DIM:(h + 1) * HEAD_DIM] = o.astype(BF16)


def _attn_ctx(proj, sink, *, n_seq, seq_len, col):
    d_attn = col["d_attn"]
    kv_w = 2 * col["d_kv"]
    return pl.pallas_call(
        _attn_ctx_kernel,
        grid=(n_seq,),
        in_specs=[pl.BlockSpec(memory_space=pltpu.SMEM),
                  pl.BlockSpec((seq_len, d_attn), lambda b: (b, col["q"] // d_attn)),
                  pl.BlockSpec((seq_len, kv_w), lambda b: (b, col["k"] // kv_w))],
        out_specs=pl.BlockSpec((seq_len, d_attn), lambda b: (b, 0)),
        out_shape=jax.ShapeDtypeStruct((n_seq * seq_len, d_attn), BF16),
        compiler_params=_cparams(("arbitrary",)),
        name="attn_ctx",
    )(sink, proj, proj)


def _rope(x, cos, sin):
    reps = x.shape[1] // HEAD_DIM
    xf = x.astype(F32)
    return (xf * jnp.concatenate([cos] * reps, axis=1) + _swap32(xf) * jnp.concatenate([sin] * reps, axis=1)).astype(BF16)


def _attn_lat_kernel(sink_ref, q_ref, kvp_ref, kvc_ref, kvn_ref, ck_ref, cv_ref, cos_ref, sin_ref, o_ref, *, n_blocks):
    i = pl.program_id(1)
    d_kv = N_KV_HEADS * HEAD_DIM

    def tables(blk):
        rows = pl.ds(pl.multiple_of(blk * BLOCK, BLOCK), BLOCK)
        return cos_ref[rows, :], sin_ref[rows, :]

    q_all = _rope(q_ref[...], *tables(i))
    k_all = jnp.concatenate([_rope(kvp_ref[:, :d_kv], *tables(jnp.maximum(i - 1, 0))),
                             _rope(kvc_ref[:, :d_kv], *tables(i)),
                             _rope(kvn_ref[:, :d_kv], *tables(jnp.minimum(i + 1, n_blocks - 1)))], axis=0)
    r = lax.broadcasted_iota(jnp.int32, (BLOCK, 3 * BLOCK), 0)
    c = lax.broadcasted_iota(jnp.int32, (BLOCK, 3 * BLOCK), 1)
    cc = c & (BLOCK - 1)
    valid = (((c < BLOCK) & (cc >= r) & (i > 0)) | ((c >= BLOCK) & (c < 2 * BLOCK))
             | ((c >= 2 * BLOCK) & (cc <= r) & (i < n_blocks - 1)))
    for kh in range(N_KV_HEADS):
        ks = slice(kh * HEAD_DIM, (kh + 1) * HEAD_DIM)
        vs = slice(d_kv + kh * HEAD_DIM, d_kv + (kh + 1) * HEAD_DIM)
        k_loc = k_all[:, ks]
        k_ctx = ck_ref[:, ks].astype(BF16)
        v_loc = _with_ones(jnp.concatenate([kvp_ref[:, vs], kvc_ref[:, vs], kvn_ref[:, vs]], axis=0))
        v_ctx = _with_ones(cv_ref[:, ks].astype(BF16))
        for h in range(kh * Q_PER_KV, (kh + 1) * Q_PER_KV):
            q = q_all[:, h * HEAD_DIM:(h + 1) * HEAD_DIM]
            s_loc = jnp.where(valid, _qk(q, k_loc), NEG_INF)
            o = _softmax_pv([(s_loc, v_loc), (_qk(q, k_ctx), v_ctx)], sink_ref[h])
            o_ref[:, h * HEAD_DIM:(h + 1) * HEAD_DIM] = o.astype(BF16)


def _attn_lat(proj, cache_k, cache_v, sink, *, n_seq, seq_len, col):
    d_attn = col["d_attn"]
    kv_w = 2 * col["d_kv"]
    nb = seq_len // BLOCK
    kv_col = col["k"] // kv_w
    past, d_kv = cache_k.shape[1], cache_k.shape[2]

    def kv_spec(off):
        return pl.BlockSpec((BLOCK, kv_w), lambda b, i: (b * nb + jnp.clip(i + off, 0, nb - 1), kv_col))
    cache_spec = pl.BlockSpec((None, past, d_kv), lambda b, i: (b, 0, 0))
    table_spec = pl.BlockSpec((seq_len, HEAD_DIM), lambda b, i: (0, 0))
    cos, sin = _rope_tables(seq_len)
    return pl.pallas_call(
        functools.partial(_attn_lat_kernel, n_blocks=nb),
        grid=(n_seq, nb),
        in_specs=[pl.BlockSpec(memory_space=pltpu.SMEM),
                  pl.BlockSpec((BLOCK, d_attn), lambda b, i: (b * nb + i, col["q"] // d_attn)),
                  kv_spec(-1), kv_spec(0), kv_spec(1), cache_spec, cache_spec, table_spec, table_spec],
        out_specs=pl.BlockSpec((BLOCK, d_attn), lambda b, i: (b * nb + i, 0)),
        out_shape=jax.ShapeDtypeStruct((n_seq * seq_len, d_attn), BF16),
        compiler_params=_cparams(("arbitrary", "arbitrary")),
        name="attn_lat",
    )(sink, proj, proj, proj, proj, cache_k, cache_v, cos, sin)


def _gelu_tanh(x):
    return 0.5 * x * (1.0 + jnp.tanh(math.sqrt(2.0 / math.pi) * (x + 0.044715 * (x * x * x))))


def _mixer_out_kernel(x_ref, y_ref, o_ref, gs_ref, ga_ref, mod_ref, g_ref, wglu_ref, wso_ref, wao_ref, wout_ref,
                      x1_ref, xm2_ref):
    z = _gelu_tanh(y_ref[...].astype(F32))
    z = z * _sigmoid(jnp.dot(z.astype(BF16), wglu_ref[...], preferred_element_type=F32))
    s_br = jnp.dot(z.astype(BF16), wso_ref[...], preferred_element_type=F32)
    a_br = jnp.dot(o_ref[...], wao_ref[...], preferred_element_type=F32)
    merged = _sigmoid(gs_ref[...].astype(F32)) * s_br + _sigmoid(ga_ref[...].astype(F32)) * a_br
    out = jnp.dot(merged.astype(BF16), wout_ref[...], preferred_element_type=F32)
    x1 = x_ref[...] + mod_ref[2:3, :] * out
    x1_ref[...] = x1
    xm2_ref[...] = _rms_modulate(x1, g_ref[...], mod_ref[4:5, :], mod_ref[3:4, :]).astype(BF16)


def _mixer_out(x, y, o, proj, mod, norm_g, w_glu, w_ssm_o, w_attn_o, w_out, *, tm, tiles_per_seq, col):
    n, d = x.shape
    d_ssm = y.shape[1]
    d_attn = o.shape[1]
    seq_of = (lambda i: i // tiles_per_seq) if mod.shape[0] > 1 else (lambda i: 0)

    def resident(shape):
        return pl.BlockSpec(shape, lambda i: (0, 0), pipeline_mode=pl.Buffered(1))
    return pl.pallas_call(
        _mixer_out_kernel,
        grid=(n // tm,),
        in_specs=[pl.BlockSpec((tm, d), lambda i: (i, 0)),
                  pl.BlockSpec((tm, d_ssm), lambda i: (i, 0)),
                  pl.BlockSpec((tm, d_attn), lambda i: (i, 0)),
                  pl.BlockSpec((tm, d), lambda i: (i, col["gs"] // d)),
                  pl.BlockSpec((tm, d), lambda i: (i, col["ga"] // d)),
                  pl.BlockSpec((None, 6, d), lambda i: (seq_of(i), 0, 0)),
                  pl.BlockSpec((1, d), lambda i: (0, 0)),
                  resident(w_glu.shape), resident(w_ssm_o.shape), resident(w_attn_o.shape), resident(w_out.shape)],
        out_specs=[pl.BlockSpec((tm, d), lambda i: (i, 0)), pl.BlockSpec((tm, d), lambda i: (i, 0))],
        out_shape=[jax.ShapeDtypeStruct((n, d), F32), jax.ShapeDtypeStruct((n, d), BF16)],
        compiler_params=_cparams(("arbitrary",)),
        name="mixer_out",
    )(x, y, o, proj, proj, mod, norm_g.reshape(1, d), w_glu, w_ssm_o, w_attn_o, w_out)


FFN_TM = 1024
FFN_TF = 512
FFN_TN = 256


def _ffn_kernel(xm_ref, wa_ref, wb_ref, cw_ref, cb_ref, wd_ref, x1_ref, mod_ref, g_ref, o_ref, act_scr, raw_scr,
                *, seq_len, nf, nn, d_ff):
    j = pl.program_id(1)
    tm = xm_ref.shape[0]
    tf = wa_ref.shape[1]
    tn = wd_ref.shape[1]
    ch = tf // 2
    pos = lax.broadcasted_iota(jnp.int32, (tm, 1), 0) & (seq_len - 1)
    has_prev = pos != 0
    has_next = pos != seq_len - 1

    def lanes(off):
        return slice(off, off + ch) if isinstance(off, int) else pl.ds(pl.multiple_of(off, ch), ch)

    def gate(ha, hb, col0):
        def conv(h, off):
            cols = lanes(off)
            cw = cw_ref[:, cols]
            h_prev = jnp.where(has_prev, pltpu.roll(h, 1, 0), 0.0)
            h_next = jnp.where(has_next, pltpu.roll(h, tm - 1, 0), 0.0)
            return cw[0:1] * h_prev + cw[1:2] * h + cw[2:3] * h_next + cb_ref[:, cols]

        a = conv(ha, col0)
        b = conv(hb, d_ff + col0)
        return ((a * _sigmoid(a)) * b).astype(BF16)

    def act_cols(col0):
        return lanes(tf + col0)

    @pl.when(j == 0)
    def _():
        raw_scr[...] = jnp.zeros_like(raw_scr)

    @pl.when(j < nf)
    def _():
        xm = xm_ref[...]
        ha0 = jnp.dot(xm, wa_ref[:, :ch], preferred_element_type=F32)
        hb0 = jnp.dot(xm, wb_ref[:, :ch], preferred_element_type=F32)
        carried = (j - 1) * tf + ch
        act_scr[:, act_cols(carried)] = gate(raw_scr[0], raw_scr[1], jnp.maximum(carried, 0))
        ha1 = jnp.dot(xm, wa_ref[:, ch:], preferred_element_type=F32)
        hb1 = jnp.dot(xm, wb_ref[:, ch:], preferred_element_type=F32)
        act_scr[:, act_cols(j * tf)] = gate(ha0, hb0, j * tf)
        raw_scr[0] = ha1
        raw_scr[1] = hb1

    @pl.when(j == nf)
    def _():
        last = (nf - 1) * tf + ch
        act_scr[:, act_cols(last)] = gate(raw_scr[0], raw_scr[1], last)

    @pl.when(j >= nf)
    def _():
        cols = pl.ds(pl.multiple_of((j - nf) * tn, tn), tn)
        ffn = jnp.dot(act_scr[:, tf:], wd_ref[...], preferred_element_type=F32)
        o_ref[:, cols] = x1_ref[...] + mod_ref[5:6, cols] * ffn

    @pl.when(j == nf + nn - 1)
    def _():
        x2 = o_ref[...]
        ms = jnp.mean(x2 * x2, axis=-1, keepdims=True)
        o_ref[...] = x2 * lax.rsqrt(ms + EPS) * g_ref[...]


def _ffn(x1, xm2, mod, w_up, conv_w, conv_b, w_down, final_g, *, tm, seq_len):
    n, d = x1.shape
    d_ff = w_down.shape[0]
    tf, tn = FFN_TF, FFN_TN
    nf, nn = d_ff // tf, d // tn
    assert tm % seq_len == 0 and seq_len & (seq_len - 1) == 0 and d_ff % tf == 0 and (tf // 2) % V7X_MXU_WIDTH == 0
    seqs_per_tile = tm // seq_len
    seq_of = (lambda i: i * seqs_per_tile) if mod.shape[0] > 1 else (lambda i: 0)
    assert mod.shape[0] == 1 or seqs_per_tile == 1
    cb = conv_b.reshape(1, 2 * d_ff)

    def up(j):
        return jnp.minimum(j, nf - 1)

    def down(j):
        return jnp.maximum(j - nf, 0)
    return pl.pallas_call(
        functools.partial(_ffn_kernel, seq_len=seq_len, nf=nf, nn=nn, d_ff=d_ff),
        grid=(n // tm, nf + nn),
        in_specs=[pl.BlockSpec((tm, d), lambda i, j: (i, 0), pipeline_mode=pl.Buffered(1)),
                  pl.BlockSpec((d, tf), lambda i, j: (0, up(j))),
                  pl.BlockSpec((d, tf), lambda i, j: (0, nf + up(j))),
                  pl.BlockSpec((3, 2 * d_ff), lambda i, j: (0, 0)),
                  pl.BlockSpec((1, 2 * d_ff), lambda i, j: (0, 0)),
                  pl.BlockSpec((d_ff, tn), lambda i, j: (0, down(j))),
                  pl.BlockSpec((tm, tn), lambda i, j: (i, down(j))),
                  pl.BlockSpec((None, 6, d), lambda i, j: (seq_of(i), 0, 0)),
                  pl.BlockSpec((1, d), lambda i, j: (0, 0))],
        out_specs=pl.BlockSpec((tm, d), lambda i, j: (i, 0)),
        out_shape=jax.ShapeDtypeStruct((n, d), F32),
        scratch_shapes=[pltpu.VMEM((tm, tf + d_ff), BF16), pltpu.VMEM((2, tm, tf // 2), F32)],
        compiler_params=_cparams(("arbitrary", "arbitrary")),
        name="ffn",
    )(xm2, w_up, w_up, conv_w, cb, w_down, x1, mod, final_g.reshape(1, d))


def _rope_tables(seq_len):
    rows = seq_len // GRID_W
    row = jnp.repeat(jnp.arange(rows, dtype=F32), GRID_W)
    colp = jnp.tile(jnp.arange(GRID_W, dtype=F32), rows)
    n_freq = HEAD_DIM // 4
    inv = ROPE_THETA ** (-jnp.arange(n_freq, dtype=F32) / n_freq)
    ang_r = row[:, None] * inv[None, :]
    ang_c = colp[:, None] * inv[None, :]
    cos = jnp.concatenate([jnp.cos(ang_r), jnp.cos(ang_r), jnp.cos(ang_c), jnp.cos(ang_c)], axis=1)
    sin = jnp.concatenate([-jnp.sin(ang_r), jnp.sin(ang_r), -jnp.sin(ang_c), jnp.sin(ang_c)], axis=1)
    return cos, sin


def _layer(x, mod, w, ssm_ops, h0, *, n_seq, seq_len, latent, cache=None):
    col = w["col"]
    tm_proj = 1024
    proj, tail, k_raw, v_raw = _proj(x, mod, w["norm_mix_g"], w["w_in"], tm=tm_proj,
                                     tiles_per_seq=max(seq_len // tm_proj, 1), col=col)
    y_ssm, h_fin = _ssm(tail, 0, ssm_ops, h0, n_seq=n_seq, seq_len=seq_len, seq_block=min(n_seq, SSM_SEQ_BLOCK))
    if latent:
        o = _attn_lat(proj, cache[0], cache[1], w["sink"], n_seq=n_seq, seq_len=seq_len, col=col)
    else:
        o = _attn_ctx(proj, w["sink"], n_seq=n_seq, seq_len=seq_len, col=col)
    tm = 256
    x1, xm2 = _mixer_out(x, y_ssm, o, proj, mod, w["norm_ffn_g"], w["w_glu"], w["w_ssm_o"], w["w_attn_o"],
                         w["w_out"], tm=tm, tiles_per_seq=max(seq_len // tm, 1), col=col)
    y = _ffn(x1, xm2, mod, w["w_up"], w["conv_w"], w["conv_b"], w["w_down"], w["final_norm_g"], tm=FFN_TM,
             seq_len=seq_len)
    return y, (k_raw, v_raw), h_fin


def kernel(x_prompt, x_sample, cache_k, cache_v, state_ssm_re, state_ssm_im, c, c_ctx, norm_mix_g, norm_ffn_g,
           w_mod, b_mod, w_in, ssm_lambda_re, ssm_lambda_im, ssm_log_dt, ssm_b_re, ssm_b_im, ssm_c_re, ssm_c_im,
           ssm_d, w_glu, attn_sink, w_ssm_o, w_attn_o, w_out, w_up, conv_w, conv_b, w_down, final_norm_g):
    batch, seq, d = x_prompt.shape
    dec_batch, dec_seq, _ = x_sample.shape
    depth = w_in.shape[0]
    assert depth == 1, "final norm is fused into the (single) layer's ffn kernel"
    d_ssm = w_glu.shape[1]
    d_attn = N_HEADS * HEAD_DIM
    d_kv = N_KV_HEADS * HEAD_DIM
    groups = d_ssm // SSM_CH
    assert w_in.shape[2] == d_ssm + d_attn + 2 * d_kv + 2 * d
    assert ssm_lambda_re.shape[2:] == (groups, SSM_STATE) and dec_batch <= 8 - 1
    l = 0

    col = {"gs": 0, "ga": d, "q": 2 * d, "u": 2 * d + d_attn, "k": 2 * d + d_attn + d_ssm,
           "d_ssm": d_ssm, "d_attn": d_attn, "d_kv": d_kv}
    w = {"col": col, "w_in": w_in[l].astype(BF16), "norm_mix_g": norm_mix_g[l], "norm_ffn_g": norm_ffn_g[l],
         "w_glu": w_glu[l].astype(BF16), "w_ssm_o": w_ssm_o[l].astype(BF16), "w_attn_o": w_attn_o[l].astype(BF16),
         "w_out": w_out[l].astype(BF16), "w_up": w_up[l].astype(BF16), "conv_w": conv_w[l], "conv_b": conv_b[l],
         "w_down": w_down[l].astype(BF16), "sink": attn_sink[l], "final_norm_g": final_norm_g}

    cond = jnp.concatenate([c_ctx[None], c, jnp.zeros((8 - 1 - dec_batch, d), F32)], axis=0)
    mod = _modulation(cond, w_mod[l], b_mod[l]).reshape(8, 6, d)
    mod_ctx, mod_lat = mod[0:1], mod[1:1 + dec_batch]

    ssm_ops = _ssm_operators(ssm_lambda_re[l], ssm_lambda_im[l], ssm_log_dt[l], ssm_b_re[l], ssm_b_im[l],
                             ssm_c_re[l], ssm_c_im[l], ssm_d[l])

    h0_ctx = jnp.zeros((batch, groups * 4 * SSM_STATE), F32)
    y_p, kv_raw, h_fin = _layer(x_prompt.reshape(batch * seq, d), mod_ctx, w, ssm_ops, h0_ctx,
                                n_seq=batch, seq_len=seq, latent=False)

    def lanes(s):
        return s.transpose(0, 2, 1, 3).reshape(dec_batch, groups, 2 * SSM_STATE)
    h0_lat = _slab_lanes(jnp.concatenate([lanes(state_ssm_re[:, l]), lanes(state_ssm_im[:, l])], axis=-1))
    cache = (cache_k[:, l].reshape(dec_batch, -1, d_kv), cache_v[:, l].reshape(dec_batch, -1, d_kv))
    y_s, _, _ = _layer(x_sample.reshape(dec_batch * dec_seq, d), mod_lat, w, ssm_ops, h0_lat,
                       n_seq=dec_batch, seq_len=dec_seq, latent=True, cache=cache)

    new_k = kv_raw[0].reshape(batch, 1, seq, N_KV_HEADS, HEAD_DIM)
    new_v = kv_raw[1].reshape(batch, 1, seq, N_KV_HEADS, HEAD_DIM)

    def unlanes(hl):
        return hl.reshape(batch, groups, 2, SSM_STATE).transpose(0, 2, 1, 3)[:, None]
    h_fin = _unslab_lanes(h_fin, groups)
    new_re = unlanes(h_fin[:, :, :2 * SSM_STATE])
    new_im = unlanes(h_fin[:, :, 2 * SSM_STATE:])
    return (y_p.reshape(batch, seq, d), y_s.reshape(dec_batch, dec_seq, d), new_k, new_v, new_re, new_im)
```

```python
import functools
import math

import jax
import jax.numpy as jnp
from jax import lax
from jax.experimental import pallas as pl
from jax.experimental.pallas import tpu as pltpu

F32 = jnp.float32
BF16 = jnp.bfloat16

GRID_W = 64
SSM_CH = 16
SSM_STATE = 64
N_HEADS = 8
N_KV_HEADS = 2
HEAD_DIM = 128
Q_PER_KV = N_HEADS // N_KV_HEADS
WINDOW = 128
BLOCK = 128
ROPE_THETA = 10000.0
EPS = 1e-6
NEG_INF = -1e30

SSM_CHUNK = 16
SSM_LANES = SSM_CHUNK * SSM_CH
SSM_SEQ_BLOCK = 16

V7X_VMEM_LIMIT_BYTES = 56 * 1024 * 1024
V7X_MXU_WIDTH = 256


def _cparams(semantics):
    return pltpu.CompilerParams(dimension_semantics=semantics, vmem_limit_bytes=V7X_VMEM_LIMIT_BYTES)


def _sigmoid(x):
    return 1.0 / (1.0 + jnp.exp(-x))


def _ride_casts(body, n_in, n_out, n_riders):
    def kernel(*refs):
        ins = refs[:n_in]
        rider_ins = refs[n_in:n_in + n_riders]
        outs = refs[n_in + n_riders:n_in + n_riders + n_out]
        rider_outs = refs[n_in + n_riders + n_out:n_in + 2 * n_riders + n_out]
        scratch = refs[n_in + 2 * n_riders + n_out:]
        for src, dst in zip(rider_ins, rider_outs):
            dst[...] = src[...].astype(BF16)
        body(*ins, *outs, *scratch)
    return kernel


def _rider_specs(weights, n_steps, step_of):
    specs, shapes = [], []
    for w in weights:
        r, c = w.shape
        assert r % (16 * n_steps) == 0
        specs.append(pl.BlockSpec((r // n_steps, c), lambda *idx: (step_of(*idx), 0)))
        shapes.append(jax.ShapeDtypeStruct((r, c), BF16))
    return specs, shapes


def _rms_modulate(x, g, scale, shift):
    ms = jnp.mean(x * x, axis=-1, keepdims=True)
    return (x * lax.rsqrt(ms + EPS) * g) * (1.0 + scale) + shift


def _mod_kernel(c_ref, w_ref, b_ref, o_ref):
    c = c_ref[...]
    a = (c * _sigmoid(c)).astype(BF16)
    o_ref[...] = jnp.dot(a, w_ref[...].astype(BF16), preferred_element_type=F32) + b_ref[...]


def _modulation(cond, w_mod, b_mod):
    rows, d = cond.shape
    n = w_mod.shape[1]
    tn = 1024
    return pl.pallas_call(
        _mod_kernel,
        grid=(n // tn,),
        in_specs=[pl.BlockSpec((rows, d), lambda j: (0, 0)),
                  pl.BlockSpec((d, tn), lambda j: (0, j)),
                  pl.BlockSpec((1, tn), lambda j: (0, j))],
        out_specs=pl.BlockSpec((rows, tn), lambda j: (0, j)),
        out_shape=jax.ShapeDtypeStruct((rows, n), F32),
        compiler_params=_cparams(("arbitrary",)),
        name="mod",
    )(cond, w_mod, b_mod.reshape(1, n))


PROJ_TN = 512
PROJ_NORM_ROWS = 16


def _swap32(x):
    n = x.shape[-1]
    lane = lax.broadcasted_iota(jnp.int32, x.shape, x.ndim - 1)
    return jnp.where((lane & 63) < 32, pltpu.roll(x, n - 32, x.ndim - 1), pltpu.roll(x, 32, x.ndim - 1))


def _proj_kernel(x_ref, mod_ref, g_ref, w_ref, o_ref, tail_ref, k_ref, v_ref, xm_scr, *, q_tiles):
    j = pl.program_id(1)

    @pl.when(j == 0)
    def _():
        gain = g_ref[...] * (1.0 + mod_ref[1:2, :])
        shift = mod_ref[0:1, :]

        def block(c, carry):
            rows = pl.ds(pl.multiple_of(c * PROJ_NORM_ROWS, PROJ_NORM_ROWS), PROJ_NORM_ROWS)
            x = x_ref[rows, :]
            ms = jnp.mean(x * x, axis=-1, keepdims=True)
            xm_scr[rows, :] = (x * lax.rsqrt(ms + EPS) * gain + shift).astype(BF16)
            return carry

        lax.fori_loop(0, x_ref.shape[0] // PROJ_NORM_ROWS, block, 0, unroll=8)

    acc = jnp.dot(xm_scr[...], w_ref[...], preferred_element_type=F32)
    scale = jnp.where((j >= q_tiles[0]) & (j < q_tiles[1]), HEAD_DIM ** -0.5, 1.0)
    o_ref[...] = (acc * scale).astype(BF16)
    tail_ref[...] = acc
    d_kv = k_ref.shape[1]
    k_ref[...] = acc[:, :d_kv]
    v_ref[...] = acc[:, d_kv:]


def _proj(x, mod, norm_g, w_in, *, tm, tiles_per_seq, col):
    n, d = x.shape
    cols = w_in.shape[1]
    tn = PROJ_TN
    n_tiles = cols // tn
    d_kv = col["d_kv"]
    n_gate, n_q, n_u = col["q"] // tn, col["d_attn"] // tn, col["d_ssm"] // tn
    q_tiles = (n_gate, n_gate + n_q)
    tail0 = col["u"] // tn
    n_tail = n_tiles - tail0
    assert 2 * d_kv == tn and col["u"] == col["q"] + col["d_attn"] and col["k"] == cols - tn

    def src_tile(j):
        return jnp.where(j < n_gate, j + (n_u + n_q + 1),
                         jnp.where(j < n_gate + n_q, j - n_gate + n_u,
                                   jnp.where(j < n_gate + n_q + n_u, j - (n_gate + n_q), n_u + n_q)))
    seq_of = (lambda i: i // tiles_per_seq) if mod.shape[0] > 1 else (lambda i: 0)
    return pl.pallas_call(
        functools.partial(_proj_kernel, q_tiles=q_tiles),
        grid=(n // tm, n_tiles),
        in_specs=[pl.BlockSpec((tm, d), lambda i, j: (i, 0)),
                  pl.BlockSpec((None, 6, d), lambda i, j: (seq_of(i), 0, 0)),
                  pl.BlockSpec((1, d), lambda i, j: (0, 0)),
                  pl.BlockSpec((d, tn), lambda i, j: (0, src_tile(j)))],
        out_specs=[pl.BlockSpec((tm, tn), lambda i, j: (i, j)),
                   pl.BlockSpec((tm, tn), lambda i, j: (i, jnp.clip(j - tail0, 0, n_tail - 1))),
                   pl.BlockSpec((tm, d_kv), lambda i, j: (i, 0)),
                   pl.BlockSpec((tm, d_kv), lambda i, j: (i, 0))],
        out_shape=[jax.ShapeDtypeStruct((n, cols), BF16), jax.ShapeDtypeStruct((n, n_tail * tn), F32),
                   jax.ShapeDtypeStruct((n, d_kv), F32), jax.ShapeDtypeStruct((n, d_kv), F32)],
        scratch_shapes=[pltpu.VMEM((tm, d), BF16)],
        compiler_params=_cparams(("arbitrary", "arbitrary")),
        name="proj",
    )(x, mod, norm_g.reshape(1, d), w_in)


def _cmul(ar, ai, br, bi):
    return ar * br - ai * bi, ar * bi + ai * br


def _cpow(base_r, base_i, n, bits):
    res_r = jnp.ones_like(base_r)
    res_i = jnp.zeros_like(base_r)
    for bit in range(bits):
        nr, ni = _cmul(res_r, res_i, base_r, base_i)
        take = ((n >> bit) & 1) == 1
        res_r = jnp.where(take, nr, res_r)
        res_i = jnp.where(take, ni, res_i)
        if bit + 1 < bits:
            base_r, base_i = _cmul(base_r, base_i, base_r, base_i)
    return res_r, res_i


def _ssm_disc_kernel(lre_ref, lim_ref, ldt_ref, ere_ref, eim_ref, qre_ref, qim_ref):
    lre = lre_ref[...]
    lim = lim_ref[...]
    dt = jnp.exp(ldt_ref[...])
    mag = jnp.exp(lre * dt)
    e_re = mag * jnp.cos(lim * dt)
    e_im = mag * jnp.sin(lim * dt)
    n_re = e_re - 1.0
    den = lre * lre + lim * lim
    ere_ref[...] = e_re
    eim_ref[...] = e_im
    qre_ref[...] = (n_re * lre + e_im * lim) / den
    qim_ref[...] = (e_im * lre - n_re * lim) / den


def _ssm_disc(lam_re, lam_im, log_dt):
    two, g, p = lam_re.shape
    shp = jax.ShapeDtypeStruct((two * g, p), F32)
    outs = pl.pallas_call(_ssm_disc_kernel, out_shape=[shp] * 4, name="ssm_disc")(
        lam_re.reshape(two * g, p), lam_im.reshape(two * g, p), log_dt.reshape(two * g, 1))
    return [o.reshape(two, g, p) for o in outs]


SSM_OPS_GROUPS = 8


def _ssm_ops_kernel(ecol_ref, rows_ref, b_ref, ct_ref, d_ref, m_ref, w_ref, v_ref, a_ref):
    def one(gi, carry):
        _ssm_ops_group(ecol_ref.at[gi], rows_ref.at[gi], b_ref.at[gi], ct_ref.at[gi], d_ref.at[gi],
                       m_ref.at[gi], w_ref.at[gi], v_ref.at[gi], a_ref.at[gi])
        return carry

    lax.fori_loop(0, ecol_ref.shape[0], one, 0)


def _ssm_ops_group(ecol_ref, rows_ref, b_ref, ct_ref, d_ref, m_ref, w_ref, v_ref, a_ref):
    L, ch, p = SSM_CHUNK, SSM_CH, SSM_STATE
    lanes = SSM_LANES
    half = 2 * p
    lane_c = lax.broadcasted_iota(jnp.int32, (p, lanes), 1)
    lag = lane_c // ch

    def col(k):
        return jnp.broadcast_to(ecol_ref[k], (p, lanes))

    f0 = _cpow(col(0), col(1), lag, 4)
    b0 = _cpow(col(2), col(3), (L - 1) - lag, 4)
    f1 = _cmul(f0[0], f0[1], col(0), col(1))
    b1 = _cmul(b0[0], b0[1], col(2), col(3))
    def ct(k):
        return jnp.tile(ct_ref[k], (1, L))

    cf = (ct(0), ct(1))
    cb = (ct(2), ct(3))
    yf0 = _cmul(*f0, *cf)
    yb0 = _cmul(*b0, *cb)
    yf1 = _cmul(*f1, *cf)
    yb1 = _cmul(*b1, *cb)
    v_ref[...] = jnp.concatenate([yf1[0], yb1[0], -yf1[1], -yb1[1]], axis=0).astype(BF16)

    lane_r = lax.broadcasted_iota(jnp.int32, (ch, lanes), 1)
    sgn = jnp.where(lane_r < half, -1.0, 1.0)
    b1raw = b_ref[...]
    b2raw = pltpu.roll(b1raw, half, 1) * sgn
    bb1 = rows_ref[2:3, :] * b1raw + rows_ref[3:4, :] * b2raw
    bb2 = pltpu.roll(bb1, half, 1) * sgn

    zero = jnp.zeros((p, lanes), F32)
    hi = lax.Precision.HIGHEST
    r0f = jnp.dot(bb1, jnp.concatenate([yf0[0], zero, -yf0[1], zero], axis=0), precision=hi,
                  preferred_element_type=F32)
    r0b = jnp.dot(bb1, jnp.concatenate([zero, yb0[0], zero, -yb0[1]], axis=0), precision=hi,
                  preferred_element_type=F32)
    row_r = lax.broadcasted_iota(jnp.int32, (ch, lanes), 0)
    d_col = d_ref[...]
    for s in range(L):
        blk = jnp.where(lane_r == ch * s + row_r, d_col, 0.0)
        fwd = r0f if s == 0 else jnp.where(lane_r >= ch * s, pltpu.roll(r0f, ch * s, 1), 0.0)
        k = ch * (L - 1 - s)
        bwd = r0b if k == 0 else jnp.where(lane_r < lanes - k, pltpu.roll(r0b, lanes - k, 1), 0.0)
        m_ref[s * ch:(s + 1) * ch, :] = (blk + fwd + bwd).astype(BF16)

    is_f = (lane_r & (half - 1)) < p
    n_row = jnp.where(is_f, (L - 1) - row_r, row_r)
    er, ei = _cpow(jnp.broadcast_to(rows_ref[0:1, :], (L, lanes)), jnp.broadcast_to(rows_ref[1:2, :], (L, lanes)),
                   n_row, 4)
    for s in range(L):
        w_ref[s * ch:(s + 1) * ch, :] = (er[s:s + 1, :] * bb1 + ei[s:s + 1, :] * bb2).astype(BF16)

    ar, ai = rows_ref[0:1, :], rows_ref[1:2, :]
    for _ in range(4):
        ar, ai = _cmul(ar, ai, ar, ai)
    lane_1 = lax.broadcasted_iota(jnp.int32, (1, lanes), 1)
    a_ref[...] = jnp.where(lane_1 < half, ar, ai)


def _ssm_operators(lam_re, lam_im, log_dt, b_re, b_im, c_re, c_im, ssm_d, riders=()):
    two, g, p = lam_re.shape
    ch = b_re.shape[-1]
    lanes = SSM_LANES
    assert (two, p, ch, SSM_CHUNK * ch) == (2, SSM_STATE, SSM_CH, lanes)
    e_re, e_im, q_re, q_im = _ssm_disc(lam_re, lam_im, log_dt)
    ecol = jnp.stack([e_re[0], e_im[0], e_re[1], e_im[1]], axis=1)[..., None]

    def fbfb(x):
        return jnp.concatenate([x[0], x[1], x[0], x[1]], axis=-1)
    rows = jnp.stack([fbfb(e_re), fbfb(e_im), fbfb(q_re), fbfb(q_im)], axis=1)

    def bt(x):
        return jnp.swapaxes(x, -1, -2)
    b_rows = jnp.concatenate([bt(b_re[0]), bt(b_re[1]), bt(b_im[0]), bt(b_im[1])], axis=-1)

    ct = jnp.stack([bt(c_re[0]), bt(c_im[0]), bt(c_re[1]), bt(c_im[1])], axis=1)
    d_col = ssm_d.reshape(g, ch, 1)
    gb = SSM_OPS_GROUPS
    mat = pl.BlockSpec((gb, lanes, lanes), lambda i: (i, 0, 0))
    rider_specs, rider_shapes = _rider_specs(riders, g // gb, lambda i: i)
    outs = pl.pallas_call(
        _ride_casts(_ssm_ops_kernel, 5, 4, len(riders)),
        grid=(g // gb,),
        in_specs=[pl.BlockSpec((gb, 4, p, 1), lambda i: (i, 0, 0, 0)),
                  pl.BlockSpec((gb, 4, lanes), lambda i: (i, 0, 0)),
                  pl.BlockSpec((gb, ch, lanes), lambda i: (i, 0, 0)),
                  pl.BlockSpec((gb, 4, p, ch), lambda i: (i, 0, 0, 0)),
                  pl.BlockSpec((gb, ch, 1), lambda i: (i, 0, 0))] + rider_specs,
        out_specs=[mat, mat, mat, pl.BlockSpec((gb, 1, lanes), lambda i: (i, 0, 0))] + rider_specs,
        out_shape=([jax.ShapeDtypeStruct((g, lanes, lanes), BF16)] * 3 + [jax.ShapeDtypeStruct((g, 1, lanes), F32)]
                   + rider_shapes),
        compiler_params=_cparams(("arbitrary",)),
        name="ssm_ops",
    )(ecol, rows, b_rows, ct, d_col, *riders)
    return outs[:4], outs[4:]


SSM_SLAB_GROUPS = 128 // SSM_CH


def _chunk_transpose(xs, chunk_id):
    for k in (4, 2, 1):
        keep = (chunk_id & k) == 0
        new = list(xs)
        for i in range(len(xs)):
            if i & k == 0:
                a, b = xs[i], xs[i + k]
                new[i] = jnp.where(keep, a, pltpu.roll(b, SSM_CH * k, 1))
                new[i + k] = jnp.where(keep, pltpu.roll(a, 128 - SSM_CH * k, 1), b)
        xs = new
    return xs


def _ssm_kernel(u_ref, m_ref, w_ref, v_ref, a_ref, h0_ref, y_ref, hfin_ref, ug_scr, s_scr, hf_scr, hb_scr, yg_scr,
                *, n_chunks, ns):
    gps = SSM_SLAB_GROUPS
    nc = ns * n_chunks
    half = 2 * SSM_STATE
    chunk_id = lax.broadcasted_iota(jnp.int32, (nc, 128), 1) // SSM_CH
    for hf in range(2):
        xs = [u_ref[pl.ds(hf * gps + i, nc, stride=SSM_CHUNK), :] for i in range(gps)]
        xs = _chunk_transpose(xs, chunk_id)
        for g in range(gps):
            ug_scr[g, :, hf * 128:(hf + 1) * 128] = xs[g].astype(BF16)
    pitch = n_chunks + 8
    for g in range(gps):
        s = jnp.dot(ug_scr[g], w_ref[g], preferred_element_type=F32)
        for q in range(ns):
            s_scr[g, q * pitch:q * pitch + n_chunks, :] = s[q * n_chunks:(q + 1) * n_chunks, :half]
            s_scr[gps + g, q * pitch:q * pitch + n_chunks, :] = s[q * n_chunks:(q + 1) * n_chunks, half:]
    is_fwd = lax.broadcasted_iota(jnp.int32, (1, half), 1) < SSM_STATE
    blocks = range(2 * gps)
    a = [a_ref[:, b * half:(b + 1) * half] for b in blocks]
    h = [h0_ref[:, b * half:(b + 1) * half] for b in blocks]

    def rows(c):
        return pl.ds(c, ns, stride=pitch)

    for k in range(n_chunks):
        kb = n_chunks - 1 - k
        new_h = list(h)
        for b in blocks:
            hf_scr[b, rows(k), :] = h[b]
            hb_scr[b, rows(kb), :] = h[b]
        for g in range(gps):
            s_re = jnp.where(is_fwd, s_scr[g, rows(k), :], s_scr[g, rows(kb), :])
            s_im = jnp.where(is_fwd, s_scr[gps + g, rows(k), :], s_scr[gps + g, rows(kb), :])
            new_h[g] = a[g] * h[g] - a[gps + g] * h[gps + g] + s_re
            new_h[gps + g] = a[g] * h[gps + g] + a[gps + g] * h[g] + s_im
        h = new_h
    for b in blocks:
        hfin_ref[:, b * half:(b + 1) * half] = h[b]

    def entering(b):
        return jnp.concatenate(
            [jnp.where(is_fwd, hf_scr[b, q * pitch:q * pitch + n_chunks, :], hb_scr[b, q * pitch:q * pitch + n_chunks, :])
             for q in range(ns)], axis=0)

    for g in range(gps):
        hin = jnp.concatenate([entering(g), entering(gps + g)], axis=1).astype(BF16)
        yg_scr[g] = (jnp.dot(ug_scr[g], m_ref[g], preferred_element_type=F32)
                     + jnp.dot(hin, v_ref[g], preferred_element_type=F32))
    for hf in range(2):
        xs = [yg_scr[g, :, hf * 128:(hf + 1) * 128] for g in range(gps)]
        xs = _chunk_transpose(xs, chunk_id)
        for i in range(gps):
            y_ref[pl.ds(hf * gps + i, nc, stride=SSM_CHUNK), :] = xs[i]


def _slab_lanes(x):
    lead = x.shape[:-2]
    g = x.shape[-2]
    gps = SSM_SLAB_GROUPS
    x = x.reshape(lead + (g // gps, gps, 2, 2 * SSM_STATE))
    x = jnp.swapaxes(x, -3, -2)
    return x.reshape(lead + (g * 4 * SSM_STATE,))


def _unslab_lanes(x, g):
    lead = x.shape[:-1]
    gps = SSM_SLAB_GROUPS
    x = x.reshape(lead + (g // gps, 2, gps, 2 * SSM_STATE))
    x = jnp.swapaxes(x, -3, -2)
    return x.reshape(lead + (g, 4 * SSM_STATE))


def _ssm(u, u_col, ops, h0, *, n_seq, seq_len, seq_block, riders=()):
    m_op, w_op, v_op, a_op = ops
    n = u.shape[0]
    g = m_op.shape[0]
    d_ssm = g * SSM_CH
    slab0 = u_col // 128
    gps = SSM_SLAB_GROUPS
    lanes = SSM_LANES
    n_chunks = seq_len // SSM_CHUNK
    nc = seq_block * n_chunks
    padded = seq_block * (n_chunks + 8)
    rows = seq_block * seq_len
    wide2 = gps * 4 * SSM_STATE
    mat = pl.BlockSpec((gps, lanes, lanes), lambda j, b: (j, 0, 0))
    n_blocks = n_seq // seq_block
    rider_specs, rider_shapes = _rider_specs(riders, (g // gps) * n_blocks, lambda j, b: j * n_blocks + b)
    outs = pl.pallas_call(
        _ride_casts(functools.partial(_ssm_kernel, n_chunks=n_chunks, ns=seq_block), 6, 2, len(riders)),
        grid=(g // gps, n_blocks),
        in_specs=[pl.BlockSpec((rows, 128), lambda j, b: (b, slab0 + j)), mat, mat, mat,
                  pl.BlockSpec((1, wide2), lambda j, b: (0, j)),
                  pl.BlockSpec((seq_block, wide2), lambda j, b: (b, j))] + rider_specs,
        out_specs=[pl.BlockSpec((rows, 128), lambda j, b: (b, j)),
                   pl.BlockSpec((seq_block, wide2), lambda j, b: (b, j))] + rider_specs,
        out_shape=[jax.ShapeDtypeStruct((n, d_ssm), F32),
                   jax.ShapeDtypeStruct((n_seq, g * 4 * SSM_STATE), F32)] + rider_shapes,
        scratch_shapes=[pltpu.VMEM((gps, nc, lanes), BF16), pltpu.VMEM((2 * gps, padded, 128), F32),
                        pltpu.VMEM((2 * gps, padded, 128), F32), pltpu.VMEM((2 * gps, padded, 128), F32),
                        pltpu.VMEM((gps, nc, lanes), F32)],
        compiler_params=_cparams(("arbitrary", "arbitrary")),
        name="ssm",
    )(u, m_op, w_op, v_op, _slab_lanes(a_op[:, 0])[None], h0, *riders)
    return outs[0], outs[1], outs[2:]


def _softmax_pv(pieces, sink):
    m = sink
    for s, _ in pieces:
        m = jnp.maximum(m, jnp.max(s, axis=-1, keepdims=True))
    out = None
    for s, v in pieces:
        pv = jnp.dot(jnp.exp(s - m).astype(BF16), v, preferred_element_type=F32)
        out = pv if out is None else out + pv
    return out[:, :HEAD_DIM] / (out[:, HEAD_DIM:] + jnp.exp(sink - m))


def _qk(q, k):
    return lax.dot_general(q, k, (((1,), (1,)), ((), ())), preferred_element_type=F32)


def _with_ones(v):
    return jnp.concatenate([v, jnp.ones_like(v)], axis=1)


def _attn_ctx_kernel(sink_ref, q_ref, kv_ref, o_ref):
    d_kv = N_KV_HEADS * HEAD_DIM
    for kh in range(N_KV_HEADS):
        k = kv_ref[:, kh * HEAD_DIM:(kh + 1) * HEAD_DIM]
        v = _with_ones(kv_ref[:, d_kv + kh * HEAD_DIM:d_kv + (kh + 1) * HEAD_DIM])
        for h in range(kh * Q_PER_KV, (kh + 1) * Q_PER_KV):
            q = q_ref[:, h * HEAD_DIM:(h + 1) * HEAD_DIM]
            o = _softmax_pv([(_qk(q, k), v)], sink_ref[h])
            o_ref[:, h * HEAD_DIM:(h + 1) * HEAD_DIM] = o.astype(BF16)


def _attn_ctx(proj, sink, *, n_seq, seq_len, col, riders=()):
    d_attn = col["d_attn"]
    kv_w = 2 * col["d_kv"]
    rider_specs, rider_shapes = _rider_specs(riders, n_seq, lambda b: b)
    outs = pl.pallas_call(
        _ride_casts(_attn_ctx_kernel, 3, 1, len(riders)),
        grid=(n_seq,),
        in_specs=[pl.BlockSpec(memory_space=pltpu.SMEM),
                  pl.BlockSpec((seq_len, d_attn), lambda b: (b, col["q"] // d_attn)),
                  pl.BlockSpec((seq_len, kv_w), lambda b: (b, col["k"] // kv_w))] + rider_specs,
        out_specs=[pl.BlockSpec((seq_len, d_attn), lambda b: (b, 0))] + rider_specs,
        out_shape=[jax.ShapeDtypeStruct((n_seq * seq_len, d_attn), BF16)] + rider_shapes,
        compiler_params=_cparams(("arbitrary",)),
        name="attn_ctx",
    )(sink, proj, proj, *riders)
    return outs[0], outs[1:]


def _rope(x, cos, sin):
    reps = x.shape[1] // HEAD_DIM
    xf = x.astype(F32)
    return (xf * jnp.concatenate([cos] * reps, axis=1) + _swap32(xf) * jnp.concatenate([sin] * reps, axis=1)).astype(BF16)


def _attn_lat_kernel(sink_ref, q_ref, kvp_ref, kvc_ref, kvn_ref, ck_ref, cv_ref, cos_ref, sin_ref, o_ref, *, n_blocks):
    i = pl.program_id(1)
    d_kv = N_KV_HEADS * HEAD_DIM

    def tables(blk):
        rows = pl.ds(pl.multiple_of(blk * BLOCK, BLOCK), BLOCK)
        return cos_ref[rows, :], sin_ref[rows, :]

    q_all = _rope(q_ref[...], *tables(i))
    k_all = jnp.concatenate([_rope(kvp_ref[:, :d_kv], *tables(jnp.maximum(i - 1, 0))),
                             _rope(kvc_ref[:, :d_kv], *tables(i)),
                             _rope(kvn_ref[:, :d_kv], *tables(jnp.minimum(i + 1, n_blocks - 1)))], axis=0)
    r = lax.broadcasted_iota(jnp.int32, (BLOCK, 3 * BLOCK), 0)
    c = lax.broadcasted_iota(jnp.int32, (BLOCK, 3 * BLOCK), 1)
    cc = c & (BLOCK - 1)
    valid = (((c < BLOCK) & (cc >= r) & (i > 0)) | ((c >= BLOCK) & (c < 2 * BLOCK))
             | ((c >= 2 * BLOCK) & (cc <= r) & (i < n_blocks - 1)))
    for kh in range(N_KV_HEADS):
        ks = slice(kh * HEAD_DIM, (kh + 1) * HEAD_DIM)
        vs = slice(d_kv + kh * HEAD_DIM, d_kv + (kh + 1) * HEAD_DIM)
        k_loc = k_all[:, ks]
        k_ctx = ck_ref[:, ks].astype(BF16)
        v_loc = _with_ones(jnp.concatenate([kvp_ref[:, vs], kvc_ref[:, vs], kvn_ref[:, vs]], axis=0))
        v_ctx = _with_ones(cv_ref[:, ks].astype(BF16))
        for h in range(kh * Q_PER_KV, (kh + 1) * Q_PER_KV):
            q = q_all[:, h * HEAD_DIM:(h + 1) * HEAD_DIM]
            s_loc = jnp.where(valid, _qk(q, k_loc), NEG_INF)
            o = _softmax_pv([(s_loc, v_loc), (_qk(q, k_ctx), v_ctx)], sink_ref[h])
            o_ref[:, h * HEAD_DIM:(h + 1) * HEAD_DIM] = o.astype(BF16)


def _attn_lat(proj, cache_k, cache_v, sink, *, n_seq, seq_len, col):
    d_attn = col["d_attn"]
    kv_w = 2 * col["d_kv"]
    nb = seq_len // BLOCK
    kv_col = col["k"] // kv_w
    past, d_kv = cache_k.shape[1], cache_k.shape[2]

    def kv_spec(off):
        return pl.BlockSpec((BLOCK, kv_w), lambda b, i: (b * nb + jnp.clip(i + off, 0, nb - 1), kv_col))
    cache_spec = pl.BlockSpec((None, past, d_kv), lambda b, i: (b, 0, 0))
    table_spec = pl.BlockSpec((seq_len, HEAD_DIM), lambda b, i: (0, 0))
    cos, sin = _rope_tables(seq_len)
    return pl.pallas_call(
        functools.partial(_attn_lat_kernel, n_blocks=nb),
        grid=(n_seq, nb),
        in_specs=[pl.BlockSpec(memory_space=pltpu.SMEM),
                  pl.BlockSpec((BLOCK, d_attn), lambda b, i: (b * nb + i, col["q"] // d_attn)),
                  kv_spec(-1), kv_spec(0), kv_spec(1), cache_spec, cache_spec, table_spec, table_spec],
        out_specs=pl.BlockSpec((BLOCK, d_attn), lambda b, i: (b * nb + i, 0)),
        out_shape=jax.ShapeDtypeStruct((n_seq * seq_len, d_attn), BF16),
        compiler_params=_cparams(("arbitrary", "arbitrary")),
        name="attn_lat",
    )(sink, proj, proj, proj, proj, cache_k, cache_v, cos, sin)


def _gelu_tanh(x):
    return 0.5 * x * (1.0 + jnp.tanh(math.sqrt(2.0 / math.pi) * (x + 0.044715 * (x * x * x))))


def _mixer_out_kernel(x_ref, y_ref, o_ref, gs_ref, ga_ref, mod_ref, g_ref, wglu_ref, wso_ref, wao_ref, wout_ref,
                      x1_ref, xm2_ref):
    z = _gelu_tanh(y_ref[...].astype(F32))
    z = z * _sigmoid(jnp.dot(z.astype(BF16), wglu_ref[...], preferred_element_type=F32))
    s_br = jnp.dot(z.astype(BF16), wso_ref[...], preferred_element_type=F32)
    a_br = jnp.dot(o_ref[...], wao_ref[...], preferred_element_type=F32)
    merged = _sigmoid(gs_ref[...].astype(F32)) * s_br + _sigmoid(ga_ref[...].astype(F32)) * a_br
    out = jnp.dot(merged.astype(BF16), wout_ref[...], preferred_element_type=F32)
    x1 = x_ref[...] + mod_ref[2:3, :] * out
    x1_ref[...] = x1
    xm2_ref[...] = _rms_modulate(x1, g_ref[...], mod_ref[4:5, :], mod_ref[3:4, :]).astype(BF16)


def _mixer_out(x, y, o, proj, mod, norm_g, w_glu, w_ssm_o, w_attn_o, w_out, *, tm, tiles_per_seq, col):
    n, d = x.shape
    d_ssm = y.shape[1]
    d_attn = o.shape[1]
    seq_of = (lambda i: i // tiles_per_seq) if mod.shape[0] > 1 else (lambda i: 0)

    def resident(shape):
        return pl.BlockSpec(shape, lambda i: (0, 0), pipeline_mode=pl.Buffered(1))
    return pl.pallas_call(
        _mixer_out_kernel,
        grid=(n // tm,),
        in_specs=[pl.BlockSpec((tm, d), lambda i: (i, 0)),
                  pl.BlockSpec((tm, d_ssm), lambda i: (i, 0)),
                  pl.BlockSpec((tm, d_attn), lambda i: (i, 0)),
                  pl.BlockSpec((tm, d), lambda i: (i, col["gs"] // d)),
                  pl.BlockSpec((tm, d), lambda i: (i, col["ga"] // d)),
                  pl.BlockSpec((None, 6, d), lambda i: (seq_of(i), 0, 0)),
                  pl.BlockSpec((1, d), lambda i: (0, 0)),
                  resident(w_glu.shape), resident(w_ssm_o.shape), resident(w_attn_o.shape), resident(w_out.shape)],
        out_specs=[pl.BlockSpec((tm, d), lambda i: (i, 0)), pl.BlockSpec((tm, d), lambda i: (i, 0))],
        out_shape=[jax.ShapeDtypeStruct((n, d), F32), jax.ShapeDtypeStruct((n, d), BF16)],
        compiler_params=_cparams(("arbitrary",)),
        name="mixer_out",
    )(x, y, o, proj, proj, mod, norm_g.reshape(1, d), w_glu, w_ssm_o, w_attn_o, w_out)


FFN_TM = 1024
FFN_TF = 512
FFN_TN = 256


def _ffn_kernel(xm_ref, wa_ref, wb_ref, cw_ref, cb_ref, wd_ref, x1_ref, mod_ref, g_ref, o_ref, act_scr, raw_scr,
                *, seq_len, nf, nn, d_ff):
    j = pl.program_id(1)
    tm = xm_ref.shape[0]
    tf = wa_ref.shape[1]
    tn = wd_ref.shape[1]
    ch = tf // 2
    pos = lax.broadcasted_iota(jnp.int32, (tm, 1), 0) & (seq_len - 1)
    has_prev = pos != 0
    has_next = pos != seq_len - 1

    def lanes(off):
        return slice(off, off + ch) if isinstance(off, int) else pl.ds(pl.multiple_of(off, ch), ch)

    def gate(ha, hb, col0):
        def conv(h, off):
            cols = lanes(off)
            cw = cw_ref[:, cols]
            h_prev = jnp.where(has_prev, pltpu.roll(h, 1, 0), 0.0)
            h_next = jnp.where(has_next, pltpu.roll(h, tm - 1, 0), 0.0)
            return cw[0:1] * h_prev + cw[1:2] * h + cw[2:3] * h_next + cb_ref[:, cols]

        a = conv(ha, col0)
        b = conv(hb, d_ff + col0)
        return ((a * _sigmoid(a)) * b).astype(BF16)

    def act_cols(col0):
        return lanes(tf + col0)

    @pl.when(j == 0)
    def _():
        raw_scr[...] = jnp.zeros_like(raw_scr)

    @pl.when(j < nf)
    def _():
        xm = xm_ref[...]
        ha0 = jnp.dot(xm, wa_ref[:, :ch], preferred_element_type=F32)
        hb0 = jnp.dot(xm, wb_ref[:, :ch], preferred_element_type=F32)
        carried = (j - 1) * tf + ch
        act_scr[:, act_cols(carried)] = gate(raw_scr[0], raw_scr[1], jnp.maximum(carried, 0))
        ha1 = jnp.dot(xm, wa_ref[:, ch:], preferred_element_type=F32)
        hb1 = jnp.dot(xm, wb_ref[:, ch:], preferred_element_type=F32)
        act_scr[:, act_cols(j * tf)] = gate(ha0, hb0, j * tf)
        raw_scr[0] = ha1
        raw_scr[1] = hb1

    @pl.when(j == nf)
    def _():
        last = (nf - 1) * tf + ch
        act_scr[:, act_cols(last)] = gate(raw_scr[0], raw_scr[1], last)

    @pl.when(j >= nf)
    def _():
        cols = pl.ds(pl.multiple_of((j - nf) * tn, tn), tn)
        ffn = jnp.dot(act_scr[:, tf:], wd_ref[...], preferred_element_type=F32)
        o_ref[:, cols] = x1_ref[...] + mod_ref[5:6, cols] * ffn

    @pl.when(j == nf + nn - 1)
    def _():
        x2 = o_ref[...]
        ms = jnp.mean(x2 * x2, axis=-1, keepdims=True)
        o_ref[...] = x2 * lax.rsqrt(ms + EPS) * g_ref[...]


def _ffn(x1, xm2, mod, w_up, conv_w, conv_b, w_down, final_g, *, tm, seq_len):
    n, d = x1.shape
    d_ff = w_down.shape[0]
    tf, tn = FFN_TF, FFN_TN
    nf, nn = d_ff // tf, d // tn
    assert tm % seq_len == 0 and seq_len & (seq_len - 1) == 0 and d_ff % tf == 0 and (tf // 2) % V7X_MXU_WIDTH == 0
    seqs_per_tile = tm // seq_len
    seq_of = (lambda i: i * seqs_per_tile) if mod.shape[0] > 1 else (lambda i: 0)
    assert mod.shape[0] == 1 or seqs_per_tile == 1
    cb = conv_b.reshape(1, 2 * d_ff)

    def up(j):
        return jnp.minimum(j, nf - 1)

    def down(j):
        return jnp.maximum(j - nf, 0)
    return pl.pallas_call(
        functools.partial(_ffn_kernel, seq_len=seq_len, nf=nf, nn=nn, d_ff=d_ff),
        grid=(n // tm, nf + nn),
        in_specs=[pl.BlockSpec((tm, d), lambda i, j: (i, 0), pipeline_mode=pl.Buffered(1)),
                  pl.BlockSpec((d, tf), lambda i, j: (0, up(j))),
                  pl.BlockSpec((d, tf), lambda i, j: (0, nf + up(j))),
                  pl.BlockSpec((3, 2 * d_ff), lambda i, j: (0, 0)),
                  pl.BlockSpec((1, 2 * d_ff), lambda i, j: (0, 0)),
                  pl.BlockSpec((d_ff, tn), lambda i, j: (0, down(j))),
                  pl.BlockSpec((tm, tn), lambda i, j: (i, down(j))),
                  pl.BlockSpec((None, 6, d), lambda i, j: (seq_of(i), 0, 0)),
                  pl.BlockSpec((1, d), lambda i, j: (0, 0))],
        out_specs=pl.BlockSpec((tm, d), lambda i, j: (i, 0)),
        out_shape=jax.ShapeDtypeStruct((n, d), F32),
        scratch_shapes=[pltpu.VMEM((tm, tf + d_ff), BF16), pltpu.VMEM((2, tm, tf // 2), F32)],
        compiler_params=_cparams(("arbitrary", "arbitrary")),
        name="ffn",
    )(xm2, w_up, w_up, conv_w, cb, w_down, x1, mod, final_g.reshape(1, d))


def _rope_tables(seq_len):
    rows = seq_len // GRID_W
    row = jnp.repeat(jnp.arange(rows, dtype=F32), GRID_W)
    colp = jnp.tile(jnp.arange(GRID_W, dtype=F32), rows)
    n_freq = HEAD_DIM // 4
    inv = ROPE_THETA ** (-jnp.arange(n_freq, dtype=F32) / n_freq)
    ang_r = row[:, None] * inv[None, :]
    ang_c = colp[:, None] * inv[None, :]
    cos = jnp.concatenate([jnp.cos(ang_r), jnp.cos(ang_r), jnp.cos(ang_c), jnp.cos(ang_c)], axis=1)
    sin = jnp.concatenate([-jnp.sin(ang_r), jnp.sin(ang_r), -jnp.sin(ang_c), jnp.sin(ang_c)], axis=1)
    return cos, sin


PROJ_TM = 1024
MIXER_TM = 256


def kernel(x_prompt, x_sample, cache_k, cache_v, state_ssm_re, state_ssm_im, c, c_ctx, norm_mix_g, norm_ffn_g,
           w_mod, b_mod, w_in, ssm_lambda_re, ssm_lambda_im, ssm_log_dt, ssm_b_re, ssm_b_im, ssm_c_re, ssm_c_im,
           ssm_d, w_glu, attn_sink, w_ssm_o, w_attn_o, w_out, w_up, conv_w, conv_b, w_down, final_norm_g):
    batch, seq, d = x_prompt.shape
    dec_batch, dec_seq, _ = x_sample.shape
    depth = w_in.shape[0]
    assert depth == 1, "final norm is fused into the (single) layer's ffn kernel"
    d_ssm = w_glu.shape[1]
    d_attn = N_HEADS * HEAD_DIM
    d_kv = N_KV_HEADS * HEAD_DIM
    groups = d_ssm // SSM_CH
    assert w_in.shape[2] == d_ssm + d_attn + 2 * d_kv + 2 * d
    assert ssm_lambda_re.shape[2:] == (groups, SSM_STATE) and dec_batch <= 8 - 1
    l = 0

    col = {"gs": 0, "ga": d, "q": 2 * d, "u": 2 * d + d_attn, "k": 2 * d + d_attn + d_ssm,
           "d_ssm": d_ssm, "d_attn": d_attn, "d_kv": d_kv}
    xp = x_prompt.reshape(batch * seq, d)
    xs = x_sample.reshape(dec_batch * dec_seq, d)

    cond = jnp.concatenate([c_ctx[None], c, jnp.zeros((8 - 1 - dec_batch, d), F32)], axis=0)
    mod = _modulation(cond, w_mod[l], b_mod[l]).reshape(8, 6, d)
    mod_p, mod_s = mod[0:1], mod[1:1 + dec_batch]

    ssm_ops, (w_in_b,) = _ssm_operators(ssm_lambda_re[l], ssm_lambda_im[l], ssm_log_dt[l], ssm_b_re[l], ssm_b_im[l],
                                        ssm_c_re[l], ssm_c_im[l], ssm_d[l], riders=(w_in[l],))
    proj_p, tail_p, k_raw, v_raw = _proj(xp, mod_p, norm_mix_g[l], w_in_b, tm=PROJ_TM,
                                         tiles_per_seq=max(seq // PROJ_TM, 1), col=col)
    proj_s, tail_s, _, _ = _proj(xs, mod_s, norm_mix_g[l], w_in_b, tm=PROJ_TM,
                                 tiles_per_seq=max(dec_seq // PROJ_TM, 1), col=col)

    def lanes(s):
        return s.transpose(0, 2, 1, 3).reshape(dec_batch, groups, 2 * SSM_STATE)
    h0_p = jnp.zeros((batch, groups * 4 * SSM_STATE), F32)
    h0_s = _slab_lanes(jnp.concatenate([lanes(state_ssm_re[:, l]), lanes(state_ssm_im[:, l])], axis=-1))
    y_ssm_p, h_fin, (w_up_b,) = _ssm(tail_p, 0, ssm_ops, h0_p, n_seq=batch, seq_len=seq,
                                     seq_block=min(batch, SSM_SEQ_BLOCK), riders=(w_up[l],))
    y_ssm_s, _, (w_down_b,) = _ssm(tail_s, 0, ssm_ops, h0_s, n_seq=dec_batch, seq_len=dec_seq,
                                   seq_block=min(dec_batch, SSM_SEQ_BLOCK), riders=(w_down[l],))

    sink = attn_sink[l]
    o_p, mixer_w = _attn_ctx(proj_p, sink, n_seq=batch, seq_len=seq, col=col,
                             riders=(w_glu[l], w_ssm_o[l], w_attn_o[l], w_out[l]))
    o_s = _attn_lat(proj_s, cache_k[:, l].reshape(dec_batch, -1, d_kv), cache_v[:, l].reshape(dec_batch, -1, d_kv),
                    sink, n_seq=dec_batch, seq_len=dec_seq, col=col)

    x1_p, xm2_p = _mixer_out(xp, y_ssm_p, o_p, proj_p, mod_p, norm_ffn_g[l], *mixer_w, tm=MIXER_TM,
                             tiles_per_seq=max(seq // MIXER_TM, 1), col=col)
    x1_s, xm2_s = _mixer_out(xs, y_ssm_s, o_s, proj_s, mod_s, norm_ffn_g[l], *mixer_w, tm=MIXER_TM,
                             tiles_per_seq=max(dec_seq // MIXER_TM, 1), col=col)
    y_p = _ffn(x1_p, xm2_p, mod_p, w_up_b, conv_w[l], conv_b[l], w_down_b, final_norm_g, tm=FFN_TM, seq_len=seq)
    y_s = _ffn(x1_s, xm2_s, mod_s, w_up_b, conv_w[l], conv_b[l], w_down_b, final_norm_g, tm=FFN_TM, seq_len=dec_seq)

    new_k = k_raw.reshape(batch, 1, seq, N_KV_HEADS, HEAD_DIM)
    new_v = v_raw.reshape(batch, 1, seq, N_KV_HEADS, HEAD_DIM)

    def unlanes(hl):
        return hl.reshape(batch, groups, 2, SSM_STATE).transpose(0, 2, 1, 3)[:, None]
    h_fin = _unslab_lanes(h_fin, groups)
    new_re = unlanes(h_fin[:, :, :2 * SSM_STATE])
    new_im = unlanes(h_fin[:, :, 2 * SSM_STATE:])
    return (y_p.reshape(batch, seq, d), y_s.reshape(dec_batch, dec_seq, d), new_k, new_v, new_re, new_im)
```

```python
import functools
import math

import jax
import jax.numpy as jnp
from jax import lax
from jax.experimental import pallas as pl
from jax.experimental.pallas import tpu as pltpu

F32 = jnp.float32
BF16 = jnp.bfloat16

GRID_W = 64
SSM_CH = 16
SSM_STATE = 64
N_HEADS = 8
N_KV_HEADS = 2
HEAD_DIM = 128
Q_PER_KV = N_HEADS // N_KV_HEADS
WINDOW = 128
BLOCK = 128
ROPE_THETA = 10000.0
EPS = 1e-6
NEG_INF = -1e30

SSM_CHUNK = 16
SSM_LANES = SSM_CHUNK * SSM_CH
SSM_SEQ_BLOCK = 16

V7X_VMEM_LIMIT_BYTES = 56 * 1024 * 1024
V7X_MXU_WIDTH = 256


def _cparams(semantics):
    return pltpu.CompilerParams(dimension_semantics=semantics, vmem_limit_bytes=V7X_VMEM_LIMIT_BYTES)


def _sigmoid(x):
    return 1.0 / (1.0 + jnp.exp(-x))


def _ride_casts(body, n_in, n_out, n_riders):
    def kernel(*refs):
        ins = refs[:n_in]
        rider_ins = refs[n_in:n_in + n_riders]
        outs = refs[n_in + n_riders:n_in + n_riders + n_out]
        rider_outs = refs[n_in + n_riders + n_out:n_in + 2 * n_riders + n_out]
        scratch = refs[n_in + 2 * n_riders + n_out:]
        for src, dst in zip(rider_ins, rider_outs):
            dst[...] = src[...].astype(BF16)
        body(*ins, *outs, *scratch)
    return kernel


def _rider_specs(weights, n_steps, step_of):
    specs, shapes = [], []
    for w in weights:
        r, c = w.shape
        assert r % (16 * n_steps) == 0
        specs.append(pl.BlockSpec((r // n_steps, c), lambda *idx: (step_of(*idx), 0)))
        shapes.append(jax.ShapeDtypeStruct((r, c), BF16))
    return specs, shapes


def _rms_modulate(x, g, scale, shift):
    ms = jnp.mean(x * x, axis=-1, keepdims=True)
    return (x * lax.rsqrt(ms + EPS) * g) * (1.0 + scale) + shift


MOD_K_CHUNK = 256


def _mod_kernel(c_ref, w_ref, b_ref, o_ref):
    c = c_ref[...]
    a = (c * _sigmoid(c)).astype(BF16)
    acc = b_ref[...]
    for k in range(0, w_ref.shape[0], MOD_K_CHUNK):
        acc = acc + jnp.dot(a[:, k:k + MOD_K_CHUNK], w_ref[k:k + MOD_K_CHUNK, :].astype(BF16),
                            preferred_element_type=F32)
    o_ref[...] = acc


PROJ_TN = 512
PROJ_NORM_ROWS = 16


def _swap32(x):
    n = x.shape[-1]
    lane = lax.broadcasted_iota(jnp.int32, x.shape, x.ndim - 1)
    return jnp.where((lane & 63) < 32, pltpu.roll(x, n - 32, x.ndim - 1), pltpu.roll(x, 32, x.ndim - 1))


def _proj_kernel(x_ref, mod_ref, g_ref, w_ref, o_ref, tail_ref, k_ref, v_ref, xm_scr, *, q_tiles):
    j = pl.program_id(1)

    @pl.when(j == 0)
    def _():
        gain = g_ref[...] * (1.0 + mod_ref[1:2, :])
        shift = mod_ref[0:1, :]

        def block(c, carry):
            rows = pl.ds(pl.multiple_of(c * PROJ_NORM_ROWS, PROJ_NORM_ROWS), PROJ_NORM_ROWS)
            x = x_ref[rows, :]
            ms = jnp.mean(x * x, axis=-1, keepdims=True)
            xm_scr[rows, :] = (x * lax.rsqrt(ms + EPS) * gain + shift).astype(BF16)
            return carry

        lax.fori_loop(0, x_ref.shape[0] // PROJ_NORM_ROWS, block, 0, unroll=8)

    acc = jnp.dot(xm_scr[...], w_ref[...], preferred_element_type=F32)
    scale = jnp.where((j >= q_tiles[0]) & (j < q_tiles[1]), HEAD_DIM ** -0.5, 1.0)
    o_ref[...] = (acc * scale).astype(BF16)
    tail_ref[...] = acc
    d_kv = k_ref.shape[1]
    k_ref[...] = acc[:, :d_kv]
    v_ref[...] = acc[:, d_kv:]


def _proj(x, mod, norm_g, w_in, *, tm, tiles_per_seq, col):
    n, d = x.shape
    cols = w_in.shape[1]
    tn = PROJ_TN
    n_tiles = cols // tn
    d_kv = col["d_kv"]
    n_gate, n_q, n_u = col["q"] // tn, col["d_attn"] // tn, col["d_ssm"] // tn
    q_tiles = (n_gate, n_gate + n_q)
    tail0 = col["u"] // tn
    n_tail = n_tiles - tail0
    assert 2 * d_kv == tn and col["u"] == col["q"] + col["d_attn"] and col["k"] == cols - tn

    def src_tile(j):
        return jnp.where(j < n_gate, j + (n_u + n_q + 1),
                         jnp.where(j < n_gate + n_q, j - n_gate + n_u,
                                   jnp.where(j < n_gate + n_q + n_u, j - (n_gate + n_q), n_u + n_q)))
    seq_of = (lambda i: i // tiles_per_seq) if mod.shape[0] > 1 else (lambda i: 0)
    return pl.pallas_call(
        functools.partial(_proj_kernel, q_tiles=q_tiles),
        grid=(n // tm, n_tiles),
        in_specs=[pl.BlockSpec((tm, d), lambda i, j: (i, 0)),
                  pl.BlockSpec((None, 6, d), lambda i, j: (seq_of(i), 0, 0)),
                  pl.BlockSpec((1, d), lambda i, j: (0, 0)),
                  pl.BlockSpec((d, tn), lambda i, j: (0, src_tile(j)))],
        out_specs=[pl.BlockSpec((tm, tn), lambda i, j: (i, j)),
                   pl.BlockSpec((tm, tn), lambda i, j: (i, jnp.clip(j - tail0, 0, n_tail - 1))),
                   pl.BlockSpec((tm, d_kv), lambda i, j: (i, 0)),
                   pl.BlockSpec((tm, d_kv), lambda i, j: (i, 0))],
        out_shape=[jax.ShapeDtypeStruct((n, cols), BF16), jax.ShapeDtypeStruct((n, n_tail * tn), F32),
                   jax.ShapeDtypeStruct((n, d_kv), F32), jax.ShapeDtypeStruct((n, d_kv), F32)],
        scratch_shapes=[pltpu.VMEM((tm, d), BF16)],
        compiler_params=_cparams(("arbitrary", "arbitrary")),
        name="proj",
    )(x, mod, norm_g.reshape(1, d), w_in)


def _cmul(ar, ai, br, bi):
    return ar * br - ai * bi, ar * bi + ai * br


def _cpow(base_r, base_i, n, bits):
    res_r = jnp.ones_like(base_r)
    res_i = jnp.zeros_like(base_r)
    for bit in range(bits):
        nr, ni = _cmul(res_r, res_i, base_r, base_i)
        take = ((n >> bit) & 1) == 1
        res_r = jnp.where(take, nr, res_r)
        res_i = jnp.where(take, ni, res_i)
        if bit + 1 < bits:
            base_r, base_i = _cmul(base_r, base_i, base_r, base_i)
    return res_r, res_i


def _ssm_disc_kernel(lre_ref, lim_ref, ldt_ref, ere_ref, eim_ref, qre_ref, qim_ref):
    lre = lre_ref[...]
    lim = lim_ref[...]
    dt = jnp.exp(ldt_ref[...])
    mag = jnp.exp(lre * dt)
    e_re = mag * jnp.cos(lim * dt)
    e_im = mag * jnp.sin(lim * dt)
    n_re = e_re - 1.0
    den = lre * lre + lim * lim
    ere_ref[...] = e_re
    eim_ref[...] = e_im
    qre_ref[...] = (n_re * lre + e_im * lim) / den
    qim_ref[...] = (e_im * lre - n_re * lim) / den


def _ssm_disc(lam_re, lam_im, log_dt):
    two, g, p = lam_re.shape
    shp = jax.ShapeDtypeStruct((two * g, p), F32)
    outs = pl.pallas_call(_ssm_disc_kernel, out_shape=[shp] * 4, name="ssm_disc")(
        lam_re.reshape(two * g, p), lam_im.reshape(two * g, p), log_dt.reshape(two * g, 1))
    return [o.reshape(two, g, p) for o in outs]


SSM_OPS_GROUPS = 4


def _ssm_ops_mod_kernel(ecol_ref, rows_ref, b_ref, ct_ref, d_ref, c_ref, wm_ref, bm_ref,
                        m_ref, w_ref, v_ref, a_ref, mod_ref):
    _mod_kernel(c_ref, wm_ref, bm_ref, mod_ref)
    _ssm_ops_kernel(ecol_ref, rows_ref, b_ref, ct_ref, d_ref, m_ref, w_ref, v_ref, a_ref)


def _ssm_ops_kernel(ecol_ref, rows_ref, b_ref, ct_ref, d_ref, m_ref, w_ref, v_ref, a_ref):
    def one(gi, carry):
        _ssm_ops_group(ecol_ref.at[gi], rows_ref.at[gi], b_ref.at[gi], ct_ref.at[gi], d_ref.at[gi],
                       m_ref.at[gi], w_ref.at[gi], v_ref.at[gi], a_ref.at[gi])
        return carry

    lax.fori_loop(0, ecol_ref.shape[0], one, 0)


def _ssm_ops_group(ecol_ref, rows_ref, b_ref, ct_ref, d_ref, m_ref, w_ref, v_ref, a_ref):
    L, ch, p = SSM_CHUNK, SSM_CH, SSM_STATE
    lanes = SSM_LANES
    half = 2 * p
    lane_c = lax.broadcasted_iota(jnp.int32, (p, lanes), 1)
    lag = lane_c // ch

    def col(k):
        return jnp.broadcast_to(ecol_ref[k], (p, lanes))

    f0 = _cpow(col(0), col(1), lag, 4)
    b0 = _cpow(col(2), col(3), (L - 1) - lag, 4)
    f1 = _cmul(f0[0], f0[1], col(0), col(1))
    b1 = _cmul(b0[0], b0[1], col(2), col(3))
    def ct(k):
        return jnp.tile(ct_ref[k], (1, L))

    cf = (ct(0), ct(1))
    cb = (ct(2), ct(3))
    yf0 = _cmul(*f0, *cf)
    yb0 = _cmul(*b0, *cb)
    yf1 = _cmul(*f1, *cf)
    yb1 = _cmul(*b1, *cb)
    v_ref[...] = jnp.concatenate([yf1[0], yb1[0], -yf1[1], -yb1[1]], axis=0).astype(BF16)

    lane_r = lax.broadcasted_iota(jnp.int32, (ch, lanes), 1)
    sgn = jnp.where(lane_r < half, -1.0, 1.0)
    b1raw = b_ref[...]
    b2raw = pltpu.roll(b1raw, half, 1) * sgn
    bb1 = rows_ref[2:3, :] * b1raw + rows_ref[3:4, :] * b2raw
    bb2 = pltpu.roll(bb1, half, 1) * sgn

    zero = jnp.zeros((p, lanes), F32)
    hi = lax.Precision.HIGHEST
    r0f = jnp.dot(bb1, jnp.concatenate([yf0[0], zero, -yf0[1], zero], axis=0), precision=hi,
                  preferred_element_type=F32)
    r0b = jnp.dot(bb1, jnp.concatenate([zero, yb0[0], zero, -yb0[1]], axis=0), precision=hi,
                  preferred_element_type=F32)
    row_r = lax.broadcasted_iota(jnp.int32, (ch, lanes), 0)
    d_col = d_ref[...]
    for s in range(L):
        blk = jnp.where(lane_r == ch * s + row_r, d_col, 0.0)
        fwd = r0f if s == 0 else jnp.where(lane_r >= ch * s, pltpu.roll(r0f, ch * s, 1), 0.0)
        k = ch * (L - 1 - s)
        bwd = r0b if k == 0 else jnp.where(lane_r < lanes - k, pltpu.roll(r0b, lanes - k, 1), 0.0)
        m_ref[s * ch:(s + 1) * ch, :] = (blk + fwd + bwd).astype(BF16)

    is_f = (lane_r & (half - 1)) < p
    n_row = jnp.where(is_f, (L - 1) - row_r, row_r)
    er, ei = _cpow(jnp.broadcast_to(rows_ref[0:1, :], (L, lanes)), jnp.broadcast_to(rows_ref[1:2, :], (L, lanes)),
                   n_row, 4)
    for s in range(L):
        w_ref[s * ch:(s + 1) * ch, :] = (er[s:s + 1, :] * bb1 + ei[s:s + 1, :] * bb2).astype(BF16)

    ar, ai = rows_ref[0:1, :], rows_ref[1:2, :]
    for _ in range(4):
        ar, ai = _cmul(ar, ai, ar, ai)
    lane_1 = lax.broadcasted_iota(jnp.int32, (1, lanes), 1)
    a_ref[...] = jnp.where(lane_1 < half, ar, ai)


def _ssm_operators_and_mod(lam_re, lam_im, log_dt, b_re, b_im, c_re, c_im, ssm_d, cond, w_mod, b_mod, riders=()):
    two, g, p = lam_re.shape
    ch = b_re.shape[-1]
    lanes = SSM_LANES
    assert (two, p, ch, SSM_CHUNK * ch) == (2, SSM_STATE, SSM_CH, lanes)
    e_re, e_im, q_re, q_im = _ssm_disc(lam_re, lam_im, log_dt)
    ecol = jnp.stack([e_re[0], e_im[0], e_re[1], e_im[1]], axis=1)[..., None]

    def fbfb(x):
        return jnp.concatenate([x[0], x[1], x[0], x[1]], axis=-1)
    rows = jnp.stack([fbfb(e_re), fbfb(e_im), fbfb(q_re), fbfb(q_im)], axis=1)

    def bt(x):
        return jnp.swapaxes(x, -1, -2)
    b_rows = jnp.concatenate([bt(b_re[0]), bt(b_re[1]), bt(b_im[0]), bt(b_im[1])], axis=-1)

    ct = jnp.stack([bt(c_re[0]), bt(c_im[0]), bt(c_re[1]), bt(c_im[1])], axis=1)
    d_col = ssm_d.reshape(g, ch, 1)
    gb = SSM_OPS_GROUPS
    mat = pl.BlockSpec((gb, lanes, lanes), lambda i: (i, 0, 0))
    steps = g // gb
    rider_specs, rider_shapes = _rider_specs(riders, steps, lambda i: i)
    c_rows, d_model = cond.shape
    n_mod = w_mod.shape[1]
    tn = n_mod // steps
    assert n_mod % steps == 0 and tn % 128 == 0
    outs = pl.pallas_call(
        _ride_casts(_ssm_ops_mod_kernel, 8, 5, len(riders)),
        grid=(steps,),
        in_specs=[pl.BlockSpec((gb, 4, p, 1), lambda i: (i, 0, 0, 0)),
                  pl.BlockSpec((gb, 4, lanes), lambda i: (i, 0, 0)),
                  pl.BlockSpec((gb, ch, lanes), lambda i: (i, 0, 0)),
                  pl.BlockSpec((gb, 4, p, ch), lambda i: (i, 0, 0, 0)),
                  pl.BlockSpec((gb, ch, 1), lambda i: (i, 0, 0)),
                  pl.BlockSpec((c_rows, d_model), lambda i: (0, 0)),
                  pl.BlockSpec((d_model, tn), lambda i: (0, i)),
                  pl.BlockSpec((1, tn), lambda i: (0, i))] + rider_specs,
        out_specs=[mat, mat, mat, pl.BlockSpec((gb, 1, lanes), lambda i: (i, 0, 0)),
                   pl.BlockSpec((c_rows, tn), lambda i: (0, i))] + rider_specs,
        out_shape=([jax.ShapeDtypeStruct((g, lanes, lanes), BF16)] * 3 + [jax.ShapeDtypeStruct((g, 1, lanes), F32),
                                                                          jax.ShapeDtypeStruct((c_rows, n_mod), F32)]
                   + rider_shapes),
        compiler_params=_cparams(("arbitrary",)),
        name="ssm_ops_mod",
    )(ecol, rows, b_rows, ct, d_col, cond, w_mod, b_mod.reshape(1, n_mod), *riders)
    return outs[:4], outs[4], outs[5:]


SSM_SLAB_GROUPS = 128 // SSM_CH


def _chunk_transpose(xs, chunk_id):
    for k in (4, 2, 1):
        keep = (chunk_id & k) == 0
        new = list(xs)
        for i in range(len(xs)):
            if i & k == 0:
                a, b = xs[i], xs[i + k]
                new[i] = jnp.where(keep, a, pltpu.roll(b, SSM_CH * k, 1))
                new[i + k] = jnp.where(keep, pltpu.roll(a, 128 - SSM_CH * k, 1), b)
        xs = new
    return xs


def _ssm_kernel(u_ref, m_ref, w_ref, v_ref, a_ref, h0_ref, y_ref, hfin_ref, ug_scr, s_scr, hf_scr, hb_scr, yg_scr,
                *, n_chunks, ns):
    gps = SSM_SLAB_GROUPS
    nc = ns * n_chunks
    half = 2 * SSM_STATE
    chunk_id = lax.broadcasted_iota(jnp.int32, (nc, 128), 1) // SSM_CH
    for hf in range(2):
        xs = [u_ref[pl.ds(hf * gps + i, nc, stride=SSM_CHUNK), :] for i in range(gps)]
        xs = _chunk_transpose(xs, chunk_id)
        for g in range(gps):
            ug_scr[g, :, hf * 128:(hf + 1) * 128] = xs[g].astype(BF16)
    pitch = n_chunks + 8
    for g in range(gps):
        s = jnp.dot(ug_scr[g], w_ref[g], preferred_element_type=F32)
        for q in range(ns):
            s_scr[g, q * pitch:q * pitch + n_chunks, :] = s[q * n_chunks:(q + 1) * n_chunks, :half]
            s_scr[gps + g, q * pitch:q * pitch + n_chunks, :] = s[q * n_chunks:(q + 1) * n_chunks, half:]
    is_fwd = lax.broadcasted_iota(jnp.int32, (1, half), 1) < SSM_STATE
    blocks = range(2 * gps)
    a = [a_ref[:, b * half:(b + 1) * half] for b in blocks]
    h = [h0_ref[:, b * half:(b + 1) * half] for b in blocks]

    def rows(c):
        return pl.ds(c, ns, stride=pitch)

    for k in range(n_chunks):
        kb = n_chunks - 1 - k
        new_h = list(h)
        for b in blocks:
            hf_scr[b, rows(k), :] = h[b]
            hb_scr[b, rows(kb), :] = h[b]
        for g in range(gps):
            s_re = jnp.where(is_fwd, s_scr[g, rows(k), :], s_scr[g, rows(kb), :])
            s_im = jnp.where(is_fwd, s_scr[gps + g, rows(k), :], s_scr[gps + g, rows(kb), :])
            new_h[g] = a[g] * h[g] - a[gps + g] * h[gps + g] + s_re
            new_h[gps + g] = a[g] * h[gps + g] + a[gps + g] * h[g] + s_im
        h = new_h
    for b in blocks:
        hfin_ref[:, b * half:(b + 1) * half] = h[b]

    def entering(b):
        return jnp.concatenate(
            [jnp.where(is_fwd, hf_scr[b, q * pitch:q * pitch + n_chunks, :], hb_scr[b, q * pitch:q * pitch + n_chunks, :])
             for q in range(ns)], axis=0)

    for g in range(gps):
        hin = jnp.concatenate([entering(g), entering(gps + g)], axis=1).astype(BF16)
        yg_scr[g] = (jnp.dot(ug_scr[g], m_ref[g], preferred_element_type=F32)
                     + jnp.dot(hin, v_ref[g], preferred_element_type=F32))
    for hf in range(2):
        xs = [yg_scr[g, :, hf * 128:(hf + 1) * 128] for g in range(gps)]
        xs = _chunk_transpose(xs, chunk_id)
        for i in range(gps):
            y_ref[pl.ds(hf * gps + i, nc, stride=SSM_CHUNK), :] = xs[i]


def _slab_lanes(x):
    lead = x.shape[:-2]
    g = x.shape[-2]
    gps = SSM_SLAB_GROUPS
    x = x.reshape(lead + (g // gps, gps, 2, 2 * SSM_STATE))
    x = jnp.swapaxes(x, -3, -2)
    return x.reshape(lead + (g * 4 * SSM_STATE,))


def _unslab_lanes(x, g):
    lead = x.shape[:-1]
    gps = SSM_SLAB_GROUPS
    x = x.reshape(lead + (g // gps, 2, gps, 2 * SSM_STATE))
    x = jnp.swapaxes(x, -3, -2)
    return x.reshape(lead + (g, 4 * SSM_STATE))


def _ssm(u, u_col, ops, h0, *, n_seq, seq_len, seq_block, riders=()):
    m_op, w_op, v_op, a_op = ops
    n = u.shape[0]
    g = m_op.shape[0]
    d_ssm = g * SSM_CH
    slab0 = u_col // 128
    gps = SSM_SLAB_GROUPS
    lanes = SSM_LANES
    n_chunks = seq_len // SSM_CHUNK
    nc = seq_block * n_chunks
    padded = seq_block * (n_chunks + 8)
    rows = seq_block * seq_len
    wide2 = gps * 4 * SSM_STATE
    mat = pl.BlockSpec((gps, lanes, lanes), lambda j, b: (j, 0, 0))
    n_blocks = n_seq // seq_block
    rider_specs, rider_shapes = _rider_specs(riders, (g // gps) * n_blocks, lambda j, b: j * n_blocks + b)
    outs = pl.pallas_call(
        _ride_casts(functools.partial(_ssm_kernel, n_chunks=n_chunks, ns=seq_block), 6, 2, len(riders)),
        grid=(g // gps, n_blocks),
        in_specs=[pl.BlockSpec((rows, 128), lambda j, b: (b, slab0 + j)), mat, mat, mat,
                  pl.BlockSpec((1, wide2), lambda j, b: (0, j)),
                  pl.BlockSpec((seq_block, wide2), lambda j, b: (b, j))] + rider_specs,
        out_specs=[pl.BlockSpec((rows, 128), lambda j, b: (b, j)),
                   pl.BlockSpec((seq_block, wide2), lambda j, b: (b, j))] + rider_specs,
        out_shape=[jax.ShapeDtypeStruct((n, d_ssm), F32),
                   jax.ShapeDtypeStruct((n_seq, g * 4 * SSM_STATE), F32)] + rider_shapes,
        scratch_shapes=[pltpu.VMEM((gps, nc, lanes), BF16), pltpu.VMEM((2 * gps, padded, 128), F32),
                        pltpu.VMEM((2 * gps, padded, 128), F32), pltpu.VMEM((2 * gps, padded, 128), F32),
                        pltpu.VMEM((gps, nc, lanes), F32)],
        compiler_params=_cparams(("arbitrary", "arbitrary")),
        name="ssm",
    )(u, m_op, w_op, v_op, _slab_lanes(a_op[:, 0])[None], h0, *riders)
    return outs[0], outs[1], outs[2:]


def _softmax_pv(pieces, sink):
    m = sink
    for s, _ in pieces:
        m = jnp.maximum(m, jnp.max(s, axis=-1, keepdims=True))
    out = None
    for s, v in pieces:
        pv = jnp.dot(jnp.exp(s - m).astype(BF16), v, preferred_element_type=F32)
        out = pv if out is None else out + pv
    return out[:, :HEAD_DIM] / (out[:, HEAD_DIM:] + jnp.exp(sink - m))


def _qk(q, k):
    return lax.dot_general(q, k, (((1,), (1,)), ((), ())), preferred_element_type=F32)


def _with_ones(v):
    return jnp.concatenate([v, jnp.ones_like(v)], axis=1)


def _attn_ctx_kernel(sink_ref, q_ref, kv_ref, o_ref):
    d_kv = N_KV_HEADS * HEAD_DIM
    for kh in range(N_KV_HEADS):
        k = kv_ref[:, kh * HEAD_DIM:(kh + 1) * HEAD_DIM]
        v = _with_ones(kv_ref[:, d_kv + kh * HEAD_DIM:d_kv + (kh + 1) * HEAD_DIM])
        for h in range(kh * Q_PER_KV, (kh + 1) * Q_PER_KV):
            q = q_ref[:, h * HEAD_DIM:(h + 1) * HEAD_DIM]
            o = _softmax_pv([(_qk(q, k), v)], sink_ref[h])
            o_ref[:, h * HEAD_DIM:(h + 1) * HEAD_DIM] = o.astype(BF16)


def _attn_ctx(proj, sink, *, n_seq, seq_len, col, riders=()):
    d_attn = col["d_attn"]
    kv_w = 2 * col["d_kv"]
    rider_specs, rider_shapes = _rider_specs(riders, n_seq, lambda b: b)
    outs = pl.pallas_call(
        _ride_casts(_attn_ctx_kernel, 3, 1, len(riders)),
        grid=(n_seq,),
        in_specs=[pl.BlockSpec(memory_space=pltpu.SMEM),
                  pl.BlockSpec((seq_len, d_attn), lambda b: (b, col["q"] // d_attn)),
                  pl.BlockSpec((seq_len, kv_w), lambda b: (b, col["k"] // kv_w))] + rider_specs,
        out_specs=[pl.BlockSpec((seq_len, d_attn), lambda b: (b, 0))] + rider_specs,
        out_shape=[jax.ShapeDtypeStruct((n_seq * seq_len, d_attn), BF16)] + rider_shapes,
        compiler_params=_cparams(("arbitrary",)),
        name="attn_ctx",
    )(sink, proj, proj, *riders)
    return outs[0], outs[1:]


def _rope(x, cos, sin):
    reps = x.shape[1] // HEAD_DIM
    xf = x.astype(F32)
    return (xf * jnp.concatenate([cos] * reps, axis=1) + _swap32(xf) * jnp.concatenate([sin] * reps, axis=1)).astype(BF16)


def _attn_lat_kernel(sink_ref, q_ref, kvp_ref, kvc_ref, kvn_ref, ck_ref, cv_ref, cos_ref, sin_ref, o_ref, *, n_blocks):
    i = pl.program_id(1)
    d_kv = N_KV_HEADS * HEAD_DIM

    def tables(blk):
        rows = pl.ds(pl.multiple_of(blk * BLOCK, BLOCK), BLOCK)
        return cos_ref[rows, :], sin_ref[rows, :]

    q_all = _rope(q_ref[...], *tables(i))
    k_all = jnp.concatenate([_rope(kvp_ref[:, :d_kv], *tables(jnp.maximum(i - 1, 0))),
                             _rope(kvc_ref[:, :d_kv], *tables(i)),
                             _rope(kvn_ref[:, :d_kv], *tables(jnp.minimum(i + 1, n_blocks - 1)))], axis=0)
    r = lax.broadcasted_iota(jnp.int32, (BLOCK, 3 * BLOCK), 0)
    c = lax.broadcasted_iota(jnp.int32, (BLOCK, 3 * BLOCK), 1)
    cc = c & (BLOCK - 1)
    valid = (((c < BLOCK) & (cc >= r) & (i > 0)) | ((c >= BLOCK) & (c < 2 * BLOCK))
             | ((c >= 2 * BLOCK) & (cc <= r) & (i < n_blocks - 1)))
    for kh in range(N_KV_HEADS):
        ks = slice(kh * HEAD_DIM, (kh + 1) * HEAD_DIM)
        vs = slice(d_kv + kh * HEAD_DIM, d_kv + (kh + 1) * HEAD_DIM)
        k_loc = k_all[:, ks]
        k_ctx = ck_ref[:, ks].astype(BF16)
        v_loc = _with_ones(jnp.concatenate([kvp_ref[:, vs], kvc_ref[:, vs], kvn_ref[:, vs]], axis=0))
        v_ctx = _with_ones(cv_ref[:, ks].astype(BF16))
        for h in range(kh * Q_PER_KV, (kh + 1) * Q_PER_KV):
            q = q_all[:, h * HEAD_DIM:(h + 1) * HEAD_DIM]
            s_loc = jnp.where(valid, _qk(q, k_loc), NEG_INF)
            o = _softmax_pv([(s_loc, v_loc), (_qk(q, k_ctx), v_ctx)], sink_ref[h])
            o_ref[:, h * HEAD_DIM:(h + 1) * HEAD_DIM] = o.astype(BF16)


def _attn_lat(proj, cache_k, cache_v, sink, *, n_seq, seq_len, col):
    d_attn = col["d_attn"]
    kv_w = 2 * col["d_kv"]
    nb = seq_len // BLOCK
    kv_col = col["k"] // kv_w
    past, d_kv = cache_k.shape[1], cache_k.shape[2]

    def kv_spec(off):
        return pl.BlockSpec((BLOCK, kv_w), lambda b, i: (b * nb + jnp.clip(i + off, 0, nb - 1), kv_col))
    cache_spec = pl.BlockSpec((None, past, d_kv), lambda b, i: (b, 0, 0))
    table_spec = pl.BlockSpec((seq_len, HEAD_DIM), lambda b, i: (0, 0))
    cos, sin = _rope_tables(seq_len)
    return pl.pallas_call(
        functools.partial(_attn_lat_kernel, n_blocks=nb),
        grid=(n_seq, nb),
        in_specs=[pl.BlockSpec(memory_space=pltpu.SMEM),
                  pl.BlockSpec((BLOCK, d_attn), lambda b, i: (b * nb + i, col["q"] // d_attn)),
                  kv_spec(-1), kv_spec(0), kv_spec(1), cache_spec, cache_spec, table_spec, table_spec],
        out_specs=pl.BlockSpec((BLOCK, d_attn), lambda b, i: (b * nb + i, 0)),
        out_shape=jax.ShapeDtypeStruct((n_seq * seq_len, d_attn), BF16),
        compiler_params=_cparams(("arbitrary", "arbitrary")),
        name="attn_lat",
    )(sink, proj, proj, proj, proj, cache_k, cache_v, cos, sin)


def _gelu_tanh(x):
    return 0.5 * x * (1.0 + jnp.tanh(math.sqrt(2.0 / math.pi) * (x + 0.044715 * (x * x * x))))


def _mixer_out_kernel(x_ref, y_ref, o_ref, gs_ref, ga_ref, mod_ref, g_ref, wglu_ref, wso_ref, wao_ref, wout_ref,
                      x1_ref, xm2_ref):
    z = _gelu_tanh(y_ref[...].astype(F32))
    z = z * _sigmoid(jnp.dot(z.astype(BF16), wglu_ref[...], preferred_element_type=F32))
    s_br = jnp.dot(z.astype(BF16), wso_ref[...], preferred_element_type=F32)
    a_br = jnp.dot(o_ref[...], wao_ref[...], preferred_element_type=F32)
    merged = _sigmoid(gs_ref[...].astype(F32)) * s_br + _sigmoid(ga_ref[...].astype(F32)) * a_br
    out = jnp.dot(merged.astype(BF16), wout_ref[...], preferred_element_type=F32)
    x1 = x_ref[...] + mod_ref[2:3, :] * out
    x1_ref[...] = x1
    xm2_ref[...] = _rms_modulate(x1, g_ref[...], mod_ref[4:5, :], mod_ref[3:4, :]).astype(BF16)


def _mixer_out(x, y, o, proj, mod, norm_g, w_glu, w_ssm_o, w_attn_o, w_out, *, tm, tiles_per_seq, col):
    n, d = x.shape
    d_ssm = y.shape[1]
    d_attn = o.shape[1]
    seq_of = (lambda i: i // tiles_per_seq) if mod.shape[0] > 1 else (lambda i: 0)

    def resident(shape):
        return pl.BlockSpec(shape, lambda i: (0, 0), pipeline_mode=pl.Buffered(1))
    return pl.pallas_call(
        _mixer_out_kernel,
        grid=(n // tm,),
        in_specs=[pl.BlockSpec((tm, d), lambda i: (i, 0)),
                  pl.BlockSpec((tm, d_ssm), lambda i: (i, 0)),
                  pl.BlockSpec((tm, d_attn), lambda i: (i, 0)),
                  pl.BlockSpec((tm, d), lambda i: (i, col["gs"] // d)),
                  pl.BlockSpec((tm, d), lambda i: (i, col["ga"] // d)),
                  pl.BlockSpec((None, 6, d), lambda i: (seq_of(i), 0, 0)),
                  pl.BlockSpec((1, d), lambda i: (0, 0)),
                  resident(w_glu.shape), resident(w_ssm_o.shape), resident(w_attn_o.shape), resident(w_out.shape)],
        out_specs=[pl.BlockSpec((tm, d), lambda i: (i, 0)), pl.BlockSpec((tm, d), lambda i: (i, 0))],
        out_shape=[jax.ShapeDtypeStruct((n, d), F32), jax.ShapeDtypeStruct((n, d), BF16)],
        compiler_params=_cparams(("arbitrary",)),
        name="mixer_out",
    )(x, y, o, proj, proj, mod, norm_g.reshape(1, d), w_glu, w_ssm_o, w_attn_o, w_out)


FFN_TM = 1024
FFN_TF = 512
FFN_TN = 256


def _ffn_kernel(xm_ref, wa_ref, wb_ref, cw_ref, cb_ref, wd_ref, x1_ref, mod_ref, g_ref, o_ref, act_scr, raw_scr,
                *, seq_len, nf, nn, d_ff):
    j = pl.program_id(1)
    tm = xm_ref.shape[0]
    tf = wa_ref.shape[1]
    tn = wd_ref.shape[1]
    ch = tf // 2
    pos = lax.broadcasted_iota(jnp.int32, (tm, 1), 0) & (seq_len - 1)
    has_prev = pos != 0
    has_next = pos != seq_len - 1

    def lanes(off):
        return slice(off, off + ch) if isinstance(off, int) else pl.ds(pl.multiple_of(off, ch), ch)

    def gate(ha, hb, col0):
        def conv(h, off):
            cols = lanes(off)
            cw = cw_ref[:, cols]
            h_prev = jnp.where(has_prev, pltpu.roll(h, 1, 0), 0.0)
            h_next = jnp.where(has_next, pltpu.roll(h, tm - 1, 0), 0.0)
            return cw[0:1] * h_prev + cw[1:2] * h + cw[2:3] * h_next + cb_ref[:, cols]

        a = conv(ha, col0)
        b = conv(hb, d_ff + col0)
        return ((a * _sigmoid(a)) * b).astype(BF16)

    def act_cols(col0):
        return lanes(tf + col0)

    @pl.when(j == 0)
    def _():
        raw_scr[...] = jnp.zeros_like(raw_scr)

    @pl.when(j < nf)
    def _():
        xm = xm_ref[...]
        ha0 = jnp.dot(xm, wa_ref[:, :ch], preferred_element_type=F32)
        hb0 = jnp.dot(xm, wb_ref[:, :ch], preferred_element_type=F32)
        carried = (j - 1) * tf + ch
        act_scr[:, act_cols(carried)] = gate(raw_scr[0], raw_scr[1], jnp.maximum(carried, 0))
        ha1 = jnp.dot(xm, wa_ref[:, ch:], preferred_element_type=F32)
        hb1 = jnp.dot(xm, wb_ref[:, ch:], preferred_element_type=F32)
        act_scr[:, act_cols(j * tf)] = gate(ha0, hb0, j * tf)
        raw_scr[0] = ha1
        raw_scr[1] = hb1

    @pl.when(j == nf)
    def _():
        last = (nf - 1) * tf + ch
        act_scr[:, act_cols(last)] = gate(raw_scr[0], raw_scr[1], last)

    @pl.when(j >= nf)
    def _():
        cols = pl.ds(pl.multiple_of((j - nf) * tn, tn), tn)
        ffn = jnp.dot(act_scr[:, tf:], wd_ref[...], preferred_element_type=F32)
        o_ref[:, cols] = x1_ref[...] + mod_ref[5:6, cols] * ffn

    @pl.when(j == nf + nn - 1)
    def _():
        x2 = o_ref[...]
        ms = jnp.mean(x2 * x2, axis=-1, keepdims=True)
        o_ref[...] = x2 * lax.rsqrt(ms + EPS) * g_ref[...]


def _ffn(x1, xm2, mod, w_up, conv_w, conv_b, w_down, final_g, *, tm, seq_len):
    n, d = x1.shape
    d_ff = w_down.shape[0]
    tf, tn = FFN_TF, FFN_TN
    nf, nn = d_ff // tf, d // tn
    assert tm % seq_len == 0 and seq_len & (seq_len - 1) == 0 and d_ff % tf == 0 and (tf // 2) % V7X_MXU_WIDTH == 0
    seqs_per_tile = tm // seq_len
    seq_of = (lambda i: i * seqs_per_tile) if mod.shape[0] > 1 else (lambda i: 0)
    assert mod.shape[0] == 1 or seqs_per_tile == 1
    cb = conv_b.reshape(1, 2 * d_ff)

    def up(j):
        return jnp.minimum(j, nf - 1)

    def down(j):
        return jnp.maximum(j - nf, 0)
    return pl.pallas_call(
        functools.partial(_ffn_kernel, seq_len=seq_len, nf=nf, nn=nn, d_ff=d_ff),
        grid=(n // tm, nf + nn),
        in_specs=[pl.BlockSpec((tm, d), lambda i, j: (i, 0), pipeline_mode=pl.Buffered(1)),
                  pl.BlockSpec((d, tf), lambda i, j: (0, up(j))),
                  pl.BlockSpec((d, tf), lambda i, j: (0, nf + up(j))),
                  pl.BlockSpec((3, 2 * d_ff), lambda i, j: (0, 0)),
                  pl.BlockSpec((1, 2 * d_ff), lambda i, j: (0, 0)),
                  pl.BlockSpec((d_ff, tn), lambda i, j: (0, down(j))),
                  pl.BlockSpec((tm, tn), lambda i, j: (i, down(j))),
                  pl.BlockSpec((None, 6, d), lambda i, j: (seq_of(i), 0, 0)),
                  pl.BlockSpec((1, d), lambda i, j: (0, 0))],
        out_specs=pl.BlockSpec((tm, d), lambda i, j: (i, 0)),
        out_shape=jax.ShapeDtypeStruct((n, d), F32),
        scratch_shapes=[pltpu.VMEM((tm, tf + d_ff), BF16), pltpu.VMEM((2, tm, tf // 2), F32)],
        compiler_params=_cparams(("arbitrary", "arbitrary")),
        name="ffn",
    )(xm2, w_up, w_up, conv_w, cb, w_down, x1, mod, final_g.reshape(1, d))


def _rope_tables(seq_len):
    rows = seq_len // GRID_W
    row = jnp.repeat(jnp.arange(rows, dtype=F32), GRID_W)
    colp = jnp.tile(jnp.arange(GRID_W, dtype=F32), rows)
    n_freq = HEAD_DIM // 4
    inv = ROPE_THETA ** (-jnp.arange(n_freq, dtype=F32) / n_freq)
    ang_r = row[:, None] * inv[None, :]
    ang_c = colp[:, None] * inv[None, :]
    cos = jnp.concatenate([jnp.cos(ang_r), jnp.cos(ang_r), jnp.cos(ang_c), jnp.cos(ang_c)], axis=1)
    sin = jnp.concatenate([-jnp.sin(ang_r), jnp.sin(ang_r), -jnp.sin(ang_c), jnp.sin(ang_c)], axis=1)
    return cos, sin


PROJ_TM = 1024
MIXER_TM = 256


def kernel(x_prompt, x_sample, cache_k, cache_v, state_ssm_re, state_ssm_im, c, c_ctx, norm_mix_g, norm_ffn_g,
           w_mod, b_mod, w_in, ssm_lambda_re, ssm_lambda_im, ssm_log_dt, ssm_b_re, ssm_b_im, ssm_c_re, ssm_c_im,
           ssm_d, w_glu, attn_sink, w_ssm_o, w_attn_o, w_out, w_up, conv_w, conv_b, w_down, final_norm_g):
    batch, seq, d = x_prompt.shape
    dec_batch, dec_seq, _ = x_sample.shape
    depth = w_in.shape[0]
    assert depth == 1, "final norm is fused into the (single) layer's ffn kernel"
    d_ssm = w_glu.shape[1]
    d_attn = N_HEADS * HEAD_DIM
    d_kv = N_KV_HEADS * HEAD_DIM
    groups = d_ssm // SSM_CH
    assert w_in.shape[2] == d_ssm + d_attn + 2 * d_kv + 2 * d
    assert ssm_lambda_re.shape[2:] == (groups, SSM_STATE) and dec_batch <= 8 - 1
    l = 0

    col = {"gs": 0, "ga": d, "q": 2 * d, "u": 2 * d + d_attn, "k": 2 * d + d_attn + d_ssm,
           "d_ssm": d_ssm, "d_attn": d_attn, "d_kv": d_kv}
    xp = x_prompt.reshape(batch * seq, d)
    xs = x_sample.reshape(dec_batch * dec_seq, d)

    cond = jnp.concatenate([c_ctx[None], c, jnp.zeros((8 - 1 - dec_batch, d), F32)], axis=0)

    ssm_ops, mod, (w_in_b,) = _ssm_operators_and_mod(
        ssm_lambda_re[l], ssm_lambda_im[l], ssm_log_dt[l], ssm_b_re[l], ssm_b_im[l], ssm_c_re[l], ssm_c_im[l],
        ssm_d[l], cond, w_mod[l], b_mod[l], riders=(w_in[l],))
    mod = mod.reshape(8, 6, d)
    mod_p, mod_s = mod[0:1], mod[1:1 + dec_batch]
    proj_p, tail_p, k_raw, v_raw = _proj(xp, mod_p, norm_mix_g[l], w_in_b, tm=PROJ_TM,
                                         tiles_per_seq=max(seq // PROJ_TM, 1), col=col)
    proj_s, tail_s, _, _ = _proj(xs, mod_s, norm_mix_g[l], w_in_b, tm=PROJ_TM,
                                 tiles_per_seq=max(dec_seq // PROJ_TM, 1), col=col)

    def lanes(s):
        return s.transpose(0, 2, 1, 3).reshape(dec_batch, groups, 2 * SSM_STATE)
    h0_p = jnp.zeros((batch, groups * 4 * SSM_STATE), F32)
    h0_s = _slab_lanes(jnp.concatenate([lanes(state_ssm_re[:, l]), lanes(state_ssm_im[:, l])], axis=-1))
    y_ssm_p, h_fin, (w_up_b,) = _ssm(tail_p, 0, ssm_ops, h0_p, n_seq=batch, seq_len=seq,
                                     seq_block=min(batch, SSM_SEQ_BLOCK), riders=(w_up[l],))
    y_ssm_s, _, (w_down_b,) = _ssm(tail_s, 0, ssm_ops, h0_s, n_seq=dec_batch, seq_len=dec_seq,
                                   seq_block=min(dec_batch, SSM_SEQ_BLOCK), riders=(w_down[l],))

    sink = attn_sink[l]
    o_p, mixer_w = _attn_ctx(proj_p, sink, n_seq=batch, seq_len=seq, col=col,
                             riders=(w_glu[l], w_ssm_o[l], w_attn_o[l], w_out[l]))
    o_s = _attn_lat(proj_s, cache_k[:, l].reshape(dec_batch, -1, d_kv), cache_v[:, l].reshape(dec_batch, -1, d_kv),
                    sink, n_seq=dec_batch, seq_len=dec_seq, col=col)

    x1_p, xm2_p = _mixer_out(xp, y_ssm_p, o_p, proj_p, mod_p, norm_ffn_g[l], *mixer_w, tm=MIXER_TM,
                             tiles_per_seq=max(seq // MIXER_TM, 1), col=col)
    x1_s, xm2_s = _mixer_out(xs, y_ssm_s, o_s, proj_s, mod_s, norm_ffn_g[l], *mixer_w, tm=MIXER_TM,
                             tiles_per_seq=max(dec_seq // MIXER_TM, 1), col=col)
    y_p = _ffn(x1_p, xm2_p, mod_p, w_up_b, conv_w[l], conv_b[l], w_down_b, final_norm_g, tm=FFN_TM, seq_len=seq)
    y_s = _ffn(x1_s, xm2_s, mod_s, w_up_b, conv_w[l], conv_b[l], w_down_b, final_norm_g, tm=FFN_TM, seq_len=dec_seq)

    new_k = k_raw.reshape(batch, 1, seq, N_KV_HEADS, HEAD_DIM)
    new_v = v_raw.reshape(batch, 1, seq, N_KV_HEADS, HEAD_DIM)

    def unlanes(hl):
        return hl.reshape(batch, groups, 2, SSM_STATE).transpose(0, 2, 1, 3)[:, None]
    h_fin = _unslab_lanes(h_fin, groups)
    new_re = unlanes(h_fin[:, :, :2 * SSM_STATE])
    new_im = unlanes(h_fin[:, :, 2 * SSM_STATE:])
    return (y_p.reshape(batch, seq, d), y_s.reshape(dec_batch, dec_seq, d), new_k, new_v, new_re, new_im)
```

```python
import functools
import math

import jax
import jax.numpy as jnp
from jax import lax
from jax.experimental import pallas as pl
from jax.experimental.pallas import tpu as pltpu

F32 = jnp.float32
BF16 = jnp.bfloat16

GRID_W = 64
SSM_CH = 16
SSM_STATE = 64
N_HEADS = 8
N_KV_HEADS = 2
HEAD_DIM = 128
Q_PER_KV = N_HEADS // N_KV_HEADS
WINDOW = 128
BLOCK = 128
ROPE_THETA = 10000.0
EPS = 1e-6
NEG_INF = -1e30

SSM_CHUNK = 16
SSM_LANES = SSM_CHUNK * SSM_CH
SSM_SEQ_BLOCK = 16

V7X_VMEM_LIMIT_BYTES = 56 * 1024 * 1024
V7X_MXU_WIDTH = 256


def _cparams(semantics):
    return pltpu.CompilerParams(dimension_semantics=semantics, vmem_limit_bytes=V7X_VMEM_LIMIT_BYTES)


def _sigmoid(x):
    return 1.0 / (1.0 + jnp.exp(-x))


def _ride_casts(body, n_in, n_out, n_riders):
    def kernel(*refs):
        ins = refs[:n_in]
        rider_ins = refs[n_in:n_in + n_riders]
        outs = refs[n_in + n_riders:n_in + n_riders + n_out]
        rider_outs = refs[n_in + n_riders + n_out:n_in + 2 * n_riders + n_out]
        scratch = refs[n_in + 2 * n_riders + n_out:]
        for src, dst in zip(rider_ins, rider_outs):
            dst[...] = src[...].astype(BF16)
        body(*ins, *outs, *scratch)
    return kernel


def _rider_specs(weights, n_steps, step_of):
    specs, shapes = [], []
    for w in weights:
        r, c = w.shape
        assert r % (16 * n_steps) == 0
        specs.append(pl.BlockSpec((r // n_steps, c), lambda *idx: (step_of(*idx), 0)))
        shapes.append(jax.ShapeDtypeStruct((r, c), BF16))
    return specs, shapes


def _rms_modulate(x, g, scale, shift):
    ms = jnp.mean(x * x, axis=-1, keepdims=True)
    return (x * lax.rsqrt(ms + EPS) * g) * (1.0 + scale) + shift


MOD_K_CHUNK = 256


def _mod_kernel(c_ref, w_ref, b_ref, o_ref):
    c = c_ref[...]
    a = (c * _sigmoid(c)).astype(BF16)
    acc = b_ref[...]
    for k in range(0, w_ref.shape[0], MOD_K_CHUNK):
        acc = acc + jnp.dot(a[:, k:k + MOD_K_CHUNK], w_ref[k:k + MOD_K_CHUNK, :].astype(BF16),
                            preferred_element_type=F32)
    o_ref[...] = acc


PROJ_TN = 512
PROJ_NORM_ROWS = 16


def _swap32(x):
    n = x.shape[-1]
    lane = lax.broadcasted_iota(jnp.int32, x.shape, x.ndim - 1)
    return jnp.where((lane & 63) < 32, pltpu.roll(x, n - 32, x.ndim - 1), pltpu.roll(x, 32, x.ndim - 1))


def _proj_kernel(x_ref, mod_ref, g_ref, w_ref, o_ref, tail_ref, k_ref, v_ref, xm_scr, *, q_tiles):
    j = pl.program_id(1)

    @pl.when(j == 0)
    def _():
        gain = g_ref[...] * (1.0 + mod_ref[1:2, :])
        shift = mod_ref[0:1, :]

        def block(c, carry):
            rows = pl.ds(pl.multiple_of(c * PROJ_NORM_ROWS, PROJ_NORM_ROWS), PROJ_NORM_ROWS)
            x = x_ref[rows, :]
            ms = jnp.mean(x * x, axis=-1, keepdims=True)
            xm_scr[rows, :] = (x * lax.rsqrt(ms + EPS) * gain + shift).astype(BF16)
            return carry

        lax.fori_loop(0, x_ref.shape[0] // PROJ_NORM_ROWS, block, 0, unroll=8)

    acc = jnp.dot(xm_scr[...], w_ref[...], preferred_element_type=F32)
    scale = jnp.where((j >= q_tiles[0]) & (j < q_tiles[1]), HEAD_DIM ** -0.5, 1.0)
    o_ref[...] = (acc * scale).astype(BF16)
    tail_ref[...] = acc
    d_kv = k_ref.shape[1]
    k_ref[...] = acc[:, :d_kv]
    v_ref[...] = acc[:, d_kv:]


def _proj(x, mod, norm_g, w_in, *, tm, tiles_per_seq, col):
    n, d = x.shape
    cols = w_in.shape[1]
    tn = PROJ_TN
    n_tiles = cols // tn
    d_kv = col["d_kv"]
    n_gate, n_q, n_u = col["q"] // tn, col["d_attn"] // tn, col["d_ssm"] // tn
    q_tiles = (n_gate, n_gate + n_q)
    tail0 = col["u"] // tn
    n_tail = n_tiles - tail0
    assert 2 * d_kv == tn and col["u"] == col["q"] + col["d_attn"] and col["k"] == cols - tn

    def src_tile(j):
        return jnp.where(j < n_gate, j + (n_u + n_q + 1),
                         jnp.where(j < n_gate + n_q, j - n_gate + n_u,
                                   jnp.where(j < n_gate + n_q + n_u, j - (n_gate + n_q), n_u + n_q)))
    seq_of = (lambda i: i // tiles_per_seq) if mod.shape[0] > 1 else (lambda i: 0)
    return pl.pallas_call(
        functools.partial(_proj_kernel, q_tiles=q_tiles),
        grid=(n // tm, n_tiles),
        in_specs=[pl.BlockSpec((tm, d), lambda i, j: (i, 0)),
                  pl.BlockSpec((None, 6, d), lambda i, j: (seq_of(i), 0, 0)),
                  pl.BlockSpec((1, d), lambda i, j: (0, 0)),
                  pl.BlockSpec((d, tn), lambda i, j: (0, src_tile(j)))],
        out_specs=[pl.BlockSpec((tm, tn), lambda i, j: (i, j)),
                   pl.BlockSpec((tm, tn), lambda i, j: (i, jnp.clip(j - tail0, 0, n_tail - 1))),
                   pl.BlockSpec((tm, d_kv), lambda i, j: (i, 0)),
                   pl.BlockSpec((tm, d_kv), lambda i, j: (i, 0))],
        out_shape=[jax.ShapeDtypeStruct((n, cols), BF16), jax.ShapeDtypeStruct((n, n_tail * tn), F32),
                   jax.ShapeDtypeStruct((n, d_kv), F32), jax.ShapeDtypeStruct((n, d_kv), F32)],
        scratch_shapes=[pltpu.VMEM((tm, d), BF16)],
        compiler_params=_cparams(("arbitrary", "arbitrary")),
        name="proj",
    )(x, mod, norm_g.reshape(1, d), w_in)


def _cmul(ar, ai, br, bi):
    return ar * br - ai * bi, ar * bi + ai * br


def _cpow(base_r, base_i, n, bits):
    res_r = jnp.ones_like(base_r)
    res_i = jnp.zeros_like(base_r)
    for bit in range(bits):
        nr, ni = _cmul(res_r, res_i, base_r, base_i)
        take = ((n >> bit) & 1) == 1
        res_r = jnp.where(take, nr, res_r)
        res_i = jnp.where(take, ni, res_i)
        if bit + 1 < bits:
            base_r, base_i = _cmul(base_r, base_i, base_r, base_i)
    return res_r, res_i


def _ssm_disc_kernel(lre_ref, lim_ref, ldt_ref, ere_ref, eim_ref, qre_ref, qim_ref):
    lre = lre_ref[...]
    lim = lim_ref[...]
    dt = jnp.exp(ldt_ref[...])
    mag = jnp.exp(lre * dt)
    e_re = mag * jnp.cos(lim * dt)
    e_im = mag * jnp.sin(lim * dt)
    n_re = e_re - 1.0
    den = lre * lre + lim * lim
    ere_ref[...] = e_re
    eim_ref[...] = e_im
    qre_ref[...] = (n_re * lre + e_im * lim) / den
    qim_ref[...] = (e_im * lre - n_re * lim) / den


def _ssm_disc(lam_re, lam_im, log_dt):
    two, g, p = lam_re.shape
    shp = jax.ShapeDtypeStruct((two * g, p), F32)
    outs = pl.pallas_call(_ssm_disc_kernel, out_shape=[shp] * 4, name="ssm_disc")(
        lam_re.reshape(two * g, p), lam_im.reshape(two * g, p), log_dt.reshape(two * g, 1))
    return [o.reshape(two, g, p) for o in outs]


SSM_OPS_GROUPS = 4


def _ssm_ops_mod_kernel(ecol_ref, rows_ref, b_ref, ct_ref, d_ref, c_ref, wm_ref, bm_ref,
                        m_ref, w_ref, v_ref, a_ref, mod_ref):
    _mod_kernel(c_ref, wm_ref, bm_ref, mod_ref)
    _ssm_ops_kernel(ecol_ref, rows_ref, b_ref, ct_ref, d_ref, m_ref, w_ref, v_ref, a_ref)


def _ssm_ops_kernel(ecol_ref, rows_ref, b_ref, ct_ref, d_ref, m_ref, w_ref, v_ref, a_ref):
    def one(gi, carry):
        _ssm_ops_group(ecol_ref.at[gi], rows_ref.at[gi], b_ref.at[gi], ct_ref.at[gi], d_ref.at[gi],
                       m_ref.at[gi], w_ref.at[gi], v_ref.at[gi], a_ref.at[gi])
        return carry

    lax.fori_loop(0, ecol_ref.shape[0], one, 0)


def _ssm_ops_group(ecol_ref, rows_ref, b_ref, ct_ref, d_ref, m_ref, w_ref, v_ref, a_ref):
    L, ch, p = SSM_CHUNK, SSM_CH, SSM_STATE
    lanes = SSM_LANES
    half = 2 * p
    lane_c = lax.broadcasted_iota(jnp.int32, (p, lanes), 1)
    lag = lane_c // ch

    def col(k):
        return jnp.broadcast_to(ecol_ref[k], (p, lanes))

    f0 = _cpow(col(0), col(1), lag, 4)
    b0 = _cpow(col(2), col(3), (L - 1) - lag, 4)
    f1 = _cmul(f0[0], f0[1], col(0), col(1))
    b1 = _cmul(b0[0], b0[1], col(2), col(3))
    def ct(k):
        return jnp.tile(ct_ref[k], (1, L))

    cf = (ct(0), ct(1))
    cb = (ct(2), ct(3))
    yf0 = _cmul(*f0, *cf)
    yb0 = _cmul(*b0, *cb)
    yf1 = _cmul(*f1, *cf)
    yb1 = _cmul(*b1, *cb)
    v_ref[...] = jnp.concatenate([yf1[0], yb1[0], -yf1[1], -yb1[1]], axis=0).astype(BF16)

    lane_r = lax.broadcasted_iota(jnp.int32, (ch, lanes), 1)
    sgn = jnp.where(lane_r < half, -1.0, 1.0)
    b1raw = b_ref[...]
    b2raw = pltpu.roll(b1raw, half, 1) * sgn
    bb1 = rows_ref[2:3, :] * b1raw + rows_ref[3:4, :] * b2raw
    bb2 = pltpu.roll(bb1, half, 1) * sgn

    zero = jnp.zeros((p, lanes), F32)
    hi = lax.Precision.HIGHEST
    r0f = jnp.dot(bb1, jnp.concatenate([yf0[0], zero, -yf0[1], zero], axis=0), precision=hi,
                  preferred_element_type=F32)
    r0b = jnp.dot(bb1, jnp.concatenate([zero, yb0[0], zero, -yb0[1]], axis=0), precision=hi,
                  preferred_element_type=F32)
    row_r = lax.broadcasted_iota(jnp.int32, (ch, lanes), 0)
    d_col = d_ref[...]
    for s in range(L):
        blk = jnp.where(lane_r == ch * s + row_r, d_col, 0.0)
        fwd = r0f if s == 0 else jnp.where(lane_r >= ch * s, pltpu.roll(r0f, ch * s, 1), 0.0)
        k = ch * (L - 1 - s)
        bwd = r0b if k == 0 else jnp.where(lane_r < lanes - k, pltpu.roll(r0b, lanes - k, 1), 0.0)
        m_ref[s * ch:(s + 1) * ch, :] = (blk + fwd + bwd).astype(BF16)

    is_f = (lane_r & (half - 1)) < p
    n_row = jnp.where(is_f, (L - 1) - row_r, row_r)
    er, ei = _cpow(jnp.broadcast_to(rows_ref[0:1, :], (L, lanes)), jnp.broadcast_to(rows_ref[1:2, :], (L, lanes)),
                   n_row, 4)
    for s in range(L):
        w_ref[s * ch:(s + 1) * ch, :] = (er[s:s + 1, :] * bb1 + ei[s:s + 1, :] * bb2).astype(BF16)

    ar, ai = rows_ref[0:1, :], rows_ref[1:2, :]
    for _ in range(4):
        ar, ai = _cmul(ar, ai, ar, ai)
    lane_1 = lax.broadcasted_iota(jnp.int32, (1, lanes), 1)
    a_ref[...] = jnp.where(lane_1 < half, ar, ai)


def _ssm_operators_and_mod(lam_re, lam_im, log_dt, b_re, b_im, c_re, c_im, ssm_d, cond, w_mod, b_mod, riders=()):
    two, g, p = lam_re.shape
    ch = b_re.shape[-1]
    lanes = SSM_LANES
    assert (two, p, ch, SSM_CHUNK * ch) == (2, SSM_STATE, SSM_CH, lanes)
    e_re, e_im, q_re, q_im = _ssm_disc(lam_re, lam_im, log_dt)
    ecol = jnp.stack([e_re[0], e_im[0], e_re[1], e_im[1]], axis=1)[..., None]

    def fbfb(x):
        return jnp.concatenate([x[0], x[1], x[0], x[1]], axis=-1)
    rows = jnp.stack([fbfb(e_re), fbfb(e_im), fbfb(q_re), fbfb(q_im)], axis=1)

    def bt(x):
        return jnp.swapaxes(x, -1, -2)
    b_rows = jnp.concatenate([bt(b_re[0]), bt(b_re[1]), bt(b_im[0]), bt(b_im[1])], axis=-1)

    ct = jnp.stack([bt(c_re[0]), bt(c_im[0]), bt(c_re[1]), bt(c_im[1])], axis=1)
    d_col = ssm_d.reshape(g, ch, 1)
    gb = SSM_OPS_GROUPS
    mat = pl.BlockSpec((gb, lanes, lanes), lambda i: (i, 0, 0))
    steps = g // gb
    rider_specs, rider_shapes = _rider_specs(riders, steps, lambda i: i)
    c_rows, d_model = cond.shape
    n_mod = w_mod.shape[1]
    tn = n_mod // steps
    assert n_mod % steps == 0 and tn % 128 == 0
    outs = pl.pallas_call(
        _ride_casts(_ssm_ops_mod_kernel, 8, 5, len(riders)),
        grid=(steps,),
        in_specs=[pl.BlockSpec((gb, 4, p, 1), lambda i: (i, 0, 0, 0)),
                  pl.BlockSpec((gb, 4, lanes), lambda i: (i, 0, 0)),
                  pl.BlockSpec((gb, ch, lanes), lambda i: (i, 0, 0)),
                  pl.BlockSpec((gb, 4, p, ch), lambda i: (i, 0, 0, 0)),
                  pl.BlockSpec((gb, ch, 1), lambda i: (i, 0, 0)),
                  pl.BlockSpec((c_rows, d_model), lambda i: (0, 0)),
                  pl.BlockSpec((d_model, tn), lambda i: (0, i)),
                  pl.BlockSpec((1, tn), lambda i: (0, i))] + rider_specs,
        out_specs=[mat, mat, mat, pl.BlockSpec((gb, 1, lanes), lambda i: (i, 0, 0)),
                   pl.BlockSpec((c_rows, tn), lambda i: (0, i))] + rider_specs,
        out_shape=([jax.ShapeDtypeStruct((g, lanes, lanes), BF16)] * 3 + [jax.ShapeDtypeStruct((g, 1, lanes), F32),
                                                                          jax.ShapeDtypeStruct((c_rows, n_mod), F32)]
                   + rider_shapes),
        compiler_params=_cparams(("arbitrary",)),
        name="ssm_ops_mod",
    )(ecol, rows, b_rows, ct, d_col, cond, w_mod, b_mod.reshape(1, n_mod), *riders)
    return outs[:4], outs[4], outs[5:]


SSM_SLAB_GROUPS = 128 // SSM_CH


def _chunk_transpose(xs, chunk_id):
    for k in (4, 2, 1):
        keep = (chunk_id & k) == 0
        new = list(xs)
        for i in range(len(xs)):
            if i & k == 0:
                a, b = xs[i], xs[i + k]
                if 2 * SSM_CH * k == 128:
                    moved = pltpu.roll(jnp.where(keep, b, a), SSM_CH * k, 1)
                    new[i] = jnp.where(keep, a, moved)
                    new[i + k] = jnp.where(keep, moved, b)
                else:
                    new[i] = jnp.where(keep, a, pltpu.roll(b, SSM_CH * k, 1))
                    new[i + k] = jnp.where(keep, pltpu.roll(a, 128 - SSM_CH * k, 1), b)
        xs = new
    return xs


def _ssm_kernel(u_ref, m_ref, w_ref, v_ref, a_ref, h0_ref, y_ref, hfin_ref, ug_scr, s_scr, hf_scr, hb_scr, yg_scr,
                *, n_chunks, ns):
    gps = SSM_SLAB_GROUPS
    nc = ns * n_chunks
    half = 2 * SSM_STATE
    chunk_id = lax.broadcasted_iota(jnp.int32, (nc, 128), 1) // SSM_CH
    for hf in range(2):
        xs = [u_ref[pl.ds(hf * gps + i, nc, stride=SSM_CHUNK), :] for i in range(gps)]
        xs = _chunk_transpose(xs, chunk_id)
        for g in range(gps):
            ug_scr[g, :, hf * 128:(hf + 1) * 128] = xs[g].astype(BF16)
    pitch = n_chunks + 8
    for g in range(gps):
        s = jnp.dot(ug_scr[g], w_ref[g], preferred_element_type=F32)
        for q in range(ns):
            s_scr[g, q * pitch:q * pitch + n_chunks, :] = s[q * n_chunks:(q + 1) * n_chunks, :half]
            s_scr[gps + g, q * pitch:q * pitch + n_chunks, :] = s[q * n_chunks:(q + 1) * n_chunks, half:]
    is_fwd = lax.broadcasted_iota(jnp.int32, (1, half), 1) < SSM_STATE
    blocks = range(2 * gps)
    a = [a_ref[:, b * half:(b + 1) * half] for b in blocks]
    h = [h0_ref[:, b * half:(b + 1) * half] for b in blocks]

    def rows(c):
        return pl.ds(c, ns, stride=pitch)

    for k in range(n_chunks):
        kb = n_chunks - 1 - k
        new_h = list(h)
        for b in blocks:
            hf_scr[b, rows(k), :] = h[b]
            hb_scr[b, rows(kb), :] = h[b]
        for g in range(gps):
            s_re = jnp.where(is_fwd, s_scr[g, rows(k), :], s_scr[g, rows(kb), :])
            s_im = jnp.where(is_fwd, s_scr[gps + g, rows(k), :], s_scr[gps + g, rows(kb), :])
            new_h[g] = a[g] * h[g] - a[gps + g] * h[gps + g] + s_re
            new_h[gps + g] = a[g] * h[gps + g] + a[gps + g] * h[g] + s_im
        h = new_h
    for b in blocks:
        hfin_ref[:, b * half:(b + 1) * half] = h[b]

    def entering(b):
        return jnp.concatenate(
            [jnp.where(is_fwd, hf_scr[b, q * pitch:q * pitch + n_chunks, :], hb_scr[b, q * pitch:q * pitch + n_chunks, :])
             for q in range(ns)], axis=0)

    for g in range(gps):
        hin = jnp.concatenate([entering(g), entering(gps + g)], axis=1).astype(BF16)
        yg_scr[g] = (jnp.dot(ug_scr[g], m_ref[g], preferred_element_type=F32)
                     + jnp.dot(hin, v_ref[g], preferred_element_type=F32))
    for hf in range(2):
        xs = [yg_scr[g, :, hf * 128:(hf + 1) * 128] for g in range(gps)]
        xs = _chunk_transpose(xs, chunk_id)
        for i in range(gps):
            y_ref[pl.ds(hf * gps + i, nc, stride=SSM_CHUNK), :] = xs[i]


def _slab_lanes(x):
    lead = x.shape[:-2]
    g = x.shape[-2]
    gps = SSM_SLAB_GROUPS
    x = x.reshape(lead + (g // gps, gps, 2, 2 * SSM_STATE))
    x = jnp.swapaxes(x, -3, -2)
    return x.reshape(lead + (g * 4 * SSM_STATE,))


def _unslab_lanes(x, g):
    lead = x.shape[:-1]
    gps = SSM_SLAB_GROUPS
    x = x.reshape(lead + (g // gps, 2, gps, 2 * SSM_STATE))
    x = jnp.swapaxes(x, -3, -2)
    return x.reshape(lead + (g, 4 * SSM_STATE))


def _ssm(u, u_col, ops, h0, *, n_seq, seq_len, seq_block, riders=()):
    m_op, w_op, v_op, a_op = ops
    n = u.shape[0]
    g = m_op.shape[0]
    d_ssm = g * SSM_CH
    slab0 = u_col // 128
    gps = SSM_SLAB_GROUPS
    lanes = SSM_LANES
    n_chunks = seq_len // SSM_CHUNK
    nc = seq_block * n_chunks
    padded = seq_block * (n_chunks + 8)
    rows = seq_block * seq_len
    wide2 = gps * 4 * SSM_STATE
    mat = pl.BlockSpec((gps, lanes, lanes), lambda j, b: (j, 0, 0))
    n_blocks = n_seq // seq_block
    rider_specs, rider_shapes = _rider_specs(riders, (g // gps) * n_blocks, lambda j, b: j * n_blocks + b)
    outs = pl.pallas_call(
        _ride_casts(functools.partial(_ssm_kernel, n_chunks=n_chunks, ns=seq_block), 6, 2, len(riders)),
        grid=(g // gps, n_blocks),
        in_specs=[pl.BlockSpec((rows, 128), lambda j, b: (b, slab0 + j)), mat, mat, mat,
                  pl.BlockSpec((1, wide2), lambda j, b: (0, j)),
                  pl.BlockSpec((seq_block, wide2), lambda j, b: (b, j))] + rider_specs,
        out_specs=[pl.BlockSpec((rows, 128), lambda j, b: (b, j)),
                   pl.BlockSpec((seq_block, wide2), lambda j, b: (b, j))] + rider_specs,
        out_shape=[jax.ShapeDtypeStruct((n, d_ssm), F32),
                   jax.ShapeDtypeStruct((n_seq, g * 4 * SSM_STATE), F32)] + rider_shapes,
        scratch_shapes=[pltpu.VMEM((gps, nc, lanes), BF16), pltpu.VMEM((2 * gps, padded, 128), F32),
                        pltpu.VMEM((2 * gps, padded, 128), F32), pltpu.VMEM((2 * gps, padded, 128), F32),
                        pltpu.VMEM((gps, nc, lanes), F32)],
        compiler_params=_cparams(("arbitrary", "arbitrary")),
        name="ssm",
    )(u, m_op, w_op, v_op, _slab_lanes(a_op[:, 0])[None], h0, *riders)
    return outs[0], outs[1], outs[2:]


def _softmax_pv(pieces, sink):
    m = sink
    for s, _ in pieces:
        m = jnp.maximum(m, jnp.max(s, axis=-1, keepdims=True))
    out = None
    for s, v in pieces:
        pv = jnp.dot(jnp.exp(s - m).astype(BF16), v, preferred_element_type=F32)
        out = pv if out is None else out + pv
    return out[:, :HEAD_DIM] / (out[:, HEAD_DIM:] + jnp.exp(sink - m))


def _qk(q, k):
    return lax.dot_general(q, k, (((1,), (1,)), ((), ())), preferred_element_type=F32)


def _with_ones(v):
    return jnp.concatenate([v, jnp.ones_like(v)], axis=1)


def _attn_ctx_kernel(sink_ref, q_ref, kv_ref, o_ref):
    d_kv = N_KV_HEADS * HEAD_DIM
    for kh in range(N_KV_HEADS):
        k = kv_ref[:, kh * HEAD_DIM:(kh + 1) * HEAD_DIM]
        v = _with_ones(kv_ref[:, d_kv + kh * HEAD_DIM:d_kv + (kh + 1) * HEAD_DIM])
        for h in range(kh * Q_PER_KV, (kh + 1) * Q_PER_KV):
            q = q_ref[:, h * HEAD_DIM:(h + 1) * HEAD_DIM]
            o = _softmax_pv([(_qk(q, k), v)], sink_ref[h])
            o_ref[:, h * HEAD_DIM:(h + 1) * HEAD_DIM] = o.astype(BF16)


def _attn_ctx(proj, sink, *, n_seq, seq_len, col, riders=()):
    d_attn = col["d_attn"]
    kv_w = 2 * col["d_kv"]
    rider_specs, rider_shapes = _rider_specs(riders, n_seq, lambda b: b)
    outs = pl.pallas_call(
        _ride_casts(_attn_ctx_kernel, 3, 1, len(riders)),
        grid=(n_seq,),
        in_specs=[pl.BlockSpec(memory_space=pltpu.SMEM),
                  pl.BlockSpec((seq_len, d_attn), lambda b: (b, col["q"] // d_attn)),
                  pl.BlockSpec((seq_len, kv_w), lambda b: (b, col["k"] // kv_w))] + rider_specs,
        out_specs=[pl.BlockSpec((seq_len, d_attn), lambda b: (b, 0))] + rider_specs,
        out_shape=[jax.ShapeDtypeStruct((n_seq * seq_len, d_attn), BF16)] + rider_shapes,
        compiler_params=_cparams(("arbitrary",)),
        name="attn_ctx",
    )(sink, proj, proj, *riders)
    return outs[0], outs[1:]


def _rope(x, cos, sin):
    reps = x.shape[1] // HEAD_DIM
    xf = x.astype(F32)
    return (xf * jnp.concatenate([cos] * reps, axis=1) + _swap32(xf) * jnp.concatenate([sin] * reps, axis=1)).astype(BF16)


def _attn_lat_kernel(sink_ref, q_ref, kvp_ref, kvc_ref, kvn_ref, ck_ref, cv_ref, cos_ref, sin_ref, o_ref, *, n_blocks):
    i = pl.program_id(1)
    d_kv = N_KV_HEADS * HEAD_DIM

    def tables(blk):
        rows = pl.ds(pl.multiple_of(blk * BLOCK, BLOCK), BLOCK)
        return cos_ref[rows, :], sin_ref[rows, :]

    q_all = _rope(q_ref[...], *tables(i))
    k_all = jnp.concatenate([_rope(kvp_ref[:, :d_kv], *tables(jnp.maximum(i - 1, 0))),
                             _rope(kvc_ref[:, :d_kv], *tables(i)),
                             _rope(kvn_ref[:, :d_kv], *tables(jnp.minimum(i + 1, n_blocks - 1)))], axis=0)
    r = lax.broadcasted_iota(jnp.int32, (BLOCK, 3 * BLOCK), 0)
    c = lax.broadcasted_iota(jnp.int32, (BLOCK, 3 * BLOCK), 1)
    cc = c & (BLOCK - 1)
    valid = (((c < BLOCK) & (cc >= r) & (i > 0)) | ((c >= BLOCK) & (c < 2 * BLOCK))
             | ((c >= 2 * BLOCK) & (cc <= r) & (i < n_blocks - 1)))
    for kh in range(N_KV_HEADS):
        ks = slice(kh * HEAD_DIM, (kh + 1) * HEAD_DIM)
        vs = slice(d_kv + kh * HEAD_DIM, d_kv + (kh + 1) * HEAD_DIM)
        k_loc = k_all[:, ks]
        k_ctx = ck_ref[:, ks].astype(BF16)
        v_loc = _with_ones(jnp.concatenate([kvp_ref[:, vs], kvc_ref[:, vs], kvn_ref[:, vs]], axis=0))
        v_ctx = _with_ones(cv_ref[:, ks].astype(BF16))
        for h in range(kh * Q_PER_KV, (kh + 1) * Q_PER_KV):
            q = q_all[:, h * HEAD_DIM:(h + 1) * HEAD_DIM]
            s_loc = jnp.where(valid, _qk(q, k_loc), NEG_INF)
            o = _softmax_pv([(s_loc, v_loc), (_qk(q, k_ctx), v_ctx)], sink_ref[h])
            o_ref[:, h * HEAD_DIM:(h + 1) * HEAD_DIM] = o.astype(BF16)


def _attn_lat(proj, cache_k, cache_v, sink, *, n_seq, seq_len, col):
    d_attn = col["d_attn"]
    kv_w = 2 * col["d_kv"]
    assert WINDOW == BLOCK and seq_len % BLOCK == 0
    nb = seq_len // BLOCK
    kv_col = col["k"] // kv_w
    past, d_kv = cache_k.shape[1], cache_k.shape[2]

    def kv_spec(off):
        return pl.BlockSpec((BLOCK, kv_w), lambda b, i: (b * nb + jnp.clip(i + off, 0, nb - 1), kv_col))
    cache_spec = pl.BlockSpec((None, past, d_kv), lambda b, i: (b, 0, 0))
    table_spec = pl.BlockSpec((seq_len, HEAD_DIM), lambda b, i: (0, 0))
    cos, sin = _rope_tables(seq_len)
    return pl.pallas_call(
        functools.partial(_attn_lat_kernel, n_blocks=nb),
        grid=(n_seq, nb),
        in_specs=[pl.BlockSpec(memory_space=pltpu.SMEM),
                  pl.BlockSpec((BLOCK, d_attn), lambda b, i: (b * nb + i, col["q"] // d_attn)),
                  kv_spec(-1), kv_spec(0), kv_spec(1), cache_spec, cache_spec, table_spec, table_spec],
        out_specs=pl.BlockSpec((BLOCK, d_attn), lambda b, i: (b * nb + i, 0)),
        out_shape=jax.ShapeDtypeStruct((n_seq * seq_len, d_attn), BF16),
        compiler_params=_cparams(("arbitrary", "arbitrary")),
        name="attn_lat",
    )(sink, proj, proj, proj, proj, cache_k, cache_v, cos, sin)


def _gelu_tanh(x):
    return 0.5 * x * (1.0 + jnp.tanh(math.sqrt(2.0 / math.pi) * (x + 0.044715 * (x * x * x))))


def _mixer_out_kernel(x_ref, y_ref, o_ref, gs_ref, ga_ref, mod_ref, g_ref, wglu_ref, wso_ref, wao_ref, wout_ref,
                      x1_ref, xm2_ref):
    z = _gelu_tanh(y_ref[...].astype(F32))
    z = z * _sigmoid(jnp.dot(z.astype(BF16), wglu_ref[...], preferred_element_type=F32))
    s_br = jnp.dot(z.astype(BF16), wso_ref[...], preferred_element_type=F32)
    a_br = jnp.dot(o_ref[...], wao_ref[...], preferred_element_type=F32)
    merged = _sigmoid(gs_ref[...].astype(F32)) * s_br + _sigmoid(ga_ref[...].astype(F32)) * a_br
    out = jnp.dot(merged.astype(BF16), wout_ref[...], preferred_element_type=F32)
    x1 = x_ref[...] + mod_ref[2:3, :] * out
    x1_ref[...] = x1
    xm2_ref[...] = _rms_modulate(x1, g_ref[...], mod_ref[4:5, :], mod_ref[3:4, :]).astype(BF16)


def _mixer_out(x, y, o, proj, mod, norm_g, w_glu, w_ssm_o, w_attn_o, w_out, *, tm, tiles_per_seq, col):
    n, d = x.shape
    d_ssm = y.shape[1]
    d_attn = o.shape[1]
    seq_of = (lambda i: i // tiles_per_seq) if mod.shape[0] > 1 else (lambda i: 0)

    def resident(shape):
        return pl.BlockSpec(shape, lambda i: (0, 0), pipeline_mode=pl.Buffered(1))
    return pl.pallas_call(
        _mixer_out_kernel,
        grid=(n // tm,),
        in_specs=[pl.BlockSpec((tm, d), lambda i: (i, 0)),
                  pl.BlockSpec((tm, d_ssm), lambda i: (i, 0)),
                  pl.BlockSpec((tm, d_attn), lambda i: (i, 0)),
                  pl.BlockSpec((tm, d), lambda i: (i, col["gs"] // d)),
                  pl.BlockSpec((tm, d), lambda i: (i, col["ga"] // d)),
                  pl.BlockSpec((None, 6, d), lambda i: (seq_of(i), 0, 0)),
                  pl.BlockSpec((1, d), lambda i: (0, 0)),
                  resident(w_glu.shape), resident(w_ssm_o.shape), resident(w_attn_o.shape), resident(w_out.shape)],
        out_specs=[pl.BlockSpec((tm, d), lambda i: (i, 0)), pl.BlockSpec((tm, d), lambda i: (i, 0))],
        out_shape=[jax.ShapeDtypeStruct((n, d), F32), jax.ShapeDtypeStruct((n, d), BF16)],
        compiler_params=_cparams(("arbitrary",)),
        name="mixer_out",
    )(x, y, o, proj, proj, mod, norm_g.reshape(1, d), w_glu, w_ssm_o, w_attn_o, w_out)


FFN_TM = 1024
FFN_TF = 512
FFN_TN = 256


def _ffn_kernel(xm_ref, wa_ref, wb_ref, cw_ref, cb_ref, wd_ref, x1_ref, mod_ref, g_ref, o_ref, act_scr, raw_scr,
                *, seq_len, nf, nn, d_ff):
    j = pl.program_id(1)
    tm = xm_ref.shape[0]
    tf = wa_ref.shape[1]
    tn = wd_ref.shape[1]
    ch = tf // 2
    pos = lax.broadcasted_iota(jnp.int32, (tm, 1), 0) & (seq_len - 1)
    has_prev = pos != 0
    has_next = pos != seq_len - 1

    def lanes(off):
        return slice(off, off + ch) if isinstance(off, int) else pl.ds(pl.multiple_of(off, ch), ch)

    def gate(ha, hb, col0):
        def conv(h, off):
            cols = lanes(off)
            cw = cw_ref[:, cols]
            h_prev = jnp.where(has_prev, pltpu.roll(h, 1, 0), 0.0)
            h_next = jnp.where(has_next, pltpu.roll(h, tm - 1, 0), 0.0)
            return cw[0:1] * h_prev + cw[1:2] * h + cw[2:3] * h_next + cb_ref[:, cols]

        a = conv(ha, col0)
        b = conv(hb, d_ff + col0)
        return ((a * _sigmoid(a)) * b).astype(BF16)

    def act_cols(col0):
        return lanes(tf + col0)

    @pl.when(j == 0)
    def _():
        raw_scr[...] = jnp.zeros_like(raw_scr)

    @pl.when(j < nf)
    def _():
        xm = xm_ref[...]
        ha0 = jnp.dot(xm, wa_ref[:, :ch], preferred_element_type=F32)
        hb0 = jnp.dot(xm, wb_ref[:, :ch], preferred_element_type=F32)
        carried = (j - 1) * tf + ch
        act_scr[:, act_cols(carried)] = gate(raw_scr[0], raw_scr[1], jnp.maximum(carried, 0))
        ha1 = jnp.dot(xm, wa_ref[:, ch:], preferred_element_type=F32)
        hb1 = jnp.dot(xm, wb_ref[:, ch:], preferred_element_type=F32)
        act_scr[:, act_cols(j * tf)] = gate(ha0, hb0, j * tf)
        raw_scr[0] = ha1
        raw_scr[1] = hb1

    @pl.when(j == nf)
    def _():
        last = (nf - 1) * tf + ch
        act_scr[:, act_cols(last)] = gate(raw_scr[0], raw_scr[1], last)

    @pl.when(j >= nf)
    def _():
        cols = pl.ds(pl.multiple_of((j - nf) * tn, tn), tn)
        ffn = jnp.dot(act_scr[:, tf:], wd_ref[...], preferred_element_type=F32)
        o_ref[:, cols] = x1_ref[...] + mod_ref[5:6, cols] * ffn

    @pl.when(j == nf + nn - 1)
    def _():
        x2 = o_ref[...]
        ms = jnp.mean(x2 * x2, axis=-1, keepdims=True)
        o_ref[...] = x2 * lax.rsqrt(ms + EPS) * g_ref[...]


def _ffn(x1, xm2, mod, w_up, conv_w, conv_b, w_down, final_g, *, tm, seq_len):
    n, d = x1.shape
    d_ff = w_down.shape[0]
    tf, tn = FFN_TF, FFN_TN
    nf, nn = d_ff // tf, d // tn
    assert tm % seq_len == 0 and seq_len & (seq_len - 1) == 0 and d_ff % tf == 0 and (tf // 2) % V7X_MXU_WIDTH == 0
    seqs_per_tile = tm // seq_len
    seq_of = (lambda i: i * seqs_per_tile) if mod.shape[0] > 1 else (lambda i: 0)
    assert mod.shape[0] == 1 or seqs_per_tile == 1
    cb = conv_b.reshape(1, 2 * d_ff)

    def up(j):
        return jnp.minimum(j, nf - 1)

    def down(j):
        return jnp.maximum(j - nf, 0)
    return pl.pallas_call(
        functools.partial(_ffn_kernel, seq_len=seq_len, nf=nf, nn=nn, d_ff=d_ff),
        grid=(n // tm, nf + nn),
        in_specs=[pl.BlockSpec((tm, d), lambda i, j: (i, 0), pipeline_mode=pl.Buffered(1)),
                  pl.BlockSpec((d, tf), lambda i, j: (0, up(j))),
                  pl.BlockSpec((d, tf), lambda i, j: (0, nf + up(j))),
                  pl.BlockSpec((3, 2 * d_ff), lambda i, j: (0, 0)),
                  pl.BlockSpec((1, 2 * d_ff), lambda i, j: (0, 0)),
                  pl.BlockSpec((d_ff, tn), lambda i, j: (0, down(j))),
                  pl.BlockSpec((tm, tn), lambda i, j: (i, down(j))),
                  pl.BlockSpec((None, 6, d), lambda i, j: (seq_of(i), 0, 0)),
                  pl.BlockSpec((1, d), lambda i, j: (0, 0))],
        out_specs=pl.BlockSpec((tm, d), lambda i, j: (i, 0)),
        out_shape=jax.ShapeDtypeStruct((n, d), F32),
        scratch_shapes=[pltpu.VMEM((tm, tf + d_ff), BF16), pltpu.VMEM((2, tm, tf // 2), F32)],
        compiler_params=_cparams(("arbitrary", "arbitrary")),
        name="ffn",
    )(xm2, w_up, w_up, conv_w, cb, w_down, x1, mod, final_g.reshape(1, d))


def _rope_tables(seq_len):
    rows = seq_len // GRID_W
    row = jnp.repeat(jnp.arange(rows, dtype=F32), GRID_W)
    colp = jnp.tile(jnp.arange(GRID_W, dtype=F32), rows)
    n_freq = HEAD_DIM // 4
    inv = ROPE_THETA ** (-jnp.arange(n_freq, dtype=F32) / n_freq)
    ang_r = row[:, None] * inv[None, :]
    ang_c = colp[:, None] * inv[None, :]
    cos = jnp.concatenate([jnp.cos(ang_r), jnp.cos(ang_r), jnp.cos(ang_c), jnp.cos(ang_c)], axis=1)
    sin = jnp.concatenate([-jnp.sin(ang_r), jnp.sin(ang_r), -jnp.sin(ang_c), jnp.sin(ang_c)], axis=1)
    return cos, sin


PROJ_TM = 1024
MIXER_TM = 256


def kernel(x_prompt, x_sample, cache_k, cache_v, state_ssm_re, state_ssm_im, c, c_ctx, norm_mix_g, norm_ffn_g,
           w_mod, b_mod, w_in, ssm_lambda_re, ssm_lambda_im, ssm_log_dt, ssm_b_re, ssm_b_im, ssm_c_re, ssm_c_im,
           ssm_d, w_glu, attn_sink, w_ssm_o, w_attn_o, w_out, w_up, conv_w, conv_b, w_down, final_norm_g):
    batch, seq, d = x_prompt.shape
    dec_batch, dec_seq, _ = x_sample.shape
    depth = w_in.shape[0]
    assert depth == 1, "final norm is fused into the (single) layer's ffn kernel"
    d_ssm = w_glu.shape[1]
    d_attn = N_HEADS * HEAD_DIM
    d_kv = N_KV_HEADS * HEAD_DIM
    groups = d_ssm // SSM_CH
    assert w_in.shape[2] == d_ssm + d_attn + 2 * d_kv + 2 * d
    assert ssm_lambda_re.shape[2:] == (groups, SSM_STATE) and dec_batch <= 8 - 1
    l = 0

    col = {"gs": 0, "ga": d, "q": 2 * d, "u": 2 * d + d_attn, "k": 2 * d + d_attn + d_ssm,
           "d_ssm": d_ssm, "d_attn": d_attn, "d_kv": d_kv}
    xp = x_prompt.reshape(batch * seq, d)
    xs = x_sample.reshape(dec_batch * dec_seq, d)

    cond = jnp.concatenate([c_ctx[None], c, jnp.zeros((8 - 1 - dec_batch, d), F32)], axis=0)

    ssm_ops, mod, (w_in_b,) = _ssm_operators_and_mod(
        ssm_lambda_re[l], ssm_lambda_im[l], ssm_log_dt[l], ssm_b_re[l], ssm_b_im[l], ssm_c_re[l], ssm_c_im[l],
        ssm_d[l], cond, w_mod[l], b_mod[l], riders=(w_in[l],))
    mod = mod.reshape(8, 6, d)
    mod_p, mod_s = mod[0:1], mod[1:1 + dec_batch]
    proj_p, tail_p, k_raw, v_raw = _proj(xp, mod_p, norm_mix_g[l], w_in_b, tm=PROJ_TM,
                                         tiles_per_seq=max(seq // PROJ_TM, 1), col=col)
    proj_s, tail_s, _, _ = _proj(xs, mod_s, norm_mix_g[l], w_in_b, tm=PROJ_TM,
                                 tiles_per_seq=max(dec_seq // PROJ_TM, 1), col=col)

    def lanes(s):
        return s.transpose(0, 2, 1, 3).reshape(dec_batch, groups, 2 * SSM_STATE)
    h0_p = jnp.zeros((batch, groups * 4 * SSM_STATE), F32)
    h0_s = _slab_lanes(jnp.concatenate([lanes(state_ssm_re[:, l]), lanes(state_ssm_im[:, l])], axis=-1))
    y_ssm_p, h_fin, (w_up_b,) = _ssm(tail_p, 0, ssm_ops, h0_p, n_seq=batch, seq_len=seq,
                                     seq_block=min(batch, SSM_SEQ_BLOCK), riders=(w_up[l],))
    y_ssm_s, _, (w_down_b,) = _ssm(tail_s, 0, ssm_ops, h0_s, n_seq=dec_batch, seq_len=dec_seq,
                                   seq_block=min(dec_batch, SSM_SEQ_BLOCK), riders=(w_down[l],))

    sink = attn_sink[l]
    o_p, mixer_w = _attn_ctx(proj_p, sink, n_seq=batch, seq_len=seq, col=col,
                             riders=(w_glu[l], w_ssm_o[l], w_attn_o[l], w_out[l]))
    o_s = _attn_lat(proj_s, cache_k[:, l].reshape(dec_batch, -1, d_kv), cache_v[:, l].reshape(dec_batch, -1, d_kv),
                    sink, n_seq=dec_batch, seq_len=dec_seq, col=col)

    x1_p, xm2_p = _mixer_out(xp, y_ssm_p, o_p, proj_p, mod_p, norm_ffn_g[l], *mixer_w, tm=MIXER_TM,
                             tiles_per_seq=max(seq // MIXER_TM, 1), col=col)
    x1_s, xm2_s = _mixer_out(xs, y_ssm_s, o_s, proj_s, mod_s, norm_ffn_g[l], *mixer_w, tm=MIXER_TM,
                             tiles_per_seq=max(dec_seq // MIXER_TM, 1), col=col)
    y_p = _ffn(x1_p, xm2_p, mod_p, w_up_b, conv_w[l], conv_b[l], w_down_b, final_norm_g, tm=FFN_TM, seq_len=seq)
    y_s = _ffn(x1_s, xm2_s, mod_s, w_up_b, conv_w[l], conv_b[l], w_down_b, final_norm_g, tm=FFN_TM, seq_len=dec_seq)

    new_k = k_raw.reshape(batch, 1, seq, N_KV_HEADS, HEAD_DIM)
    new_v = v_raw.reshape(batch, 1, seq, N_KV_HEADS, HEAD_DIM)

    def unlanes(hl):
        return hl.reshape(batch, groups, 2, SSM_STATE).transpose(0, 2, 1, 3)[:, None]
    h_fin = _unslab_lanes(h_fin, groups)
    new_re = unlanes(h_fin[:, :, :2 * SSM_STATE])
    new_im = unlanes(h_fin[:, :, 2 * SSM_STATE:])
    return (y_p.reshape(batch, seq, d), y_s.reshape(dec_batch, dec_seq, d), new_k, new_v, new_re, new_im)
```

```python
import functools
import math

import jax
import jax.numpy as jnp
from jax import lax
from jax.experimental import pallas as pl
from jax.experimental.pallas import tpu as pltpu

F32 = jnp.float32
BF16 = jnp.bfloat16

GRID_W = 64
SSM_CH = 16
SSM_STATE = 64
N_HEADS = 8
N_KV_HEADS = 2
HEAD_DIM = 128
Q_PER_KV = N_HEADS // N_KV_HEADS
WINDOW = 128
BLOCK = 128
ROPE_THETA = 10000.0
EPS = 1e-6
NEG_INF = -1e30

SSM_CHUNK = 16
SSM_LANES = SSM_CHUNK * SSM_CH
SSM_SEQ_BLOCK = 16

V7X_VMEM_LIMIT_BYTES = 56 * 1024 * 1024
V7X_MXU_WIDTH = 256


def _cparams(semantics):
    return pltpu.CompilerParams(dimension_semantics=semantics, vmem_limit_bytes=V7X_VMEM_LIMIT_BYTES)


def _sigmoid(x):
    return 1.0 / (1.0 + jnp.exp(-x))


def _ride_casts(body, n_in, n_out, n_riders):
    def kernel(*refs):
        ins = refs[:n_in]
        rider_ins = refs[n_in:n_in + n_riders]
        outs = refs[n_in + n_riders:n_in + n_riders + n_out]
        rider_outs = refs[n_in + n_riders + n_out:n_in + 2 * n_riders + n_out]
        scratch = refs[n_in + 2 * n_riders + n_out:]
        for src, dst in zip(rider_ins, rider_outs):
            dst[...] = src[...].astype(BF16)
        body(*ins, *outs, *scratch)
    return kernel


def _rider_specs(weights, n_steps, step_of):
    specs, shapes = [], []
    for w in weights:
        r, c = w.shape
        assert r % (16 * n_steps) == 0
        specs.append(pl.BlockSpec((r // n_steps, c), lambda *idx: (step_of(*idx), 0)))
        shapes.append(jax.ShapeDtypeStruct((r, c), BF16))
    return specs, shapes


def _rms_modulate(x, g, scale, shift):
    ms = jnp.mean(x * x, axis=-1, keepdims=True)
    return (x * lax.rsqrt(ms + EPS) * g) * (1.0 + scale) + shift


MOD_K_CHUNK = 256


def _mod_kernel(c_ref, w_ref, b_ref, o_ref):
    c = c_ref[...]
    a = (c * _sigmoid(c)).astype(BF16)
    acc = b_ref[...]
    for k in range(0, w_ref.shape[0], MOD_K_CHUNK):
        acc = acc + jnp.dot(a[:, k:k + MOD_K_CHUNK], w_ref[k:k + MOD_K_CHUNK, :].astype(BF16),
                            preferred_element_type=F32)
    o_ref[...] = acc


PROJ_TN = 512
PROJ_NORM_ROWS = 16


def _swap32(x):
    n = x.shape[-1]
    lane = lax.broadcasted_iota(jnp.int32, x.shape, x.ndim - 1)
    return jnp.where((lane & 63) < 32, pltpu.roll(x, n - 32, x.ndim - 1), pltpu.roll(x, 32, x.ndim - 1))


def _proj_kernel(x_ref, mod_ref, g_ref, w_ref, o_ref, tail_ref, k_ref, v_ref, xm_scr, *, q_tiles):
    j = pl.program_id(1)

    @pl.when(j == 0)
    def _():
        gain = g_ref[...] * (1.0 + mod_ref[1:2, :])
        shift = mod_ref[0:1, :]

        def block(c, carry):
            rows = pl.ds(pl.multiple_of(c * PROJ_NORM_ROWS, PROJ_NORM_ROWS), PROJ_NORM_ROWS)
            x = x_ref[rows, :]
            ms = jnp.mean(x * x, axis=-1, keepdims=True)
            xm_scr[rows, :] = (x * lax.rsqrt(ms + EPS) * gain + shift).astype(BF16)
            return carry

        lax.fori_loop(0, x_ref.shape[0] // PROJ_NORM_ROWS, block, 0, unroll=8)

    acc = jnp.dot(xm_scr[...], w_ref[...], preferred_element_type=F32)
    scale = jnp.where((j >= q_tiles[0]) & (j < q_tiles[1]), HEAD_DIM ** -0.5, 1.0)
    o_ref[...] = (acc * scale).astype(BF16)
    tail_ref[...] = acc
    d_kv = k_ref.shape[1]
    k_ref[...] = acc[:, :d_kv]
    v_ref[...] = acc[:, d_kv:]


def _proj(x, mod, norm_g, w_in, *, tm, tiles_per_seq, col):
    n, d = x.shape
    cols = w_in.shape[1]
    tn = PROJ_TN
    n_tiles = cols // tn
    d_kv = col["d_kv"]
    n_gate, n_q, n_u = col["q"] // tn, col["d_attn"] // tn, col["d_ssm"] // tn
    q_tiles = (n_gate, n_gate + n_q)
    tail0 = col["u"] // tn
    n_tail = n_tiles - tail0
    assert 2 * d_kv == tn and col["u"] == col["q"] + col["d_attn"] and col["k"] == cols - tn

    def src_tile(j):
        return jnp.where(j < n_gate, j + (n_u + n_q + 1),
                         jnp.where(j < n_gate + n_q, j - n_gate + n_u,
                                   jnp.where(j < n_gate + n_q + n_u, j - (n_gate + n_q), n_u + n_q)))
    seq_of = (lambda i: i // tiles_per_seq) if mod.shape[0] > 1 else (lambda i: 0)
    return pl.pallas_call(
        functools.partial(_proj_kernel, q_tiles=q_tiles),
        grid=(n // tm, n_tiles),
        in_specs=[pl.BlockSpec((tm, d), lambda i, j: (i, 0)),
                  pl.BlockSpec((None, 6, d), lambda i, j: (seq_of(i), 0, 0)),
                  pl.BlockSpec((1, d), lambda i, j: (0, 0)),
                  pl.BlockSpec((d, tn), lambda i, j: (0, src_tile(j)))],
        out_specs=[pl.BlockSpec((tm, tn), lambda i, j: (i, j)),
                   pl.BlockSpec((tm, tn), lambda i, j: (i, jnp.clip(j - tail0, 0, n_tail - 1))),
                   pl.BlockSpec((tm, d_kv), lambda i, j: (i, 0)),
                   pl.BlockSpec((tm, d_kv), lambda i, j: (i, 0))],
        out_shape=[jax.ShapeDtypeStruct((n, cols), BF16), jax.ShapeDtypeStruct((n, n_tail * tn), F32),
                   jax.ShapeDtypeStruct((n, d_kv), F32), jax.ShapeDtypeStruct((n, d_kv), F32)],
        scratch_shapes=[pltpu.VMEM((tm, d), BF16)],
        compiler_params=_cparams(("arbitrary", "arbitrary")),
        name="proj",
    )(x, mod, norm_g.reshape(1, d), w_in)


def _cmul(ar, ai, br, bi):
    return ar * br - ai * bi, ar * bi + ai * br


def _cpow(base_r, base_i, n, bits):
    res_r = jnp.ones_like(base_r)
    res_i = jnp.zeros_like(base_r)
    for bit in range(bits):
        nr, ni = _cmul(res_r, res_i, base_r, base_i)
        take = ((n >> bit) & 1) == 1
        res_r = jnp.where(take, nr, res_r)
        res_i = jnp.where(take, ni, res_i)
        if bit + 1 < bits:
            base_r, base_i = _cmul(base_r, base_i, base_r, base_i)
    return res_r, res_i


def _ssm_disc_kernel(lre_ref, lim_ref, ldt_ref, ere_ref, eim_ref, qre_ref, qim_ref):
    lre = lre_ref[...]
    lim = lim_ref[...]
    dt = jnp.exp(ldt_ref[...])
    mag = jnp.exp(lre * dt)
    e_re = mag * jnp.cos(lim * dt)
    e_im = mag * jnp.sin(lim * dt)
    n_re = e_re - 1.0
    den = lre * lre + lim * lim
    ere_ref[...] = e_re
    eim_ref[...] = e_im
    qre_ref[...] = (n_re * lre + e_im * lim) / den
    qim_ref[...] = (e_im * lre - n_re * lim) / den


def _ssm_disc(lam_re, lam_im, log_dt):
    two, g, p = lam_re.shape
    shp = jax.ShapeDtypeStruct((two * g, p), F32)
    outs = pl.pallas_call(_ssm_disc_kernel, out_shape=[shp] * 4, name="ssm_disc")(
        lam_re.reshape(two * g, p), lam_im.reshape(two * g, p), log_dt.reshape(two * g, 1))
    return [o.reshape(two, g, p) for o in outs]


SSM_OPS_GROUPS = 4


def _ssm_ops_mod_kernel(ecol_ref, rows_ref, b_ref, ct_ref, d_ref, c_ref, wm_ref, bm_ref,
                        m_ref, w_ref, v_ref, a_ref, mod_ref):
    _mod_kernel(c_ref, wm_ref, bm_ref, mod_ref)
    _ssm_ops_kernel(ecol_ref, rows_ref, b_ref, ct_ref, d_ref, m_ref, w_ref, v_ref, a_ref)


def _ssm_ops_kernel(ecol_ref, rows_ref, b_ref, ct_ref, d_ref, m_ref, w_ref, v_ref, a_ref):
    def one(gi, carry):
        _ssm_ops_group(ecol_ref.at[gi], rows_ref.at[gi], b_ref.at[gi], ct_ref.at[gi], d_ref.at[gi],
                       m_ref.at[gi], w_ref.at[gi], v_ref.at[gi], a_ref.at[gi])
        return carry

    lax.fori_loop(0, ecol_ref.shape[0], one, 0)


def _ssm_ops_group(ecol_ref, rows_ref, b_ref, ct_ref, d_ref, m_ref, w_ref, v_ref, a_ref):
    L, ch, p = SSM_CHUNK, SSM_CH, SSM_STATE
    lanes = SSM_LANES
    half = 2 * p
    lane_c = lax.broadcasted_iota(jnp.int32, (p, lanes), 1)
    lag = lane_c // ch

    def col(k):
        return jnp.broadcast_to(ecol_ref[k], (p, lanes))

    f0 = _cpow(col(0), col(1), lag, 4)
    b0 = _cpow(col(2), col(3), (L - 1) - lag, 4)
    f1 = _cmul(f0[0], f0[1], col(0), col(1))
    b1 = _cmul(b0[0], b0[1], col(2), col(3))
    def ct(k):
        return jnp.tile(ct_ref[k], (1, L))

    cf = (ct(0), ct(1))
    cb = (ct(2), ct(3))
    yf0 = _cmul(*f0, *cf)
    yb0 = _cmul(*b0, *cb)
    yf1 = _cmul(*f1, *cf)
    yb1 = _cmul(*b1, *cb)
    v_ref[...] = jnp.concatenate([yf1[0], yb1[0], -yf1[1], -yb1[1]], axis=0).astype(BF16)

    lane_r = lax.broadcasted_iota(jnp.int32, (ch, lanes), 1)
    sgn = jnp.where(lane_r < half, -1.0, 1.0)
    b1raw = b_ref[...]
    b2raw = pltpu.roll(b1raw, half, 1) * sgn
    bb1 = rows_ref[2:3, :] * b1raw + rows_ref[3:4, :] * b2raw
    bb2 = pltpu.roll(bb1, half, 1) * sgn

    zero = jnp.zeros((p, lanes), F32)
    hi = lax.Precision.HIGHEST
    r0f = jnp.dot(bb1, jnp.concatenate([yf0[0], zero, -yf0[1], zero], axis=0), precision=hi,
                  preferred_element_type=F32)
    r0b = jnp.dot(bb1, jnp.concatenate([zero, yb0[0], zero, -yb0[1]], axis=0), precision=hi,
                  preferred_element_type=F32)
    row_r = lax.broadcasted_iota(jnp.int32, (ch, lanes), 0)
    d_col = d_ref[...]
    for s in range(L):
        blk = jnp.where(lane_r == ch * s + row_r, d_col, 0.0)
        fwd = r0f if s == 0 else jnp.where(lane_r >= ch * s, pltpu.roll(r0f, ch * s, 1), 0.0)
        k = ch * (L - 1 - s)
        bwd = r0b if k == 0 else jnp.where(lane_r < lanes - k, pltpu.roll(r0b, lanes - k, 1), 0.0)
        m_ref[s * ch:(s + 1) * ch, :] = (blk + fwd + bwd).astype(BF16)

    is_f = (lane_r & (half - 1)) < p
    n_row = jnp.where(is_f, (L - 1) - row_r, row_r)
    er, ei = _cpow(jnp.broadcast_to(rows_ref[0:1, :], (L, lanes)), jnp.broadcast_to(rows_ref[1:2, :], (L, lanes)),
                   n_row, 4)
    for s in range(L):
        w_ref[s * ch:(s + 1) * ch, :] = (er[s:s + 1, :] * bb1 + ei[s:s + 1, :] * bb2).astype(BF16)

    ar, ai = rows_ref[0:1, :], rows_ref[1:2, :]
    for _ in range(4):
        ar, ai = _cmul(ar, ai, ar, ai)
    lane_1 = lax.broadcasted_iota(jnp.int32, (1, lanes), 1)
    a_ref[...] = jnp.where(lane_1 < half, ar, ai)


def _ssm_operators_and_mod(lam_re, lam_im, log_dt, b_re, b_im, c_re, c_im, ssm_d, cond, w_mod, b_mod, riders=()):
    two, g, p = lam_re.shape
    ch = b_re.shape[-1]
    lanes = SSM_LANES
    assert (two, p, ch, SSM_CHUNK * ch) == (2, SSM_STATE, SSM_CH, lanes)
    e_re, e_im, q_re, q_im = _ssm_disc(lam_re, lam_im, log_dt)
    ecol = jnp.stack([e_re[0], e_im[0], e_re[1], e_im[1]], axis=1)[..., None]

    def fbfb(x):
        return jnp.concatenate([x[0], x[1], x[0], x[1]], axis=-1)
    rows = jnp.stack([fbfb(e_re), fbfb(e_im), fbfb(q_re), fbfb(q_im)], axis=1)

    def bt(x):
        return jnp.swapaxes(x, -1, -2)
    b_rows = jnp.concatenate([bt(b_re[0]), bt(b_re[1]), bt(b_im[0]), bt(b_im[1])], axis=-1)

    ct = jnp.stack([bt(c_re[0]), bt(c_im[0]), bt(c_re[1]), bt(c_im[1])], axis=1)
    d_col = ssm_d.reshape(g, ch, 1)
    gb = SSM_OPS_GROUPS
    mat = pl.BlockSpec((gb, lanes, lanes), lambda i: (i, 0, 0))
    steps = g // gb
    rider_specs, rider_shapes = _rider_specs(riders, steps, lambda i: i)
    c_rows, d_model = cond.shape
    n_mod = w_mod.shape[1]
    tn = n_mod // steps
    assert n_mod % steps == 0 and tn % 128 == 0
    outs = pl.pallas_call(
        _ride_casts(_ssm_ops_mod_kernel, 8, 5, len(riders)),
        grid=(steps,),
        in_specs=[pl.BlockSpec((gb, 4, p, 1), lambda i: (i, 0, 0, 0)),
                  pl.BlockSpec((gb, 4, lanes), lambda i: (i, 0, 0)),
                  pl.BlockSpec((gb, ch, lanes), lambda i: (i, 0, 0)),
                  pl.BlockSpec((gb, 4, p, ch), lambda i: (i, 0, 0, 0)),
                  pl.BlockSpec((gb, ch, 1), lambda i: (i, 0, 0)),
                  pl.BlockSpec((c_rows, d_model), lambda i: (0, 0)),
                  pl.BlockSpec((d_model, tn), lambda i: (0, i)),
                  pl.BlockSpec((1, tn), lambda i: (0, i))] + rider_specs,
        out_specs=[mat, mat, mat, pl.BlockSpec((gb, 1, lanes), lambda i: (i, 0, 0)),
                   pl.BlockSpec((c_rows, tn), lambda i: (0, i))] + rider_specs,
        out_shape=([jax.ShapeDtypeStruct((g, lanes, lanes), BF16)] * 3 + [jax.ShapeDtypeStruct((g, 1, lanes), F32),
                                                                          jax.ShapeDtypeStruct((c_rows, n_mod), F32)]
                   + rider_shapes),
        compiler_params=_cparams(("arbitrary",)),
        name="ssm_ops_mod",
    )(ecol, rows, b_rows, ct, d_col, cond, w_mod, b_mod.reshape(1, n_mod), *riders)
    return outs[:4], outs[4], outs[5:]


SSM_SLAB_GROUPS = 128 // SSM_CH


def _chunk_transpose(xs, chunk_id):
    for k in (4, 2, 1):
        keep = (chunk_id & k) == 0
        new = list(xs)
        for i in range(len(xs)):
            if i & k == 0:
                a, b = xs[i], xs[i + k]
                if 2 * SSM_CH * k == 128:
                    moved = pltpu.roll(jnp.where(keep, b, a), SSM_CH * k, 1)
                    new[i] = jnp.where(keep, a, moved)
                    new[i + k] = jnp.where(keep, moved, b)
                else:
                    new[i] = jnp.where(keep, a, pltpu.roll(b, SSM_CH * k, 1))
                    new[i + k] = jnp.where(keep, pltpu.roll(a, 128 - SSM_CH * k, 1), b)
        xs = new
    return xs


def _ssm_kernel(u_ref, m_ref, w_ref, v_ref, a_ref, h0_ref, y_ref, hfin_ref, ug_scr, s_scr, hf_scr, hb_scr, yg_scr,
                *, n_chunks, ns):
    gps = SSM_SLAB_GROUPS
    nc = ns * n_chunks
    half = 2 * SSM_STATE
    chunk_id = lax.broadcasted_iota(jnp.int32, (nc, 128), 1) // SSM_CH
    for hf in range(2):
        xs = [u_ref[pl.ds(hf * gps + i, nc, stride=SSM_CHUNK), :] for i in range(gps)]
        xs = _chunk_transpose(xs, chunk_id)
        for g in range(gps):
            ug_scr[g, :, hf * 128:(hf + 1) * 128] = xs[g].astype(BF16)
    pitch = n_chunks + 8
    for g in range(gps):
        s = jnp.dot(ug_scr[g], w_ref[g], preferred_element_type=F32)
        for q in range(ns):
            s_scr[g, q * pitch:q * pitch + n_chunks, :] = s[q * n_chunks:(q + 1) * n_chunks, :half]
            s_scr[gps + g, q * pitch:q * pitch + n_chunks, :] = s[q * n_chunks:(q + 1) * n_chunks, half:]
    is_fwd = lax.broadcasted_iota(jnp.int32, (1, half), 1) < SSM_STATE
    blocks = range(2 * gps)
    a = [a_ref[:, b * half:(b + 1) * half] for b in blocks]
    h = [h0_ref[:, b * half:(b + 1) * half] for b in blocks]

    def rows(c):
        return pl.ds(c, ns, stride=pitch)

    for k in range(n_chunks):
        kb = n_chunks - 1 - k
        new_h = list(h)
        for b in blocks:
            hf_scr[b, rows(k), :] = h[b]
            hb_scr[b, rows(kb), :] = h[b]
        for g in range(gps):
            s_re = jnp.where(is_fwd, s_scr[g, rows(k), :], s_scr[g, rows(kb), :])
            s_im = jnp.where(is_fwd, s_scr[gps + g, rows(k), :], s_scr[gps + g, rows(kb), :])
            new_h[g] = a[g] * h[g] - a[gps + g] * h[gps + g] + s_re
            new_h[gps + g] = a[g] * h[gps + g] + a[gps + g] * h[g] + s_im
        h = new_h
    for b in blocks:
        hfin_ref[:, b * half:(b + 1) * half] = h[b]

    def entering(b):
        return jnp.concatenate(
            [jnp.where(is_fwd, hf_scr[b, q * pitch:q * pitch + n_chunks, :], hb_scr[b, q * pitch:q * pitch + n_chunks, :])
             for q in range(ns)], axis=0)

    for g in range(gps):
        hin = jnp.concatenate([entering(g), entering(gps + g)], axis=1).astype(BF16)
        yg_scr[g] = (jnp.dot(ug_scr[g], m_ref[g], preferred_element_type=F32)
                     + jnp.dot(hin, v_ref[g], preferred_element_type=F32))
    for hf in range(2):
        xs = [yg_scr[g, :, hf * 128:(hf + 1) * 128] for g in range(gps)]
        xs = _chunk_transpose(xs, chunk_id)
        for i in range(gps):
            y_ref[pl.ds(hf * gps + i, nc, stride=SSM_CHUNK), :] = xs[i]


def _slab_lanes(x):
    lead = x.shape[:-2]
    g = x.shape[-2]
    gps = SSM_SLAB_GROUPS
    x = x.reshape(lead + (g // gps, gps, 2, 2 * SSM_STATE))
    x = jnp.swapaxes(x, -3, -2)
    return x.reshape(lead + (g * 4 * SSM_STATE,))


def _unslab_lanes(x, g):
    lead = x.shape[:-1]
    gps = SSM_SLAB_GROUPS
    x = x.reshape(lead + (g // gps, 2, gps, 2 * SSM_STATE))
    x = jnp.swapaxes(x, -3, -2)
    return x.reshape(lead + (g, 4 * SSM_STATE))


def _ssm(u, u_col, ops, h0, *, n_seq, seq_len, seq_block, riders=()):
    m_op, w_op, v_op, a_op = ops
    n = u.shape[0]
    g = m_op.shape[0]
    d_ssm = g * SSM_CH
    slab0 = u_col // 128
    gps = SSM_SLAB_GROUPS
    lanes = SSM_LANES
    n_chunks = seq_len // SSM_CHUNK
    nc = seq_block * n_chunks
    padded = seq_block * (n_chunks + 8)
    rows = seq_block * seq_len
    wide2 = gps * 4 * SSM_STATE
    mat = pl.BlockSpec((gps, lanes, lanes), lambda j, b: (j, 0, 0))
    n_blocks = n_seq // seq_block
    rider_specs, rider_shapes = _rider_specs(riders, (g // gps) * n_blocks, lambda j, b: j * n_blocks + b)
    outs = pl.pallas_call(
        _ride_casts(functools.partial(_ssm_kernel, n_chunks=n_chunks, ns=seq_block), 6, 2, len(riders)),
        grid=(g // gps, n_blocks),
        in_specs=[pl.BlockSpec((rows, 128), lambda j, b: (b, slab0 + j)), mat, mat, mat,
                  pl.BlockSpec((1, wide2), lambda j, b: (0, j)),
                  pl.BlockSpec((seq_block, wide2), lambda j, b: (b, j))] + rider_specs,
        out_specs=[pl.BlockSpec((rows, 128), lambda j, b: (b, j)),
                   pl.BlockSpec((seq_block, wide2), lambda j, b: (b, j))] + rider_specs,
        out_shape=[jax.ShapeDtypeStruct((n, d_ssm), F32),
                   jax.ShapeDtypeStruct((n_seq, g * 4 * SSM_STATE), F32)] + rider_shapes,
        scratch_shapes=[pltpu.VMEM((gps, nc, lanes), BF16), pltpu.VMEM((2 * gps, padded, 128), F32),
                        pltpu.VMEM((2 * gps, padded, 128), F32), pltpu.VMEM((2 * gps, padded, 128), F32),
                        pltpu.VMEM((gps, nc, lanes), F32)],
        compiler_params=_cparams(("arbitrary", "arbitrary")),
        name="ssm",
    )(u, m_op, w_op, v_op, _slab_lanes(a_op[:, 0])[None], h0, *riders)
    return outs[0], outs[1], outs[2:]


def _softmax_pv(pieces, sink):
    m = sink
    for s, _ in pieces:
        m = jnp.maximum(m, jnp.max(s, axis=-1, keepdims=True))
    out = None
    for s, v in pieces:
        pv = jnp.dot(jnp.exp(s - m).astype(BF16), v, preferred_element_type=F32)
        out = pv if out is None else out + pv
    return out[:, :HEAD_DIM] / (out[:, HEAD_DIM:] + jnp.exp(sink - m))


def _qk(q, k):
    return lax.dot_general(q, k, (((1,), (1,)), ((), ())), preferred_element_type=F32)


def _with_ones(v):
    return jnp.concatenate([v, jnp.ones_like(v)], axis=1)


ATTN_CTX_SEQS = 4


def _attn_ctx_kernel(sink_ref, q_ref, kv_ref, o_ref, *, seq_len):
    d_kv = N_KV_HEADS * HEAD_DIM
    for r0 in range(0, q_ref.shape[0], seq_len):
        rows = slice(r0, r0 + seq_len)
        for kh in range(N_KV_HEADS):
            k = kv_ref[rows, kh * HEAD_DIM:(kh + 1) * HEAD_DIM]
            v = _with_ones(kv_ref[rows, d_kv + kh * HEAD_DIM:d_kv + (kh + 1) * HEAD_DIM])
            for h in range(kh * Q_PER_KV, (kh + 1) * Q_PER_KV):
                q = q_ref[rows, h * HEAD_DIM:(h + 1) * HEAD_DIM]
                o = _softmax_pv([(_qk(q, k), v)], sink_ref[h])
                o_ref[rows, h * HEAD_DIM:(h + 1) * HEAD_DIM] = o.astype(BF16)


def _attn_ctx(proj, sink, *, n_seq, seq_len, col, riders=()):
    d_attn = col["d_attn"]
    kv_w = 2 * col["d_kv"]
    spb = ATTN_CTX_SEQS if n_seq % ATTN_CTX_SEQS == 0 else 1
    rows = spb * seq_len
    rider_specs, rider_shapes = _rider_specs(riders, n_seq // spb, lambda b: b)
    outs = pl.pallas_call(
        _ride_casts(functools.partial(_attn_ctx_kernel, seq_len=seq_len), 3, 1, len(riders)),
        grid=(n_seq // spb,),
        in_specs=[pl.BlockSpec(memory_space=pltpu.SMEM),
                  pl.BlockSpec((rows, d_attn), lambda b: (b, col["q"] // d_attn)),
                  pl.BlockSpec((rows, kv_w), lambda b: (b, col["k"] // kv_w))] + rider_specs,
        out_specs=[pl.BlockSpec((rows, d_attn), lambda b: (b, 0))] + rider_specs,
        out_shape=[jax.ShapeDtypeStruct((n_seq * seq_len, d_attn), BF16)] + rider_shapes,
        compiler_params=_cparams(("arbitrary",)),
        name="attn_ctx",
    )(sink, proj, proj, *riders)
    return outs[0], outs[1:]


def _rope(x, cos, sin):
    reps = x.shape[1] // HEAD_DIM
    xf = x.astype(F32)
    return (xf * jnp.concatenate([cos] * reps, axis=1) + _swap32(xf) * jnp.concatenate([sin] * reps, axis=1)).astype(BF16)


def _attn_lat_kernel(sink_ref, q_ref, kvp_ref, kvc_ref, kvn_ref, ck_ref, cv_ref, cos_ref, sin_ref, o_ref, *, n_blocks):
    i = pl.program_id(1)
    d_kv = N_KV_HEADS * HEAD_DIM

    def tables(blk):
        rows = pl.ds(pl.multiple_of(blk * BLOCK, BLOCK), BLOCK)
        return cos_ref[rows, :], sin_ref[rows, :]

    q_all = _rope(q_ref[...], *tables(i))
    k_all = jnp.concatenate([_rope(kvp_ref[:, :d_kv], *tables(jnp.maximum(i - 1, 0))),
                             _rope(kvc_ref[:, :d_kv], *tables(i)),
                             _rope(kvn_ref[:, :d_kv], *tables(jnp.minimum(i + 1, n_blocks - 1)))], axis=0)
    r = lax.broadcasted_iota(jnp.int32, (BLOCK, 3 * BLOCK), 0)
    c = lax.broadcasted_iota(jnp.int32, (BLOCK, 3 * BLOCK), 1)
    cc = c & (BLOCK - 1)
    valid = (((c < BLOCK) & (cc >= r) & (i > 0)) | ((c >= BLOCK) & (c < 2 * BLOCK))
             | ((c >= 2 * BLOCK) & (cc <= r) & (i < n_blocks - 1)))
    for kh in range(N_KV_HEADS):
        ks = slice(kh * HEAD_DIM, (kh + 1) * HEAD_DIM)
        vs = slice(d_kv + kh * HEAD_DIM, d_kv + (kh + 1) * HEAD_DIM)
        k_loc = k_all[:, ks]
        k_ctx = ck_ref[:, ks].astype(BF16)
        v_loc = _with_ones(jnp.concatenate([kvp_ref[:, vs], kvc_ref[:, vs], kvn_ref[:, vs]], axis=0))
        v_ctx = _with_ones(cv_ref[:, ks].astype(BF16))
        for h in range(kh * Q_PER_KV, (kh + 1) * Q_PER_KV):
            q = q_all[:, h * HEAD_DIM:(h + 1) * HEAD_DIM]
            s_loc = jnp.where(valid, _qk(q, k_loc), NEG_INF)
            o = _softmax_pv([(s_loc, v_loc), (_qk(q, k_ctx), v_ctx)], sink_ref[h])
            o_ref[:, h * HEAD_DIM:(h + 1) * HEAD_DIM] = o.astype(BF16)


def _attn_lat(proj, cache_k, cache_v, sink, *, n_seq, seq_len, col):
    d_attn = col["d_attn"]
    kv_w = 2 * col["d_kv"]
    assert WINDOW == BLOCK and seq_len % BLOCK == 0
    nb = seq_len // BLOCK
    kv_col = col["k"] // kv_w
    past, d_kv = cache_k.shape[1], cache_k.shape[2]

    def kv_spec(off):
        return pl.BlockSpec((BLOCK, kv_w), lambda b, i: (b * nb + jnp.clip(i + off, 0, nb - 1), kv_col))
    cache_spec = pl.BlockSpec((None, past, d_kv), lambda b, i: (b, 0, 0))
    table_spec = pl.BlockSpec((seq_len, HEAD_DIM), lambda b, i: (0, 0))
    cos, sin = _rope_tables(seq_len)
    return pl.pallas_call(
        functools.partial(_attn_lat_kernel, n_blocks=nb),
        grid=(n_seq, nb),
        in_specs=[pl.BlockSpec(memory_space=pltpu.SMEM),
                  pl.BlockSpec((BLOCK, d_attn), lambda b, i: (b * nb + i, col["q"] // d_attn)),
                  kv_spec(-1), kv_spec(0), kv_spec(1), cache_spec, cache_spec, table_spec, table_spec],
        out_specs=pl.BlockSpec((BLOCK, d_attn), lambda b, i: (b * nb + i, 0)),
        out_shape=jax.ShapeDtypeStruct((n_seq * seq_len, d_attn), BF16),
        compiler_params=_cparams(("arbitrary", "arbitrary")),
        name="attn_lat",
    )(sink, proj, proj, proj, proj, cache_k, cache_v, cos, sin)


def _gelu_tanh(x):
    return 0.5 * x * (1.0 + jnp.tanh(math.sqrt(2.0 / math.pi) * (x + 0.044715 * (x * x * x))))


def _mixer_out_kernel(x_ref, y_ref, o_ref, gs_ref, ga_ref, mod_ref, g_ref, wglu_ref, wso_ref, wao_ref, wout_ref,
                      x1_ref, xm2_ref):
    z = _gelu_tanh(y_ref[...].astype(F32))
    z = z * _sigmoid(jnp.dot(z.astype(BF16), wglu_ref[...], preferred_element_type=F32))
    s_br = jnp.dot(z.astype(BF16), wso_ref[...], preferred_element_type=F32)
    a_br = jnp.dot(o_ref[...], wao_ref[...], preferred_element_type=F32)
    merged = _sigmoid(gs_ref[...].astype(F32)) * s_br + _sigmoid(ga_ref[...].astype(F32)) * a_br
    out = jnp.dot(merged.astype(BF16), wout_ref[...], preferred_element_type=F32)
    x1 = x_ref[...] + mod_ref[2:3, :] * out
    x1_ref[...] = x1
    xm2_ref[...] = _rms_modulate(x1, g_ref[...], mod_ref[4:5, :], mod_ref[3:4, :]).astype(BF16)


def _mixer_out(x, y, o, proj, mod, norm_g, w_glu, w_ssm_o, w_attn_o, w_out, *, tm, tiles_per_seq, col):
    n, d = x.shape
    d_ssm = y.shape[1]
    d_attn = o.shape[1]
    seq_of = (lambda i: i // tiles_per_seq) if mod.shape[0] > 1 else (lambda i: 0)

    def resident(shape):
        return pl.BlockSpec(shape, lambda i: (0, 0), pipeline_mode=pl.Buffered(1))
    return pl.pallas_call(
        _mixer_out_kernel,
        grid=(n // tm,),
        in_specs=[pl.BlockSpec((tm, d), lambda i: (i, 0)),
                  pl.BlockSpec((tm, d_ssm), lambda i: (i, 0)),
                  pl.BlockSpec((tm, d_attn), lambda i: (i, 0)),
                  pl.BlockSpec((tm, d), lambda i: (i, col["gs"] // d)),
                  pl.BlockSpec((tm, d), lambda i: (i, col["ga"] // d)),
                  pl.BlockSpec((None, 6, d), lambda i: (seq_of(i), 0, 0)),
                  pl.BlockSpec((1, d), lambda i: (0, 0)),
                  resident(w_glu.shape), resident(w_ssm_o.shape), resident(w_attn_o.shape), resident(w_out.shape)],
        out_specs=[pl.BlockSpec((tm, d), lambda i: (i, 0)), pl.BlockSpec((tm, d), lambda i: (i, 0))],
        out_shape=[jax.ShapeDtypeStruct((n, d), F32), jax.ShapeDtypeStruct((n, d), BF16)],
        compiler_params=_cparams(("arbitrary",)),
        name="mixer_out",
    )(x, y, o, proj, proj, mod, norm_g.reshape(1, d), w_glu, w_ssm_o, w_attn_o, w_out)


FFN_TM = 1024
FFN_TF = 512
FFN_TN = 256


def _ffn_kernel(xm_ref, wa_ref, wb_ref, cw_ref, cb_ref, wd_ref, x1_ref, mod_ref, g_ref, o_ref, act_scr, raw_scr,
                *, seq_len, nf, nn, d_ff):
    j = pl.program_id(1)
    tm = xm_ref.shape[0]
    tf = wa_ref.shape[1]
    tn = wd_ref.shape[1]
    ch = tf // 2
    pos = lax.broadcasted_iota(jnp.int32, (tm, 1), 0) & (seq_len - 1)
    has_prev = pos != 0
    has_next = pos != seq_len - 1

    def lanes(off):
        return slice(off, off + ch) if isinstance(off, int) else pl.ds(pl.multiple_of(off, ch), ch)

    def gate(ha, hb, col0):
        def conv(h, off):
            cols = lanes(off)
            cw = cw_ref[:, cols]
            h_prev = jnp.where(has_prev, pltpu.roll(h, 1, 0), 0.0)
            h_next = jnp.where(has_next, pltpu.roll(h, tm - 1, 0), 0.0)
            return cw[0:1] * h_prev + cw[1:2] * h + cw[2:3] * h_next + cb_ref[:, cols]

        a = conv(ha, col0)
        b = conv(hb, d_ff + col0)
        return ((a * _sigmoid(a)) * b).astype(BF16)

    def act_cols(col0):
        return lanes(tf + col0)

    @pl.when(j == 0)
    def _():
        raw_scr[...] = jnp.zeros_like(raw_scr)

    @pl.when(j < nf)
    def _():
        xm = xm_ref[...]
        ha0 = jnp.dot(xm, wa_ref[:, :ch], preferred_element_type=F32)
        hb0 = jnp.dot(xm, wb_ref[:, :ch], preferred_element_type=F32)
        carried = (j - 1) * tf + ch
        act_scr[:, act_cols(carried)] = gate(raw_scr[0], raw_scr[1], jnp.maximum(carried, 0))
        ha1 = jnp.dot(xm, wa_ref[:, ch:], preferred_element_type=F32)
        hb1 = jnp.dot(xm, wb_ref[:, ch:], preferred_element_type=F32)
        act_scr[:, act_cols(j * tf)] = gate(ha0, hb0, j * tf)
        raw_scr[0] = ha1
        raw_scr[1] = hb1

    @pl.when(j == nf)
    def _():
        last = (nf - 1) * tf + ch
        act_scr[:, act_cols(last)] = gate(raw_scr[0], raw_scr[1], last)

    @pl.when(j >= nf)
    def _():
        cols = pl.ds(pl.multiple_of((j - nf) * tn, tn), tn)
        ffn = jnp.dot(act_scr[:, tf:], wd_ref[...], preferred_element_type=F32)
        o_ref[:, cols] = x1_ref[...] + mod_ref[5:6, cols] * ffn

    @pl.when(j == nf + nn - 1)
    def _():
        x2 = o_ref[...]
        ms = jnp.mean(x2 * x2, axis=-1, keepdims=True)
        o_ref[...] = x2 * lax.rsqrt(ms + EPS) * g_ref[...]


def _ffn(x1, xm2, mod, w_up, conv_w, conv_b, w_down, final_g, *, tm, seq_len):
    n, d = x1.shape
    d_ff = w_down.shape[0]
    tf, tn = FFN_TF, FFN_TN
    nf, nn = d_ff // tf, d // tn
    assert tm % seq_len == 0 and seq_len & (seq_len - 1) == 0 and d_ff % tf == 0 and (tf // 2) % V7X_MXU_WIDTH == 0
    seqs_per_tile = tm // seq_len
    seq_of = (lambda i: i * seqs_per_tile) if mod.shape[0] > 1 else (lambda i: 0)
    assert mod.shape[0] == 1 or seqs_per_tile == 1
    cb = conv_b.reshape(1, 2 * d_ff)

    def up(j):
        return jnp.minimum(j, nf - 1)

    def down(j):
        return jnp.maximum(j - nf, 0)
    return pl.pallas_call(
        functools.partial(_ffn_kernel, seq_len=seq_len, nf=nf, nn=nn, d_ff=d_ff),
        grid=(n // tm, nf + nn),
        in_specs=[pl.BlockSpec((tm, d), lambda i, j: (i, 0), pipeline_mode=pl.Buffered(1)),
                  pl.BlockSpec((d, tf), lambda i, j: (0, up(j))),
                  pl.BlockSpec((d, tf), lambda i, j: (0, nf + up(j))),
                  pl.BlockSpec((3, 2 * d_ff), lambda i, j: (0, 0)),
                  pl.BlockSpec((1, 2 * d_ff), lambda i, j: (0, 0)),
                  pl.BlockSpec((d_ff, tn), lambda i, j: (0, down(j))),
                  pl.BlockSpec((tm, tn), lambda i, j: (i, down(j))),
                  pl.BlockSpec((None, 6, d), lambda i, j: (seq_of(i), 0, 0)),
                  pl.BlockSpec((1, d), lambda i, j: (0, 0))],
        out_specs=pl.BlockSpec((tm, d), lambda i, j: (i, 0)),
        out_shape=jax.ShapeDtypeStruct((n, d), F32),
        scratch_shapes=[pltpu.VMEM((tm, tf + d_ff), BF16), pltpu.VMEM((2, tm, tf // 2), F32)],
        compiler_params=_cparams(("arbitrary", "arbitrary")),
        name="ffn",
    )(xm2, w_up, w_up, conv_w, cb, w_down, x1, mod, final_g.reshape(1, d))


def _rope_tables(seq_len):
    rows = seq_len // GRID_W
    row = jnp.repeat(jnp.arange(rows, dtype=F32), GRID_W)
    colp = jnp.tile(jnp.arange(GRID_W, dtype=F32), rows)
    n_freq = HEAD_DIM // 4
    inv = ROPE_THETA ** (-jnp.arange(n_freq, dtype=F32) / n_freq)
    ang_r = row[:, None] * inv[None, :]
    ang_c = colp[:, None] * inv[None, :]
    cos = jnp.concatenate([jnp.cos(ang_r), jnp.cos(ang_r), jnp.cos(ang_c), jnp.cos(ang_c)], axis=1)
    sin = jnp.concatenate([-jnp.sin(ang_r), jnp.sin(ang_r), -jnp.sin(ang_c), jnp.sin(ang_c)], axis=1)
    return cos, sin


PROJ_TM = 1024
MIXER_TM = 256


def kernel(x_prompt, x_sample, cache_k, cache_v, state_ssm_re, state_ssm_im, c, c_ctx, norm_mix_g, norm_ffn_g,
           w_mod, b_mod, w_in, ssm_lambda_re, ssm_lambda_im, ssm_log_dt, ssm_b_re, ssm_b_im, ssm_c_re, ssm_c_im,
           ssm_d, w_glu, attn_sink, w_ssm_o, w_attn_o, w_out, w_up, conv_w, conv_b, w_down, final_norm_g):
    batch, seq, d = x_prompt.shape
    dec_batch, dec_seq, _ = x_sample.shape
    depth = w_in.shape[0]
    assert depth == 1, "final norm is fused into the (single) layer's ffn kernel"
    d_ssm = w_glu.shape[1]
    d_attn = N_HEADS * HEAD_DIM
    d_kv = N_KV_HEADS * HEAD_DIM
    groups = d_ssm // SSM_CH
    assert w_in.shape[2] == d_ssm + d_attn + 2 * d_kv + 2 * d
    assert ssm_lambda_re.shape[2:] == (groups, SSM_STATE) and dec_batch <= 8 - 1
    l = 0

    col = {"gs": 0, "ga": d, "q": 2 * d, "u": 2 * d + d_attn, "k": 2 * d + d_attn + d_ssm,
           "d_ssm": d_ssm, "d_attn": d_attn, "d_kv": d_kv}
    xp = x_prompt.reshape(batch * seq, d)
    xs = x_sample.reshape(dec_batch * dec_seq, d)

    cond = jnp.concatenate([c_ctx[None], c, jnp.zeros((8 - 1 - dec_batch, d), F32)], axis=0)

    ssm_ops, mod, (w_in_b,) = _ssm_operators_and_mod(
        ssm_lambda_re[l], ssm_lambda_im[l], ssm_log_dt[l], ssm_b_re[l], ssm_b_im[l], ssm_c_re[l], ssm_c_im[l],
        ssm_d[l], cond, w_mod[l], b_mod[l], riders=(w_in[l],))
    mod = mod.reshape(8, 6, d)
    mod_p, mod_s = mod[0:1], mod[1:1 + dec_batch]
    proj_p, tail_p, k_raw, v_raw = _proj(xp, mod_p, norm_mix_g[l], w_in_b, tm=PROJ_TM,
                                         tiles_per_seq=max(seq // PROJ_TM, 1), col=col)
    proj_s, tail_s, _, _ = _proj(xs, mod_s, norm_mix_g[l], w_in_b, tm=PROJ_TM,
                                 tiles_per_seq=max(dec_seq // PROJ_TM, 1), col=col)

    def lanes(s):
        return s.transpose(0, 2, 1, 3).reshape(dec_batch, groups, 2 * SSM_STATE)
    h0_p = jnp.zeros((batch, groups * 4 * SSM_STATE), F32)
    h0_s = _slab_lanes(jnp.concatenate([lanes(state_ssm_re[:, l]), lanes(state_ssm_im[:, l])], axis=-1))
    y_ssm_p, h_fin, (w_up_b,) = _ssm(tail_p, 0, ssm_ops, h0_p, n_seq=batch, seq_len=seq,
                                     seq_block=min(batch, SSM_SEQ_BLOCK), riders=(w_up[l],))
    y_ssm_s, _, (w_down_b,) = _ssm(tail_s, 0, ssm_ops, h0_s, n_seq=dec_batch, seq_len=dec_seq,
                                   seq_block=min(dec_batch, SSM_SEQ_BLOCK), riders=(w_down[l],))

    sink = attn_sink[l]
    o_p, mixer_w = _attn_ctx(proj_p, sink, n_seq=batch, seq_len=seq, col=col,
                             riders=(w_glu[l], w_ssm_o[l], w_attn_o[l], w_out[l]))
    o_s = _attn_lat(proj_s, cache_k[:, l].reshape(dec_batch, -1, d_kv), cache_v[:, l].reshape(dec_batch, -1, d_kv),
                    sink, n_seq=dec_batch, seq_len=dec_seq, col=col)

    x1_p, xm2_p = _mixer_out(xp, y_ssm_p, o_p, proj_p, mod_p, norm_ffn_g[l], *mixer_w, tm=MIXER_TM,
                             tiles_per_seq=max(seq // MIXER_TM, 1), col=col)
    x1_s, xm2_s = _mixer_out(xs, y_ssm_s, o_s, proj_s, mod_s, norm_ffn_g[l], *mixer_w, tm=MIXER_TM,
                             tiles_per_seq=max(dec_seq // MIXER_TM, 1), col=col)
    y_p = _ffn(x1_p, xm2_p, mod_p, w_up_b, conv_w[l], conv_b[l], w_down_b, final_norm_g, tm=FFN_TM, seq_len=seq)
    y_s = _ffn(x1_s, xm2_s, mod_s, w_up_b, conv_w[l], conv_b[l], w_down_b, final_norm_g, tm=FFN_TM, seq_len=dec_seq)

    new_k = k_raw.reshape(batch, 1, seq, N_KV_HEADS, HEAD_DIM)
    new_v = v_raw.reshape(batch, 1, seq, N_KV_HEADS, HEAD_DIM)

    def unlanes(hl):
        return hl.reshape(batch, groups, 2, SSM_STATE).transpose(0, 2, 1, 3)[:, None]
    h_fin = _unslab_lanes(h_fin, groups)
    new_re = unlanes(h_fin[:, :, :2 * SSM_STATE])
    new_im = unlanes(h_fin[:, :, 2 * SSM_STATE:])
    return (y_p.reshape(batch, seq, d), y_s.reshape(dec_batch, dec_seq, d), new_k, new_v, new_re, new_im)
```

```python
import functools
import math

import jax
import jax.numpy as jnp
from jax import lax
from jax.experimental import pallas as pl
from jax.experimental.pallas import tpu as pltpu

F32 = jnp.float32
BF16 = jnp.bfloat16

GRID_W = 64
SSM_CH = 16
SSM_STATE = 64
N_HEADS = 8
N_KV_HEADS = 2
HEAD_DIM = 128
Q_PER_KV = N_HEADS // N_KV_HEADS
WINDOW = 128
BLOCK = 128
ROPE_THETA = 10000.0
EPS = 1e-6
NEG_INF = -1e30

SSM_CHUNK = 16
SSM_LANES = SSM_CHUNK * SSM_CH
SSM_SEQ_BLOCK = 16

V7X_VMEM_LIMIT_BYTES = 56 * 1024 * 1024
V7X_MXU_WIDTH = 256


def _cparams(semantics):
    return pltpu.CompilerParams(dimension_semantics=semantics, vmem_limit_bytes=V7X_VMEM_LIMIT_BYTES)


def _sigmoid(x):
    return 1.0 / (1.0 + jnp.exp(-x))


def _ride_casts(body, n_in, n_out, n_riders):
    def kernel(*refs):
        ins = refs[:n_in]
        rider_ins = refs[n_in:n_in + n_riders]
        outs = refs[n_in + n_riders:n_in + n_riders + n_out]
        rider_outs = refs[n_in + n_riders + n_out:n_in + 2 * n_riders + n_out]
        scratch = refs[n_in + 2 * n_riders + n_out:]
        for src, dst in zip(rider_ins, rider_outs):
            dst[...] = src[...].astype(BF16)
        body(*ins, *outs, *scratch)
    return kernel


def _rider_specs(weights, n_steps, step_of):
    specs, shapes = [], []
    for w in weights:
        r, c = w.shape
        assert r % (16 * n_steps) == 0
        specs.append(pl.BlockSpec((r // n_steps, c), lambda *idx: (step_of(*idx), 0)))
        shapes.append(jax.ShapeDtypeStruct((r, c), BF16))
    return specs, shapes


def _rms_modulate(x, g, scale, shift):
    ms = jnp.mean(x * x, axis=-1, keepdims=True)
    return (x * lax.rsqrt(ms + EPS) * g) * (1.0 + scale) + shift


MOD_K_CHUNK = 256


def _mod_kernel(c_ref, w_ref, b_ref, o_ref):
    c = c_ref[...]
    a = (c * _sigmoid(c)).astype(BF16)
    acc = b_ref[...]
    for k in range(0, w_ref.shape[0], MOD_K_CHUNK):
        acc = acc + jnp.dot(a[:, k:k + MOD_K_CHUNK], w_ref[k:k + MOD_K_CHUNK, :].astype(BF16),
                            preferred_element_type=F32)
    o_ref[...] = acc


PROJ_TN = 512
PROJ_NORM_ROWS = 16


def _swap32(x):
    n = x.shape[-1]
    lane = lax.broadcasted_iota(jnp.int32, x.shape, x.ndim - 1)
    return jnp.where((lane & 63) < 32, pltpu.roll(x, n - 32, x.ndim - 1), pltpu.roll(x, 32, x.ndim - 1))


def _proj_kernel(x_ref, mod_ref, g_ref, w_ref, o_ref, tail_ref, k_ref, v_ref, xm_scr, *, q_tiles):
    j = pl.program_id(1)

    @pl.when(j == 0)
    def _():
        gain = g_ref[...] * (1.0 + mod_ref[1:2, :])
        shift = mod_ref[0:1, :]

        def block(c, carry):
            rows = pl.ds(pl.multiple_of(c * PROJ_NORM_ROWS, PROJ_NORM_ROWS), PROJ_NORM_ROWS)
            x = x_ref[rows, :]
            ms = jnp.mean(x * x, axis=-1, keepdims=True)
            xm_scr[rows, :] = (x * lax.rsqrt(ms + EPS) * gain + shift).astype(BF16)
            return carry

        lax.fori_loop(0, x_ref.shape[0] // PROJ_NORM_ROWS, block, 0, unroll=8)

    acc = jnp.dot(xm_scr[...], w_ref[...], preferred_element_type=F32)
    scale = jnp.where((j >= q_tiles[0]) & (j < q_tiles[1]), HEAD_DIM ** -0.5, 1.0)
    o_ref[...] = (acc * scale).astype(BF16)
    tail_ref[...] = acc
    d_kv = k_ref.shape[1]
    k_ref[...] = acc[:, :d_kv]
    v_ref[...] = acc[:, d_kv:]


def _proj(x, mod, norm_g, w_in, *, tm, tiles_per_seq, col):
    n, d = x.shape
    cols = w_in.shape[1]
    tn = PROJ_TN
    n_tiles = cols // tn
    d_kv = col["d_kv"]
    n_gate, n_q, n_u = col["q"] // tn, col["d_attn"] // tn, col["d_ssm"] // tn
    q_tiles = (n_gate, n_gate + n_q)
    tail0 = col["u"] // tn
    n_tail = n_tiles - tail0
    assert 2 * d_kv == tn and col["u"] == col["q"] + col["d_attn"] and col["k"] == cols - tn

    def src_tile(j):
        return jnp.where(j < n_gate, j + (n_u + n_q + 1),
                         jnp.where(j < n_gate + n_q, j - n_gate + n_u,
                                   jnp.where(j < n_gate + n_q + n_u, j - (n_gate + n_q), n_u + n_q)))
    seq_of = (lambda i: i // tiles_per_seq) if mod.shape[0] > 1 else (lambda i: 0)
    return pl.pallas_call(
        functools.partial(_proj_kernel, q_tiles=q_tiles),
        grid=(n // tm, n_tiles),
        in_specs=[pl.BlockSpec((tm, d), lambda i, j: (i, 0)),
                  pl.BlockSpec((None, 6, d), lambda i, j: (seq_of(i), 0, 0)),
                  pl.BlockSpec((1, d), lambda i, j: (0, 0)),
                  pl.BlockSpec((d, tn), lambda i, j: (0, src_tile(j)))],
        out_specs=[pl.BlockSpec((tm, tn), lambda i, j: (i, j)),
                   pl.BlockSpec((tm, tn), lambda i, j: (i, jnp.clip(j - tail0, 0, n_tail - 1))),
                   pl.BlockSpec((tm, d_kv), lambda i, j: (i, 0)),
                   pl.BlockSpec((tm, d_kv), lambda i, j: (i, 0))],
        out_shape=[jax.ShapeDtypeStruct((n, cols), BF16), jax.ShapeDtypeStruct((n, n_tail * tn), F32),
                   jax.ShapeDtypeStruct((n, d_kv), F32), jax.ShapeDtypeStruct((n, d_kv), F32)],
        scratch_shapes=[pltpu.VMEM((tm, d), BF16)],
        compiler_params=_cparams(("arbitrary", "arbitrary")),
        name="proj",
    )(x, mod, norm_g.reshape(1, d), w_in)


def _cmul(ar, ai, br, bi):
    return ar * br - ai * bi, ar * bi + ai * br


def _cpow(base_r, base_i, n, bits):
    res_r = jnp.ones_like(base_r)
    res_i = jnp.zeros_like(base_r)
    for bit in range(bits):
        nr, ni = _cmul(res_r, res_i, base_r, base_i)
        take = ((n >> bit) & 1) == 1
        res_r = jnp.where(take, nr, res_r)
        res_i = jnp.where(take, ni, res_i)
        if bit + 1 < bits:
            base_r, base_i = _cmul(base_r, base_i, base_r, base_i)
    return res_r, res_i


def _ssm_disc_kernel(lre_ref, lim_ref, ldt_ref, ere_ref, eim_ref, qre_ref, qim_ref):
    lre = lre_ref[...]
    lim = lim_ref[...]
    dt = jnp.exp(ldt_ref[...])
    mag = jnp.exp(lre * dt)
    e_re = mag * jnp.cos(lim * dt)
    e_im = mag * jnp.sin(lim * dt)
    n_re = e_re - 1.0
    den = lre * lre + lim * lim
    ere_ref[...] = e_re
    eim_ref[...] = e_im
    qre_ref[...] = (n_re * lre + e_im * lim) / den
    qim_ref[...] = (e_im * lre - n_re * lim) / den


def _ssm_disc(lam_re, lam_im, log_dt):
    two, g, p = lam_re.shape
    shp = jax.ShapeDtypeStruct((two * g, p), F32)
    outs = pl.pallas_call(_ssm_disc_kernel, out_shape=[shp] * 4, name="ssm_disc")(
        lam_re.reshape(two * g, p), lam_im.reshape(two * g, p), log_dt.reshape(two * g, 1))
    return [o.reshape(two, g, p) for o in outs]


SSM_OPS_GROUPS = 4


def _ssm_ops_mod_kernel(ecol_ref, rows_ref, b_ref, ct_ref, d_ref, c_ref, wm_ref, bm_ref,
                        m_ref, w_ref, v_ref, a_ref, mod_ref):
    _mod_kernel(c_ref, wm_ref, bm_ref, mod_ref)
    _ssm_ops_kernel(ecol_ref, rows_ref, b_ref, ct_ref, d_ref, m_ref, w_ref, v_ref, a_ref)


def _ssm_ops_kernel(ecol_ref, rows_ref, b_ref, ct_ref, d_ref, m_ref, w_ref, v_ref, a_ref):
    def one(gi, carry):
        _ssm_ops_group(ecol_ref.at[gi], rows_ref.at[gi], b_ref.at[gi], ct_ref.at[gi], d_ref.at[gi],
                       m_ref.at[gi], w_ref.at[gi], v_ref.at[gi], a_ref.at[gi])
        return carry

    lax.fori_loop(0, ecol_ref.shape[0], one, 0, unroll=True)


def _ssm_ops_group(ecol_ref, rows_ref, b_ref, ct_ref, d_ref, m_ref, w_ref, v_ref, a_ref):
    L, ch, p = SSM_CHUNK, SSM_CH, SSM_STATE
    lanes = SSM_LANES
    half = 2 * p
    lane_c = lax.broadcasted_iota(jnp.int32, (p, lanes), 1)
    lag = lane_c // ch

    def col(k):
        return jnp.broadcast_to(ecol_ref[k], (p, lanes))

    f0 = _cpow(col(0), col(1), lag, 4)
    b0 = _cpow(col(2), col(3), (L - 1) - lag, 4)
    f1 = _cmul(f0[0], f0[1], col(0), col(1))
    b1 = _cmul(b0[0], b0[1], col(2), col(3))
    def ct(k):
        return jnp.tile(ct_ref[k], (1, L))

    cf = (ct(0), ct(1))
    cb = (ct(2), ct(3))
    yf0 = _cmul(*f0, *cf)
    yb0 = _cmul(*b0, *cb)
    yf1 = _cmul(*f1, *cf)
    yb1 = _cmul(*b1, *cb)
    v_ref[...] = jnp.concatenate([yf1[0], yb1[0], -yf1[1], -yb1[1]], axis=0).astype(BF16)

    lane_r = lax.broadcasted_iota(jnp.int32, (ch, lanes), 1)
    sgn = jnp.where(lane_r < half, -1.0, 1.0)
    b1raw = b_ref[...]
    b2raw = pltpu.roll(b1raw, half, 1) * sgn
    bb1 = rows_ref[2:3, :] * b1raw + rows_ref[3:4, :] * b2raw
    bb2 = pltpu.roll(bb1, half, 1) * sgn

    zero = jnp.zeros((p, lanes), F32)
    hi = lax.Precision.HIGHEST
    r0f = jnp.dot(bb1, jnp.concatenate([yf0[0], zero, -yf0[1], zero], axis=0), precision=hi,
                  preferred_element_type=F32)
    r0b = jnp.dot(bb1, jnp.concatenate([zero, yb0[0], zero, -yb0[1]], axis=0), precision=hi,
                  preferred_element_type=F32)
    row_r = lax.broadcasted_iota(jnp.int32, (ch, lanes), 0)
    d_col = d_ref[...]
    for s in range(L):
        blk = jnp.where(lane_r == ch * s + row_r, d_col, 0.0)
        fwd = r0f if s == 0 else jnp.where(lane_r >= ch * s, pltpu.roll(r0f, ch * s, 1), 0.0)
        k = ch * (L - 1 - s)
        bwd = r0b if k == 0 else jnp.where(lane_r < lanes - k, pltpu.roll(r0b, lanes - k, 1), 0.0)
        m_ref[s * ch:(s + 1) * ch, :] = (blk + fwd + bwd).astype(BF16)

    is_f = (lane_r & (half - 1)) < p
    n_row = jnp.where(is_f, (L - 1) - row_r, row_r)
    er, ei = _cpow(jnp.broadcast_to(rows_ref[0:1, :], (L, lanes)), jnp.broadcast_to(rows_ref[1:2, :], (L, lanes)),
                   n_row, 4)
    for s in range(L):
        w_ref[s * ch:(s + 1) * ch, :] = (er[s:s + 1, :] * bb1 + ei[s:s + 1, :] * bb2).astype(BF16)

    ar, ai = rows_ref[0:1, :], rows_ref[1:2, :]
    for _ in range(4):
        ar, ai = _cmul(ar, ai, ar, ai)
    lane_1 = lax.broadcasted_iota(jnp.int32, (1, lanes), 1)
    a_ref[...] = jnp.where(lane_1 < half, ar, ai)


def _ssm_operators_and_mod(lam_re, lam_im, log_dt, b_re, b_im, c_re, c_im, ssm_d, cond, w_mod, b_mod, riders=()):
    two, g, p = lam_re.shape
    ch = b_re.shape[-1]
    lanes = SSM_LANES
    assert (two, p, ch, SSM_CHUNK * ch) == (2, SSM_STATE, SSM_CH, lanes)
    e_re, e_im, q_re, q_im = _ssm_disc(lam_re, lam_im, log_dt)
    ecol = jnp.stack([e_re[0], e_im[0], e_re[1], e_im[1]], axis=1)[..., None]

    def fbfb(x):
        return jnp.concatenate([x[0], x[1], x[0], x[1]], axis=-1)
    rows = jnp.stack([fbfb(e_re), fbfb(e_im), fbfb(q_re), fbfb(q_im)], axis=1)

    def bt(x):
        return jnp.swapaxes(x, -1, -2)
    b_rows = jnp.concatenate([bt(b_re[0]), bt(b_re[1]), bt(b_im[0]), bt(b_im[1])], axis=-1)

    ct = jnp.stack([bt(c_re[0]), bt(c_im[0]), bt(c_re[1]), bt(c_im[1])], axis=1)
    d_col = ssm_d.reshape(g, ch, 1)
    gb = SSM_OPS_GROUPS
    mat = pl.BlockSpec((gb, lanes, lanes), lambda i: (i, 0, 0))
    steps = g // gb
    rider_specs, rider_shapes = _rider_specs(riders, steps, lambda i: i)
    c_rows, d_model = cond.shape
    n_mod = w_mod.shape[1]
    tn = n_mod // steps
    assert n_mod % steps == 0 and tn % 128 == 0
    outs = pl.pallas_call(
        _ride_casts(_ssm_ops_mod_kernel, 8, 5, len(riders)),
        grid=(steps,),
        in_specs=[pl.BlockSpec((gb, 4, p, 1), lambda i: (i, 0, 0, 0)),
                  pl.BlockSpec((gb, 4, lanes), lambda i: (i, 0, 0)),
                  pl.BlockSpec((gb, ch, lanes), lambda i: (i, 0, 0)),
                  pl.BlockSpec((gb, 4, p, ch), lambda i: (i, 0, 0, 0)),
                  pl.BlockSpec((gb, ch, 1), lambda i: (i, 0, 0)),
                  pl.BlockSpec((c_rows, d_model), lambda i: (0, 0)),
                  pl.BlockSpec((d_model, tn), lambda i: (0, i)),
                  pl.BlockSpec((1, tn), lambda i: (0, i))] + rider_specs,
        out_specs=[mat, mat, mat, pl.BlockSpec((gb, 1, lanes), lambda i: (i, 0, 0)),
                   pl.BlockSpec((c_rows, tn), lambda i: (0, i))] + rider_specs,
        out_shape=([jax.ShapeDtypeStruct((g, lanes, lanes), BF16)] * 3 + [jax.ShapeDtypeStruct((g, 1, lanes), F32),
                                                                          jax.ShapeDtypeStruct((c_rows, n_mod), F32)]
                   + rider_shapes),
        compiler_params=_cparams(("arbitrary",)),
        name="ssm_ops_mod",
    )(ecol, rows, b_rows, ct, d_col, cond, w_mod, b_mod.reshape(1, n_mod), *riders)
    return outs[:4], outs[4], outs[5:]


SSM_SLAB_GROUPS = 128 // SSM_CH


def _chunk_transpose(xs, chunk_id):
    for k in (4, 2, 1):
        keep = (chunk_id & k) == 0
        new = list(xs)
        for i in range(len(xs)):
            if i & k == 0:
                a, b = xs[i], xs[i + k]
                if 2 * SSM_CH * k == 128:
                    moved = pltpu.roll(jnp.where(keep, b, a), SSM_CH * k, 1)
                    new[i] = jnp.where(keep, a, moved)
                    new[i + k] = jnp.where(keep, moved, b)
                else:
                    new[i] = jnp.where(keep, a, pltpu.roll(b, SSM_CH * k, 1))
                    new[i + k] = jnp.where(keep, pltpu.roll(a, 128 - SSM_CH * k, 1), b)
        xs = new
    return xs


def _ssm_kernel(u_ref, m_ref, w_ref, v_ref, a_ref, h0_ref, y_ref, hfin_ref, ug_scr, s_scr, hf_scr, hb_scr, yg_scr,
                *, n_chunks, ns):
    gps = SSM_SLAB_GROUPS
    nc = ns * n_chunks
    half = 2 * SSM_STATE
    chunk_id = lax.broadcasted_iota(jnp.int32, (nc, 128), 1) // SSM_CH
    for hf in range(2):
        xs = [u_ref[pl.ds(hf * gps + i, nc, stride=SSM_CHUNK), :] for i in range(gps)]
        xs = _chunk_transpose(xs, chunk_id)
        for g in range(gps):
            ug_scr[g, :, hf * 128:(hf + 1) * 128] = xs[g].astype(BF16)
    pitch = n_chunks + 8
    for g in range(gps):
        s = jnp.dot(ug_scr[g], w_ref[g], preferred_element_type=F32)
        for q in range(ns):
            s_scr[g, q * pitch:q * pitch + n_chunks, :] = s[q * n_chunks:(q + 1) * n_chunks, :half]
            s_scr[gps + g, q * pitch:q * pitch + n_chunks, :] = s[q * n_chunks:(q + 1) * n_chunks, half:]
    is_fwd = lax.broadcasted_iota(jnp.int32, (1, half), 1) < SSM_STATE
    blocks = range(2 * gps)
    a = [a_ref[:, b * half:(b + 1) * half] for b in blocks]
    h = [h0_ref[:, b * half:(b + 1) * half] for b in blocks]

    def rows(c):
        return pl.ds(c, ns, stride=pitch)

    for k in range(n_chunks):
        kb = n_chunks - 1 - k
        new_h = list(h)
        for b in blocks:
            hf_scr[b, rows(k), :] = h[b]
            hb_scr[b, rows(kb), :] = h[b]
        for g in range(gps):
            s_re = jnp.where(is_fwd, s_scr[g, rows(k), :], s_scr[g, rows(kb), :])
            s_im = jnp.where(is_fwd, s_scr[gps + g, rows(k), :], s_scr[gps + g, rows(kb), :])
            new_h[g] = a[g] * h[g] - a[gps + g] * h[gps + g] + s_re
            new_h[gps + g] = a[g] * h[gps + g] + a[gps + g] * h[g] + s_im
        h = new_h
    for b in blocks:
        hfin_ref[:, b * half:(b + 1) * half] = h[b]

    def entering(b):
        return jnp.concatenate(
            [jnp.where(is_fwd, hf_scr[b, q * pitch:q * pitch + n_chunks, :], hb_scr[b, q * pitch:q * pitch + n_chunks, :])
             for q in range(ns)], axis=0)

    for g in range(gps):
        hin = jnp.concatenate([entering(g), entering(gps + g)], axis=1).astype(BF16)
        yg_scr[g] = (jnp.dot(ug_scr[g], m_ref[g], preferred_element_type=F32)
                     + jnp.dot(hin, v_ref[g], preferred_element_type=F32))
    for hf in range(2):
        xs = [yg_scr[g, :, hf * 128:(hf + 1) * 128] for g in range(gps)]
        xs = _chunk_transpose(xs, chunk_id)
        for i in range(gps):
            y_ref[pl.ds(hf * gps + i, nc, stride=SSM_CHUNK), :] = xs[i]


def _slab_lanes(x):
    lead = x.shape[:-2]
    g = x.shape[-2]
    gps = SSM_SLAB_GROUPS
    x = x.reshape(lead + (g // gps, gps, 2, 2 * SSM_STATE))
    x = jnp.swapaxes(x, -3, -2)
    return x.reshape(lead + (g * 4 * SSM_STATE,))


def _unslab_lanes(x, g):
    lead = x.shape[:-1]
    gps = SSM_SLAB_GROUPS
    x = x.reshape(lead + (g // gps, 2, gps, 2 * SSM_STATE))
    x = jnp.swapaxes(x, -3, -2)
    return x.reshape(lead + (g, 4 * SSM_STATE))


def _ssm(u, u_col, ops, h0, *, n_seq, seq_len, seq_block, riders=()):
    m_op, w_op, v_op, a_op = ops
    n = u.shape[0]
    g = m_op.shape[0]
    d_ssm = g * SSM_CH
    slab0 = u_col // 128
    gps = SSM_SLAB_GROUPS
    lanes = SSM_LANES
    n_chunks = seq_len // SSM_CHUNK
    nc = seq_block * n_chunks
    padded = seq_block * (n_chunks + 8)
    rows = seq_block * seq_len
    wide2 = gps * 4 * SSM_STATE
    mat = pl.BlockSpec((gps, lanes, lanes), lambda j, b: (j, 0, 0))
    n_blocks = n_seq // seq_block
    rider_specs, rider_shapes = _rider_specs(riders, (g // gps) * n_blocks, lambda j, b: j * n_blocks + b)
    outs = pl.pallas_call(
        _ride_casts(functools.partial(_ssm_kernel, n_chunks=n_chunks, ns=seq_block), 6, 2, len(riders)),
        grid=(g // gps, n_blocks),
        in_specs=[pl.BlockSpec((rows, 128), lambda j, b: (b, slab0 + j)), mat, mat, mat,
                  pl.BlockSpec((1, wide2), lambda j, b: (0, j)),
                  pl.BlockSpec((seq_block, wide2), lambda j, b: (b, j))] + rider_specs,
        out_specs=[pl.BlockSpec((rows, 128), lambda j, b: (b, j)),
                   pl.BlockSpec((seq_block, wide2), lambda j, b: (b, j))] + rider_specs,
        out_shape=[jax.ShapeDtypeStruct((n, d_ssm), F32),
                   jax.ShapeDtypeStruct((n_seq, g * 4 * SSM_STATE), F32)] + rider_shapes,
        scratch_shapes=[pltpu.VMEM((gps, nc, lanes), BF16), pltpu.VMEM((2 * gps, padded, 128), F32),
                        pltpu.VMEM((2 * gps, padded, 128), F32), pltpu.VMEM((2 * gps, padded, 128), F32),
                        pltpu.VMEM((gps, nc, lanes), F32)],
        compiler_params=_cparams(("arbitrary", "arbitrary")),
        name="ssm",
    )(u, m_op, w_op, v_op, _slab_lanes(a_op[:, 0])[None], h0, *riders)
    return outs[0], outs[1], outs[2:]


def _softmax_pv(pieces, sink):
    m = sink
    for s, _ in pieces:
        m = jnp.maximum(m, jnp.max(s, axis=-1, keepdims=True))
    out = None
    for s, v in pieces:
        pv = jnp.dot(jnp.exp(s - m).astype(BF16), v, preferred_element_type=F32)
        out = pv if out is None else out + pv
    return out[:, :HEAD_DIM] / (out[:, HEAD_DIM:] + jnp.exp(sink - m))


def _qk(q, k):
    return lax.dot_general(q, k, (((1,), (1,)), ((), ())), preferred_element_type=F32)


def _with_ones(v):
    return jnp.concatenate([v, jnp.ones_like(v)], axis=1)


ATTN_CTX_SEQS = 4


def _attn_ctx_kernel(sink_ref, q_ref, kv_ref, o_ref, *, seq_len):
    d_kv = N_KV_HEADS * HEAD_DIM
    for r0 in range(0, q_ref.shape[0], seq_len):
        rows = slice(r0, r0 + seq_len)
        for kh in range(N_KV_HEADS):
            k = kv_ref[rows, kh * HEAD_DIM:(kh + 1) * HEAD_DIM]
            v = _with_ones(kv_ref[rows, d_kv + kh * HEAD_DIM:d_kv + (kh + 1) * HEAD_DIM])
            for h in range(kh * Q_PER_KV, (kh + 1) * Q_PER_KV):
                q = q_ref[rows, h * HEAD_DIM:(h + 1) * HEAD_DIM]
                o = _softmax_pv([(_qk(q, k), v)], sink_ref[h])
                o_ref[rows, h * HEAD_DIM:(h + 1) * HEAD_DIM] = o.astype(BF16)


def _attn_ctx(proj, sink, *, n_seq, seq_len, col, riders=()):
    d_attn = col["d_attn"]
    kv_w = 2 * col["d_kv"]
    spb = ATTN_CTX_SEQS if n_seq % ATTN_CTX_SEQS == 0 else 1
    rows = spb * seq_len
    rider_specs, rider_shapes = _rider_specs(riders, n_seq // spb, lambda b: b)
    outs = pl.pallas_call(
        _ride_casts(functools.partial(_attn_ctx_kernel, seq_len=seq_len), 3, 1, len(riders)),
        grid=(n_seq // spb,),
        in_specs=[pl.BlockSpec(memory_space=pltpu.SMEM),
                  pl.BlockSpec((rows, d_attn), lambda b: (b, col["q"] // d_attn)),
                  pl.BlockSpec((rows, kv_w), lambda b: (b, col["k"] // kv_w))] + rider_specs,
        out_specs=[pl.BlockSpec((rows, d_attn), lambda b: (b, 0))] + rider_specs,
        out_shape=[jax.ShapeDtypeStruct((n_seq * seq_len, d_attn), BF16)] + rider_shapes,
        compiler_params=_cparams(("arbitrary",)),
        name="attn_ctx",
    )(sink, proj, proj, *riders)
    return outs[0], outs[1:]


def _rope(x, cos, sin):
    reps = x.shape[1] // HEAD_DIM
    xf = x.astype(F32)
    return (xf * jnp.concatenate([cos] * reps, axis=1) + _swap32(xf) * jnp.concatenate([sin] * reps, axis=1)).astype(BF16)


ATTN_LAT_QBLOCKS = 4


def _attn_lat_kernel(*refs, n_blocks, qb):
    sink_ref, q_ref = refs[:2]
    kv_refs = refs[2:4 + qb]
    ck_ref, cv_ref, cos_ref, sin_ref, o_ref = refs[4 + qb:]
    first = pl.program_id(1) * qb
    d_kv = N_KV_HEADS * HEAD_DIM

    def tables(blk):
        rows = pl.ds(pl.multiple_of(blk * BLOCK, BLOCK), BLOCK)
        return cos_ref[rows, :], sin_ref[rows, :]

    k_rot = [_rope(kv_refs[n][:, :d_kv], *tables(jnp.clip(first + n - 1, 0, n_blocks - 1))) for n in range(qb + 2)]
    r = lax.broadcasted_iota(jnp.int32, (BLOCK, 3 * BLOCK), 0)
    c = lax.broadcasted_iota(jnp.int32, (BLOCK, 3 * BLOCK), 1)
    cc = c & (BLOCK - 1)
    for sub in range(qb):
        blk = first + sub
        q_rows = slice(sub * BLOCK, (sub + 1) * BLOCK)
        q_all = _rope(q_ref[q_rows, :], *tables(blk))
        k_all = jnp.concatenate(k_rot[sub:sub + 3], axis=0)
        valid = (((c < BLOCK) & (cc >= r) & (blk > 0)) | ((c >= BLOCK) & (c < 2 * BLOCK))
                 | ((c >= 2 * BLOCK) & (cc <= r) & (blk < n_blocks - 1)))
        for kh in range(N_KV_HEADS):
            ks = slice(kh * HEAD_DIM, (kh + 1) * HEAD_DIM)
            vs = slice(d_kv + kh * HEAD_DIM, d_kv + (kh + 1) * HEAD_DIM)
            k_loc = k_all[:, ks]
            k_ctx = ck_ref[:, ks].astype(BF16)
            v_loc = _with_ones(jnp.concatenate([kv_refs[sub + n][:, vs] for n in range(3)], axis=0))
            v_ctx = _with_ones(cv_ref[:, ks].astype(BF16))
            for h in range(kh * Q_PER_KV, (kh + 1) * Q_PER_KV):
                q = q_all[:, h * HEAD_DIM:(h + 1) * HEAD_DIM]
                s_loc = jnp.where(valid, _qk(q, k_loc), NEG_INF)
                o = _softmax_pv([(s_loc, v_loc), (_qk(q, k_ctx), v_ctx)], sink_ref[h])
                o_ref[q_rows, h * HEAD_DIM:(h + 1) * HEAD_DIM] = o.astype(BF16)


def _attn_lat(proj, cache_k, cache_v, sink, *, n_seq, seq_len, col):
    d_attn = col["d_attn"]
    kv_w = 2 * col["d_kv"]
    assert WINDOW == BLOCK and seq_len % BLOCK == 0
    nb = seq_len // BLOCK
    qb = ATTN_LAT_QBLOCKS if nb % ATTN_LAT_QBLOCKS == 0 else 1
    steps = nb // qb
    kv_col = col["k"] // kv_w
    past, d_kv = cache_k.shape[1], cache_k.shape[2]

    def kv_spec(off):
        return pl.BlockSpec((BLOCK, kv_w), lambda b, i: (b * nb + jnp.clip(i * qb + off, 0, nb - 1), kv_col))
    cache_spec = pl.BlockSpec((None, past, d_kv), lambda b, i: (b, 0, 0))
    table_spec = pl.BlockSpec((seq_len, HEAD_DIM), lambda b, i: (0, 0))
    cos, sin = _rope_tables(seq_len)
    return pl.pallas_call(
        functools.partial(_attn_lat_kernel, n_blocks=nb, qb=qb),
        grid=(n_seq, steps),
        in_specs=[pl.BlockSpec(memory_space=pltpu.SMEM),
                  pl.BlockSpec((qb * BLOCK, d_attn), lambda b, i: (b * steps + i, col["q"] // d_attn))]
                 + [kv_spec(off) for off in range(-1, qb + 1)]
                 + [cache_spec, cache_spec, table_spec, table_spec],
        out_specs=pl.BlockSpec((qb * BLOCK, d_attn), lambda b, i: (b * steps + i, 0)),
        out_shape=jax.ShapeDtypeStruct((n_seq * seq_len, d_attn), BF16),
        compiler_params=_cparams(("arbitrary", "arbitrary")),
        name="attn_lat",
    )(sink, proj, *([proj] * (qb + 2)), cache_k, cache_v, cos, sin)


def _gelu_tanh(x):
    return 0.5 * x * (1.0 + jnp.tanh(math.sqrt(2.0 / math.pi) * (x + 0.044715 * (x * x * x))))


def _mixer_out_kernel(x_ref, y_ref, o_ref, gs_ref, ga_ref, mod_ref, g_ref, wglu_ref, wso_ref, wao_ref, wout_ref,
                      x1_ref, xm2_ref):
    z = _gelu_tanh(y_ref[...].astype(F32))
    z = z * _sigmoid(jnp.dot(z.astype(BF16), wglu_ref[...], preferred_element_type=F32))
    s_br = jnp.dot(z.astype(BF16), wso_ref[...], preferred_element_type=F32)
    a_br = jnp.dot(o_ref[...], wao_ref[...], preferred_element_type=F32)
    merged = _sigmoid(gs_ref[...].astype(F32)) * s_br + _sigmoid(ga_ref[...].astype(F32)) * a_br
    out = jnp.dot(merged.astype(BF16), wout_ref[...], preferred_element_type=F32)
    x1 = x_ref[...] + mod_ref[2:3, :] * out
    x1_ref[...] = x1
    xm2_ref[...] = _rms_modulate(x1, g_ref[...], mod_ref[4:5, :], mod_ref[3:4, :]).astype(BF16)


def _mixer_out(x, y, o, proj, mod, norm_g, w_glu, w_ssm_o, w_attn_o, w_out, *, tm, tiles_per_seq, col):
    n, d = x.shape
    d_ssm = y.shape[1]
    d_attn = o.shape[1]
    seq_of = (lambda i: i // tiles_per_seq) if mod.shape[0] > 1 else (lambda i: 0)

    def resident(shape):
        return pl.BlockSpec(shape, lambda i: (0, 0), pipeline_mode=pl.Buffered(1))
    return pl.pallas_call(
        _mixer_out_kernel,
        grid=(n // tm,),
        in_specs=[pl.BlockSpec((tm, d), lambda i: (i, 0)),
                  pl.BlockSpec((tm, d_ssm), lambda i: (i, 0)),
                  pl.BlockSpec((tm, d_attn), lambda i: (i, 0)),
                  pl.BlockSpec((tm, d), lambda i: (i, col["gs"] // d)),
                  pl.BlockSpec((tm, d), lambda i: (i, col["ga"] // d)),
                  pl.BlockSpec((None, 6, d), lambda i: (seq_of(i), 0, 0)),
                  pl.BlockSpec((1, d), lambda i: (0, 0)),
                  resident(w_glu.shape), resident(w_ssm_o.shape), resident(w_attn_o.shape), resident(w_out.shape)],
        out_specs=[pl.BlockSpec((tm, d), lambda i: (i, 0)), pl.BlockSpec((tm, d), lambda i: (i, 0))],
        out_shape=[jax.ShapeDtypeStruct((n, d), F32), jax.ShapeDtypeStruct((n, d), BF16)],
        compiler_params=_cparams(("arbitrary",)),
        name="mixer_out",
    )(x, y, o, proj, proj, mod, norm_g.reshape(1, d), w_glu, w_ssm_o, w_attn_o, w_out)


FFN_TM = 1024
FFN_TF = 512
FFN_TN = 256


def _ffn_kernel(xm_ref, wa_ref, wb_ref, cw_ref, cb_ref, wd_ref, x1_ref, mod_ref, g_ref, o_ref, act_scr, raw_scr,
                *, seq_len, nf, nn, d_ff):
    j = pl.program_id(1)
    tm = xm_ref.shape[0]
    tf = wa_ref.shape[1]
    tn = wd_ref.shape[1]
    ch = tf // 2
    pos = lax.broadcasted_iota(jnp.int32, (tm, 1), 0) & (seq_len - 1)
    has_prev = pos != 0
    has_next = pos != seq_len - 1

    def lanes(off):
        return slice(off, off + ch) if isinstance(off, int) else pl.ds(pl.multiple_of(off, ch), ch)

    def gate(ha, hb, col0):
        def conv(h, off):
            cols = lanes(off)
            cw = cw_ref[:, cols]
            h_prev = jnp.where(has_prev, pltpu.roll(h, 1, 0), 0.0)
            h_next = jnp.where(has_next, pltpu.roll(h, tm - 1, 0), 0.0)
            return cw[0:1] * h_prev + cw[1:2] * h + cw[2:3] * h_next + cb_ref[:, cols]

        a = conv(ha, col0)
        b = conv(hb, d_ff + col0)
        return ((a * _sigmoid(a)) * b).astype(BF16)

    def act_cols(col0):
        return lanes(tf + col0)

    @pl.when(j == 0)
    def _():
        raw_scr[...] = jnp.zeros_like(raw_scr)

    @pl.when(j < nf)
    def _():
        xm = xm_ref[...]
        ha0 = jnp.dot(xm, wa_ref[:, :ch], preferred_element_type=F32)
        hb0 = jnp.dot(xm, wb_ref[:, :ch], preferred_element_type=F32)
        carried = (j - 1) * tf + ch
        act_scr[:, act_cols(carried)] = gate(raw_scr[0], raw_scr[1], jnp.maximum(carried, 0))
        ha1 = jnp.dot(xm, wa_ref[:, ch:], preferred_element_type=F32)
        hb1 = jnp.dot(xm, wb_ref[:, ch:], preferred_element_type=F32)
        act_scr[:, act_cols(j * tf)] = gate(ha0, hb0, j * tf)
        raw_scr[0] = ha1
        raw_scr[1] = hb1

    @pl.when(j == nf)
    def _():
        last = (nf - 1) * tf + ch
        act_scr[:, act_cols(last)] = gate(raw_scr[0], raw_scr[1], last)

    @pl.when(j >= nf)
    def _():
        cols = pl.ds(pl.multiple_of((j - nf) * tn, tn), tn)
        ffn = jnp.dot(act_scr[:, tf:], wd_ref[...], preferred_element_type=F32)
        o_ref[:, cols] = x1_ref[...] + mod_ref[5:6, cols] * ffn

    @pl.when(j == nf + nn - 1)
    def _():
        x2 = o_ref[...]
        ms = jnp.mean(x2 * x2, axis=-1, keepdims=True)
        o_ref[...] = x2 * lax.rsqrt(ms + EPS) * g_ref[...]


def _ffn(x1, xm2, mod, w_up, conv_w, conv_b, w_down, final_g, *, tm, seq_len):
    n, d = x1.shape
    d_ff = w_down.shape[0]
    tf, tn = FFN_TF, FFN_TN
    nf, nn = d_ff // tf, d // tn
    assert tm % seq_len == 0 and seq_len & (seq_len - 1) == 0 and d_ff % tf == 0 and (tf // 2) % V7X_MXU_WIDTH == 0
    seqs_per_tile = tm // seq_len
    seq_of = (lambda i: i * seqs_per_tile) if mod.shape[0] > 1 else (lambda i: 0)
    assert mod.shape[0] == 1 or seqs_per_tile == 1
    cb = conv_b.reshape(1, 2 * d_ff)

    def up(j):
        return jnp.minimum(j, nf - 1)

    def down(j):
        return jnp.maximum(j - nf, 0)
    return pl.pallas_call(
        functools.partial(_ffn_kernel, seq_len=seq_len, nf=nf, nn=nn, d_ff=d_ff),
        grid=(n // tm, nf + nn),
        in_specs=[pl.BlockSpec((tm, d), lambda i, j: (i, 0), pipeline_mode=pl.Buffered(1)),
                  pl.BlockSpec((d, tf), lambda i, j: (0, up(j))),
                  pl.BlockSpec((d, tf), lambda i, j: (0, nf + up(j))),
                  pl.BlockSpec((3, 2 * d_ff), lambda i, j: (0, 0)),
                  pl.BlockSpec((1, 2 * d_ff), lambda i, j: (0, 0)),
                  pl.BlockSpec((d_ff, tn), lambda i, j: (0, down(j))),
                  pl.BlockSpec((tm, tn), lambda i, j: (i, down(j))),
                  pl.BlockSpec((None, 6, d), lambda i, j: (seq_of(i), 0, 0)),
                  pl.BlockSpec((1, d), lambda i, j: (0, 0))],
        out_specs=pl.BlockSpec((tm, d), lambda i, j: (i, 0)),
        out_shape=jax.ShapeDtypeStruct((n, d), F32),
        scratch_shapes=[pltpu.VMEM((tm, tf + d_ff), BF16), pltpu.VMEM((2, tm, tf // 2), F32)],
        compiler_params=_cparams(("arbitrary", "arbitrary")),
        name="ffn",
    )(xm2, w_up, w_up, conv_w, cb, w_down, x1, mod, final_g.reshape(1, d))


def _rope_tables(seq_len):
    rows = seq_len // GRID_W
    row = jnp.repeat(jnp.arange(rows, dtype=F32), GRID_W)
    colp = jnp.tile(jnp.arange(GRID_W, dtype=F32), rows)
    n_freq = HEAD_DIM // 4
    inv = ROPE_THETA ** (-jnp.arange(n_freq, dtype=F32) / n_freq)
    ang_r = row[:, None] * inv[None, :]
    ang_c = colp[:, None] * inv[None, :]
    cos = jnp.concatenate([jnp.cos(ang_r), jnp.cos(ang_r), jnp.cos(ang_c), jnp.cos(ang_c)], axis=1)
    sin = jnp.concatenate([-jnp.sin(ang_r), jnp.sin(ang_r), -jnp.sin(ang_c), jnp.sin(ang_c)], axis=1)
    return cos, sin


PROJ_TM = 1024
MIXER_TM = 256


def kernel(x_prompt, x_sample, cache_k, cache_v, state_ssm_re, state_ssm_im, c, c_ctx, norm_mix_g, norm_ffn_g,
           w_mod, b_mod, w_in, ssm_lambda_re, ssm_lambda_im, ssm_log_dt, ssm_b_re, ssm_b_im, ssm_c_re, ssm_c_im,
           ssm_d, w_glu, attn_sink, w_ssm_o, w_attn_o, w_out, w_up, conv_w, conv_b, w_down, final_norm_g):
    batch, seq, d = x_prompt.shape
    dec_batch, dec_seq, _ = x_sample.shape
    depth = w_in.shape[0]
    assert depth == 1, "final norm is fused into the (single) layer's ffn kernel"
    d_ssm = w_glu.shape[1]
    d_attn = N_HEADS * HEAD_DIM
    d_kv = N_KV_HEADS * HEAD_DIM
    groups = d_ssm // SSM_CH
    assert w_in.shape[2] == d_ssm + d_attn + 2 * d_kv + 2 * d
    assert ssm_lambda_re.shape[2:] == (groups, SSM_STATE) and dec_batch <= 8 - 1
    l = 0

    col = {"gs": 0, "ga": d, "q": 2 * d, "u": 2 * d + d_attn, "k": 2 * d + d_attn + d_ssm,
           "d_ssm": d_ssm, "d_attn": d_attn, "d_kv": d_kv}
    xp = x_prompt.reshape(batch * seq, d)
    xs = x_sample.reshape(dec_batch * dec_seq, d)

    cond = jnp.concatenate([c_ctx[None], c, jnp.zeros((8 - 1 - dec_batch, d), F32)], axis=0)

    ssm_ops, mod, (w_in_b,) = _ssm_operators_and_mod(
        ssm_lambda_re[l], ssm_lambda_im[l], ssm_log_dt[l], ssm_b_re[l], ssm_b_im[l], ssm_c_re[l], ssm_c_im[l],
        ssm_d[l], cond, w_mod[l], b_mod[l], riders=(w_in[l],))
    mod = mod.reshape(8, 6, d)
    mod_p, mod_s = mod[0:1], mod[1:1 + dec_batch]
    proj_p, tail_p, k_raw, v_raw = _proj(xp, mod_p, norm_mix_g[l], w_in_b, tm=PROJ_TM,
                                         tiles_per_seq=max(seq // PROJ_TM, 1), col=col)
    proj_s, tail_s, _, _ = _proj(xs, mod_s, norm_mix_g[l], w_in_b, tm=PROJ_TM,
                                 tiles_per_seq=max(dec_seq // PROJ_TM, 1), col=col)

    def lanes(s):
        return s.transpose(0, 2, 1, 3).reshape(dec_batch, groups, 2 * SSM_STATE)
    h0_p = jnp.zeros((batch, groups * 4 * SSM_STATE), F32)
    h0_s = _slab_lanes(jnp.concatenate([lanes(state_ssm_re[:, l]), lanes(state_ssm_im[:, l])], axis=-1))
    y_ssm_p, h_fin, (w_up_b,) = _ssm(tail_p, 0, ssm_ops, h0_p, n_seq=batch, seq_len=seq,
                                     seq_block=min(batch, SSM_SEQ_BLOCK), riders=(w_up[l],))
    y_ssm_s, _, (w_down_b,) = _ssm(tail_s, 0, ssm_ops, h0_s, n_seq=dec_batch, seq_len=dec_seq,
                                   seq_block=min(dec_batch, SSM_SEQ_BLOCK), riders=(w_down[l],))

    sink = attn_sink[l]
    o_p, mixer_w = _attn_ctx(proj_p, sink, n_seq=batch, seq_len=seq, col=col,
                             riders=(w_glu[l], w_ssm_o[l], w_attn_o[l], w_out[l]))
    o_s = _attn_lat(proj_s, cache_k[:, l].reshape(dec_batch, -1, d_kv), cache_v[:, l].reshape(dec_batch, -1, d_kv),
                    sink, n_seq=dec_batch, seq_len=dec_seq, col=col)

    x1_p, xm2_p = _mixer_out(xp, y_ssm_p, o_p, proj_p, mod_p, norm_ffn_g[l], *mixer_w, tm=MIXER_TM,
                             tiles_per_seq=max(seq // MIXER_TM, 1), col=col)
    x1_s, xm2_s = _mixer_out(xs, y_ssm_s, o_s, proj_s, mod_s, norm_ffn_g[l], *mixer_w, tm=MIXER_TM,
                             tiles_per_seq=max(dec_seq // MIXER_TM, 1), col=col)
    y_p = _ffn(x1_p, xm2_p, mod_p, w_up_b, conv_w[l], conv_b[l], w_down_b, final_norm_g, tm=FFN_TM, seq_len=seq)
    y_s = _ffn(x1_s, xm2_s, mod_s, w_up_b, conv_w[l], conv_b[l], w_down_b, final_norm_g, tm=FFN_TM, seq_len=dec_seq)

    new_k = k_raw.reshape(batch, 1, seq, N_KV_HEADS, HEAD_DIM)
    new_v = v_raw.reshape(batch, 1, seq, N_KV_HEADS, HEAD_DIM)

    def unlanes(hl):
        return hl.reshape(batch, groups, 2, SSM_STATE).transpose(0, 2, 1, 3)[:, None]
    h_fin = _unslab_lanes(h_fin, groups)
    new_re = unlanes(h_fin[:, :, :2 * SSM_STATE])
    new_im = unlanes(h_fin[:, :, 2 * SSM_STATE:])
    return (y_p.reshape(batch, seq, d), y_s.reshape(dec_batch, dec_seq, d), new_k, new_v, new_re, new_im)
```

```python
import functools
import math

import jax
import jax.numpy as jnp
from jax import lax
from jax.experimental import pallas as pl
from jax.experimental.pallas import tpu as pltpu

F32 = jnp.float32
BF16 = jnp.bfloat16

GRID_W = 64
SSM_CH = 16
SSM_STATE = 64
N_HEADS = 8
N_KV_HEADS = 2
HEAD_DIM = 128
Q_PER_KV = N_HEADS // N_KV_HEADS
WINDOW = 128
BLOCK = 128
ROPE_THETA = 10000.0
EPS = 1e-6
NEG_INF = -1e30

SSM_CHUNK = 16
SSM_LANES = SSM_CHUNK * SSM_CH
SSM_SEQ_BLOCK = 16

V7X_VMEM_LIMIT_BYTES = 56 * 1024 * 1024
V7X_MXU_WIDTH = 256


def _cparams(semantics):
    return pltpu.CompilerParams(dimension_semantics=semantics, vmem_limit_bytes=V7X_VMEM_LIMIT_BYTES)


def _sigmoid(x):
    return 1.0 / (1.0 + jnp.exp(-x))


def _ride_casts(body, n_in, n_out, n_riders):
    def kernel(*refs):
        ins = refs[:n_in]
        rider_ins = refs[n_in:n_in + n_riders]
        outs = refs[n_in + n_riders:n_in + n_riders + n_out]
        rider_outs = refs[n_in + n_riders + n_out:n_in + 2 * n_riders + n_out]
        scratch = refs[n_in + 2 * n_riders + n_out:]
        for src, dst in zip(rider_ins, rider_outs):
            dst[...] = src[...].astype(BF16)
        body(*ins, *outs, *scratch)
    return kernel


def _rider_specs(weights, n_steps, step_of):
    specs, shapes = [], []
    for w in weights:
        r, c = w.shape
        assert r % (16 * n_steps) == 0
        specs.append(pl.BlockSpec((r // n_steps, c), lambda *idx: (step_of(*idx), 0)))
        shapes.append(jax.ShapeDtypeStruct((r, c), BF16))
    return specs, shapes


def _rms_modulate(x, g, scale, shift):
    ms = jnp.mean(x * x, axis=-1, keepdims=True)
    return (x * lax.rsqrt(ms + EPS) * g) * (1.0 + scale) + shift


MOD_K_CHUNK = 256


def _mod_kernel(c_ref, w_ref, b_ref, o_ref):
    c = c_ref[...]
    a = (c * _sigmoid(c)).astype(BF16)
    acc = b_ref[...]
    for k in range(0, w_ref.shape[0], MOD_K_CHUNK):
        acc = acc + jnp.dot(a[:, k:k + MOD_K_CHUNK], w_ref[k:k + MOD_K_CHUNK, :].astype(BF16),
                            preferred_element_type=F32)
    o_ref[...] = acc


PROJ_TN = 512
NORM_ROWS = 16
NORM_UNROLL = 16


def _swap32(x):
    n = x.shape[-1]
    lane = lax.broadcasted_iota(jnp.int32, x.shape, x.ndim - 1)
    return jnp.where((lane & 63) < 32, pltpu.roll(x, n - 32, x.ndim - 1), pltpu.roll(x, 32, x.ndim - 1))


def _proj_kernel(x_ref, mod_ref, g_ref, w_ref, o_ref, tail_ref, k_ref, v_ref, xm_scr, *, q_tiles):
    j = pl.program_id(1)

    @pl.when(j == 0)
    def _():
        gain = g_ref[...] * (1.0 + mod_ref[1:2, :])
        shift = mod_ref[0:1, :]

        def block(c, carry):
            rows = pl.ds(pl.multiple_of(c * NORM_ROWS, NORM_ROWS), NORM_ROWS)
            x = x_ref[rows, :]
            ms = jnp.mean(x * x, axis=-1, keepdims=True)
            xm_scr[rows, :] = (x * lax.rsqrt(ms + EPS) * gain + shift).astype(BF16)
            return carry

        lax.fori_loop(0, x_ref.shape[0] // NORM_ROWS, block, 0, unroll=NORM_UNROLL)

    acc = jnp.dot(xm_scr[...], w_ref[...], preferred_element_type=F32)
    scale = jnp.where((j >= q_tiles[0]) & (j < q_tiles[1]), HEAD_DIM ** -0.5, 1.0)
    o_ref[...] = (acc * scale).astype(BF16)
    tail_ref[...] = acc
    d_kv = k_ref.shape[1]
    k_ref[...] = acc[:, :d_kv]
    v_ref[...] = acc[:, d_kv:]


def _proj(x, mod, norm_g, w_in, *, tm, tiles_per_seq, col):
    n, d = x.shape
    cols = w_in.shape[1]
    tn = PROJ_TN
    n_tiles = cols // tn
    d_kv = col["d_kv"]
    n_gate, n_q, n_u = col["q"] // tn, col["d_attn"] // tn, col["d_ssm"] // tn
    q_tiles = (n_gate, n_gate + n_q)
    tail0 = col["u"] // tn
    n_tail = n_tiles - tail0
    assert 2 * d_kv == tn and col["u"] == col["q"] + col["d_attn"] and col["k"] == cols - tn

    def src_tile(j):
        return jnp.where(j < n_gate, j + (n_u + n_q + 1),
                         jnp.where(j < n_gate + n_q, j - n_gate + n_u,
                                   jnp.where(j < n_gate + n_q + n_u, j - (n_gate + n_q), n_u + n_q)))
    seq_of = (lambda i: i // tiles_per_seq) if mod.shape[0] > 1 else (lambda i: 0)
    return pl.pallas_call(
        functools.partial(_proj_kernel, q_tiles=q_tiles),
        grid=(n // tm, n_tiles),
        in_specs=[pl.BlockSpec((tm, d), lambda i, j: (i, 0)),
                  pl.BlockSpec((None, 6, d), lambda i, j: (seq_of(i), 0, 0)),
                  pl.BlockSpec((1, d), lambda i, j: (0, 0)),
                  pl.BlockSpec((d, tn), lambda i, j: (0, src_tile(j)))],
        out_specs=[pl.BlockSpec((tm, tn), lambda i, j: (i, j)),
                   pl.BlockSpec((tm, tn), lambda i, j: (i, jnp.clip(j - tail0, 0, n_tail - 1))),
                   pl.BlockSpec((tm, d_kv), lambda i, j: (i, 0)),
                   pl.BlockSpec((tm, d_kv), lambda i, j: (i, 0))],
        out_shape=[jax.ShapeDtypeStruct((n, cols), BF16), jax.ShapeDtypeStruct((n, n_tail * tn), F32),
                   jax.ShapeDtypeStruct((n, d_kv), F32), jax.ShapeDtypeStruct((n, d_kv), F32)],
        scratch_shapes=[pltpu.VMEM((tm, d), BF16)],
        compiler_params=_cparams(("arbitrary", "arbitrary")),
        name="proj",
    )(x, mod, norm_g.reshape(1, d), w_in)


def _cmul(ar, ai, br, bi):
    return ar * br - ai * bi, ar * bi + ai * br


def _cpow(base_r, base_i, n, bits):
    res_r = jnp.ones_like(base_r)
    res_i = jnp.zeros_like(base_r)
    for bit in range(bits):
        nr, ni = _cmul(res_r, res_i, base_r, base_i)
        take = ((n >> bit) & 1) == 1
        res_r = jnp.where(take, nr, res_r)
        res_i = jnp.where(take, ni, res_i)
        if bit + 1 < bits:
            base_r, base_i = _cmul(base_r, base_i, base_r, base_i)
    return res_r, res_i


def _ssm_disc_kernel(lre_ref, lim_ref, ldt_ref, ere_ref, eim_ref, qre_ref, qim_ref):
    lre = lre_ref[...]
    lim = lim_ref[...]
    dt = jnp.exp(ldt_ref[...])
    mag = jnp.exp(lre * dt)
    e_re = mag * jnp.cos(lim * dt)
    e_im = mag * jnp.sin(lim * dt)
    n_re = e_re - 1.0
    den = lre * lre + lim * lim
    ere_ref[...] = e_re
    eim_ref[...] = e_im
    qre_ref[...] = (n_re * lre + e_im * lim) / den
    qim_ref[...] = (e_im * lre - n_re * lim) / den


def _ssm_disc(lam_re, lam_im, log_dt):
    two, g, p = lam_re.shape
    shp = jax.ShapeDtypeStruct((two * g, p), F32)
    outs = pl.pallas_call(_ssm_disc_kernel, out_shape=[shp] * 4, name="ssm_disc")(
        lam_re.reshape(two * g, p), lam_im.reshape(two * g, p), log_dt.reshape(two * g, 1))
    return [o.reshape(two, g, p) for o in outs]


SSM_OPS_GROUPS = 4


def _ssm_ops_mod_kernel(ecol_ref, rows_ref, b_ref, ct_ref, d_ref, c_ref, wm_ref, bm_ref,
                        m_ref, w_ref, v_ref, a_ref, mod_ref):
    _mod_kernel(c_ref, wm_ref, bm_ref, mod_ref)
    _ssm_ops_kernel(ecol_ref, rows_ref, b_ref, ct_ref, d_ref, m_ref, w_ref, v_ref, a_ref)


def _ssm_ops_kernel(ecol_ref, rows_ref, b_ref, ct_ref, d_ref, m_ref, w_ref, v_ref, a_ref):
    def one(gi, carry):
        _ssm_ops_group(ecol_ref.at[gi], rows_ref.at[gi], b_ref.at[gi], ct_ref.at[gi], d_ref.at[gi],
                       m_ref.at[gi], w_ref.at[gi], v_ref.at[gi], a_ref.at[gi])
        return carry

    lax.fori_loop(0, ecol_ref.shape[0], one, 0, unroll=True)


def _ssm_ops_group(ecol_ref, rows_ref, b_ref, ct_ref, d_ref, m_ref, w_ref, v_ref, a_ref):
    L, ch, p = SSM_CHUNK, SSM_CH, SSM_STATE
    lanes = SSM_LANES
    half = 2 * p
    lane_c = lax.broadcasted_iota(jnp.int32, (p, lanes), 1)
    lag = lane_c // ch

    def col(k):
        return jnp.broadcast_to(ecol_ref[k], (p, lanes))

    f0 = _cpow(col(0), col(1), lag, 4)
    b0 = _cpow(col(2), col(3), (L - 1) - lag, 4)
    f1 = _cmul(f0[0], f0[1], col(0), col(1))
    b1 = _cmul(b0[0], b0[1], col(2), col(3))
    def ct(k):
        return jnp.tile(ct_ref[k], (1, L))

    cf = (ct(0), ct(1))
    cb = (ct(2), ct(3))
    yf0 = _cmul(*f0, *cf)
    yb0 = _cmul(*b0, *cb)
    yf1 = _cmul(*f1, *cf)
    yb1 = _cmul(*b1, *cb)
    v_ref[...] = jnp.concatenate([yf1[0], yb1[0], -yf1[1], -yb1[1]], axis=0).astype(BF16)

    lane_r = lax.broadcasted_iota(jnp.int32, (ch, lanes), 1)
    sgn = jnp.where(lane_r < half, -1.0, 1.0)
    b1raw = b_ref[...]
    b2raw = pltpu.roll(b1raw, half, 1) * sgn
    bb1 = rows_ref[2:3, :] * b1raw + rows_ref[3:4, :] * b2raw
    bb2 = pltpu.roll(bb1, half, 1) * sgn

    zero = jnp.zeros((p, lanes), F32)
    hi = lax.Precision.HIGHEST
    r0f = jnp.dot(bb1, jnp.concatenate([yf0[0], zero, -yf0[1], zero], axis=0), precision=hi,
                  preferred_element_type=F32)
    r0b = jnp.dot(bb1, jnp.concatenate([zero, yb0[0], zero, -yb0[1]], axis=0), precision=hi,
                  preferred_element_type=F32)
    row_r = lax.broadcasted_iota(jnp.int32, (ch, lanes), 0)
    d_col = d_ref[...]
    for s in range(L):
        blk = jnp.where(lane_r == ch * s + row_r, d_col, 0.0)
        fwd = r0f if s == 0 else jnp.where(lane_r >= ch * s, pltpu.roll(r0f, ch * s, 1), 0.0)
        k = ch * (L - 1 - s)
        bwd = r0b if k == 0 else jnp.where(lane_r < lanes - k, pltpu.roll(r0b, lanes - k, 1), 0.0)
        m_ref[s * ch:(s + 1) * ch, :] = (blk + fwd + bwd).astype(BF16)

    is_f = (lane_r & (half - 1)) < p
    n_row = jnp.where(is_f, (L - 1) - row_r, row_r)
    er, ei = _cpow(jnp.broadcast_to(rows_ref[0:1, :], (L, lanes)), jnp.broadcast_to(rows_ref[1:2, :], (L, lanes)),
                   n_row, 4)
    for s in range(L):
        w_ref[s * ch:(s + 1) * ch, :] = (er[s:s + 1, :] * bb1 + ei[s:s + 1, :] * bb2).astype(BF16)

    ar, ai = rows_ref[0:1, :], rows_ref[1:2, :]
    for _ in range(4):
        ar, ai = _cmul(ar, ai, ar, ai)
    lane_1 = lax.broadcasted_iota(jnp.int32, (1, lanes), 1)
    a_ref[...] = jnp.where(lane_1 < half, ar, ai)


def _ssm_operators_and_mod(lam_re, lam_im, log_dt, b_re, b_im, c_re, c_im, ssm_d, cond, w_mod, b_mod, riders=()):
    two, g, p = lam_re.shape
    ch = b_re.shape[-1]
    lanes = SSM_LANES
    assert (two, p, ch, SSM_CHUNK * ch) == (2, SSM_STATE, SSM_CH, lanes)
    e_re, e_im, q_re, q_im = _ssm_disc(lam_re, lam_im, log_dt)
    ecol = jnp.stack([e_re[0], e_im[0], e_re[1], e_im[1]], axis=1)[..., None]

    def fbfb(x):
        return jnp.concatenate([x[0], x[1], x[0], x[1]], axis=-1)
    rows = jnp.stack([fbfb(e_re), fbfb(e_im), fbfb(q_re), fbfb(q_im)], axis=1)

    def bt(x):
        return jnp.swapaxes(x, -1, -2)
    b_rows = jnp.concatenate([bt(b_re[0]), bt(b_re[1]), bt(b_im[0]), bt(b_im[1])], axis=-1)

    ct = jnp.stack([bt(c_re[0]), bt(c_im[0]), bt(c_re[1]), bt(c_im[1])], axis=1)
    d_col = ssm_d.reshape(g, ch, 1)
    gb = SSM_OPS_GROUPS
    mat = pl.BlockSpec((gb, lanes, lanes), lambda i: (i, 0, 0))
    steps = g // gb
    rider_specs, rider_shapes = _rider_specs(riders, steps, lambda i: i)
    c_rows, d_model = cond.shape
    n_mod = w_mod.shape[1]
    tn = n_mod // steps
    assert n_mod % steps == 0 and tn % 128 == 0
    outs = pl.pallas_call(
        _ride_casts(_ssm_ops_mod_kernel, 8, 5, len(riders)),
        grid=(steps,),
        in_specs=[pl.BlockSpec((gb, 4, p, 1), lambda i: (i, 0, 0, 0)),
                  pl.BlockSpec((gb, 4, lanes), lambda i: (i, 0, 0)),
                  pl.BlockSpec((gb, ch, lanes), lambda i: (i, 0, 0)),
                  pl.BlockSpec((gb, 4, p, ch), lambda i: (i, 0, 0, 0)),
                  pl.BlockSpec((gb, ch, 1), lambda i: (i, 0, 0)),
                  pl.BlockSpec((c_rows, d_model), lambda i: (0, 0)),
                  pl.BlockSpec((d_model, tn), lambda i: (0, i)),
                  pl.BlockSpec((1, tn), lambda i: (0, i))] + rider_specs,
        out_specs=[mat, mat, mat, pl.BlockSpec((gb, 1, lanes), lambda i: (i, 0, 0)),
                   pl.BlockSpec((c_rows, tn), lambda i: (0, i))] + rider_specs,
        out_shape=([jax.ShapeDtypeStruct((g, lanes, lanes), BF16)] * 3 + [jax.ShapeDtypeStruct((g, 1, lanes), F32),
                                                                          jax.ShapeDtypeStruct((c_rows, n_mod), F32)]
                   + rider_shapes),
        compiler_params=_cparams(("arbitrary",)),
        name="ssm_ops_mod",
    )(ecol, rows, b_rows, ct, d_col, cond, w_mod, b_mod.reshape(1, n_mod), *riders)
    return outs[:4], outs[4], outs[5:]


SSM_SLAB_GROUPS = 128 // SSM_CH


def _chunk_transpose(xs, chunk_id):
    for k in (4, 2, 1):
        keep = (chunk_id & k) == 0
        new = list(xs)
        for i in range(len(xs)):
            if i & k == 0:
                a, b = xs[i], xs[i + k]
                if 2 * SSM_CH * k == 128:
                    moved = pltpu.roll(jnp.where(keep, b, a), SSM_CH * k, 1)
                    new[i] = jnp.where(keep, a, moved)
                    new[i + k] = jnp.where(keep, moved, b)
                else:
                    new[i] = jnp.where(keep, a, pltpu.roll(b, SSM_CH * k, 1))
                    new[i + k] = jnp.where(keep, pltpu.roll(a, 128 - SSM_CH * k, 1), b)
        xs = new
    return xs


def _ssm_kernel(u_ref, m_ref, w_ref, v_ref, a_ref, h0_ref, y_ref, hfin_ref, ug_scr, s_scr, hf_scr, hb_scr, yg_scr,
                *, n_chunks, ns):
    gps = SSM_SLAB_GROUPS
    nc = ns * n_chunks
    half = 2 * SSM_STATE
    chunk_id = lax.broadcasted_iota(jnp.int32, (nc, 128), 1) // SSM_CH
    for hf in range(2):
        xs = [u_ref[pl.ds(hf * gps + i, nc, stride=SSM_CHUNK), :] for i in range(gps)]
        xs = _chunk_transpose(xs, chunk_id)
        for g in range(gps):
            ug_scr[g, :, hf * 128:(hf + 1) * 128] = xs[g].astype(BF16)
    pitch = n_chunks + 8
    for g in range(gps):
        s = jnp.dot(ug_scr[g], w_ref[g], preferred_element_type=F32)
        for q in range(ns):
            s_scr[g, q * pitch:q * pitch + n_chunks, :] = s[q * n_chunks:(q + 1) * n_chunks, :half]
            s_scr[gps + g, q * pitch:q * pitch + n_chunks, :] = s[q * n_chunks:(q + 1) * n_chunks, half:]
    is_fwd = lax.broadcasted_iota(jnp.int32, (1, half), 1) < SSM_STATE
    blocks = range(2 * gps)
    a = [a_ref[:, b * half:(b + 1) * half] for b in blocks]
    h = [h0_ref[:, b * half:(b + 1) * half] for b in blocks]

    def rows(c):
        return pl.ds(c, ns, stride=pitch)

    for k in range(n_chunks):
        kb = n_chunks - 1 - k
        new_h = list(h)
        for b in blocks:
            hf_scr[b, rows(k), :] = h[b]
            hb_scr[b, rows(kb), :] = h[b]
        for g in range(gps):
            s_re = jnp.where(is_fwd, s_scr[g, rows(k), :], s_scr[g, rows(kb), :])
            s_im = jnp.where(is_fwd, s_scr[gps + g, rows(k), :], s_scr[gps + g, rows(kb), :])
            new_h[g] = a[g] * h[g] - a[gps + g] * h[gps + g] + s_re
            new_h[gps + g] = a[g] * h[gps + g] + a[gps + g] * h[g] + s_im
        h = new_h
    for b in blocks:
        hfin_ref[:, b * half:(b + 1) * half] = h[b]

    def entering(b):
        return jnp.concatenate(
            [jnp.where(is_fwd, hf_scr[b, q * pitch:q * pitch + n_chunks, :], hb_scr[b, q * pitch:q * pitch + n_chunks, :])
             for q in range(ns)], axis=0)

    for g in range(gps):
        hin = jnp.concatenate([entering(g), entering(gps + g)], axis=1).astype(BF16)
        yg_scr[g] = (jnp.dot(ug_scr[g], m_ref[g], preferred_element_type=F32)
                     + jnp.dot(hin, v_ref[g], preferred_element_type=F32))
    for hf in range(2):
        xs = [yg_scr[g, :, hf * 128:(hf + 1) * 128] for g in range(gps)]
        xs = _chunk_transpose(xs, chunk_id)
        for i in range(gps):
            y_ref[pl.ds(hf * gps + i, nc, stride=SSM_CHUNK), :] = xs[i]


def _slab_lanes(x):
    lead = x.shape[:-2]
    g = x.shape[-2]
    gps = SSM_SLAB_GROUPS
    x = x.reshape(lead + (g // gps, gps, 2, 2 * SSM_STATE))
    x = jnp.swapaxes(x, -3, -2)
    return x.reshape(lead + (g * 4 * SSM_STATE,))


def _unslab_lanes(x, g):
    lead = x.shape[:-1]
    gps = SSM_SLAB_GROUPS
    x = x.reshape(lead + (g // gps, 2, gps, 2 * SSM_STATE))
    x = jnp.swapaxes(x, -3, -2)
    return x.reshape(lead + (g, 4 * SSM_STATE))


def _ssm(u, u_col, ops, h0, *, n_seq, seq_len, seq_block, riders=()):
    m_op, w_op, v_op, a_op = ops
    n = u.shape[0]
    g = m_op.shape[0]
    d_ssm = g * SSM_CH
    slab0 = u_col // 128
    gps = SSM_SLAB_GROUPS
    lanes = SSM_LANES
    n_chunks = seq_len // SSM_CHUNK
    nc = seq_block * n_chunks
    padded = seq_block * (n_chunks + 8)
    rows = seq_block * seq_len
    wide2 = gps * 4 * SSM_STATE
    mat = pl.BlockSpec((gps, lanes, lanes), lambda j, b: (j, 0, 0))
    n_blocks = n_seq // seq_block
    rider_specs, rider_shapes = _rider_specs(riders, (g // gps) * n_blocks, lambda j, b: j * n_blocks + b)
    outs = pl.pallas_call(
        _ride_casts(functools.partial(_ssm_kernel, n_chunks=n_chunks, ns=seq_block), 6, 2, len(riders)),
        grid=(g // gps, n_blocks),
        in_specs=[pl.BlockSpec((rows, 128), lambda j, b: (b, slab0 + j)), mat, mat, mat,
                  pl.BlockSpec((1, wide2), lambda j, b: (0, j)),
                  pl.BlockSpec((seq_block, wide2), lambda j, b: (b, j))] + rider_specs,
        out_specs=[pl.BlockSpec((rows, 128), lambda j, b: (b, j)),
                   pl.BlockSpec((seq_block, wide2), lambda j, b: (b, j))] + rider_specs,
        out_shape=[jax.ShapeDtypeStruct((n, d_ssm), F32),
                   jax.ShapeDtypeStruct((n_seq, g * 4 * SSM_STATE), F32)] + rider_shapes,
        scratch_shapes=[pltpu.VMEM((gps, nc, lanes), BF16), pltpu.VMEM((2 * gps, padded, 128), F32),
                        pltpu.VMEM((2 * gps, padded, 128), F32), pltpu.VMEM((2 * gps, padded, 128), F32),
                        pltpu.VMEM((gps, nc, lanes), F32)],
        compiler_params=_cparams(("arbitrary", "arbitrary")),
        name="ssm",
    )(u, m_op, w_op, v_op, _slab_lanes(a_op[:, 0])[None], h0, *riders)
    return outs[0], outs[1], outs[2:]


def _softmax_pv(pieces, sink):
    m = sink
    for s, _ in pieces:
        m = jnp.maximum(m, jnp.max(s, axis=-1, keepdims=True))
    out = None
    for s, v in pieces:
        pv = jnp.dot(jnp.exp(s - m).astype(BF16), v, preferred_element_type=F32)
        out = pv if out is None else out + pv
    return out[:, :HEAD_DIM] / (out[:, HEAD_DIM:] + jnp.exp(sink - m))


def _qk(q, k):
    return lax.dot_general(q, k, (((1,), (1,)), ((), ())), preferred_element_type=F32)


def _with_ones(v):
    return jnp.concatenate([v, jnp.ones_like(v)], axis=1)


ATTN_CTX_SEQS = 8


def _attn_ctx_kernel(sink_ref, q_ref, kv_ref, o_ref, *, seq_len):
    d_kv = N_KV_HEADS * HEAD_DIM
    for r0 in range(0, q_ref.shape[0], seq_len):
        rows = slice(r0, r0 + seq_len)
        for kh in range(N_KV_HEADS):
            k = kv_ref[rows, kh * HEAD_DIM:(kh + 1) * HEAD_DIM]
            v = _with_ones(kv_ref[rows, d_kv + kh * HEAD_DIM:d_kv + (kh + 1) * HEAD_DIM])
            for h in range(kh * Q_PER_KV, (kh + 1) * Q_PER_KV):
                q = q_ref[rows, h * HEAD_DIM:(h + 1) * HEAD_DIM]
                o = _softmax_pv([(_qk(q, k), v)], sink_ref[h])
                o_ref[rows, h * HEAD_DIM:(h + 1) * HEAD_DIM] = o.astype(BF16)


def _attn_ctx(proj, sink, *, n_seq, seq_len, col, riders=()):
    d_attn = col["d_attn"]
    kv_w = 2 * col["d_kv"]
    spb = ATTN_CTX_SEQS if n_seq % ATTN_CTX_SEQS == 0 else 1
    rows = spb * seq_len
    rider_specs, rider_shapes = _rider_specs(riders, n_seq // spb, lambda b: b)
    outs = pl.pallas_call(
        _ride_casts(functools.partial(_attn_ctx_kernel, seq_len=seq_len), 3, 1, len(riders)),
        grid=(n_seq // spb,),
        in_specs=[pl.BlockSpec(memory_space=pltpu.SMEM),
                  pl.BlockSpec((rows, d_attn), lambda b: (b, col["q"] // d_attn)),
                  pl.BlockSpec((rows, kv_w), lambda b: (b, col["k"] // kv_w))] + rider_specs,
        out_specs=[pl.BlockSpec((rows, d_attn), lambda b: (b, 0))] + rider_specs,
        out_shape=[jax.ShapeDtypeStruct((n_seq * seq_len, d_attn), BF16)] + rider_shapes,
        compiler_params=_cparams(("arbitrary",)),
        name="attn_ctx",
    )(sink, proj, proj, *riders)
    return outs[0], outs[1:]


def _rope(x, cos, sin):
    reps = x.shape[1] // HEAD_DIM
    xf = x.astype(F32)
    return (xf * jnp.concatenate([cos] * reps, axis=1) + _swap32(xf) * jnp.concatenate([sin] * reps, axis=1)).astype(BF16)


ATTN_LAT_QBLOCKS = 4


def _attn_lat_kernel(*refs, n_blocks, qb):
    sink_ref, q_ref = refs[:2]
    kv_refs = refs[2:4 + qb]
    ck_ref, cv_ref, cos_ref, sin_ref, o_ref = refs[4 + qb:]
    first = pl.program_id(1) * qb
    d_kv = N_KV_HEADS * HEAD_DIM

    def tables(blk):
        rows = pl.ds(pl.multiple_of(blk * BLOCK, BLOCK), BLOCK)
        return cos_ref[rows, :], sin_ref[rows, :]

    k_rot = [_rope(kv_refs[n][:, :d_kv], *tables(jnp.clip(first + n - 1, 0, n_blocks - 1))) for n in range(qb + 2)]
    r = lax.broadcasted_iota(jnp.int32, (BLOCK, 3 * BLOCK), 0)
    c = lax.broadcasted_iota(jnp.int32, (BLOCK, 3 * BLOCK), 1)
    cc = c & (BLOCK - 1)
    for sub in range(qb):
        blk = first + sub
        q_rows = slice(sub * BLOCK, (sub + 1) * BLOCK)
        q_all = _rope(q_ref[q_rows, :], *tables(blk))
        k_all = jnp.concatenate(k_rot[sub:sub + 3], axis=0)
        valid = (((c < BLOCK) & (cc >= r) & (blk > 0)) | ((c >= BLOCK) & (c < 2 * BLOCK))
                 | ((c >= 2 * BLOCK) & (cc <= r) & (blk < n_blocks - 1)))
        for kh in range(N_KV_HEADS):
            ks = slice(kh * HEAD_DIM, (kh + 1) * HEAD_DIM)
            vs = slice(d_kv + kh * HEAD_DIM, d_kv + (kh + 1) * HEAD_DIM)
            k_loc = k_all[:, ks]
            k_ctx = ck_ref[:, ks].astype(BF16)
            v_loc = _with_ones(jnp.concatenate([kv_refs[sub + n][:, vs] for n in range(3)], axis=0))
            v_ctx = _with_ones(cv_ref[:, ks].astype(BF16))
            for h in range(kh * Q_PER_KV, (kh + 1) * Q_PER_KV):
                q = q_all[:, h * HEAD_DIM:(h + 1) * HEAD_DIM]
                s_loc = jnp.where(valid, _qk(q, k_loc), NEG_INF)
                o = _softmax_pv([(s_loc, v_loc), (_qk(q, k_ctx), v_ctx)], sink_ref[h])
                o_ref[q_rows, h * HEAD_DIM:(h + 1) * HEAD_DIM] = o.astype(BF16)


def _attn_lat(proj, cache_k, cache_v, sink, *, n_seq, seq_len, col):
    d_attn = col["d_attn"]
    kv_w = 2 * col["d_kv"]
    assert WINDOW == BLOCK and seq_len % BLOCK == 0
    nb = seq_len // BLOCK
    qb = ATTN_LAT_QBLOCKS if nb % ATTN_LAT_QBLOCKS == 0 else 1
    steps = nb // qb
    kv_col = col["k"] // kv_w
    past, d_kv = cache_k.shape[1], cache_k.shape[2]

    def kv_spec(off):
        return pl.BlockSpec((BLOCK, kv_w), lambda b, i: (b * nb + jnp.clip(i * qb + off, 0, nb - 1), kv_col))
    cache_spec = pl.BlockSpec((None, past, d_kv), lambda b, i: (b, 0, 0))
    table_spec = pl.BlockSpec((seq_len, HEAD_DIM), lambda b, i: (0, 0))
    cos, sin = _rope_tables(seq_len)
    return pl.pallas_call(
        functools.partial(_attn_lat_kernel, n_blocks=nb, qb=qb),
        grid=(n_seq, steps),
        in_specs=[pl.BlockSpec(memory_space=pltpu.SMEM),
                  pl.BlockSpec((qb * BLOCK, d_attn), lambda b, i: (b * steps + i, col["q"] // d_attn))]
                 + [kv_spec(off) for off in range(-1, qb + 1)]
                 + [cache_spec, cache_spec, table_spec, table_spec],
        out_specs=pl.BlockSpec((qb * BLOCK, d_attn), lambda b, i: (b * steps + i, 0)),
        out_shape=jax.ShapeDtypeStruct((n_seq * seq_len, d_attn), BF16),
        compiler_params=_cparams(("arbitrary", "arbitrary")),
        name="attn_lat",
    )(sink, proj, *([proj] * (qb + 2)), cache_k, cache_v, cos, sin)


def _gelu_tanh(x):
    return 0.5 * x * (1.0 + jnp.tanh(math.sqrt(2.0 / math.pi) * (x + 0.044715 * (x * x * x))))


def _mixer_out_kernel(x_ref, y_ref, o_ref, gs_ref, ga_ref, mod_ref, g_ref, wglu_ref, wso_ref, wao_ref, wout_ref,
                      x1_ref, xm2_ref):
    z = _gelu_tanh(y_ref[...].astype(F32))
    z = z * _sigmoid(jnp.dot(z.astype(BF16), wglu_ref[...], preferred_element_type=F32))
    s_br = jnp.dot(z.astype(BF16), wso_ref[...], preferred_element_type=F32)
    a_br = jnp.dot(o_ref[...], wao_ref[...], preferred_element_type=F32)
    merged = _sigmoid(gs_ref[...].astype(F32)) * s_br + _sigmoid(ga_ref[...].astype(F32)) * a_br
    out = jnp.dot(merged.astype(BF16), wout_ref[...], preferred_element_type=F32)
    x1 = x_ref[...] + mod_ref[2:3, :] * out
    x1_ref[...] = x1
    xm2_ref[...] = _rms_modulate(x1, g_ref[...], mod_ref[4:5, :], mod_ref[3:4, :]).astype(BF16)


def _mixer_out(x, y, o, proj, mod, norm_g, w_glu, w_ssm_o, w_attn_o, w_out, *, tm, tiles_per_seq, col):
    n, d = x.shape
    d_ssm = y.shape[1]
    d_attn = o.shape[1]
    seq_of = (lambda i: i // tiles_per_seq) if mod.shape[0] > 1 else (lambda i: 0)

    def resident(shape):
        return pl.BlockSpec(shape, lambda i: (0, 0), pipeline_mode=pl.Buffered(1))
    return pl.pallas_call(
        _mixer_out_kernel,
        grid=(n // tm,),
        in_specs=[pl.BlockSpec((tm, d), lambda i: (i, 0)),
                  pl.BlockSpec((tm, d_ssm), lambda i: (i, 0)),
                  pl.BlockSpec((tm, d_attn), lambda i: (i, 0)),
                  pl.BlockSpec((tm, d), lambda i: (i, col["gs"] // d)),
                  pl.BlockSpec((tm, d), lambda i: (i, col["ga"] // d)),
                  pl.BlockSpec((None, 6, d), lambda i: (seq_of(i), 0, 0)),
                  pl.BlockSpec((1, d), lambda i: (0, 0)),
                  resident(w_glu.shape), resident(w_ssm_o.shape), resident(w_attn_o.shape), resident(w_out.shape)],
        out_specs=[pl.BlockSpec((tm, d), lambda i: (i, 0)), pl.BlockSpec((tm, d), lambda i: (i, 0))],
        out_shape=[jax.ShapeDtypeStruct((n, d), F32), jax.ShapeDtypeStruct((n, d), BF16)],
        compiler_params=_cparams(("arbitrary",)),
        name="mixer_out",
    )(x, y, o, proj, proj, mod, norm_g.reshape(1, d), w_glu, w_ssm_o, w_attn_o, w_out)


FFN_TM = 1024
FFN_TF = 512
FFN_TN = 256


def _ffn_kernel(xm_ref, wa_ref, wb_ref, cw_ref, cb_ref, wd_ref, x1_ref, mod_ref, g_ref, o_ref, act_scr, raw_scr,
                *, seq_len, nf, nn, d_ff):
    j = pl.program_id(1)
    tm = xm_ref.shape[0]
    tf = wa_ref.shape[1]
    tn = wd_ref.shape[1]
    ch = tf // 2
    pos = lax.broadcasted_iota(jnp.int32, (tm, 1), 0) & (seq_len - 1)
    has_prev = pos != 0
    has_next = pos != seq_len - 1

    def lanes(off):
        return slice(off, off + ch) if isinstance(off, int) else pl.ds(pl.multiple_of(off, ch), ch)

    def gate(ha, hb, col0):
        def conv(h, off):
            cols = lanes(off)
            cw = cw_ref[:, cols]
            h_prev = jnp.where(has_prev, pltpu.roll(h, 1, 0), 0.0)
            h_next = jnp.where(has_next, pltpu.roll(h, tm - 1, 0), 0.0)
            return cw[0:1] * h_prev + cw[1:2] * h + cw[2:3] * h_next + cb_ref[:, cols]

        a = conv(ha, col0)
        b = conv(hb, d_ff + col0)
        return ((a * _sigmoid(a)) * b).astype(BF16)

    def act_cols(col0):
        return lanes(tf + col0)

    @pl.when(j == 0)
    def _():
        raw_scr[...] = jnp.zeros_like(raw_scr)

    @pl.when(j < nf)
    def _():
        xm = xm_ref[...]
        ha0 = jnp.dot(xm, wa_ref[:, :ch], preferred_element_type=F32)
        hb0 = jnp.dot(xm, wb_ref[:, :ch], preferred_element_type=F32)
        carried = (j - 1) * tf + ch
        act_scr[:, act_cols(carried)] = gate(raw_scr[0], raw_scr[1], jnp.maximum(carried, 0))
        ha1 = jnp.dot(xm, wa_ref[:, ch:], preferred_element_type=F32)
        hb1 = jnp.dot(xm, wb_ref[:, ch:], preferred_element_type=F32)
        act_scr[:, act_cols(j * tf)] = gate(ha0, hb0, j * tf)
        raw_scr[0] = ha1
        raw_scr[1] = hb1

    @pl.when(j == nf)
    def _():
        last = (nf - 1) * tf + ch
        act_scr[:, act_cols(last)] = gate(raw_scr[0], raw_scr[1], last)

    @pl.when(j >= nf)
    def _():
        cols = pl.ds(pl.multiple_of((j - nf) * tn, tn), tn)
        ffn = jnp.dot(act_scr[:, tf:], wd_ref[...], preferred_element_type=F32)
        o_ref[:, cols] = x1_ref[...] + mod_ref[5:6, cols] * ffn

    @pl.when(j == nf + nn - 1)
    def _():
        x2 = o_ref[...]
        ms = jnp.mean(x2 * x2, axis=-1, keepdims=True)
        o_ref[...] = x2 * lax.rsqrt(ms + EPS) * g_ref[...]


def _ffn(x1, xm2, mod, w_up, conv_w, conv_b, w_down, final_g, *, tm, seq_len):
    n, d = x1.shape
    d_ff = w_down.shape[0]
    tf, tn = FFN_TF, FFN_TN
    nf, nn = d_ff // tf, d // tn
    assert tm % seq_len == 0 and seq_len & (seq_len - 1) == 0 and d_ff % tf == 0 and (tf // 2) % V7X_MXU_WIDTH == 0
    seqs_per_tile = tm // seq_len
    seq_of = (lambda i: i * seqs_per_tile) if mod.shape[0] > 1 else (lambda i: 0)
    assert mod.shape[0] == 1 or seqs_per_tile == 1
    cb = conv_b.reshape(1, 2 * d_ff)

    def up(j):
        return jnp.minimum(j, nf - 1)

    def down(j):
        return jnp.maximum(j - nf, 0)
    return pl.pallas_call(
        functools.partial(_ffn_kernel, seq_len=seq_len, nf=nf, nn=nn, d_ff=d_ff),
        grid=(n // tm, nf + nn),
        in_specs=[pl.BlockSpec((tm, d), lambda i, j: (i, 0), pipeline_mode=pl.Buffered(1)),
                  pl.BlockSpec((d, tf), lambda i, j: (0, up(j))),
                  pl.BlockSpec((d, tf), lambda i, j: (0, nf + up(j))),
                  pl.BlockSpec((3, 2 * d_ff), lambda i, j: (0, 0)),
                  pl.BlockSpec((1, 2 * d_ff), lambda i, j: (0, 0)),
                  pl.BlockSpec((d_ff, tn), lambda i, j: (0, down(j))),
                  pl.BlockSpec((tm, tn), lambda i, j: (i, down(j))),
                  pl.BlockSpec((None, 6, d), lambda i, j: (seq_of(i), 0, 0)),
                  pl.BlockSpec((1, d), lambda i, j: (0, 0))],
        out_specs=pl.BlockSpec((tm, d), lambda i, j: (i, 0)),
        out_shape=jax.ShapeDtypeStruct((n, d), F32),
        scratch_shapes=[pltpu.VMEM((tm, tf + d_ff), BF16), pltpu.VMEM((2, tm, tf // 2), F32)],
        compiler_params=_cparams(("arbitrary", "arbitrary")),
        name="ffn",
    )(xm2, w_up, w_up, conv_w, cb, w_down, x1, mod, final_g.reshape(1, d))


def _rope_tables(seq_len):
    rows = seq_len // GRID_W
    row = jnp.repeat(jnp.arange(rows, dtype=F32), GRID_W)
    colp = jnp.tile(jnp.arange(GRID_W, dtype=F32), rows)
    n_freq = HEAD_DIM // 4
    inv = ROPE_THETA ** (-jnp.arange(n_freq, dtype=F32) / n_freq)
    ang_r = row[:, None] * inv[None, :]
    ang_c = colp[:, None] * inv[None, :]
    cos = jnp.concatenate([jnp.cos(ang_r), jnp.cos(ang_r), jnp.cos(ang_c), jnp.cos(ang_c)], axis=1)
    sin = jnp.concatenate([-jnp.sin(ang_r), jnp.sin(ang_r), -jnp.sin(ang_c), jnp.sin(ang_c)], axis=1)
    return cos, sin


PROJ_TM = 1024
MIXER_TM = 256


def kernel(x_prompt, x_sample, cache_k, cache_v, state_ssm_re, state_ssm_im, c, c_ctx, norm_mix_g, norm_ffn_g,
           w_mod, b_mod, w_in, ssm_lambda_re, ssm_lambda_im, ssm_log_dt, ssm_b_re, ssm_b_im, ssm_c_re, ssm_c_im,
           ssm_d, w_glu, attn_sink, w_ssm_o, w_attn_o, w_out, w_up, conv_w, conv_b, w_down, final_norm_g):
    batch, seq, d = x_prompt.shape
    dec_batch, dec_seq, _ = x_sample.shape
    depth = w_in.shape[0]
    assert depth == 1, "final norm is fused into the (single) layer's ffn kernel"
    d_ssm = w_glu.shape[1]
    d_attn = N_HEADS * HEAD_DIM
    d_kv = N_KV_HEADS * HEAD_DIM
    groups = d_ssm // SSM_CH
    assert w_in.shape[2] == d_ssm + d_attn + 2 * d_kv + 2 * d
    assert ssm_lambda_re.shape[2:] == (groups, SSM_STATE) and dec_batch <= 8 - 1
    l = 0

    col = {"gs": 0, "ga": d, "q": 2 * d, "u": 2 * d + d_attn, "k": 2 * d + d_attn + d_ssm,
           "d_ssm": d_ssm, "d_attn": d_attn, "d_kv": d_kv}
    xp = x_prompt.reshape(batch * seq, d)
    xs = x_sample.reshape(dec_batch * dec_seq, d)

    cond = jnp.concatenate([c_ctx[None], c, jnp.zeros((8 - 1 - dec_batch, d), F32)], axis=0)

    ssm_ops, mod, (w_in_b,) = _ssm_operators_and_mod(
        ssm_lambda_re[l], ssm_lambda_im[l], ssm_log_dt[l], ssm_b_re[l], ssm_b_im[l], ssm_c_re[l], ssm_c_im[l],
        ssm_d[l], cond, w_mod[l], b_mod[l], riders=(w_in[l],))
    mod = mod.reshape(8, 6, d)
    mod_p, mod_s = mod[0:1], mod[1:1 + dec_batch]
    proj_p, tail_p, k_raw, v_raw = _proj(xp, mod_p, norm_mix_g[l], w_in_b, tm=PROJ_TM,
                                         tiles_per_seq=max(seq // PROJ_TM, 1), col=col)
    proj_s, tail_s, _, _ = _proj(xs, mod_s, norm_mix_g[l], w_in_b, tm=PROJ_TM,
                                 tiles_per_seq=max(dec_seq // PROJ_TM, 1), col=col)

    def lanes(s):
        return s.transpose(0, 2, 1, 3).reshape(dec_batch, groups, 2 * SSM_STATE)
    h0_p = jnp.zeros((batch, groups * 4 * SSM_STATE), F32)
    h0_s = _slab_lanes(jnp.concatenate([lanes(state_ssm_re[:, l]), lanes(state_ssm_im[:, l])], axis=-1))
    y_ssm_p, h_fin, (w_up_b,) = _ssm(tail_p, 0, ssm_ops, h0_p, n_seq=batch, seq_len=seq,
                                     seq_block=min(batch, SSM_SEQ_BLOCK), riders=(w_up[l],))
    y_ssm_s, _, (w_down_b,) = _ssm(tail_s, 0, ssm_ops, h0_s, n_seq=dec_batch, seq_len=dec_seq,
                                   seq_block=min(dec_batch, SSM_SEQ_BLOCK), riders=(w_down[l],))

    sink = attn_sink[l]
    o_p, mixer_w = _attn_ctx(proj_p, sink, n_seq=batch, seq_len=seq, col=col,
                             riders=(w_glu[l], w_ssm_o[l], w_attn_o[l], w_out[l]))
    o_s = _attn_lat(proj_s, cache_k[:, l].reshape(dec_batch, -1, d_kv), cache_v[:, l].reshape(dec_batch, -1, d_kv),
                    sink, n_seq=dec_batch, seq_len=dec_seq, col=col)

    x1_p, xm2_p = _mixer_out(xp, y_ssm_p, o_p, proj_p, mod_p, norm_ffn_g[l], *mixer_w, tm=MIXER_TM,
                             tiles_per_seq=max(seq // MIXER_TM, 1), col=col)
    x1_s, xm2_s = _mixer_out(xs, y_ssm_s, o_s, proj_s, mod_s, norm_ffn_g[l], *mixer_w, tm=MIXER_TM,
                             tiles_per_seq=max(dec_seq // MIXER_TM, 1), col=col)
    y_p = _ffn(x1_p, xm2_p, mod_p, w_up_b, conv_w[l], conv_b[l], w_down_b, final_norm_g, tm=FFN_TM, seq_len=seq)
    y_s = _ffn(x1_s, xm2_s, mod_s, w_up_b, conv_w[l], conv_b[l], w_down_b, final_norm_g, tm=FFN_TM, seq_len=dec_seq)

    new_k = k_raw.reshape(batch, 1, seq, N_KV_HEADS, HEAD_DIM)
    new_v = v_raw.reshape(batch, 1, seq, N_KV_HEADS, HEAD_DIM)

    def unlanes(hl):
        return hl.reshape(batch, groups, 2, SSM_STATE).transpose(0, 2, 1, 3)[:, None]
    h_fin = _unslab_lanes(h_fin, groups)
    new_re = unlanes(h_fin[:, :, :2 * SSM_STATE])
    new_im = unlanes(h_fin[:, :, 2 * SSM_STATE:])
    return (y_p.reshape(batch, seq, d), y_s.reshape(dec_batch, dec_seq, d), new_k, new_v, new_re, new_im)
```

```python
import functools
import math

import jax
import jax.numpy as jnp
from jax import lax
from jax.experimental import pallas as pl
from jax.experimental.pallas import tpu as pltpu

F32 = jnp.float32
BF16 = jnp.bfloat16

GRID_W = 64
SSM_CH = 16
SSM_STATE = 64
N_HEADS = 8
N_KV_HEADS = 2
HEAD_DIM = 128
Q_PER_KV = N_HEADS // N_KV_HEADS
WINDOW = 128
BLOCK = 128
ROPE_THETA = 10000.0
EPS = 1e-6
NEG_INF = -1e30

SSM_CHUNK = 16
SSM_LANES = SSM_CHUNK * SSM_CH
SSM_SEQ_BLOCK = 16

V7X_VMEM_LIMIT_BYTES = 56 * 1024 * 1024
V7X_MXU_WIDTH = 256


def _cparams(semantics):
    return pltpu.CompilerParams(dimension_semantics=semantics, vmem_limit_bytes=V7X_VMEM_LIMIT_BYTES)


def _sigmoid(x):
    return 1.0 / (1.0 + jnp.exp(-x))


def _ride_casts(body, n_in, n_out, n_riders):
    def kernel(*refs):
        ins = refs[:n_in]
        rider_ins = refs[n_in:n_in + n_riders]
        outs = refs[n_in + n_riders:n_in + n_riders + n_out]
        rider_outs = refs[n_in + n_riders + n_out:n_in + 2 * n_riders + n_out]
        scratch = refs[n_in + 2 * n_riders + n_out:]
        for src, dst in zip(rider_ins, rider_outs):
            dst[...] = src[...].astype(BF16)
        body(*ins, *outs, *scratch)
    return kernel


def _rider_specs(weights, n_steps, step_of):
    specs, shapes = [], []
    for w in weights:
        r, c = w.shape
        assert r % (16 * n_steps) == 0
        specs.append(pl.BlockSpec((r // n_steps, c), lambda *idx: (step_of(*idx), 0)))
        shapes.append(jax.ShapeDtypeStruct((r, c), BF16))
    return specs, shapes


def _rms_modulate(x, g, scale, shift):
    ms = jnp.mean(x * x, axis=-1, keepdims=True)
    return (x * lax.rsqrt(ms + EPS) * g) * (1.0 + scale) + shift


MOD_K_CHUNK = 256


def _mod_kernel(c_ref, w_ref, b_ref, o_ref):
    c = c_ref[...]
    a = (c * _sigmoid(c)).astype(BF16)
    acc = b_ref[...]
    for k in range(0, w_ref.shape[0], MOD_K_CHUNK):
        acc = acc + jnp.dot(a[:, k:k + MOD_K_CHUNK], w_ref[k:k + MOD_K_CHUNK, :].astype(BF16),
                            preferred_element_type=F32)
    o_ref[...] = acc


PROJ_TN = 512
NORM_ROWS = 16
NORM_UNROLL = 16


def _swap32(x):
    n = x.shape[-1]
    lane = lax.broadcasted_iota(jnp.int32, x.shape, x.ndim - 1)
    return jnp.where((lane & 63) < 32, pltpu.roll(x, n - 32, x.ndim - 1), pltpu.roll(x, 32, x.ndim - 1))


def _proj_kernel(x_ref, mod_ref, g_ref, w_ref, o_ref, tail_ref, k_ref, v_ref, *rest, q_tiles):
    xm_scr = rest[-1]
    j = pl.program_id(1)

    @pl.when(j == 0)
    def _():
        gain = g_ref[...] * (1.0 + mod_ref[1:2, :])
        shift = mod_ref[0:1, :]

        def block(c, carry):
            rows = pl.ds(pl.multiple_of(c * NORM_ROWS, NORM_ROWS), NORM_ROWS)
            x = x_ref[rows, :]
            ms = jnp.mean(x * x, axis=-1, keepdims=True)
            xm_scr[rows, :] = (x * lax.rsqrt(ms + EPS) * gain + shift).astype(BF16)
            return carry

        lax.fori_loop(0, x_ref.shape[0] // NORM_ROWS, block, 0, unroll=NORM_UNROLL)

    w = w_ref[...]
    if len(rest) == 2:
        w = w.astype(BF16)
        rest[0][...] = w
    acc = jnp.dot(xm_scr[...], w, preferred_element_type=F32)
    scale = jnp.where((j >= q_tiles[0]) & (j < q_tiles[1]), HEAD_DIM ** -0.5, 1.0)
    o_ref[...] = (acc * scale).astype(BF16)
    tail_ref[...] = acc
    d_kv = k_ref.shape[1]
    k_ref[...] = acc[:, :d_kv]
    v_ref[...] = acc[:, d_kv:]


def _proj(x, mod, norm_g, w_in, *, tm, tiles_per_seq, col):
    n, d = x.shape
    cols = w_in.shape[1]
    tn = PROJ_TN
    n_tiles = cols // tn
    d_kv = col["d_kv"]
    n_gate, n_q, n_u = col["q"] // tn, col["d_attn"] // tn, col["d_ssm"] // tn
    q_tiles = (n_gate, n_gate + n_q)
    tail0 = col["u"] // tn
    n_tail = n_tiles - tail0
    assert 2 * d_kv == tn and col["u"] == col["q"] + col["d_attn"] and col["k"] == cols - tn

    def src_tile(j):
        return jnp.where(j < n_gate, j + (n_u + n_q + 1),
                         jnp.where(j < n_gate + n_q, j - n_gate + n_u,
                                   jnp.where(j < n_gate + n_q + n_u, j - (n_gate + n_q), n_u + n_q)))
    seq_of = (lambda i: i // tiles_per_seq) if mod.shape[0] > 1 else (lambda i: 0)
    out_specs = [pl.BlockSpec((tm, tn), lambda i, j: (i, j)),
                 pl.BlockSpec((tm, tn), lambda i, j: (i, jnp.clip(j - tail0, 0, n_tail - 1))),
                 pl.BlockSpec((tm, d_kv), lambda i, j: (i, 0)),
                 pl.BlockSpec((tm, d_kv), lambda i, j: (i, 0))]
    out_shape = [jax.ShapeDtypeStruct((n, cols), BF16), jax.ShapeDtypeStruct((n, n_tail * tn), F32),
                 jax.ShapeDtypeStruct((n, d_kv), F32), jax.ShapeDtypeStruct((n, d_kv), F32)]
    if w_in.dtype != BF16:
        out_specs.append(pl.BlockSpec((d, tn), lambda i, j: (0, src_tile(jnp.where(i == 0, j, n_tiles - 1)))))
        out_shape.append(jax.ShapeDtypeStruct((d, cols), BF16))
    return pl.pallas_call(
        functools.partial(_proj_kernel, q_tiles=q_tiles),
        grid=(n // tm, n_tiles),
        in_specs=[pl.BlockSpec((tm, d), lambda i, j: (i, 0)),
                  pl.BlockSpec((None, 6, d), lambda i, j: (seq_of(i), 0, 0)),
                  pl.BlockSpec((1, d), lambda i, j: (0, 0)),
                  pl.BlockSpec((d, tn), lambda i, j: (0, src_tile(j)))],
        out_specs=out_specs,
        out_shape=out_shape,
        scratch_shapes=[pltpu.VMEM((tm, d), BF16)],
        compiler_params=_cparams(("arbitrary", "arbitrary")),
        name="proj",
    )(x, mod, norm_g.reshape(1, d), w_in)


def _cmul(ar, ai, br, bi):
    return ar * br - ai * bi, ar * bi + ai * br


def _cpow(base_r, base_i, n, bits):
    res_r = jnp.ones_like(base_r)
    res_i = jnp.zeros_like(base_r)
    for bit in range(bits):
        nr, ni = _cmul(res_r, res_i, base_r, base_i)
        take = ((n >> bit) & 1) == 1
        res_r = jnp.where(take, nr, res_r)
        res_i = jnp.where(take, ni, res_i)
        if bit + 1 < bits:
            base_r, base_i = _cmul(base_r, base_i, base_r, base_i)
    return res_r, res_i


def _ssm_disc_kernel(lre_ref, lim_ref, ldt_ref, ere_ref, eim_ref, qre_ref, qim_ref):
    lre = lre_ref[...]
    lim = lim_ref[...]
    dt = jnp.exp(ldt_ref[...])
    mag = jnp.exp(lre * dt)
    e_re = mag * jnp.cos(lim * dt)
    e_im = mag * jnp.sin(lim * dt)
    n_re = e_re - 1.0
    den = lre * lre + lim * lim
    ere_ref[...] = e_re
    eim_ref[...] = e_im
    qre_ref[...] = (n_re * lre + e_im * lim) / den
    qim_ref[...] = (e_im * lre - n_re * lim) / den


def _ssm_disc(lam_re, lam_im, log_dt):
    two, g, p = lam_re.shape
    shp = jax.ShapeDtypeStruct((two * g, p), F32)
    outs = pl.pallas_call(_ssm_disc_kernel, out_shape=[shp] * 4, name="ssm_disc")(
        lam_re.reshape(two * g, p), lam_im.reshape(two * g, p), log_dt.reshape(two * g, 1))
    return [o.reshape(two, g, p) for o in outs]


SSM_OPS_GROUPS = 4


def _ssm_ops_mod_kernel(ecol_ref, rows_ref, b_ref, ct_ref, d_ref, c_ref, wm_ref, bm_ref,
                        m_ref, w_ref, v_ref, a_ref, mod_ref):
    _mod_kernel(c_ref, wm_ref, bm_ref, mod_ref)
    _ssm_ops_kernel(ecol_ref, rows_ref, b_ref, ct_ref, d_ref, m_ref, w_ref, v_ref, a_ref)


def _ssm_ops_kernel(ecol_ref, rows_ref, b_ref, ct_ref, d_ref, m_ref, w_ref, v_ref, a_ref):
    def one(gi, carry):
        _ssm_ops_group(ecol_ref.at[gi], rows_ref.at[gi], b_ref.at[gi], ct_ref.at[gi], d_ref.at[gi],
                       m_ref.at[gi], w_ref.at[gi], v_ref.at[gi], a_ref.at[gi])
        return carry

    lax.fori_loop(0, ecol_ref.shape[0], one, 0, unroll=True)


def _ssm_ops_group(ecol_ref, rows_ref, b_ref, ct_ref, d_ref, m_ref, w_ref, v_ref, a_ref):
    L, ch, p = SSM_CHUNK, SSM_CH, SSM_STATE
    lanes = SSM_LANES
    half = 2 * p
    lane_c = lax.broadcasted_iota(jnp.int32, (p, lanes), 1)
    lag = lane_c // ch

    def col(k):
        return jnp.broadcast_to(ecol_ref[k], (p, lanes))

    f0 = _cpow(col(0), col(1), lag, 4)
    b0 = _cpow(col(2), col(3), (L - 1) - lag, 4)
    f1 = _cmul(f0[0], f0[1], col(0), col(1))
    b1 = _cmul(b0[0], b0[1], col(2), col(3))
    def ct(k):
        return jnp.tile(ct_ref[k], (1, L))

    cf = (ct(0), ct(1))
    cb = (ct(2), ct(3))
    yf0 = _cmul(*f0, *cf)
    yb0 = _cmul(*b0, *cb)
    yf1 = _cmul(*f1, *cf)
    yb1 = _cmul(*b1, *cb)
    v_ref[...] = jnp.concatenate([yf1[0], yb1[0], -yf1[1], -yb1[1]], axis=0).astype(BF16)

    lane_r = lax.broadcasted_iota(jnp.int32, (ch, lanes), 1)
    sgn = jnp.where(lane_r < half, -1.0, 1.0)
    b1raw = b_ref[...]
    b2raw = pltpu.roll(b1raw, half, 1) * sgn
    bb1 = rows_ref[2:3, :] * b1raw + rows_ref[3:4, :] * b2raw
    bb2 = pltpu.roll(bb1, half, 1) * sgn

    zero = jnp.zeros((p, lanes), F32)
    hi = lax.Precision.HIGHEST
    r0f = jnp.dot(bb1, jnp.concatenate([yf0[0], zero, -yf0[1], zero], axis=0), precision=hi,
                  preferred_element_type=F32)
    r0b = jnp.dot(bb1, jnp.concatenate([zero, yb0[0], zero, -yb0[1]], axis=0), precision=hi,
                  preferred_element_type=F32)
    row_r = lax.broadcasted_iota(jnp.int32, (ch, lanes), 0)
    d_col = d_ref[...]
    for s in range(L):
        blk = jnp.where(lane_r == ch * s + row_r, d_col, 0.0)
        fwd = r0f if s == 0 else jnp.where(lane_r >= ch * s, pltpu.roll(r0f, ch * s, 1), 0.0)
        k = ch * (L - 1 - s)
        bwd = r0b if k == 0 else jnp.where(lane_r < lanes - k, pltpu.roll(r0b, lanes - k, 1), 0.0)
        m_ref[s * ch:(s + 1) * ch, :] = (blk + fwd + bwd).astype(BF16)

    is_f = (lane_r & (half - 1)) < p
    n_row = jnp.where(is_f, (L - 1) - row_r, row_r)
    er, ei = _cpow(jnp.broadcast_to(rows_ref[0:1, :], (L, lanes)), jnp.broadcast_to(rows_ref[1:2, :], (L, lanes)),
                   n_row, 4)
    for s in range(L):
        w_ref[s * ch:(s + 1) * ch, :] = (er[s:s + 1, :] * bb1 + ei[s:s + 1, :] * bb2).astype(BF16)

    ar, ai = rows_ref[0:1, :], rows_ref[1:2, :]
    for _ in range(4):
        ar, ai = _cmul(ar, ai, ar, ai)
    lane_1 = lax.broadcasted_iota(jnp.int32, (1, lanes), 1)
    a_ref[...] = jnp.where(lane_1 < half, ar, ai)


def _ssm_operators_and_mod(lam_re, lam_im, log_dt, b_re, b_im, c_re, c_im, ssm_d, cond, w_mod, b_mod, riders=()):
    two, g, p = lam_re.shape
    ch = b_re.shape[-1]
    lanes = SSM_LANES
    assert (two, p, ch, SSM_CHUNK * ch) == (2, SSM_STATE, SSM_CH, lanes)
    e_re, e_im, q_re, q_im = _ssm_disc(lam_re, lam_im, log_dt)
    ecol = jnp.stack([e_re[0], e_im[0], e_re[1], e_im[1]], axis=1)[..., None]

    def fbfb(x):
        return jnp.concatenate([x[0], x[1], x[0], x[1]], axis=-1)
    rows = jnp.stack([fbfb(e_re), fbfb(e_im), fbfb(q_re), fbfb(q_im)], axis=1)

    def bt(x):
        return jnp.swapaxes(x, -1, -2)
    b_rows = jnp.concatenate([bt(b_re[0]), bt(b_re[1]), bt(b_im[0]), bt(b_im[1])], axis=-1)

    ct = jnp.stack([bt(c_re[0]), bt(c_im[0]), bt(c_re[1]), bt(c_im[1])], axis=1)
    d_col = ssm_d.reshape(g, ch, 1)
    gb = SSM_OPS_GROUPS
    mat = pl.BlockSpec((gb, lanes, lanes), lambda i: (i, 0, 0))
    steps = g // gb
    rider_specs, rider_shapes = _rider_specs(riders, steps, lambda i: i)
    c_rows, d_model = cond.shape
    n_mod = w_mod.shape[1]
    tn = n_mod // steps
    assert n_mod % steps == 0 and tn % 128 == 0
    outs = pl.pallas_call(
        _ride_casts(_ssm_ops_mod_kernel, 8, 5, len(riders)),
        grid=(steps,),
        in_specs=[pl.BlockSpec((gb, 4, p, 1), lambda i: (i, 0, 0, 0)),
                  pl.BlockSpec((gb, 4, lanes), lambda i: (i, 0, 0)),
                  pl.BlockSpec((gb, ch, lanes), lambda i: (i, 0, 0)),
                  pl.BlockSpec((gb, 4, p, ch), lambda i: (i, 0, 0, 0)),
                  pl.BlockSpec((gb, ch, 1), lambda i: (i, 0, 0)),
                  pl.BlockSpec((c_rows, d_model), lambda i: (0, 0)),
                  pl.BlockSpec((d_model, tn), lambda i: (0, i)),
                  pl.BlockSpec((1, tn), lambda i: (0, i))] + rider_specs,
        out_specs=[mat, mat, mat, pl.BlockSpec((gb, 1, lanes), lambda i: (i, 0, 0)),
                   pl.BlockSpec((c_rows, tn), lambda i: (0, i))] + rider_specs,
        out_shape=([jax.ShapeDtypeStruct((g, lanes, lanes), BF16)] * 3 + [jax.ShapeDtypeStruct((g, 1, lanes), F32),
                                                                          jax.ShapeDtypeStruct((c_rows, n_mod), F32)]
                   + rider_shapes),
        compiler_params=_cparams(("arbitrary",)),
        name="ssm_ops_mod",
    )(ecol, rows, b_rows, ct, d_col, cond, w_mod, b_mod.reshape(1, n_mod), *riders)
    return outs[:4], outs[4], outs[5:]


SSM_SLAB_GROUPS = 128 // SSM_CH


def _chunk_transpose(xs, chunk_id):
    for k in (4, 2, 1):
        keep = (chunk_id & k) == 0
        new = list(xs)
        for i in range(len(xs)):
            if i & k == 0:
                a, b = xs[i], xs[i + k]
                if 2 * SSM_CH * k == 128:
                    moved = pltpu.roll(jnp.where(keep, b, a), SSM_CH * k, 1)
                    new[i] = jnp.where(keep, a, moved)
                    new[i + k] = jnp.where(keep, moved, b)
                else:
                    new[i] = jnp.where(keep, a, pltpu.roll(b, SSM_CH * k, 1))
                    new[i + k] = jnp.where(keep, pltpu.roll(a, 128 - SSM_CH * k, 1), b)
        xs = new
    return xs


def _ssm_kernel(u_ref, m_ref, w_ref, v_ref, a_ref, h0_ref, y_ref, hfin_ref, ug_scr, s_scr, hf_scr, hb_scr, yg_scr,
                *, n_chunks, ns):
    gps = SSM_SLAB_GROUPS
    nc = ns * n_chunks
    half = 2 * SSM_STATE
    chunk_id = lax.broadcasted_iota(jnp.int32, (nc, 128), 1) // SSM_CH
    for hf in range(2):
        xs = [u_ref[pl.ds(hf * gps + i, nc, stride=SSM_CHUNK), :] for i in range(gps)]
        xs = _chunk_transpose(xs, chunk_id)
        for g in range(gps):
            ug_scr[g, :, hf * 128:(hf + 1) * 128] = xs[g].astype(BF16)
    pitch = n_chunks + 8
    for g in range(gps):
        s = jnp.dot(ug_scr[g], w_ref[g], preferred_element_type=F32)
        for q in range(ns):
            s_scr[g, q * pitch:q * pitch + n_chunks, :] = s[q * n_chunks:(q + 1) * n_chunks, :half]
            s_scr[gps + g, q * pitch:q * pitch + n_chunks, :] = s[q * n_chunks:(q + 1) * n_chunks, half:]
    is_fwd = lax.broadcasted_iota(jnp.int32, (1, half), 1) < SSM_STATE
    blocks = range(2 * gps)
    a = [a_ref[:, b * half:(b + 1) * half] for b in blocks]
    h = [h0_ref[:, b * half:(b + 1) * half] for b in blocks]

    def rows(c):
        return pl.ds(c, ns, stride=pitch)

    for k in range(n_chunks):
        kb = n_chunks - 1 - k
        new_h = list(h)
        for b in blocks:
            hf_scr[b, rows(k), :] = h[b]
            hb_scr[b, rows(kb), :] = h[b]
        for g in range(gps):
            s_re = jnp.where(is_fwd, s_scr[g, rows(k), :], s_scr[g, rows(kb), :])
            s_im = jnp.where(is_fwd, s_scr[gps + g, rows(k), :], s_scr[gps + g, rows(kb), :])
            new_h[g] = a[g] * h[g] - a[gps + g] * h[gps + g] + s_re
            new_h[gps + g] = a[g] * h[gps + g] + a[gps + g] * h[g] + s_im
        h = new_h
    for b in blocks:
        hfin_ref[:, b * half:(b + 1) * half] = h[b]

    def entering(b):
        return jnp.concatenate(
            [jnp.where(is_fwd, hf_scr[b, q * pitch:q * pitch + n_chunks, :], hb_scr[b, q * pitch:q * pitch + n_chunks, :])
             for q in range(ns)], axis=0)

    for g in range(gps):
        hin = jnp.concatenate([entering(g), entering(gps + g)], axis=1).astype(BF16)
        yg_scr[g] = (jnp.dot(ug_scr[g], m_ref[g], preferred_element_type=F32)
                     + jnp.dot(hin, v_ref[g], preferred_element_type=F32))
    for hf in range(2):
        xs = [yg_scr[g, :, hf * 128:(hf + 1) * 128] for g in range(gps)]
        xs = _chunk_transpose(xs, chunk_id)
        for i in range(gps):
            y_ref[pl.ds(hf * gps + i, nc, stride=SSM_CHUNK), :] = xs[i]


def _slab_lanes(x):
    lead = x.shape[:-2]
    g = x.shape[-2]
    gps = SSM_SLAB_GROUPS
    x = x.reshape(lead + (g // gps, gps, 2, 2 * SSM_STATE))
    x = jnp.swapaxes(x, -3, -2)
    return x.reshape(lead + (g * 4 * SSM_STATE,))


def _unslab_lanes(x, g):
    lead = x.shape[:-1]
    gps = SSM_SLAB_GROUPS
    x = x.reshape(lead + (g // gps, 2, gps, 2 * SSM_STATE))
    x = jnp.swapaxes(x, -3, -2)
    return x.reshape(lead + (g, 4 * SSM_STATE))


def _ssm(u, u_col, ops, h0, *, n_seq, seq_len, seq_block, riders=()):
    m_op, w_op, v_op, a_op = ops
    n = u.shape[0]
    g = m_op.shape[0]
    d_ssm = g * SSM_CH
    slab0 = u_col // 128
    gps = SSM_SLAB_GROUPS
    lanes = SSM_LANES
    n_chunks = seq_len // SSM_CHUNK
    nc = seq_block * n_chunks
    padded = seq_block * (n_chunks + 8)
    rows = seq_block * seq_len
    wide2 = gps * 4 * SSM_STATE
    mat = pl.BlockSpec((gps, lanes, lanes), lambda j, b: (j, 0, 0))
    n_blocks = n_seq // seq_block
    rider_specs, rider_shapes = _rider_specs(riders, (g // gps) * n_blocks, lambda j, b: j * n_blocks + b)
    outs = pl.pallas_call(
        _ride_casts(functools.partial(_ssm_kernel, n_chunks=n_chunks, ns=seq_block), 6, 2, len(riders)),
        grid=(g // gps, n_blocks),
        in_specs=[pl.BlockSpec((rows, 128), lambda j, b: (b, slab0 + j)), mat, mat, mat,
                  pl.BlockSpec((1, wide2), lambda j, b: (0, j)),
                  pl.BlockSpec((seq_block, wide2), lambda j, b: (b, j))] + rider_specs,
        out_specs=[pl.BlockSpec((rows, 128), lambda j, b: (b, j)),
                   pl.BlockSpec((seq_block, wide2), lambda j, b: (b, j))] + rider_specs,
        out_shape=[jax.ShapeDtypeStruct((n, d_ssm), F32),
                   jax.ShapeDtypeStruct((n_seq, g * 4 * SSM_STATE), F32)] + rider_shapes,
        scratch_shapes=[pltpu.VMEM((gps, nc, lanes), BF16), pltpu.VMEM((2 * gps, padded, 128), F32),
                        pltpu.VMEM((2 * gps, padded, 128), F32), pltpu.VMEM((2 * gps, padded, 128), F32),
                        pltpu.VMEM((gps, nc, lanes), F32)],
        compiler_params=_cparams(("arbitrary", "arbitrary")),
        name="ssm",
    )(u, m_op, w_op, v_op, _slab_lanes(a_op[:, 0])[None], h0, *riders)
    return outs[0], outs[1], outs[2:]


def _softmax_pv(pieces, sink):
    m = sink
    for s, _ in pieces:
        m = jnp.maximum(m, jnp.max(s, axis=-1, keepdims=True))
    out = None
    for s, v in pieces:
        pv = jnp.dot(jnp.exp(s - m).astype(BF16), v, preferred_element_type=F32)
        out = pv if out is None else out + pv
    return out[:, :HEAD_DIM] / (out[:, HEAD_DIM:] + jnp.exp(sink - m))


def _qk(q, k):
    return lax.dot_general(q, k, (((1,), (1,)), ((), ())), preferred_element_type=F32)


def _with_ones(v):
    return jnp.concatenate([v, jnp.ones_like(v)], axis=1)


ATTN_CTX_SEQS = 8


def _attn_ctx_kernel(sink_ref, q_ref, kv_ref, o_ref, *, seq_len):
    d_kv = N_KV_HEADS * HEAD_DIM
    for r0 in range(0, q_ref.shape[0], seq_len):
        rows = slice(r0, r0 + seq_len)
        for kh in range(N_KV_HEADS):
            k = kv_ref[rows, kh * HEAD_DIM:(kh + 1) * HEAD_DIM]
            v = _with_ones(kv_ref[rows, d_kv + kh * HEAD_DIM:d_kv + (kh + 1) * HEAD_DIM])
            for h in range(kh * Q_PER_KV, (kh + 1) * Q_PER_KV):
                q = q_ref[rows, h * HEAD_DIM:(h + 1) * HEAD_DIM]
                o = _softmax_pv([(_qk(q, k), v)], sink_ref[h])
                o_ref[rows, h * HEAD_DIM:(h + 1) * HEAD_DIM] = o.astype(BF16)


def _attn_ctx(proj, sink, *, n_seq, seq_len, col, riders=()):
    d_attn = col["d_attn"]
    kv_w = 2 * col["d_kv"]
    spb = ATTN_CTX_SEQS if n_seq % ATTN_CTX_SEQS == 0 else 1
    rows = spb * seq_len
    rider_specs, rider_shapes = _rider_specs(riders, n_seq // spb, lambda b: b)
    outs = pl.pallas_call(
        _ride_casts(functools.partial(_attn_ctx_kernel, seq_len=seq_len), 3, 1, len(riders)),
        grid=(n_seq // spb,),
        in_specs=[pl.BlockSpec(memory_space=pltpu.SMEM),
                  pl.BlockSpec((rows, d_attn), lambda b: (b, col["q"] // d_attn)),
                  pl.BlockSpec((rows, kv_w), lambda b: (b, col["k"] // kv_w))] + rider_specs,
        out_specs=[pl.BlockSpec((rows, d_attn), lambda b: (b, 0))] + rider_specs,
        out_shape=[jax.ShapeDtypeStruct((n_seq * seq_len, d_attn), BF16)] + rider_shapes,
        compiler_params=_cparams(("arbitrary",)),
        name="attn_ctx",
    )(sink, proj, proj, *riders)
    return outs[0], outs[1:]


def _rope(x, cos, sin):
    reps = x.shape[1] // HEAD_DIM
    xf = x.astype(F32)
    return (xf * jnp.concatenate([cos] * reps, axis=1) + _swap32(xf) * jnp.concatenate([sin] * reps, axis=1)).astype(BF16)


ATTN_LAT_QBLOCKS = 4


def _attn_lat_kernel(*refs, n_blocks, qb):
    sink_ref, q_ref = refs[:2]
    kv_refs = refs[2:4 + qb]
    ck_ref, cv_ref, cos_ref, sin_ref, o_ref = refs[4 + qb:]
    first = pl.program_id(1) * qb
    d_kv = N_KV_HEADS * HEAD_DIM

    def tables(blk):
        rows = pl.ds(pl.multiple_of(blk * BLOCK, BLOCK), BLOCK)
        return cos_ref[rows, :], sin_ref[rows, :]

    k_rot = [_rope(kv_refs[n][:, :d_kv], *tables(jnp.clip(first + n - 1, 0, n_blocks - 1))) for n in range(qb + 2)]
    r = lax.broadcasted_iota(jnp.int32, (BLOCK, 3 * BLOCK), 0)
    c = lax.broadcasted_iota(jnp.int32, (BLOCK, 3 * BLOCK), 1)
    cc = c & (BLOCK - 1)
    for sub in range(qb):
        blk = first + sub
        q_rows = slice(sub * BLOCK, (sub + 1) * BLOCK)
        q_all = _rope(q_ref[q_rows, :], *tables(blk))
        k_all = jnp.concatenate(k_rot[sub:sub + 3], axis=0)
        valid = (((c < BLOCK) & (cc >= r) & (blk > 0)) | ((c >= BLOCK) & (c < 2 * BLOCK))
                 | ((c >= 2 * BLOCK) & (cc <= r) & (blk < n_blocks - 1)))
        for kh in range(N_KV_HEADS):
            ks = slice(kh * HEAD_DIM, (kh + 1) * HEAD_DIM)
            vs = slice(d_kv + kh * HEAD_DIM, d_kv + (kh + 1) * HEAD_DIM)
            k_loc = k_all[:, ks]
            k_ctx = ck_ref[:, ks].astype(BF16)
            v_loc = _with_ones(jnp.concatenate([kv_refs[sub + n][:, vs] for n in range(3)], axis=0))
            v_ctx = _with_ones(cv_ref[:, ks].astype(BF16))
            for h in range(kh * Q_PER_KV, (kh + 1) * Q_PER_KV):
                q = q_all[:, h * HEAD_DIM:(h + 1) * HEAD_DIM]
                s_loc = jnp.where(valid, _qk(q, k_loc), NEG_INF)
                o = _softmax_pv([(s_loc, v_loc), (_qk(q, k_ctx), v_ctx)], sink_ref[h])
                o_ref[q_rows, h * HEAD_DIM:(h + 1) * HEAD_DIM] = o.astype(BF16)


def _attn_lat(proj, cache_k, cache_v, sink, *, n_seq, seq_len, col):
    d_attn = col["d_attn"]
    kv_w = 2 * col["d_kv"]
    assert WINDOW == BLOCK and seq_len % BLOCK == 0
    nb = seq_len // BLOCK
    qb = ATTN_LAT_QBLOCKS if nb % ATTN_LAT_QBLOCKS == 0 else 1
    steps = nb // qb
    kv_col = col["k"] // kv_w
    past, d_kv = cache_k.shape[1], cache_k.shape[2]

    def kv_spec(off):
        return pl.BlockSpec((BLOCK, kv_w), lambda b, i: (b * nb + jnp.clip(i * qb + off, 0, nb - 1), kv_col))
    cache_spec = pl.BlockSpec((None, past, d_kv), lambda b, i: (b, 0, 0))
    table_spec = pl.BlockSpec((seq_len, HEAD_DIM), lambda b, i: (0, 0))
    cos, sin = _rope_tables(seq_len)
    return pl.pallas_call(
        functools.partial(_attn_lat_kernel, n_blocks=nb, qb=qb),
        grid=(n_seq, steps),
        in_specs=[pl.BlockSpec(memory_space=pltpu.SMEM),
                  pl.BlockSpec((qb * BLOCK, d_attn), lambda b, i: (b * steps + i, col["q"] // d_attn))]
                 + [kv_spec(off) for off in range(-1, qb + 1)]
                 + [cache_spec, cache_spec, table_spec, table_spec],
        out_specs=pl.BlockSpec((qb * BLOCK, d_attn), lambda b, i: (b * steps + i, 0)),
        out_shape=jax.ShapeDtypeStruct((n_seq * seq_len, d_attn), BF16),
        compiler_params=_cparams(("arbitrary", "arbitrary")),
        name="attn_lat",
    )(sink, proj, *([proj] * (qb + 2)), cache_k, cache_v, cos, sin)


def _gelu_tanh(x):
    return 0.5 * x * (1.0 + jnp.tanh(math.sqrt(2.0 / math.pi) * (x + 0.044715 * (x * x * x))))


def _mixer_out_kernel(x_ref, y_ref, o_ref, gs_ref, ga_ref, mod_ref, g_ref, wglu_ref, wso_ref, wao_ref, wout_ref,
                      x1_ref, xm2_ref):
    z = _gelu_tanh(y_ref[...].astype(F32))
    z = z * _sigmoid(jnp.dot(z.astype(BF16), wglu_ref[...], preferred_element_type=F32))
    s_br = jnp.dot(z.astype(BF16), wso_ref[...], preferred_element_type=F32)
    a_br = jnp.dot(o_ref[...], wao_ref[...], preferred_element_type=F32)
    merged = _sigmoid(gs_ref[...].astype(F32)) * s_br + _sigmoid(ga_ref[...].astype(F32)) * a_br
    out = jnp.dot(merged.astype(BF16), wout_ref[...], preferred_element_type=F32)
    x1 = x_ref[...] + mod_ref[2:3, :] * out
    x1_ref[...] = x1
    xm2_ref[...] = _rms_modulate(x1, g_ref[...], mod_ref[4:5, :], mod_ref[3:4, :]).astype(BF16)


def _mixer_out(x, y, o, proj, mod, norm_g, w_glu, w_ssm_o, w_attn_o, w_out, *, tm, tiles_per_seq, col):
    n, d = x.shape
    d_ssm = y.shape[1]
    d_attn = o.shape[1]
    seq_of = (lambda i: i // tiles_per_seq) if mod.shape[0] > 1 else (lambda i: 0)

    def resident(shape):
        return pl.BlockSpec(shape, lambda i: (0, 0), pipeline_mode=pl.Buffered(1))
    return pl.pallas_call(
        _mixer_out_kernel,
        grid=(n // tm,),
        in_specs=[pl.BlockSpec((tm, d), lambda i: (i, 0)),
                  pl.BlockSpec((tm, d_ssm), lambda i: (i, 0)),
                  pl.BlockSpec((tm, d_attn), lambda i: (i, 0)),
                  pl.BlockSpec((tm, d), lambda i: (i, col["gs"] // d)),
                  pl.BlockSpec((tm, d), lambda i: (i, col["ga"] // d)),
                  pl.BlockSpec((None, 6, d), lambda i: (seq_of(i), 0, 0)),
                  pl.BlockSpec((1, d), lambda i: (0, 0)),
                  resident(w_glu.shape), resident(w_ssm_o.shape), resident(w_attn_o.shape), resident(w_out.shape)],
        out_specs=[pl.BlockSpec((tm, d), lambda i: (i, 0)), pl.BlockSpec((tm, d), lambda i: (i, 0))],
        out_shape=[jax.ShapeDtypeStruct((n, d), F32), jax.ShapeDtypeStruct((n, d), BF16)],
        compiler_params=_cparams(("arbitrary",)),
        name="mixer_out",
    )(x, y, o, proj, proj, mod, norm_g.reshape(1, d), w_glu, w_ssm_o, w_attn_o, w_out)


FFN_TM = 1024
FFN_TF = 512
FFN_TN = 256


def _ffn_kernel(xm_ref, wa_ref, wb_ref, cw_ref, cb_ref, wd_ref, x1_ref, mod_ref, g_ref, o_ref, act_scr, raw_scr,
                *, seq_len, nf, nn, d_ff):
    j = pl.program_id(1)
    tm = xm_ref.shape[0]
    tf = wa_ref.shape[1]
    tn = wd_ref.shape[1]
    ch = tf // 2
    pos = lax.broadcasted_iota(jnp.int32, (tm, 1), 0) & (seq_len - 1)
    has_prev = pos != 0
    has_next = pos != seq_len - 1

    def lanes(off):
        return slice(off, off + ch) if isinstance(off, int) else pl.ds(pl.multiple_of(off, ch), ch)

    def gate(ha, hb, col0):
        def conv(h, off):
            cols = lanes(off)
            cw = cw_ref[:, cols]
            h_prev = jnp.where(has_prev, pltpu.roll(h, 1, 0), 0.0)
            h_next = jnp.where(has_next, pltpu.roll(h, tm - 1, 0), 0.0)
            return cw[0:1] * h_prev + cw[1:2] * h + cw[2:3] * h_next + cb_ref[:, cols]

        a = conv(ha, col0)
        b = conv(hb, d_ff + col0)
        return ((a * _sigmoid(a)) * b).astype(BF16)

    def act_cols(col0):
        return lanes(tf + col0)

    @pl.when(j == 0)
    def _():
        raw_scr[...] = jnp.zeros_like(raw_scr)

    @pl.when(j < nf)
    def _():
        xm = xm_ref[...]
        ha0 = jnp.dot(xm, wa_ref[:, :ch], preferred_element_type=F32)
        hb0 = jnp.dot(xm, wb_ref[:, :ch], preferred_element_type=F32)
        carried = (j - 1) * tf + ch
        act_scr[:, act_cols(carried)] = gate(raw_scr[0], raw_scr[1], jnp.maximum(carried, 0))
        ha1 = jnp.dot(xm, wa_ref[:, ch:], preferred_element_type=F32)
        hb1 = jnp.dot(xm, wb_ref[:, ch:], preferred_element_type=F32)
        act_scr[:, act_cols(j * tf)] = gate(ha0, hb0, j * tf)
        raw_scr[0] = ha1
        raw_scr[1] = hb1

    @pl.when(j == nf)
    def _():
        last = (nf - 1) * tf + ch
        act_scr[:, act_cols(last)] = gate(raw_scr[0], raw_scr[1], last)

    @pl.when(j >= nf)
    def _():
        cols = pl.ds(pl.multiple_of((j - nf) * tn, tn), tn)
        ffn = jnp.dot(act_scr[:, tf:], wd_ref[...], preferred_element_type=F32)
        o_ref[:, cols] = x1_ref[...] + mod_ref[5:6, cols] * ffn

    @pl.when(j == nf + nn - 1)
    def _():
        x2 = o_ref[...]
        ms = jnp.mean(x2 * x2, axis=-1, keepdims=True)
        o_ref[...] = x2 * lax.rsqrt(ms + EPS) * g_ref[...]


def _ffn(x1, xm2, mod, w_up, conv_w, conv_b, w_down, final_g, *, tm, seq_len):
    n, d = x1.shape
    d_ff = w_down.shape[0]
    tf, tn = FFN_TF, FFN_TN
    nf, nn = d_ff // tf, d // tn
    assert tm % seq_len == 0 and seq_len & (seq_len - 1) == 0 and d_ff % tf == 0 and (tf // 2) % V7X_MXU_WIDTH == 0
    seqs_per_tile = tm // seq_len
    seq_of = (lambda i: i * seqs_per_tile) if mod.shape[0] > 1 else (lambda i: 0)
    assert mod.shape[0] == 1 or seqs_per_tile == 1
    cb = conv_b.reshape(1, 2 * d_ff)

    def up(j):
        return jnp.minimum(j, nf - 1)

    def down(j):
        return jnp.maximum(j - nf, 0)
    return pl.pallas_call(
        functools.partial(_ffn_kernel, seq_len=seq_len, nf=nf, nn=nn, d_ff=d_ff),
        grid=(n // tm, nf + nn),
        in_specs=[pl.BlockSpec((tm, d), lambda i, j: (i, 0), pipeline_mode=pl.Buffered(1)),
                  pl.BlockSpec((d, tf), lambda i, j: (0, up(j))),
                  pl.BlockSpec((d, tf), lambda i, j: (0, nf + up(j))),
                  pl.BlockSpec((3, 2 * d_ff), lambda i, j: (0, 0)),
                  pl.BlockSpec((1, 2 * d_ff), lambda i, j: (0, 0)),
                  pl.BlockSpec((d_ff, tn), lambda i, j: (0, down(j))),
                  pl.BlockSpec((tm, tn), lambda i, j: (i, down(j))),
                  pl.BlockSpec((None, 6, d), lambda i, j: (seq_of(i), 0, 0)),
                  pl.BlockSpec((1, d), lambda i, j: (0, 0))],
        out_specs=pl.BlockSpec((tm, d), lambda i, j: (i, 0)),
        out_shape=jax.ShapeDtypeStruct((n, d), F32),
        scratch_shapes=[pltpu.VMEM((tm, tf + d_ff), BF16), pltpu.VMEM((2, tm, tf // 2), F32)],
        compiler_params=_cparams(("arbitrary", "arbitrary")),
        name="ffn",
    )(xm2, w_up, w_up, conv_w, cb, w_down, x1, mod, final_g.reshape(1, d))


def _rope_tables(seq_len):
    rows = seq_len // GRID_W
    row = jnp.repeat(jnp.arange(rows, dtype=F32), GRID_W)
    colp = jnp.tile(jnp.arange(GRID_W, dtype=F32), rows)
    n_freq = HEAD_DIM // 4
    inv = ROPE_THETA ** (-jnp.arange(n_freq, dtype=F32) / n_freq)
    ang_r = row[:, None] * inv[None, :]
    ang_c = colp[:, None] * inv[None, :]
    cos = jnp.concatenate([jnp.cos(ang_r), jnp.cos(ang_r), jnp.cos(ang_c), jnp.cos(ang_c)], axis=1)
    sin = jnp.concatenate([-jnp.sin(ang_r), jnp.sin(ang_r), -jnp.sin(ang_c), jnp.sin(ang_c)], axis=1)
    return cos, sin


PROJ_TM = 1024
MIXER_TM = 256


def kernel(x_prompt, x_sample, cache_k, cache_v, state_ssm_re, state_ssm_im, c, c_ctx, norm_mix_g, norm_ffn_g,
           w_mod, b_mod, w_in, ssm_lambda_re, ssm_lambda_im, ssm_log_dt, ssm_b_re, ssm_b_im, ssm_c_re, ssm_c_im,
           ssm_d, w_glu, attn_sink, w_ssm_o, w_attn_o, w_out, w_up, conv_w, conv_b, w_down, final_norm_g):
    batch, seq, d = x_prompt.shape
    dec_batch, dec_seq, _ = x_sample.shape
    depth = w_in.shape[0]
    assert depth == 1, "final norm is fused into the (single) layer's ffn kernel"
    d_ssm = w_glu.shape[1]
    d_attn = N_HEADS * HEAD_DIM
    d_kv = N_KV_HEADS * HEAD_DIM
    groups = d_ssm // SSM_CH
    assert w_in.shape[2] == d_ssm + d_attn + 2 * d_kv + 2 * d
    assert ssm_lambda_re.shape[2:] == (groups, SSM_STATE) and dec_batch <= 8 - 1
    l = 0

    col = {"gs": 0, "ga": d, "q": 2 * d, "u": 2 * d + d_attn, "k": 2 * d + d_attn + d_ssm,
           "d_ssm": d_ssm, "d_attn": d_attn, "d_kv": d_kv}
    xp = x_prompt.reshape(batch * seq, d)
    xs = x_sample.reshape(dec_batch * dec_seq, d)

    cond = jnp.concatenate([c_ctx[None], c, jnp.zeros((8 - 1 - dec_batch, d), F32)], axis=0)

    ssm_ops, mod, _ = _ssm_operators_and_mod(
        ssm_lambda_re[l], ssm_lambda_im[l], ssm_log_dt[l], ssm_b_re[l], ssm_b_im[l], ssm_c_re[l], ssm_c_im[l],
        ssm_d[l], cond, w_mod[l], b_mod[l])
    mod = mod.reshape(8, 6, d)
    mod_p, mod_s = mod[0:1], mod[1:1 + dec_batch]
    proj_s, tail_s, _, _, w_in_b = _proj(xs, mod_s, norm_mix_g[l], w_in[l], tm=PROJ_TM,
                                         tiles_per_seq=max(dec_seq // PROJ_TM, 1), col=col)
    proj_p, tail_p, k_raw, v_raw = _proj(xp, mod_p, norm_mix_g[l], w_in_b, tm=PROJ_TM,
                                         tiles_per_seq=max(seq // PROJ_TM, 1), col=col)

    def lanes(s):
        return s.transpose(0, 2, 1, 3).reshape(dec_batch, groups, 2 * SSM_STATE)
    h0_p = jnp.zeros((batch, groups * 4 * SSM_STATE), F32)
    h0_s = _slab_lanes(jnp.concatenate([lanes(state_ssm_re[:, l]), lanes(state_ssm_im[:, l])], axis=-1))
    y_ssm_p, h_fin, (w_up_b,) = _ssm(tail_p, 0, ssm_ops, h0_p, n_seq=batch, seq_len=seq,
                                     seq_block=min(batch, SSM_SEQ_BLOCK), riders=(w_up[l],))
    y_ssm_s, _, (w_down_b,) = _ssm(tail_s, 0, ssm_ops, h0_s, n_seq=dec_batch, seq_len=dec_seq,
                                   seq_block=min(dec_batch, SSM_SEQ_BLOCK), riders=(w_down[l],))

    sink = attn_sink[l]
    o_p, mixer_w = _attn_ctx(proj_p, sink, n_seq=batch, seq_len=seq, col=col,
                             riders=(w_glu[l], w_ssm_o[l], w_attn_o[l], w_out[l]))
    o_s = _attn_lat(proj_s, cache_k[:, l].reshape(dec_batch, -1, d_kv), cache_v[:, l].reshape(dec_batch, -1, d_kv),
                    sink, n_seq=dec_batch, seq_len=dec_seq, col=col)

    x1_p, xm2_p = _mixer_out(xp, y_ssm_p, o_p, proj_p, mod_p, norm_ffn_g[l], *mixer_w, tm=MIXER_TM,
                             tiles_per_seq=max(seq // MIXER_TM, 1), col=col)
    x1_s, xm2_s = _mixer_out(xs, y_ssm_s, o_s, proj_s, mod_s, norm_ffn_g[l], *mixer_w, tm=MIXER_TM,
                             tiles_per_seq=max(dec_seq // MIXER_TM, 1), col=col)
    y_p = _ffn(x1_p, xm2_p, mod_p, w_up_b, conv_w[l], conv_b[l], w_down_b, final_norm_g, tm=FFN_TM, seq_len=seq)
    y_s = _ffn(x1_s, xm2_s, mod_s, w_up_b, conv_w[l], conv_b[l], w_down_b, final_norm_g, tm=FFN_TM, seq_len=dec_seq)

    new_k = k_raw.reshape(batch, 1, seq, N_KV_HEADS, HEAD_DIM)
    new_v = v_raw.reshape(batch, 1, seq, N_KV_HEADS, HEAD_DIM)

    def unlanes(hl):
        return hl.reshape(batch, groups, 2, SSM_STATE).transpose(0, 2, 1, 3)[:, None]
    h_fin = _unslab_lanes(h_fin, groups)
    new_re = unlanes(h_fin[:, :, :2 * SSM_STATE])
    new_im = unlanes(h_fin[:, :, 2 * SSM_STATE:])
    return (y_p.reshape(batch, seq, d), y_s.reshape(dec_batch, dec_seq, d), new_k, new_v, new_re, new_im)
```

```python
import functools
import math

import jax
import jax.numpy as jnp
from jax import lax
from jax.experimental import pallas as pl
from jax.experimental.pallas import tpu as pltpu

F32 = jnp.float32
BF16 = jnp.bfloat16

GRID_W = 64
SSM_CH = 16
SSM_STATE = 64
N_HEADS = 8
N_KV_HEADS = 2
HEAD_DIM = 128
Q_PER_KV = N_HEADS // N_KV_HEADS
WINDOW = 128
BLOCK = 128
ROPE_THETA = 10000.0
EPS = 1e-6
NEG_INF = -1e30

SSM_CHUNK = 16
SSM_LANES = SSM_CHUNK * SSM_CH
SSM_SEQ_BLOCK = 16

V7X_VMEM_LIMIT_BYTES = 56 * 1024 * 1024
V7X_MXU_WIDTH = 256


def _cparams(semantics):
    return pltpu.CompilerParams(dimension_semantics=semantics, vmem_limit_bytes=V7X_VMEM_LIMIT_BYTES)


def _sigmoid(x):
    return 1.0 / (1.0 + jnp.exp(-x))


def _ride_casts(body, n_in, n_out, n_riders):
    def kernel(*refs):
        ins = refs[:n_in]
        rider_ins = refs[n_in:n_in + n_riders]
        outs = refs[n_in + n_riders:n_in + n_riders + n_out]
        rider_outs = refs[n_in + n_riders + n_out:n_in + 2 * n_riders + n_out]
        scratch = refs[n_in + 2 * n_riders + n_out:]
        for src, dst in zip(rider_ins, rider_outs):
            dst[...] = src[...].astype(BF16)
        body(*ins, *outs, *scratch)
    return kernel


def _rider_specs(weights, n_steps, step_of):
    specs, shapes = [], []
    for w in weights:
        r, c = w.shape
        assert r % (16 * n_steps) == 0
        specs.append(pl.BlockSpec((r // n_steps, c), lambda *idx: (step_of(*idx), 0)))
        shapes.append(jax.ShapeDtypeStruct((r, c), BF16))
    return specs, shapes


def _rms_modulate(x, g, scale, shift):
    ms = jnp.mean(x * x, axis=-1, keepdims=True)
    return (x * lax.rsqrt(ms + EPS) * g) * (1.0 + scale) + shift


MOD_K_CHUNK = 256


def _mod_kernel(c_ref, w_ref, b_ref, o_ref):
    c = c_ref[...]
    a = (c * _sigmoid(c)).astype(BF16)
    acc = b_ref[...]
    for k in range(0, w_ref.shape[0], MOD_K_CHUNK):
        acc = acc + jnp.dot(a[:, k:k + MOD_K_CHUNK], w_ref[k:k + MOD_K_CHUNK, :].astype(BF16),
                            preferred_element_type=F32)
    o_ref[...] = acc


PROJ_TN = 512
NORM_ROWS = 16
NORM_UNROLL = 16


def _swap32(x):
    n = x.shape[-1]
    lane = lax.broadcasted_iota(jnp.int32, x.shape, x.ndim - 1)
    return jnp.where((lane & 63) < 32, pltpu.roll(x, n - 32, x.ndim - 1), pltpu.roll(x, 32, x.ndim - 1))


def _proj_kernel(x_ref, mod_ref, g_ref, w_ref, o_ref, tail_ref, k_ref, v_ref, *rest, q_tiles):
    xm_scr = rest[-1]
    j = pl.program_id(1)

    @pl.when(j == 0)
    def _():
        gain = g_ref[...] * (1.0 + mod_ref[1:2, :])
        shift = mod_ref[0:1, :]

        def block(c, carry):
            rows = pl.ds(pl.multiple_of(c * NORM_ROWS, NORM_ROWS), NORM_ROWS)
            x = x_ref[rows, :]
            ms = jnp.mean(x * x, axis=-1, keepdims=True)
            xm_scr[rows, :] = (x * lax.rsqrt(ms + EPS) * gain + shift).astype(BF16)
            return carry

        lax.fori_loop(0, x_ref.shape[0] // NORM_ROWS, block, 0, unroll=NORM_UNROLL)

    w = w_ref[...]
    if len(rest) == 2:
        w = w.astype(BF16)
        rest[0][...] = w
    acc = jnp.dot(xm_scr[...], w, preferred_element_type=F32)
    scale = jnp.where((j >= q_tiles[0]) & (j < q_tiles[1]), HEAD_DIM ** -0.5, 1.0)
    o_ref[...] = (acc * scale).astype(BF16)
    tail_ref[...] = acc
    rows = acc.shape[0]
    for h in range(N_KV_HEADS):
        k_ref[pl.ds(h, rows, stride=N_KV_HEADS), :] = acc[:, h * HEAD_DIM:(h + 1) * HEAD_DIM]
        v_ref[pl.ds(h, rows, stride=N_KV_HEADS), :] = acc[:, (N_KV_HEADS + h) * HEAD_DIM:(N_KV_HEADS + h + 1) * HEAD_DIM]


def _proj(x, mod, norm_g, w_in, *, tm, tiles_per_seq, col):
    n, d = x.shape
    cols = w_in.shape[1]
    tn = PROJ_TN
    n_tiles = cols // tn
    d_kv = col["d_kv"]
    n_gate, n_q, n_u = col["q"] // tn, col["d_attn"] // tn, col["d_ssm"] // tn
    q_tiles = (n_gate, n_gate + n_q)
    tail0 = col["u"] // tn
    n_tail = n_tiles - tail0
    assert 2 * d_kv == tn and col["u"] == col["q"] + col["d_attn"] and col["k"] == cols - tn

    def src_tile(j):
        return jnp.where(j < n_gate, j + (n_u + n_q + 1),
                         jnp.where(j < n_gate + n_q, j - n_gate + n_u,
                                   jnp.where(j < n_gate + n_q + n_u, j - (n_gate + n_q), n_u + n_q)))
    seq_of = (lambda i: i // tiles_per_seq) if mod.shape[0] > 1 else (lambda i: 0)
    out_specs = [pl.BlockSpec((tm, tn), lambda i, j: (i, j)),
                 pl.BlockSpec((tm, tn), lambda i, j: (i, jnp.clip(j - tail0, 0, n_tail - 1))),
                 pl.BlockSpec((tm * N_KV_HEADS, HEAD_DIM), lambda i, j: (i, 0)),
                 pl.BlockSpec((tm * N_KV_HEADS, HEAD_DIM), lambda i, j: (i, 0))]
    out_shape = [jax.ShapeDtypeStruct((n, cols), BF16), jax.ShapeDtypeStruct((n, n_tail * tn), F32),
                 jax.ShapeDtypeStruct((n * N_KV_HEADS, HEAD_DIM), F32),
                 jax.ShapeDtypeStruct((n * N_KV_HEADS, HEAD_DIM), F32)]
    if w_in.dtype != BF16:
        out_specs.append(pl.BlockSpec((d, tn), lambda i, j: (0, src_tile(jnp.where(i == 0, j, n_tiles - 1)))))
        out_shape.append(jax.ShapeDtypeStruct((d, cols), BF16))
    return pl.pallas_call(
        functools.partial(_proj_kernel, q_tiles=q_tiles),
        grid=(n // tm, n_tiles),
        in_specs=[pl.BlockSpec((tm, d), lambda i, j: (i, 0)),
                  pl.BlockSpec((None, 6, d), lambda i, j: (seq_of(i), 0, 0)),
                  pl.BlockSpec((1, d), lambda i, j: (0, 0)),
                  pl.BlockSpec((d, tn), lambda i, j: (0, src_tile(j)))],
        out_specs=out_specs,
        out_shape=out_shape,
        scratch_shapes=[pltpu.VMEM((tm, d), BF16)],
        compiler_params=_cparams(("arbitrary", "arbitrary")),
        name="proj",
    )(x, mod, norm_g.reshape(1, d), w_in)


def _cmul(ar, ai, br, bi):
    return ar * br - ai * bi, ar * bi + ai * br


def _cpow(base_r, base_i, n, bits):
    res_r = jnp.ones_like(base_r)
    res_i = jnp.zeros_like(base_r)
    for bit in range(bits):
        nr, ni = _cmul(res_r, res_i, base_r, base_i)
        take = ((n >> bit) & 1) == 1
        res_r = jnp.where(take, nr, res_r)
        res_i = jnp.where(take, ni, res_i)
        if bit + 1 < bits:
            base_r, base_i = _cmul(base_r, base_i, base_r, base_i)
    return res_r, res_i


def _ssm_disc_kernel(lre_ref, lim_ref, ldt_ref, ere_ref, eim_ref, qre_ref, qim_ref):
    lre = lre_ref[...]
    lim = lim_ref[...]
    dt = jnp.exp(ldt_ref[...])
    mag = jnp.exp(lre * dt)
    e_re = mag * jnp.cos(lim * dt)
    e_im = mag * jnp.sin(lim * dt)
    n_re = e_re - 1.0
    den = lre * lre + lim * lim
    ere_ref[...] = e_re
    eim_ref[...] = e_im
    qre_ref[...] = (n_re * lre + e_im * lim) / den
    qim_ref[...] = (e_im * lre - n_re * lim) / den


def _ssm_disc(lam_re, lam_im, log_dt):
    two, g, p = lam_re.shape
    shp = jax.ShapeDtypeStruct((two * g, p), F32)
    outs = pl.pallas_call(_ssm_disc_kernel, out_shape=[shp] * 4, name="ssm_disc")(
        lam_re.reshape(two * g, p), lam_im.reshape(two * g, p), log_dt.reshape(two * g, 1))
    return [o.reshape(two, g, p) for o in outs]


SSM_OPS_GROUPS = 4


def _ssm_ops_mod_kernel(ecol_ref, rows_ref, b_ref, ct_ref, d_ref, c_ref, wm_ref, bm_ref,
                        m_ref, w_ref, v_ref, a_ref, mod_ref):
    _mod_kernel(c_ref, wm_ref, bm_ref, mod_ref)
    _ssm_ops_kernel(ecol_ref, rows_ref, b_ref, ct_ref, d_ref, m_ref, w_ref, v_ref, a_ref)


def _ssm_ops_kernel(ecol_ref, rows_ref, b_ref, ct_ref, d_ref, m_ref, w_ref, v_ref, a_ref):
    def one(gi, carry):
        _ssm_ops_group(ecol_ref.at[gi], rows_ref.at[gi], b_ref.at[gi], ct_ref.at[gi], d_ref.at[gi],
                       m_ref.at[gi], w_ref.at[gi], v_ref.at[gi], a_ref.at[gi])
        return carry

    lax.fori_loop(0, ecol_ref.shape[0], one, 0, unroll=True)


def _ssm_ops_group(ecol_ref, rows_ref, b_ref, ct_ref, d_ref, m_ref, w_ref, v_ref, a_ref):
    L, ch, p = SSM_CHUNK, SSM_CH, SSM_STATE
    lanes = SSM_LANES
    half = 2 * p
    lane_c = lax.broadcasted_iota(jnp.int32, (p, lanes), 1)
    lag = lane_c // ch

    def col(k):
        return jnp.broadcast_to(ecol_ref[k], (p, lanes))

    f0 = _cpow(col(0), col(1), lag, 4)
    b0 = _cpow(col(2), col(3), (L - 1) - lag, 4)
    f1 = _cmul(f0[0], f0[1], col(0), col(1))
    b1 = _cmul(b0[0], b0[1], col(2), col(3))
    def ct(k):
        return jnp.tile(ct_ref[k], (1, L))

    cf = (ct(0), ct(1))
    cb = (ct(2), ct(3))
    yf0 = _cmul(*f0, *cf)
    yb0 = _cmul(*b0, *cb)
    yf1 = _cmul(*f1, *cf)
    yb1 = _cmul(*b1, *cb)
    v_ref[...] = jnp.concatenate([yf1[0], yb1[0], -yf1[1], -yb1[1]], axis=0).astype(BF16)

    lane_r = lax.broadcasted_iota(jnp.int32, (ch, lanes), 1)
    sgn = jnp.where(lane_r < half, -1.0, 1.0)
    b1raw = b_ref[...]
    b2raw = pltpu.roll(b1raw, half, 1) * sgn
    bb1 = rows_ref[2:3, :] * b1raw + rows_ref[3:4, :] * b2raw
    bb2 = pltpu.roll(bb1, half, 1) * sgn

    zero = jnp.zeros((p, lanes), F32)
    hi = lax.Precision.HIGHEST
    r0f = jnp.dot(bb1, jnp.concatenate([yf0[0], zero, -yf0[1], zero], axis=0), precision=hi,
                  preferred_element_type=F32)
    r0b = jnp.dot(bb1, jnp.concatenate([zero, yb0[0], zero, -yb0[1]], axis=0), precision=hi,
                  preferred_element_type=F32)
    row_r = lax.broadcasted_iota(jnp.int32, (ch, lanes), 0)
    d_col = d_ref[...]
    for s in range(L):
        blk = jnp.where(lane_r == ch * s + row_r, d_col, 0.0)
        fwd = r0f if s == 0 else jnp.where(lane_r >= ch * s, pltpu.roll(r0f, ch * s, 1), 0.0)
        k = ch * (L - 1 - s)
        bwd = r0b if k == 0 else jnp.where(lane_r < lanes - k, pltpu.roll(r0b, lanes - k, 1), 0.0)
        m_ref[s * ch:(s + 1) * ch, :] = (blk + fwd + bwd).astype(BF16)

    is_f = (lane_r & (half - 1)) < p
    n_row = jnp.where(is_f, (L - 1) - row_r, row_r)
    er, ei = _cpow(jnp.broadcast_to(rows_ref[0:1, :], (L, lanes)), jnp.broadcast_to(rows_ref[1:2, :], (L, lanes)),
                   n_row, 4)
    for s in range(L):
        w_ref[s * ch:(s + 1) * ch, :] = (er[s:s + 1, :] * bb1 + ei[s:s + 1, :] * bb2).astype(BF16)

    ar, ai = rows_ref[0:1, :], rows_ref[1:2, :]
    for _ in range(4):
        ar, ai = _cmul(ar, ai, ar, ai)
    lane_1 = lax.broadcasted_iota(jnp.int32, (1, lanes), 1)
    a_ref[...] = jnp.where(lane_1 < half, ar, ai)


def _ssm_operators_and_mod(lam_re, lam_im, log_dt, b_re, b_im, c_re, c_im, ssm_d, cond, w_mod, b_mod, riders=()):
    two, g, p = lam_re.shape
    ch = b_re.shape[-1]
    lanes = SSM_LANES
    assert (two, p, ch, SSM_CHUNK * ch) == (2, SSM_STATE, SSM_CH, lanes)
    e_re, e_im, q_re, q_im = _ssm_disc(lam_re, lam_im, log_dt)
    ecol = jnp.stack([e_re[0], e_im[0], e_re[1], e_im[1]], axis=1)[..., None]

    def fbfb(x):
        return jnp.concatenate([x[0], x[1], x[0], x[1]], axis=-1)
    rows = jnp.stack([fbfb(e_re), fbfb(e_im), fbfb(q_re), fbfb(q_im)], axis=1)

    def bt(x):
        return jnp.swapaxes(x, -1, -2)
    b_rows = jnp.concatenate([bt(b_re[0]), bt(b_re[1]), bt(b_im[0]), bt(b_im[1])], axis=-1)

    ct = jnp.stack([bt(c_re[0]), bt(c_im[0]), bt(c_re[1]), bt(c_im[1])], axis=1)
    d_col = ssm_d.reshape(g, ch, 1)
    gb = SSM_OPS_GROUPS
    mat = pl.BlockSpec((gb, lanes, lanes), lambda i: (i, 0, 0))
    steps = g // gb
    rider_specs, rider_shapes = _rider_specs(riders, steps, lambda i: i)
    c_rows, d_model = cond.shape
    n_mod = w_mod.shape[1]
    tn = n_mod // steps
    assert n_mod % steps == 0 and tn % 128 == 0
    outs = pl.pallas_call(
        _ride_casts(_ssm_ops_mod_kernel, 8, 5, len(riders)),
        grid=(steps,),
        in_specs=[pl.BlockSpec((gb, 4, p, 1), lambda i: (i, 0, 0, 0)),
                  pl.BlockSpec((gb, 4, lanes), lambda i: (i, 0, 0)),
                  pl.BlockSpec((gb, ch, lanes), lambda i: (i, 0, 0)),
                  pl.BlockSpec((gb, 4, p, ch), lambda i: (i, 0, 0, 0)),
                  pl.BlockSpec((gb, ch, 1), lambda i: (i, 0, 0)),
                  pl.BlockSpec((c_rows, d_model), lambda i: (0, 0)),
                  pl.BlockSpec((d_model, tn), lambda i: (0, i)),
                  pl.BlockSpec((1, tn), lambda i: (0, i))] + rider_specs,
        out_specs=[mat, mat, mat, pl.BlockSpec((gb, 1, lanes), lambda i: (i, 0, 0)),
                   pl.BlockSpec((c_rows, tn), lambda i: (0, i))] + rider_specs,
        out_shape=([jax.ShapeDtypeStruct((g, lanes, lanes), BF16)] * 3 + [jax.ShapeDtypeStruct((g, 1, lanes), F32),
                                                                          jax.ShapeDtypeStruct((c_rows, n_mod), F32)]
                   + rider_shapes),
        compiler_params=_cparams(("arbitrary",)),
        name="ssm_ops_mod",
    )(ecol, rows, b_rows, ct, d_col, cond, w_mod, b_mod.reshape(1, n_mod), *riders)
    return outs[:4], outs[4], outs[5:]


SSM_SLAB_GROUPS = 128 // SSM_CH


def _chunk_transpose(xs, chunk_id):
    for k in (4, 2, 1):
        keep = (chunk_id & k) == 0
        new = list(xs)
        for i in range(len(xs)):
            if i & k == 0:
                a, b = xs[i], xs[i + k]
                if 2 * SSM_CH * k == 128:
                    moved = pltpu.roll(jnp.where(keep, b, a), SSM_CH * k, 1)
                    new[i] = jnp.where(keep, a, moved)
                    new[i + k] = jnp.where(keep, moved, b)
                else:
                    new[i] = jnp.where(keep, a, pltpu.roll(b, SSM_CH * k, 1))
                    new[i + k] = jnp.where(keep, pltpu.roll(a, 128 - SSM_CH * k, 1), b)
        xs = new
    return xs


def _ssm_kernel(u_ref, m_ref, w_ref, v_ref, a_ref, h0_ref, y_ref, hfin_ref, ug_scr, s_scr, hf_scr, hb_scr, yg_scr,
                *, n_chunks, ns):
    gps = SSM_SLAB_GROUPS
    nc = ns * n_chunks
    half = 2 * SSM_STATE
    chunk_id = lax.broadcasted_iota(jnp.int32, (nc, 128), 1) // SSM_CH
    for hf in range(2):
        xs = [u_ref[pl.ds(hf * gps + i, nc, stride=SSM_CHUNK), :] for i in range(gps)]
        xs = _chunk_transpose(xs, chunk_id)
        for g in range(gps):
            ug_scr[g, :, hf * 128:(hf + 1) * 128] = xs[g].astype(BF16)
    pitch = n_chunks + 8
    for g in range(gps):
        s = jnp.dot(ug_scr[g], w_ref[g], preferred_element_type=F32)
        for q in range(ns):
            s_scr[g, q * pitch:q * pitch + n_chunks, :] = s[q * n_chunks:(q + 1) * n_chunks, :half]
            s_scr[gps + g, q * pitch:q * pitch + n_chunks, :] = s[q * n_chunks:(q + 1) * n_chunks, half:]
    is_fwd = lax.broadcasted_iota(jnp.int32, (1, half), 1) < SSM_STATE
    blocks = range(2 * gps)
    a = [a_ref[:, b * half:(b + 1) * half] for b in blocks]
    h = [h0_ref[:, b * half:(b + 1) * half] for b in blocks]

    def rows(c):
        return pl.ds(c, ns, stride=pitch)

    for k in range(n_chunks):
        kb = n_chunks - 1 - k
        new_h = list(h)
        for b in blocks:
            hf_scr[b, rows(k), :] = h[b]
            hb_scr[b, rows(kb), :] = h[b]
        for g in range(gps):
            s_re = jnp.where(is_fwd, s_scr[g, rows(k), :], s_scr[g, rows(kb), :])
            s_im = jnp.where(is_fwd, s_scr[gps + g, rows(k), :], s_scr[gps + g, rows(kb), :])
            new_h[g] = a[g] * h[g] - a[gps + g] * h[gps + g] + s_re
            new_h[gps + g] = a[g] * h[gps + g] + a[gps + g] * h[g] + s_im
        h = new_h
    for b in blocks:
        hfin_ref[:, b * half:(b + 1) * half] = h[b]

    def entering(b):
        return jnp.concatenate(
            [jnp.where(is_fwd, hf_scr[b, q * pitch:q * pitch + n_chunks, :], hb_scr[b, q * pitch:q * pitch + n_chunks, :])
             for q in range(ns)], axis=0)

    for g in range(gps):
        hin = jnp.concatenate([entering(g), entering(gps + g)], axis=1).astype(BF16)
        yg_scr[g] = (jnp.dot(ug_scr[g], m_ref[g], preferred_element_type=F32)
                     + jnp.dot(hin, v_ref[g], preferred_element_type=F32))
    for hf in range(2):
        xs = [yg_scr[g, :, hf * 128:(hf + 1) * 128] for g in range(gps)]
        xs = _chunk_transpose(xs, chunk_id)
        for i in range(gps):
            y_ref[pl.ds(hf * gps + i, nc, stride=SSM_CHUNK), :] = xs[i]


def _slab_lanes(x):
    lead = x.shape[:-2]
    g = x.shape[-2]
    gps = SSM_SLAB_GROUPS
    x = x.reshape(lead + (g // gps, gps, 2, 2 * SSM_STATE))
    x = jnp.swapaxes(x, -3, -2)
    return x.reshape(lead + (g * 4 * SSM_STATE,))


def _unslab_lanes(x, g):
    lead = x.shape[:-1]
    gps = SSM_SLAB_GROUPS
    x = x.reshape(lead + (g // gps, 2, gps, 2 * SSM_STATE))
    x = jnp.swapaxes(x, -3, -2)
    return x.reshape(lead + (g, 4 * SSM_STATE))


def _ssm(u, u_col, ops, h0, *, n_seq, seq_len, seq_block, riders=()):
    m_op, w_op, v_op, a_op = ops
    n = u.shape[0]
    g = m_op.shape[0]
    d_ssm = g * SSM_CH
    slab0 = u_col // 128
    gps = SSM_SLAB_GROUPS
    lanes = SSM_LANES
    n_chunks = seq_len // SSM_CHUNK
    nc = seq_block * n_chunks
    padded = seq_block * (n_chunks + 8)
    rows = seq_block * seq_len
    wide2 = gps * 4 * SSM_STATE
    mat = pl.BlockSpec((gps, lanes, lanes), lambda j, b: (j, 0, 0))
    n_blocks = n_seq // seq_block
    rider_specs, rider_shapes = _rider_specs(riders, (g // gps) * n_blocks, lambda j, b: j * n_blocks + b)
    outs = pl.pallas_call(
        _ride_casts(functools.partial(_ssm_kernel, n_chunks=n_chunks, ns=seq_block), 6, 2, len(riders)),
        grid=(g // gps, n_blocks),
        in_specs=[pl.BlockSpec((rows, 128), lambda j, b: (b, slab0 + j)), mat, mat, mat,
                  pl.BlockSpec((1, wide2), lambda j, b: (0, j)),
                  pl.BlockSpec((seq_block, wide2), lambda j, b: (b, j))] + rider_specs,
        out_specs=[pl.BlockSpec((rows, 128), lambda j, b: (b, j)),
                   pl.BlockSpec((seq_block, wide2), lambda j, b: (b, j))] + rider_specs,
        out_shape=[jax.ShapeDtypeStruct((n, d_ssm), F32),
                   jax.ShapeDtypeStruct((n_seq, g * 4 * SSM_STATE), F32)] + rider_shapes,
        scratch_shapes=[pltpu.VMEM((gps, nc, lanes), BF16), pltpu.VMEM((2 * gps, padded, 128), F32),
                        pltpu.VMEM((2 * gps, padded, 128), F32), pltpu.VMEM((2 * gps, padded, 128), F32),
                        pltpu.VMEM((gps, nc, lanes), F32)],
        compiler_params=_cparams(("arbitrary", "arbitrary")),
        name="ssm",
    )(u, m_op, w_op, v_op, _slab_lanes(a_op[:, 0])[None], h0, *riders)
    return outs[0], outs[1], outs[2:]


def _softmax_pv(pieces, sink):
    m = sink
    for s, _ in pieces:
        m = jnp.maximum(m, jnp.max(s, axis=-1, keepdims=True))
    out = None
    for s, v in pieces:
        pv = jnp.dot(jnp.exp(s - m).astype(BF16), v, preferred_element_type=F32)
        out = pv if out is None else out + pv
    return out[:, :HEAD_DIM] / (out[:, HEAD_DIM:] + jnp.exp(sink - m))


def _qk(q, k):
    return lax.dot_general(q, k, (((1,), (1,)), ((), ())), preferred_element_type=F32)


def _with_ones(v):
    return jnp.concatenate([v, jnp.ones_like(v)], axis=1)


ATTN_CTX_SEQS = 8


def _attn_ctx_kernel(sink_ref, q_ref, kv_ref, o_ref, *, seq_len):
    d_kv = N_KV_HEADS * HEAD_DIM
    for r0 in range(0, q_ref.shape[0], seq_len):
        rows = slice(r0, r0 + seq_len)
        for kh in range(N_KV_HEADS):
            k = kv_ref[rows, kh * HEAD_DIM:(kh + 1) * HEAD_DIM]
            v = _with_ones(kv_ref[rows, d_kv + kh * HEAD_DIM:d_kv + (kh + 1) * HEAD_DIM])
            for h in range(kh * Q_PER_KV, (kh + 1) * Q_PER_KV):
                q = q_ref[rows, h * HEAD_DIM:(h + 1) * HEAD_DIM]
                o = _softmax_pv([(_qk(q, k), v)], sink_ref[h])
                o_ref[rows, h * HEAD_DIM:(h + 1) * HEAD_DIM] = o.astype(BF16)


def _attn_ctx(proj, sink, *, n_seq, seq_len, col, riders=()):
    d_attn = col["d_attn"]
    kv_w = 2 * col["d_kv"]
    spb = ATTN_CTX_SEQS if n_seq % ATTN_CTX_SEQS == 0 else 1
    rows = spb * seq_len
    rider_specs, rider_shapes = _rider_specs(riders, n_seq // spb, lambda b: b)
    outs = pl.pallas_call(
        _ride_casts(functools.partial(_attn_ctx_kernel, seq_len=seq_len), 3, 1, len(riders)),
        grid=(n_seq // spb,),
        in_specs=[pl.BlockSpec(memory_space=pltpu.SMEM),
                  pl.BlockSpec((rows, d_attn), lambda b: (b, col["q"] // d_attn)),
                  pl.BlockSpec((rows, kv_w), lambda b: (b, col["k"] // kv_w))] + rider_specs,
        out_specs=[pl.BlockSpec((rows, d_attn), lambda b: (b, 0))] + rider_specs,
        out_shape=[jax.ShapeDtypeStruct((n_seq * seq_len, d_attn), BF16)] + rider_shapes,
        compiler_params=_cparams(("arbitrary",)),
        name="attn_ctx",
    )(sink, proj, proj, *riders)
    return outs[0], outs[1:]


def _rope(x, cos, sin):
    reps = x.shape[1] // HEAD_DIM
    xf = x.astype(F32)
    return (xf * jnp.concatenate([cos] * reps, axis=1) + _swap32(xf) * jnp.concatenate([sin] * reps, axis=1)).astype(BF16)


ATTN_LAT_QBLOCKS = 4


def _attn_lat_kernel(*refs, n_blocks, qb):
    sink_ref, q_ref = refs[:2]
    kv_refs = refs[2:4 + qb]
    ck_ref, cv_ref, cos_ref, sin_ref, o_ref = refs[4 + qb:]
    first = pl.program_id(1) * qb
    d_kv = N_KV_HEADS * HEAD_DIM

    def tables(blk):
        rows = pl.ds(pl.multiple_of(blk * BLOCK, BLOCK), BLOCK)
        return cos_ref[rows, :], sin_ref[rows, :]

    k_rot = [_rope(kv_refs[n][:, :d_kv], *tables(jnp.clip(first + n - 1, 0, n_blocks - 1))) for n in range(qb + 2)]
    r = lax.broadcasted_iota(jnp.int32, (BLOCK, 3 * BLOCK), 0)
    c = lax.broadcasted_iota(jnp.int32, (BLOCK, 3 * BLOCK), 1)
    cc = c & (BLOCK - 1)
    for sub in range(qb):
        blk = first + sub
        q_rows = slice(sub * BLOCK, (sub + 1) * BLOCK)
        q_all = _rope(q_ref[q_rows, :], *tables(blk))
        k_all = jnp.concatenate(k_rot[sub:sub + 3], axis=0)
        valid = (((c < BLOCK) & (cc >= r) & (blk > 0)) | ((c >= BLOCK) & (c < 2 * BLOCK))
                 | ((c >= 2 * BLOCK) & (cc <= r) & (blk < n_blocks - 1)))
        for kh in range(N_KV_HEADS):
            ks = slice(kh * HEAD_DIM, (kh + 1) * HEAD_DIM)
            vs = slice(d_kv + kh * HEAD_DIM, d_kv + (kh + 1) * HEAD_DIM)
            k_loc = k_all[:, ks]
            k_ctx = ck_ref[:, ks].astype(BF16)
            v_loc = _with_ones(jnp.concatenate([kv_refs[sub + n][:, vs] for n in range(3)], axis=0))
            v_ctx = _with_ones(cv_ref[:, ks].astype(BF16))
            for h in range(kh * Q_PER_KV, (kh + 1) * Q_PER_KV):
                q = q_all[:, h * HEAD_DIM:(h + 1) * HEAD_DIM]
                s_loc = jnp.where(valid, _qk(q, k_loc), NEG_INF)
                o = _softmax_pv([(s_loc, v_loc), (_qk(q, k_ctx), v_ctx)], sink_ref[h])
                o_ref[q_rows, h * HEAD_DIM:(h + 1) * HEAD_DIM] = o.astype(BF16)


def _attn_lat(proj, cache_k, cache_v, sink, *, n_seq, seq_len, col):
    d_attn = col["d_attn"]
    kv_w = 2 * col["d_kv"]
    assert WINDOW == BLOCK and seq_len % BLOCK == 0
    nb = seq_len // BLOCK
    qb = ATTN_LAT_QBLOCKS if nb % ATTN_LAT_QBLOCKS == 0 else 1
    steps = nb // qb
    kv_col = col["k"] // kv_w
    past, d_kv = cache_k.shape[1], cache_k.shape[2]

    def kv_spec(off):
        return pl.BlockSpec((BLOCK, kv_w), lambda b, i: (b * nb + jnp.clip(i * qb + off, 0, nb - 1), kv_col))
    cache_spec = pl.BlockSpec((None, past, d_kv), lambda b, i: (b, 0, 0))
    table_spec = pl.BlockSpec((seq_len, HEAD_DIM), lambda b, i: (0, 0))
    cos, sin = _rope_tables(seq_len)
    return pl.pallas_call(
        functools.partial(_attn_lat_kernel, n_blocks=nb, qb=qb),
        grid=(n_seq, steps),
        in_specs=[pl.BlockSpec(memory_space=pltpu.SMEM),
                  pl.BlockSpec((qb * BLOCK, d_attn), lambda b, i: (b * steps + i, col["q"] // d_attn))]
                 + [kv_spec(off) for off in range(-1, qb + 1)]
                 + [cache_spec, cache_spec, table_spec, table_spec],
        out_specs=pl.BlockSpec((qb * BLOCK, d_attn), lambda b, i: (b * steps + i, 0)),
        out_shape=jax.ShapeDtypeStruct((n_seq * seq_len, d_attn), BF16),
        compiler_params=_cparams(("arbitrary", "arbitrary")),
        name="attn_lat",
    )(sink, proj, *([proj] * (qb + 2)), cache_k, cache_v, cos, sin)


def _gelu_tanh(x):
    return 0.5 * x * (1.0 + jnp.tanh(math.sqrt(2.0 / math.pi) * (x + 0.044715 * (x * x * x))))


def _mixer_out_kernel(x_ref, y_ref, o_ref, gs_ref, ga_ref, mod_ref, g_ref, wglu_ref, wso_ref, wao_ref, wout_ref,
                      x1_ref, xm2_ref):
    z = _gelu_tanh(y_ref[...].astype(F32))
    z = z * _sigmoid(jnp.dot(z.astype(BF16), wglu_ref[...], preferred_element_type=F32))
    s_br = jnp.dot(z.astype(BF16), wso_ref[...], preferred_element_type=F32)
    a_br = jnp.dot(o_ref[...], wao_ref[...], preferred_element_type=F32)
    merged = _sigmoid(gs_ref[...].astype(F32)) * s_br + _sigmoid(ga_ref[...].astype(F32)) * a_br
    out = jnp.dot(merged.astype(BF16), wout_ref[...], preferred_element_type=F32)
    x1 = x_ref[...] + mod_ref[2:3, :] * out
    x1_ref[...] = x1
    xm2_ref[...] = _rms_modulate(x1, g_ref[...], mod_ref[4:5, :], mod_ref[3:4, :]).astype(BF16)


def _mixer_out(x, y, o, proj, mod, norm_g, w_glu, w_ssm_o, w_attn_o, w_out, *, tm, tiles_per_seq, col):
    n, d = x.shape
    d_ssm = y.shape[1]
    d_attn = o.shape[1]
    seq_of = (lambda i: i // tiles_per_seq) if mod.shape[0] > 1 else (lambda i: 0)

    def resident(shape):
        return pl.BlockSpec(shape, lambda i: (0, 0), pipeline_mode=pl.Buffered(1))
    return pl.pallas_call(
        _mixer_out_kernel,
        grid=(n // tm,),
        in_specs=[pl.BlockSpec((tm, d), lambda i: (i, 0)),
                  pl.BlockSpec((tm, d_ssm), lambda i: (i, 0)),
                  pl.BlockSpec((tm, d_attn), lambda i: (i, 0)),
                  pl.BlockSpec((tm, d), lambda i: (i, col["gs"] // d)),
                  pl.BlockSpec((tm, d), lambda i: (i, col["ga"] // d)),
                  pl.BlockSpec((None, 6, d), lambda i: (seq_of(i), 0, 0)),
                  pl.BlockSpec((1, d), lambda i: (0, 0)),
                  resident(w_glu.shape), resident(w_ssm_o.shape), resident(w_attn_o.shape), resident(w_out.shape)],
        out_specs=[pl.BlockSpec((tm, d), lambda i: (i, 0)), pl.BlockSpec((tm, d), lambda i: (i, 0))],
        out_shape=[jax.ShapeDtypeStruct((n, d), F32), jax.ShapeDtypeStruct((n, d), BF16)],
        compiler_params=_cparams(("arbitrary",)),
        name="mixer_out",
    )(x, y, o, proj, proj, mod, norm_g.reshape(1, d), w_glu, w_ssm_o, w_attn_o, w_out)


FFN_TM = 1024
FFN_TF = 512
FFN_TN = 256


def _ffn_kernel(xm_ref, wa_ref, wb_ref, cw_ref, cb_ref, wd_ref, x1_ref, mod_ref, g_ref, o_ref, act_scr, raw_scr,
                *, seq_len, nf, nn, d_ff):
    j = pl.program_id(1)
    tm = xm_ref.shape[0]
    tf = wa_ref.shape[1]
    tn = wd_ref.shape[1]
    ch = tf // 2
    pos = lax.broadcasted_iota(jnp.int32, (tm, 1), 0) & (seq_len - 1)
    has_prev = pos != 0
    has_next = pos != seq_len - 1

    def lanes(off):
        return slice(off, off + ch) if isinstance(off, int) else pl.ds(pl.multiple_of(off, ch), ch)

    def gate(ha, hb, col0):
        def conv(h, off):
            cols = lanes(off)
            cw = cw_ref[:, cols]
            h_prev = jnp.where(has_prev, pltpu.roll(h, 1, 0), 0.0)
            h_next = jnp.where(has_next, pltpu.roll(h, tm - 1, 0), 0.0)
            return cw[0:1] * h_prev + cw[1:2] * h + cw[2:3] * h_next + cb_ref[:, cols]

        a = conv(ha, col0)
        b = conv(hb, d_ff + col0)
        return ((a * _sigmoid(a)) * b).astype(BF16)

    def act_cols(col0):
        return lanes(tf + col0)

    @pl.when(j == 0)
    def _():
        raw_scr[...] = jnp.zeros_like(raw_scr)

    @pl.when(j < nf)
    def _():
        xm = xm_ref[...]
        ha0 = jnp.dot(xm, wa_ref[:, :ch], preferred_element_type=F32)
        hb0 = jnp.dot(xm, wb_ref[:, :ch], preferred_element_type=F32)
        carried = (j - 1) * tf + ch
        act_scr[:, act_cols(carried)] = gate(raw_scr[0], raw_scr[1], jnp.maximum(carried, 0))
        ha1 = jnp.dot(xm, wa_ref[:, ch:], preferred_element_type=F32)
        hb1 = jnp.dot(xm, wb_ref[:, ch:], preferred_element_type=F32)
        act_scr[:, act_cols(j * tf)] = gate(ha0, hb0, j * tf)
        raw_scr[0] = ha1
        raw_scr[1] = hb1

    @pl.when(j == nf)
    def _():
        last = (nf - 1) * tf + ch
        act_scr[:, act_cols(last)] = gate(raw_scr[0], raw_scr[1], last)

    @pl.when(j >= nf)
    def _():
        cols = pl.ds(pl.multiple_of((j - nf) * tn, tn), tn)
        ffn = jnp.dot(act_scr[:, tf:], wd_ref[...], preferred_element_type=F32)
        o_ref[:, cols] = x1_ref[...] + mod_ref[5:6, cols] * ffn

    @pl.when(j == nf + nn - 1)
    def _():
        x2 = o_ref[...]
        ms = jnp.mean(x2 * x2, axis=-1, keepdims=True)
        o_ref[...] = x2 * lax.rsqrt(ms + EPS) * g_ref[...]


def _ffn(x1, xm2, mod, w_up, conv_w, conv_b, w_down, final_g, *, tm, seq_len):
    n, d = x1.shape
    d_ff = w_down.shape[0]
    tf, tn = FFN_TF, FFN_TN
    nf, nn = d_ff // tf, d // tn
    assert tm % seq_len == 0 and seq_len & (seq_len - 1) == 0 and d_ff % tf == 0 and (tf // 2) % V7X_MXU_WIDTH == 0
    seqs_per_tile = tm // seq_len
    seq_of = (lambda i: i * seqs_per_tile) if mod.shape[0] > 1 else (lambda i: 0)
    assert mod.shape[0] == 1 or seqs_per_tile == 1
    cb = conv_b.reshape(1, 2 * d_ff)

    def up(j):
        return jnp.minimum(j, nf - 1)

    def down(j):
        return jnp.maximum(j - nf, 0)
    return pl.pallas_call(
        functools.partial(_ffn_kernel, seq_len=seq_len, nf=nf, nn=nn, d_ff=d_ff),
        grid=(n // tm, nf + nn),
        in_specs=[pl.BlockSpec((tm, d), lambda i, j: (i, 0), pipeline_mode=pl.Buffered(1)),
                  pl.BlockSpec((d, tf), lambda i, j: (0, up(j))),
                  pl.BlockSpec((d, tf), lambda i, j: (0, nf + up(j))),
                  pl.BlockSpec((3, 2 * d_ff), lambda i, j: (0, 0)),
                  pl.BlockSpec((1, 2 * d_ff), lambda i, j: (0, 0)),
                  pl.BlockSpec((d_ff, tn), lambda i, j: (0, down(j))),
                  pl.BlockSpec((tm, tn), lambda i, j: (i, down(j))),
                  pl.BlockSpec((None, 6, d), lambda i, j: (seq_of(i), 0, 0)),
                  pl.BlockSpec((1, d), lambda i, j: (0, 0))],
        out_specs=pl.BlockSpec((tm, d), lambda i, j: (i, 0)),
        out_shape=jax.ShapeDtypeStruct((n, d), F32),
        scratch_shapes=[pltpu.VMEM((tm, tf + d_ff), BF16), pltpu.VMEM((2, tm, tf // 2), F32)],
        compiler_params=_cparams(("arbitrary", "arbitrary")),
        name="ffn",
    )(xm2, w_up, w_up, conv_w, cb, w_down, x1, mod, final_g.reshape(1, d))


def _rope_tables(seq_len):
    rows = seq_len // GRID_W
    row = jnp.repeat(jnp.arange(rows, dtype=F32), GRID_W)
    colp = jnp.tile(jnp.arange(GRID_W, dtype=F32), rows)
    n_freq = HEAD_DIM // 4
    inv = ROPE_THETA ** (-jnp.arange(n_freq, dtype=F32) / n_freq)
    ang_r = row[:, None] * inv[None, :]
    ang_c = colp[:, None] * inv[None, :]
    cos = jnp.concatenate([jnp.cos(ang_r), jnp.cos(ang_r), jnp.cos(ang_c), jnp.cos(ang_c)], axis=1)
    sin = jnp.concatenate([-jnp.sin(ang_r), jnp.sin(ang_r), -jnp.sin(ang_c), jnp.sin(ang_c)], axis=1)
    return cos, sin


PROJ_TM = 1024
MIXER_TM = 256


def kernel(x_prompt, x_sample, cache_k, cache_v, state_ssm_re, state_ssm_im, c, c_ctx, norm_mix_g, norm_ffn_g,
           w_mod, b_mod, w_in, ssm_lambda_re, ssm_lambda_im, ssm_log_dt, ssm_b_re, ssm_b_im, ssm_c_re, ssm_c_im,
           ssm_d, w_glu, attn_sink, w_ssm_o, w_attn_o, w_out, w_up, conv_w, conv_b, w_down, final_norm_g):
    batch, seq, d = x_prompt.shape
    dec_batch, dec_seq, _ = x_sample.shape
    depth = w_in.shape[0]
    assert depth == 1, "final norm is fused into the (single) layer's ffn kernel"
    d_ssm = w_glu.shape[1]
    d_attn = N_HEADS * HEAD_DIM
    d_kv = N_KV_HEADS * HEAD_DIM
    groups = d_ssm // SSM_CH
    assert w_in.shape[2] == d_ssm + d_attn + 2 * d_kv + 2 * d
    assert ssm_lambda_re.shape[2:] == (groups, SSM_STATE) and dec_batch <= 8 - 1
    l = 0

    col = {"gs": 0, "ga": d, "q": 2 * d, "u": 2 * d + d_attn, "k": 2 * d + d_attn + d_ssm,
           "d_ssm": d_ssm, "d_attn": d_attn, "d_kv": d_kv}
    xp = x_prompt.reshape(batch * seq, d)
    xs = x_sample.reshape(dec_batch * dec_seq, d)

    cond = jnp.concatenate([c_ctx[None], c, jnp.zeros((8 - 1 - dec_batch, d), F32)], axis=0)

    ssm_ops, mod, _ = _ssm_operators_and_mod(
        ssm_lambda_re[l], ssm_lambda_im[l], ssm_log_dt[l], ssm_b_re[l], ssm_b_im[l], ssm_c_re[l], ssm_c_im[l],
        ssm_d[l], cond, w_mod[l], b_mod[l])
    mod = mod.reshape(8, 6, d)
    mod_p, mod_s = mod[0:1], mod[1:1 + dec_batch]
    proj_s, tail_s, _, _, w_in_b = _proj(xs, mod_s, norm_mix_g[l], w_in[l], tm=PROJ_TM,
                                         tiles_per_seq=max(dec_seq // PROJ_TM, 1), col=col)
    proj_p, tail_p, k_raw, v_raw = _proj(xp, mod_p, norm_mix_g[l], w_in_b, tm=PROJ_TM,
                                         tiles_per_seq=max(seq // PROJ_TM, 1), col=col)

    def lanes(s):
        return s.transpose(0, 2, 1, 3).reshape(dec_batch, groups, 2 * SSM_STATE)
    h0_p = jnp.zeros((batch, groups * 4 * SSM_STATE), F32)
    h0_s = _slab_lanes(jnp.concatenate([lanes(state_ssm_re[:, l]), lanes(state_ssm_im[:, l])], axis=-1))
    y_ssm_p, h_fin, (w_up_b,) = _ssm(tail_p, 0, ssm_ops, h0_p, n_seq=batch, seq_len=seq,
                                     seq_block=min(batch, SSM_SEQ_BLOCK), riders=(w_up[l],))
    y_ssm_s, _, (w_down_b,) = _ssm(tail_s, 0, ssm_ops, h0_s, n_seq=dec_batch, seq_len=dec_seq,
                                   seq_block=min(dec_batch, SSM_SEQ_BLOCK), riders=(w_down[l],))

    sink = attn_sink[l]
    o_p, mixer_w = _attn_ctx(proj_p, sink, n_seq=batch, seq_len=seq, col=col,
                             riders=(w_glu[l], w_ssm_o[l], w_attn_o[l], w_out[l]))
    o_s = _attn_lat(proj_s, cache_k[:, l].reshape(dec_batch, -1, d_kv), cache_v[:, l].reshape(dec_batch, -1, d_kv),
                    sink, n_seq=dec_batch, seq_len=dec_seq, col=col)

    x1_p, xm2_p = _mixer_out(xp, y_ssm_p, o_p, proj_p, mod_p, norm_ffn_g[l], *mixer_w, tm=MIXER_TM,
                             tiles_per_seq=max(seq // MIXER_TM, 1), col=col)
    x1_s, xm2_s = _mixer_out(xs, y_ssm_s, o_s, proj_s, mod_s, norm_ffn_g[l], *mixer_w, tm=MIXER_TM,
                             tiles_per_seq=max(dec_seq // MIXER_TM, 1), col=col)
    y_p = _ffn(x1_p, xm2_p, mod_p, w_up_b, conv_w[l], conv_b[l], w_down_b, final_norm_g, tm=FFN_TM, seq_len=seq)
    y_s = _ffn(x1_s, xm2_s, mod_s, w_up_b, conv_w[l], conv_b[l], w_down_b, final_norm_g, tm=FFN_TM, seq_len=dec_seq)

    new_k = k_raw.reshape(batch, 1, seq, N_KV_HEADS, HEAD_DIM)
    new_v = v_raw.reshape(batch, 1, seq, N_KV_HEADS, HEAD_DIM)

    def unlanes(hl):
        return hl.reshape(batch, groups, 2, SSM_STATE).transpose(0, 2, 1, 3)[:, None]
    h_fin = _unslab_lanes(h_fin, groups)
    new_re = unlanes(h_fin[:, :, :2 * SSM_STATE])
    new_im = unlanes(h_fin[:, :, 2 * SSM_STATE:])
    return (y_p.reshape(batch, seq, d), y_s.reshape(dec_batch, dec_seq, d), new_k, new_v, new_re, new_im)
```

```python
import functools
import math

import jax
import jax.numpy as jnp
import numpy as np
from jax import lax
from jax.experimental import pallas as pl
from jax.experimental.pallas import tpu as pltpu

F32 = jnp.float32
BF16 = jnp.bfloat16

GRID_W = 64
SSM_CH = 16
SSM_STATE = 64
N_HEADS = 8
N_KV_HEADS = 2
HEAD_DIM = 128
Q_PER_KV = N_HEADS // N_KV_HEADS
WINDOW = 128
BLOCK = 128
ROPE_THETA = 10000.0
EPS = 1e-6
NEG_INF = -1e30

SSM_CHUNK = 16
SSM_LANES = SSM_CHUNK * SSM_CH
SSM_SEQ_BLOCK = 16

V7X_VMEM_LIMIT_BYTES = 56 * 1024 * 1024
V7X_MXU_WIDTH = 256


def _cparams(semantics):
    return pltpu.CompilerParams(dimension_semantics=semantics, vmem_limit_bytes=V7X_VMEM_LIMIT_BYTES)


def _sigmoid(x):
    return 1.0 / (1.0 + jnp.exp(-x))


def _ride_casts(body, n_in, n_out, n_riders):
    def kernel(*refs):
        ins = refs[:n_in]
        rider_ins = refs[n_in:n_in + n_riders]
        outs = refs[n_in + n_riders:n_in + n_riders + n_out]
        rider_outs = refs[n_in + n_riders + n_out:n_in + 2 * n_riders + n_out]
        scratch = refs[n_in + 2 * n_riders + n_out:]
        for src, dst in zip(rider_ins, rider_outs):
            dst[...] = src[...].astype(BF16)
        body(*ins, *outs, *scratch)
    return kernel


def _rider_specs(weights, n_steps, step_of):
    specs, shapes = [], []
    for w in weights:
        r, c = w.shape
        assert r % (16 * n_steps) == 0
        specs.append(pl.BlockSpec((r // n_steps, c), lambda *idx: (step_of(*idx), 0)))
        shapes.append(jax.ShapeDtypeStruct((r, c), BF16))
    return specs, shapes


def _rms_modulate(x, g, scale, shift):
    ms = jnp.mean(x * x, axis=-1, keepdims=True)
    return (x * lax.rsqrt(ms + EPS) * g) * (1.0 + scale) + shift


MOD_K_CHUNK = 256


def _mod_kernel(c_ref, w_ref, b_ref, o_ref):
    c = c_ref[...]
    a = (c * _sigmoid(c)).astype(BF16)
    acc = b_ref[...]
    for k in range(0, w_ref.shape[0], MOD_K_CHUNK):
        acc = acc + jnp.dot(a[:, k:k + MOD_K_CHUNK], w_ref[k:k + MOD_K_CHUNK, :].astype(BF16),
                            preferred_element_type=F32)
    o_ref[...] = acc


PROJ_TN = 512
NORM_ROWS = 16
NORM_UNROLL = 16


def _swap32(x):
    n = x.shape[-1]
    lane = lax.broadcasted_iota(jnp.int32, x.shape, x.ndim - 1)
    return jnp.where((lane & 63) < 32, pltpu.roll(x, n - 32, x.ndim - 1), pltpu.roll(x, 32, x.ndim - 1))


def _proj_kernel(x_ref, mod_ref, g_ref, w_ref, o_ref, tail_ref, k_ref, v_ref, *rest, q_tiles):
    xm_scr = rest[-1]
    j = pl.program_id(1)

    @pl.when(j == 0)
    def _():
        gain = g_ref[...] * (1.0 + mod_ref[1:2, :])
        shift = mod_ref[0:1, :]

        def block(c, carry):
            rows = pl.ds(pl.multiple_of(c * NORM_ROWS, NORM_ROWS), NORM_ROWS)
            x = x_ref[rows, :]
            ms = jnp.mean(x * x, axis=-1, keepdims=True)
            xm_scr[rows, :] = (x * lax.rsqrt(ms + EPS) * gain + shift).astype(BF16)
            return carry

        lax.fori_loop(0, x_ref.shape[0] // NORM_ROWS, block, 0, unroll=NORM_UNROLL)

    w = w_ref[...]
    if len(rest) == 2:
        w = w.astype(BF16)
        rest[0][...] = w
    acc = jnp.dot(xm_scr[...], w, preferred_element_type=F32)
    scale = jnp.where((j >= q_tiles[0]) & (j < q_tiles[1]), HEAD_DIM ** -0.5, 1.0)
    o_ref[...] = (acc * scale).astype(BF16)
    tail_ref[...] = acc
    rows = acc.shape[0]
    for h in range(N_KV_HEADS):
        k_ref[pl.ds(h, rows, stride=N_KV_HEADS), :] = acc[:, h * HEAD_DIM:(h + 1) * HEAD_DIM]
        v_ref[pl.ds(h, rows, stride=N_KV_HEADS), :] = acc[:, (N_KV_HEADS + h) * HEAD_DIM:(N_KV_HEADS + h + 1) * HEAD_DIM]


def _proj(x, mod, norm_g, w_in, *, tm, tiles_per_seq, col):
    n, d = x.shape
    cols = w_in.shape[1]
    tn = PROJ_TN
    n_tiles = cols // tn
    d_kv = col["d_kv"]
    n_gate, n_q, n_u = col["q"] // tn, col["d_attn"] // tn, col["d_ssm"] // tn
    q_tiles = (n_gate, n_gate + n_q)
    tail0 = col["u"] // tn
    n_tail = n_tiles - tail0
    assert 2 * d_kv == tn and col["u"] == col["q"] + col["d_attn"] and col["k"] == cols - tn

    def src_tile(j):
        return jnp.where(j < n_gate, j + (n_u + n_q + 1),
                         jnp.where(j < n_gate + n_q, j - n_gate + n_u,
                                   jnp.where(j < n_gate + n_q + n_u, j - (n_gate + n_q), n_u + n_q)))
    seq_of = (lambda i: i // tiles_per_seq) if mod.shape[0] > 1 else (lambda i: 0)
    out_specs = [pl.BlockSpec((tm, tn), lambda i, j: (i, j)),
                 pl.BlockSpec((tm, tn), lambda i, j: (i, jnp.clip(j - tail0, 0, n_tail - 1))),
                 pl.BlockSpec((tm * N_KV_HEADS, HEAD_DIM), lambda i, j: (i, 0)),
                 pl.BlockSpec((tm * N_KV_HEADS, HEAD_DIM), lambda i, j: (i, 0))]
    out_shape = [jax.ShapeDtypeStruct((n, cols), BF16), jax.ShapeDtypeStruct((n, n_tail * tn), F32),
                 jax.ShapeDtypeStruct((n * N_KV_HEADS, HEAD_DIM), F32),
                 jax.ShapeDtypeStruct((n * N_KV_HEADS, HEAD_DIM), F32)]
    if w_in.dtype != BF16:
        out_specs.append(pl.BlockSpec((d, tn), lambda i, j: (0, src_tile(jnp.where(i == 0, j, n_tiles - 1)))))
        out_shape.append(jax.ShapeDtypeStruct((d, cols), BF16))
    return pl.pallas_call(
        functools.partial(_proj_kernel, q_tiles=q_tiles),
        grid=(n // tm, n_tiles),
        in_specs=[pl.BlockSpec((tm, d), lambda i, j: (i, 0)),
                  pl.BlockSpec((None, 6, d), lambda i, j: (seq_of(i), 0, 0)),
                  pl.BlockSpec((1, d), lambda i, j: (0, 0)),
                  pl.BlockSpec((d, tn), lambda i, j: (0, src_tile(j)))],
        out_specs=out_specs,
        out_shape=out_shape,
        scratch_shapes=[pltpu.VMEM((tm, d), BF16)],
        compiler_params=_cparams(("arbitrary", "arbitrary")),
        name="proj",
    )(x, mod, norm_g.reshape(1, d), w_in)


def _cmul(ar, ai, br, bi):
    return ar * br - ai * bi, ar * bi + ai * br


def _cpow(base_r, base_i, n, bits):
    res_r = jnp.ones_like(base_r)
    res_i = jnp.zeros_like(base_r)
    for bit in range(bits):
        nr, ni = _cmul(res_r, res_i, base_r, base_i)
        take = ((n >> bit) & 1) == 1
        res_r = jnp.where(take, nr, res_r)
        res_i = jnp.where(take, ni, res_i)
        if bit + 1 < bits:
            base_r, base_i = _cmul(base_r, base_i, base_r, base_i)
    return res_r, res_i


def _ssm_disc_kernel(lre_ref, lim_ref, ldt_ref, ere_ref, eim_ref, qre_ref, qim_ref):
    lre = lre_ref[...]
    lim = lim_ref[...]
    dt = jnp.exp(ldt_ref[...])
    mag = jnp.exp(lre * dt)
    e_re = mag * jnp.cos(lim * dt)
    e_im = mag * jnp.sin(lim * dt)
    n_re = e_re - 1.0
    den = lre * lre + lim * lim
    ere_ref[...] = e_re
    eim_ref[...] = e_im
    qre_ref[...] = (n_re * lre + e_im * lim) / den
    qim_ref[...] = (e_im * lre - n_re * lim) / den


def _ssm_disc(lam_re, lam_im, log_dt):
    two, g, p = lam_re.shape
    shp = jax.ShapeDtypeStruct((two * g, p), F32)
    outs = pl.pallas_call(_ssm_disc_kernel, out_shape=[shp] * 4, name="ssm_disc")(
        lam_re.reshape(two * g, p), lam_im.reshape(two * g, p), log_dt.reshape(two * g, 1))
    return [o.reshape(two, g, p) for o in outs]


SSM_OPS_GROUPS = 4


def _ssm_ops_mod_kernel(ecol_ref, rows_ref, b_ref, ct_ref, d_ref, c_ref, wm_ref, bm_ref,
                        m_ref, w_ref, v_ref, a_ref, mod_ref):
    _mod_kernel(c_ref, wm_ref, bm_ref, mod_ref)
    _ssm_ops_kernel(ecol_ref, rows_ref, b_ref, ct_ref, d_ref, m_ref, w_ref, v_ref, a_ref)


def _ssm_ops_kernel(ecol_ref, rows_ref, b_ref, ct_ref, d_ref, m_ref, w_ref, v_ref, a_ref):
    def one(gi, carry):
        _ssm_ops_group(ecol_ref.at[gi], rows_ref.at[gi], b_ref.at[gi], ct_ref.at[gi], d_ref.at[gi],
                       m_ref.at[gi], w_ref.at[gi], v_ref.at[gi], a_ref.at[gi])
        return carry

    lax.fori_loop(0, ecol_ref.shape[0], one, 0, unroll=True)


def _ssm_ops_group(ecol_ref, rows_ref, b_ref, ct_ref, d_ref, m_ref, w_ref, v_ref, a_ref):
    L, ch, p = SSM_CHUNK, SSM_CH, SSM_STATE
    lanes = SSM_LANES
    half = 2 * p
    lane_c = lax.broadcasted_iota(jnp.int32, (p, lanes), 1)
    lag = lane_c // ch

    def col(k):
        return jnp.broadcast_to(ecol_ref[:, k:k + 1], (p, lanes))

    f0 = _cpow(col(0), col(1), lag, 4)
    b0 = _cpow(col(2), col(3), (L - 1) - lag, 4)
    f1 = _cmul(f0[0], f0[1], col(0), col(1))
    b1 = _cmul(b0[0], b0[1], col(2), col(3))
    def ct(k):
        return jnp.tile(ct_ref[:, k * ch:(k + 1) * ch], (1, L))

    cf = (ct(0), ct(1))
    cb = (ct(2), ct(3))
    yf0 = _cmul(*f0, *cf)
    yb0 = _cmul(*b0, *cb)
    yf1 = _cmul(*f1, *cf)
    yb1 = _cmul(*b1, *cb)
    v_ref[...] = jnp.concatenate([yf1[0], yb1[0], -yf1[1], -yb1[1]], axis=0).astype(BF16)

    lane_r = lax.broadcasted_iota(jnp.int32, (ch, lanes), 1)
    sgn = jnp.where(lane_r < half, -1.0, 1.0)
    b1raw = b_ref[...]
    b2raw = pltpu.roll(b1raw, half, 1) * sgn
    bb1 = rows_ref[2:3, :] * b1raw + rows_ref[3:4, :] * b2raw
    bb2 = pltpu.roll(bb1, half, 1) * sgn

    zero = jnp.zeros((p, lanes), F32)
    hi = lax.Precision.HIGHEST
    r0f = jnp.dot(bb1, jnp.concatenate([yf0[0], zero, -yf0[1], zero], axis=0), precision=hi,
                  preferred_element_type=F32)
    r0b = jnp.dot(bb1, jnp.concatenate([zero, yb0[0], zero, -yb0[1]], axis=0), precision=hi,
                  preferred_element_type=F32)
    row_r = lax.broadcasted_iota(jnp.int32, (ch, lanes), 0)
    d_col = d_ref[...]
    for s in range(L):
        blk = jnp.where(lane_r == ch * s + row_r, d_col, 0.0)
        fwd = r0f if s == 0 else jnp.where(lane_r >= ch * s, pltpu.roll(r0f, ch * s, 1), 0.0)
        k = ch * (L - 1 - s)
        bwd = r0b if k == 0 else jnp.where(lane_r < lanes - k, pltpu.roll(r0b, lanes - k, 1), 0.0)
        m_ref[s * ch:(s + 1) * ch, :] = (blk + fwd + bwd).astype(BF16)

    is_f = (lane_r & (half - 1)) < p
    n_row = jnp.where(is_f, (L - 1) - row_r, row_r)
    er, ei = _cpow(jnp.broadcast_to(rows_ref[0:1, :], (L, lanes)), jnp.broadcast_to(rows_ref[1:2, :], (L, lanes)),
                   n_row, 4)
    for s in range(L):
        w_ref[s * ch:(s + 1) * ch, :] = (er[s:s + 1, :] * bb1 + ei[s:s + 1, :] * bb2).astype(BF16)

    ar, ai = rows_ref[0:1, :], rows_ref[1:2, :]
    for _ in range(4):
        ar, ai = _cmul(ar, ai, ar, ai)
    lane_1 = lax.broadcasted_iota(jnp.int32, (1, lanes), 1)
    a_ref[...] = jnp.where(lane_1 < half, ar, ai)


def _ssm_operators_and_mod(lam_re, lam_im, log_dt, b_re, b_im, c_re, c_im, ssm_d, cond, w_mod, b_mod, riders=()):
    two, g, p = lam_re.shape
    ch = b_re.shape[-1]
    lanes = SSM_LANES
    assert (two, p, ch, SSM_CHUNK * ch) == (2, SSM_STATE, SSM_CH, lanes)
    e_re, e_im, q_re, q_im = _ssm_disc(lam_re, lam_im, log_dt)
    ecol = jnp.stack([e_re[0], e_im[0], e_re[1], e_im[1]], axis=-1)

    def fbfb(x):
        return jnp.concatenate([x[0], x[1], x[0], x[1]], axis=-1)
    rows = jnp.stack([fbfb(e_re), fbfb(e_im), fbfb(q_re), fbfb(q_im)], axis=1)

    def bt(x):
        return jnp.swapaxes(x, -1, -2)
    b_rows = jnp.concatenate([bt(b_re[0]), bt(b_re[1]), bt(b_im[0]), bt(b_im[1])], axis=-1)

    ct = jnp.concatenate([bt(c_re[0]), bt(c_im[0]), bt(c_re[1]), bt(c_im[1])], axis=-1)
    d_col = ssm_d.reshape(g, ch, 1)
    gb = SSM_OPS_GROUPS
    mat = pl.BlockSpec((gb, lanes, lanes), lambda i: (i, 0, 0))
    steps = g // gb
    rider_specs, rider_shapes = _rider_specs(riders, steps, lambda i: i)
    c_rows, d_model = cond.shape
    n_mod = w_mod.shape[1]
    tn = n_mod // steps
    assert n_mod % steps == 0 and tn % 128 == 0
    outs = pl.pallas_call(
        _ride_casts(_ssm_ops_mod_kernel, 8, 5, len(riders)),
        grid=(steps,),
        in_specs=[pl.BlockSpec((gb, p, 4), lambda i: (i, 0, 0)),
                  pl.BlockSpec((gb, 4, lanes), lambda i: (i, 0, 0)),
                  pl.BlockSpec((gb, ch, lanes), lambda i: (i, 0, 0)),
                  pl.BlockSpec((gb, p, 4 * ch), lambda i: (i, 0, 0)),
                  pl.BlockSpec((gb, ch, 1), lambda i: (i, 0, 0)),
                  pl.BlockSpec((c_rows, d_model), lambda i: (0, 0)),
                  pl.BlockSpec((d_model, tn), lambda i: (0, i)),
                  pl.BlockSpec((1, tn), lambda i: (0, i))] + rider_specs,
        out_specs=[mat, mat, mat, pl.BlockSpec((gb, 1, lanes), lambda i: (i, 0, 0)),
                   pl.BlockSpec((c_rows, tn), lambda i: (0, i))] + rider_specs,
        out_shape=([jax.ShapeDtypeStruct((g, lanes, lanes), BF16)] * 3 + [jax.ShapeDtypeStruct((g, 1, lanes), F32),
                                                                          jax.ShapeDtypeStruct((c_rows, n_mod), F32)]
                   + rider_shapes),
        compiler_params=_cparams(("arbitrary",)),
        name="ssm_ops_mod",
    )(ecol, rows, b_rows, ct, d_col, cond, w_mod, b_mod.reshape(1, n_mod), *riders)
    return outs[:4], outs[4], outs[5:]


SSM_SLAB_GROUPS = 128 // SSM_CH


def _chunk_transpose(xs, chunk_id):
    for k in (4, 2, 1):
        keep = (chunk_id & k) == 0
        new = list(xs)
        for i in range(len(xs)):
            if i & k == 0:
                a, b = xs[i], xs[i + k]
                if 2 * SSM_CH * k == 128:
                    moved = pltpu.roll(jnp.where(keep, b, a), SSM_CH * k, 1)
                    new[i] = jnp.where(keep, a, moved)
                    new[i + k] = jnp.where(keep, moved, b)
                else:
                    new[i] = jnp.where(keep, a, pltpu.roll(b, SSM_CH * k, 1))
                    new[i + k] = jnp.where(keep, pltpu.roll(a, 128 - SSM_CH * k, 1), b)
        xs = new
    return xs


def _ssm_kernel(u_ref, m_ref, w_ref, v_ref, a_ref, h0_ref, y_ref, hfin_ref, ug_scr, s_scr, hf_scr, hb_scr, yg_scr,
                *, n_chunks, ns):
    gps = SSM_SLAB_GROUPS
    nc = ns * n_chunks
    half = 2 * SSM_STATE
    chunk_id = lax.broadcasted_iota(jnp.int32, (nc, 128), 1) // SSM_CH
    for hf in range(2):
        xs = [u_ref[pl.ds(hf * gps + i, nc, stride=SSM_CHUNK), :] for i in range(gps)]
        xs = _chunk_transpose(xs, chunk_id)
        for g in range(gps):
            ug_scr[g, :, hf * 128:(hf + 1) * 128] = xs[g].astype(BF16)
    pitch = n_chunks + 8
    for g in range(gps):
        s = jnp.dot(ug_scr[g], w_ref[g], preferred_element_type=F32)
        for q in range(ns):
            s_scr[g, q * pitch:q * pitch + n_chunks, :] = s[q * n_chunks:(q + 1) * n_chunks, :half]
            s_scr[gps + g, q * pitch:q * pitch + n_chunks, :] = s[q * n_chunks:(q + 1) * n_chunks, half:]
    is_fwd = lax.broadcasted_iota(jnp.int32, (1, half), 1) < SSM_STATE
    blocks = range(2 * gps)
    a = [a_ref[:, b * half:(b + 1) * half] for b in blocks]
    h = [h0_ref[:, b * half:(b + 1) * half] for b in blocks]

    def rows(c):
        return pl.ds(c, ns, stride=pitch)

    for k in range(n_chunks):
        kb = n_chunks - 1 - k
        new_h = list(h)
        for b in blocks:
            hf_scr[b, rows(k), :] = h[b]
            hb_scr[b, rows(kb), :] = h[b]
        for g in range(gps):
            s_re = jnp.where(is_fwd, s_scr[g, rows(k), :], s_scr[g, rows(kb), :])
            s_im = jnp.where(is_fwd, s_scr[gps + g, rows(k), :], s_scr[gps + g, rows(kb), :])
            new_h[g] = a[g] * h[g] - a[gps + g] * h[gps + g] + s_re
            new_h[gps + g] = a[g] * h[gps + g] + a[gps + g] * h[g] + s_im
        h = new_h
    for b in blocks:
        hfin_ref[:, b * half:(b + 1) * half] = h[b]

    def entering(b):
        return jnp.concatenate(
            [jnp.where(is_fwd, hf_scr[b, q * pitch:q * pitch + n_chunks, :], hb_scr[b, q * pitch:q * pitch + n_chunks, :])
             for q in range(ns)], axis=0)

    for g in range(gps):
        hin = jnp.concatenate([entering(g), entering(gps + g)], axis=1).astype(BF16)
        yg_scr[g] = (jnp.dot(ug_scr[g], m_ref[g], preferred_element_type=F32)
                     + jnp.dot(hin, v_ref[g], preferred_element_type=F32))
    for hf in range(2):
        xs = [yg_scr[g, :, hf * 128:(hf + 1) * 128] for g in range(gps)]
        xs = _chunk_transpose(xs, chunk_id)
        for i in range(gps):
            y_ref[pl.ds(hf * gps + i, nc, stride=SSM_CHUNK), :] = xs[i]


def _slab_lanes(x):
    lead = x.shape[:-2]
    g = x.shape[-2]
    gps = SSM_SLAB_GROUPS
    x = x.reshape(lead + (g // gps, gps, 2, 2 * SSM_STATE))
    x = jnp.swapaxes(x, -3, -2)
    return x.reshape(lead + (g * 4 * SSM_STATE,))


def _unslab_lanes(x, g):
    lead = x.shape[:-1]
    gps = SSM_SLAB_GROUPS
    x = x.reshape(lead + (g // gps, 2, gps, 2 * SSM_STATE))
    x = jnp.swapaxes(x, -3, -2)
    return x.reshape(lead + (g, 4 * SSM_STATE))


def _ssm(u, u_col, ops, h0, *, n_seq, seq_len, seq_block, riders=()):
    m_op, w_op, v_op, a_op = ops
    n = u.shape[0]
    g = m_op.shape[0]
    d_ssm = g * SSM_CH
    slab0 = u_col // 128
    gps = SSM_SLAB_GROUPS
    lanes = SSM_LANES
    n_chunks = seq_len // SSM_CHUNK
    nc = seq_block * n_chunks
    padded = seq_block * (n_chunks + 8)
    rows = seq_block * seq_len
    wide2 = gps * 4 * SSM_STATE
    mat = pl.BlockSpec((gps, lanes, lanes), lambda j, b: (j, 0, 0))
    n_blocks = n_seq // seq_block
    rider_specs, rider_shapes = _rider_specs(riders, (g // gps) * n_blocks, lambda j, b: j * n_blocks + b)
    outs = pl.pallas_call(
        _ride_casts(functools.partial(_ssm_kernel, n_chunks=n_chunks, ns=seq_block), 6, 2, len(riders)),
        grid=(g // gps, n_blocks),
        in_specs=[pl.BlockSpec((rows, 128), lambda j, b: (b, slab0 + j)), mat, mat, mat,
                  pl.BlockSpec((1, wide2), lambda j, b: (0, j)),
                  pl.BlockSpec((seq_block, wide2), lambda j, b: (b, j))] + rider_specs,
        out_specs=[pl.BlockSpec((rows, 128), lambda j, b: (b, j)),
                   pl.BlockSpec((seq_block, wide2), lambda j, b: (b, j))] + rider_specs,
        out_shape=[jax.ShapeDtypeStruct((n, d_ssm), F32),
                   jax.ShapeDtypeStruct((n_seq, g * 4 * SSM_STATE), F32)] + rider_shapes,
        scratch_shapes=[pltpu.VMEM((gps, nc, lanes), BF16), pltpu.VMEM((2 * gps, padded, 128), F32),
                        pltpu.VMEM((2 * gps, padded, 128), F32), pltpu.VMEM((2 * gps, padded, 128), F32),
                        pltpu.VMEM((gps, nc, lanes), F32)],
        compiler_params=_cparams(("arbitrary", "arbitrary")),
        name="ssm",
    )(u, m_op, w_op, v_op, _slab_lanes(a_op[:, 0])[None], h0, *riders)
    return outs[0], outs[1], outs[2:]


def _softmax_pv(pieces, sink):
    m = sink
    for s, _ in pieces:
        m = jnp.maximum(m, jnp.max(s, axis=-1, keepdims=True))
    out = None
    for s, v in pieces:
        pv = jnp.dot(jnp.exp(s - m).astype(BF16), v, preferred_element_type=F32)
        out = pv if out is None else out + pv
    return out[:, :HEAD_DIM] / (out[:, HEAD_DIM:] + jnp.exp(sink - m))


def _qk(q, k):
    return lax.dot_general(q, k, (((1,), (1,)), ((), ())), preferred_element_type=F32)


def _with_ones(v):
    return jnp.concatenate([v, jnp.ones_like(v)], axis=1)


ATTN_CTX_SEQS = 8


def _attn_ctx_kernel(sink_ref, q_ref, kv_ref, o_ref, *, seq_len):
    d_kv = N_KV_HEADS * HEAD_DIM
    for r0 in range(0, q_ref.shape[0], seq_len):
        rows = slice(r0, r0 + seq_len)
        for kh in range(N_KV_HEADS):
            k = kv_ref[rows, kh * HEAD_DIM:(kh + 1) * HEAD_DIM]
            v = _with_ones(kv_ref[rows, d_kv + kh * HEAD_DIM:d_kv + (kh + 1) * HEAD_DIM])
            for h in range(kh * Q_PER_KV, (kh + 1) * Q_PER_KV):
                q = q_ref[rows, h * HEAD_DIM:(h + 1) * HEAD_DIM]
                o = _softmax_pv([(_qk(q, k), v)], sink_ref[h])
                o_ref[rows, h * HEAD_DIM:(h + 1) * HEAD_DIM] = o.astype(BF16)


def _attn_ctx(proj, sink, *, n_seq, seq_len, col, riders=()):
    d_attn = col["d_attn"]
    kv_w = 2 * col["d_kv"]
    spb = ATTN_CTX_SEQS if n_seq % ATTN_CTX_SEQS == 0 else 1
    rows = spb * seq_len
    rider_specs, rider_shapes = _rider_specs(riders, n_seq // spb, lambda b: b)
    outs = pl.pallas_call(
        _ride_casts(functools.partial(_attn_ctx_kernel, seq_len=seq_len), 3, 1, len(riders)),
        grid=(n_seq // spb,),
        in_specs=[pl.BlockSpec(memory_space=pltpu.SMEM),
                  pl.BlockSpec((rows, d_attn), lambda b: (b, col["q"] // d_attn)),
                  pl.BlockSpec((rows, kv_w), lambda b: (b, col["k"] // kv_w))] + rider_specs,
        out_specs=[pl.BlockSpec((rows, d_attn), lambda b: (b, 0))] + rider_specs,
        out_shape=[jax.ShapeDtypeStruct((n_seq * seq_len, d_attn), BF16)] + rider_shapes,
        compiler_params=_cparams(("arbitrary",)),
        name="attn_ctx",
    )(sink, proj, proj, *riders)
    return outs[0], outs[1:]


def _rope(x, cos, sin):
    reps = x.shape[1] // HEAD_DIM
    xf = x.astype(F32)
    return (xf * jnp.concatenate([cos] * reps, axis=1) + _swap32(xf) * jnp.concatenate([sin] * reps, axis=1)).astype(BF16)


ATTN_LAT_QBLOCKS = 4


def _attn_lat_kernel(*refs, n_blocks, qb):
    sink_ref, q_ref = refs[:2]
    kv_refs = refs[2:4 + qb]
    ck_ref, cv_ref, cos_ref, sin_ref, o_ref = refs[4 + qb:]
    first = pl.program_id(1) * qb
    d_kv = N_KV_HEADS * HEAD_DIM

    def tables(blk):
        rows = pl.ds(pl.multiple_of(blk * BLOCK, BLOCK), BLOCK)
        return cos_ref[rows, :], sin_ref[rows, :]

    k_rot = [_rope(kv_refs[n][:, :d_kv], *tables(jnp.clip(first + n - 1, 0, n_blocks - 1))) for n in range(qb + 2)]
    r = lax.broadcasted_iota(jnp.int32, (BLOCK, 3 * BLOCK), 0)
    c = lax.broadcasted_iota(jnp.int32, (BLOCK, 3 * BLOCK), 1)
    cc = c & (BLOCK - 1)
    for sub in range(qb):
        blk = first + sub
        q_rows = slice(sub * BLOCK, (sub + 1) * BLOCK)
        q_all = _rope(q_ref[q_rows, :], *tables(blk))
        k_all = jnp.concatenate(k_rot[sub:sub + 3], axis=0)
        valid = (((c < BLOCK) & (cc >= r) & (blk > 0)) | ((c >= BLOCK) & (c < 2 * BLOCK))
                 | ((c >= 2 * BLOCK) & (cc <= r) & (blk < n_blocks - 1)))
        for kh in range(N_KV_HEADS):
            ks = slice(kh * HEAD_DIM, (kh + 1) * HEAD_DIM)
            vs = slice(d_kv + kh * HEAD_DIM, d_kv + (kh + 1) * HEAD_DIM)
            k_loc = k_all[:, ks]
            k_ctx = ck_ref[:, ks].astype(BF16)
            v_loc = _with_ones(jnp.concatenate([kv_refs[sub + n][:, vs] for n in range(3)], axis=0))
            v_ctx = _with_ones(cv_ref[:, ks].astype(BF16))
            for h in range(kh * Q_PER_KV, (kh + 1) * Q_PER_KV):
                q = q_all[:, h * HEAD_DIM:(h + 1) * HEAD_DIM]
                s_loc = jnp.where(valid, _qk(q, k_loc), NEG_INF)
                o = _softmax_pv([(s_loc, v_loc), (_qk(q, k_ctx), v_ctx)], sink_ref[h])
                o_ref[q_rows, h * HEAD_DIM:(h + 1) * HEAD_DIM] = o.astype(BF16)


def _attn_lat(proj, cache_k, cache_v, sink, *, n_seq, seq_len, col):
    d_attn = col["d_attn"]
    kv_w = 2 * col["d_kv"]
    assert WINDOW == BLOCK and seq_len % BLOCK == 0
    nb = seq_len // BLOCK
    qb = ATTN_LAT_QBLOCKS if nb % ATTN_LAT_QBLOCKS == 0 else 1
    steps = nb // qb
    kv_col = col["k"] // kv_w
    past, d_kv = cache_k.shape[1], cache_k.shape[2]

    def kv_spec(off):
        return pl.BlockSpec((BLOCK, kv_w), lambda b, i: (b * nb + jnp.clip(i * qb + off, 0, nb - 1), kv_col))
    cache_spec = pl.BlockSpec((None, past, d_kv), lambda b, i: (b, 0, 0))
    table_spec = pl.BlockSpec((seq_len, HEAD_DIM), lambda b, i: (0, 0))
    cos, sin = _rope_tables(seq_len)
    return pl.pallas_call(
        functools.partial(_attn_lat_kernel, n_blocks=nb, qb=qb),
        grid=(n_seq, steps),
        in_specs=[pl.BlockSpec(memory_space=pltpu.SMEM),
                  pl.BlockSpec((qb * BLOCK, d_attn), lambda b, i: (b * steps + i, col["q"] // d_attn))]
                 + [kv_spec(off) for off in range(-1, qb + 1)]
                 + [cache_spec, cache_spec, table_spec, table_spec],
        out_specs=pl.BlockSpec((qb * BLOCK, d_attn), lambda b, i: (b * steps + i, 0)),
        out_shape=jax.ShapeDtypeStruct((n_seq * seq_len, d_attn), BF16),
        compiler_params=_cparams(("arbitrary", "arbitrary")),
        name="attn_lat",
    )(sink, proj, *([proj] * (qb + 2)), cache_k, cache_v, cos, sin)


def _gelu_tanh(x):
    return 0.5 * x * (1.0 + jnp.tanh(math.sqrt(2.0 / math.pi) * (x + 0.044715 * (x * x * x))))


def _mixer_out_kernel(x_ref, y_ref, o_ref, gs_ref, ga_ref, mod_ref, g_ref, wglu_ref, wso_ref, wao_ref, wout_ref,
                      x1_ref, xm2_ref):
    z = _gelu_tanh(y_ref[...].astype(F32))
    z = z * _sigmoid(jnp.dot(z.astype(BF16), wglu_ref[...], preferred_element_type=F32))
    s_br = jnp.dot(z.astype(BF16), wso_ref[...], preferred_element_type=F32)
    a_br = jnp.dot(o_ref[...], wao_ref[...], preferred_element_type=F32)
    merged = _sigmoid(gs_ref[...].astype(F32)) * s_br + _sigmoid(ga_ref[...].astype(F32)) * a_br
    out = jnp.dot(merged.astype(BF16), wout_ref[...], preferred_element_type=F32)
    x1 = x_ref[...] + mod_ref[2:3, :] * out
    x1_ref[...] = x1
    xm2_ref[...] = _rms_modulate(x1, g_ref[...], mod_ref[4:5, :], mod_ref[3:4, :]).astype(BF16)


def _mixer_out(x, y, o, proj, mod, norm_g, w_glu, w_ssm_o, w_attn_o, w_out, *, tm, tiles_per_seq, col):
    n, d = x.shape
    d_ssm = y.shape[1]
    d_attn = o.shape[1]
    seq_of = (lambda i: i // tiles_per_seq) if mod.shape[0] > 1 else (lambda i: 0)

    def resident(shape):
        return pl.BlockSpec(shape, lambda i: (0, 0), pipeline_mode=pl.Buffered(1))
    return pl.pallas_call(
        _mixer_out_kernel,
        grid=(n // tm,),
        in_specs=[pl.BlockSpec((tm, d), lambda i: (i, 0)),
                  pl.BlockSpec((tm, d_ssm), lambda i: (i, 0)),
                  pl.BlockSpec((tm, d_attn), lambda i: (i, 0)),
                  pl.BlockSpec((tm, d), lambda i: (i, col["gs"] // d)),
                  pl.BlockSpec((tm, d), lambda i: (i, col["ga"] // d)),
                  pl.BlockSpec((None, 6, d), lambda i: (seq_of(i), 0, 0)),
                  pl.BlockSpec((1, d), lambda i: (0, 0)),
                  resident(w_glu.shape), resident(w_ssm_o.shape), resident(w_attn_o.shape), resident(w_out.shape)],
        out_specs=[pl.BlockSpec((tm, d), lambda i: (i, 0)), pl.BlockSpec((tm, d), lambda i: (i, 0))],
        out_shape=[jax.ShapeDtypeStruct((n, d), F32), jax.ShapeDtypeStruct((n, d), BF16)],
        compiler_params=_cparams(("arbitrary",)),
        name="mixer_out",
    )(x, y, o, proj, proj, mod, norm_g.reshape(1, d), w_glu, w_ssm_o, w_attn_o, w_out)


FFN_TM = 1024
FFN_TF = 512
FFN_TN = 256


def _ffn_kernel(xm_ref, wa_ref, wb_ref, cw_ref, cb_ref, wd_ref, x1_ref, mod_ref, g_ref, o_ref, act_scr, raw_scr,
                *, seq_len, nf, nn, d_ff):
    j = pl.program_id(1)
    tm = xm_ref.shape[0]
    tf = wa_ref.shape[1]
    tn = wd_ref.shape[1]
    ch = tf // 2
    pos = lax.broadcasted_iota(jnp.int32, (tm, 1), 0) & (seq_len - 1)
    has_prev = pos != 0
    has_next = pos != seq_len - 1

    def lanes(off):
        return slice(off, off + ch) if isinstance(off, int) else pl.ds(pl.multiple_of(off, ch), ch)

    def gate(ha, hb, col0):
        def conv(h, off):
            cols = lanes(off)
            cw = cw_ref[:, cols]
            h_prev = jnp.where(has_prev, pltpu.roll(h, 1, 0), 0.0)
            h_next = jnp.where(has_next, pltpu.roll(h, tm - 1, 0), 0.0)
            return cw[0:1] * h_prev + cw[1:2] * h + cw[2:3] * h_next + cb_ref[:, cols]

        a = conv(ha, col0)
        b = conv(hb, d_ff + col0)
        return ((a * _sigmoid(a)) * b).astype(BF16)

    def act_cols(col0):
        return lanes(tf + col0)

    @pl.when(j == 0)
    def _():
        raw_scr[...] = jnp.zeros_like(raw_scr)

    @pl.when(j < nf)
    def _():
        xm = xm_ref[...]
        ha0 = jnp.dot(xm, wa_ref[:, :ch], preferred_element_type=F32)
        hb0 = jnp.dot(xm, wb_ref[:, :ch], preferred_element_type=F32)
        carried = (j - 1) * tf + ch
        act_scr[:, act_cols(carried)] = gate(raw_scr[0], raw_scr[1], jnp.maximum(carried, 0))
        ha1 = jnp.dot(xm, wa_ref[:, ch:], preferred_element_type=F32)
        hb1 = jnp.dot(xm, wb_ref[:, ch:], preferred_element_type=F32)
        act_scr[:, act_cols(j * tf)] = gate(ha0, hb0, j * tf)
        raw_scr[0] = ha1
        raw_scr[1] = hb1

    @pl.when(j == nf)
    def _():
        last = (nf - 1) * tf + ch
        act_scr[:, act_cols(last)] = gate(raw_scr[0], raw_scr[1], last)

    @pl.when(j >= nf)
    def _():
        cols = pl.ds(pl.multiple_of((j - nf) * tn, tn), tn)
        ffn = jnp.dot(act_scr[:, tf:], wd_ref[...], preferred_element_type=F32)
        o_ref[:, cols] = x1_ref[...] + mod_ref[5:6, cols] * ffn

    @pl.when(j == nf + nn - 1)
    def _():
        x2 = o_ref[...]
        ms = jnp.mean(x2 * x2, axis=-1, keepdims=True)
        o_ref[...] = x2 * lax.rsqrt(ms + EPS) * g_ref[...]


def _ffn(x1, xm2, mod, w_up, conv_w, conv_b, w_down, final_g, *, tm, seq_len):
    n, d = x1.shape
    d_ff = w_down.shape[0]
    tf, tn = FFN_TF, FFN_TN
    nf, nn = d_ff // tf, d // tn
    assert tm % seq_len == 0 and seq_len & (seq_len - 1) == 0 and d_ff % tf == 0 and (tf // 2) % V7X_MXU_WIDTH == 0
    seqs_per_tile = tm // seq_len
    seq_of = (lambda i: i * seqs_per_tile) if mod.shape[0] > 1 else (lambda i: 0)
    assert mod.shape[0] == 1 or seqs_per_tile == 1
    cb = conv_b.reshape(1, 2 * d_ff)

    def up(j):
        return jnp.minimum(j, nf - 1)

    def down(j):
        return jnp.maximum(j - nf, 0)
    return pl.pallas_call(
        functools.partial(_ffn_kernel, seq_len=seq_len, nf=nf, nn=nn, d_ff=d_ff),
        grid=(n // tm, nf + nn),
        in_specs=[pl.BlockSpec((tm, d), lambda i, j: (i, 0), pipeline_mode=pl.Buffered(1)),
                  pl.BlockSpec((d, tf), lambda i, j: (0, up(j))),
                  pl.BlockSpec((d, tf), lambda i, j: (0, nf + up(j))),
                  pl.BlockSpec((3, 2 * d_ff), lambda i, j: (0, 0)),
                  pl.BlockSpec((1, 2 * d_ff), lambda i, j: (0, 0)),
                  pl.BlockSpec((d_ff, tn), lambda i, j: (0, down(j))),
                  pl.BlockSpec((tm, tn), lambda i, j: (i, down(j))),
                  pl.BlockSpec((None, 6, d), lambda i, j: (seq_of(i), 0, 0)),
                  pl.BlockSpec((1, d), lambda i, j: (0, 0))],
        out_specs=pl.BlockSpec((tm, d), lambda i, j: (i, 0)),
        out_shape=jax.ShapeDtypeStruct((n, d), F32),
        scratch_shapes=[pltpu.VMEM((tm, tf + d_ff), BF16), pltpu.VMEM((2, tm, tf // 2), F32)],
        compiler_params=_cparams(("arbitrary", "arbitrary")),
        name="ffn",
    )(xm2, w_up, w_up, conv_w, cb, w_down, x1, mod, final_g.reshape(1, d))


def _rope_tables(seq_len):
    rows = seq_len // GRID_W
    row = np.repeat(np.arange(rows, dtype=np.float64), GRID_W)
    colp = np.tile(np.arange(GRID_W, dtype=np.float64), rows)
    n_freq = HEAD_DIM // 4
    inv = ROPE_THETA ** (-np.arange(n_freq, dtype=np.float64) / n_freq)
    ang_r = row[:, None] * inv[None, :]
    ang_c = colp[:, None] * inv[None, :]
    cos = np.concatenate([np.cos(ang_r), np.cos(ang_r), np.cos(ang_c), np.cos(ang_c)], axis=1)
    sin = np.concatenate([-np.sin(ang_r), np.sin(ang_r), -np.sin(ang_c), np.sin(ang_c)], axis=1)
    return jnp.asarray(cos, F32), jnp.asarray(sin, F32)


PROJ_TM = 1024
MIXER_TM = 256


def kernel(x_prompt, x_sample, cache_k, cache_v, state_ssm_re, state_ssm_im, c, c_ctx, norm_mix_g, norm_ffn_g,
           w_mod, b_mod, w_in, ssm_lambda_re, ssm_lambda_im, ssm_log_dt, ssm_b_re, ssm_b_im, ssm_c_re, ssm_c_im,
           ssm_d, w_glu, attn_sink, w_ssm_o, w_attn_o, w_out, w_up, conv_w, conv_b, w_down, final_norm_g):
    batch, seq, d = x_prompt.shape
    dec_batch, dec_seq, _ = x_sample.shape
    depth = w_in.shape[0]
    assert depth == 1, "final norm is fused into the (single) layer's ffn kernel"
    d_ssm = w_glu.shape[1]
    d_attn = N_HEADS * HEAD_DIM
    d_kv = N_KV_HEADS * HEAD_DIM
    groups = d_ssm // SSM_CH
    assert w_in.shape[2] == d_ssm + d_attn + 2 * d_kv + 2 * d
    assert ssm_lambda_re.shape[2:] == (groups, SSM_STATE) and dec_batch <= 8 - 1
    l = 0

    col = {"gs": 0, "ga": d, "q": 2 * d, "u": 2 * d + d_attn, "k": 2 * d + d_attn + d_ssm,
           "d_ssm": d_ssm, "d_attn": d_attn, "d_kv": d_kv}
    xp = x_prompt.reshape(batch * seq, d)
    xs = x_sample.reshape(dec_batch * dec_seq, d)

    cond = jnp.concatenate([c_ctx[None], c, jnp.zeros((8 - 1 - dec_batch, d), F32)], axis=0)

    ssm_ops, mod, _ = _ssm_operators_and_mod(
        ssm_lambda_re[l], ssm_lambda_im[l], ssm_log_dt[l], ssm_b_re[l], ssm_b_im[l], ssm_c_re[l], ssm_c_im[l],
        ssm_d[l], cond, w_mod[l], b_mod[l])
    mod = mod.reshape(8, 6, d)
    mod_p, mod_s = mod[0:1], mod[1:1 + dec_batch]
    proj_s, tail_s, _, _, w_in_b = _proj(xs, mod_s, norm_mix_g[l], w_in[l], tm=PROJ_TM,
                                         tiles_per_seq=max(dec_seq // PROJ_TM, 1), col=col)
    proj_p, tail_p, k_raw, v_raw = _proj(xp, mod_p, norm_mix_g[l], w_in_b, tm=PROJ_TM,
                                         tiles_per_seq=max(seq // PROJ_TM, 1), col=col)

    def lanes(s):
        return s.transpose(0, 2, 1, 3).reshape(dec_batch, groups, 2 * SSM_STATE)
    h0_p = jnp.zeros((batch, groups * 4 * SSM_STATE), F32)
    h0_s = _slab_lanes(jnp.concatenate([lanes(state_ssm_re[:, l]), lanes(state_ssm_im[:, l])], axis=-1))
    y_ssm_p, h_fin, (w_up_b,) = _ssm(tail_p, 0, ssm_ops, h0_p, n_seq=batch, seq_len=seq,
                                     seq_block=min(batch, SSM_SEQ_BLOCK), riders=(w_up[l],))
    y_ssm_s, _, (w_down_b,) = _ssm(tail_s, 0, ssm_ops, h0_s, n_seq=dec_batch, seq_len=dec_seq,
                                   seq_block=min(dec_batch, SSM_SEQ_BLOCK), riders=(w_down[l],))

    sink = attn_sink[l]
    o_p, mixer_w = _attn_ctx(proj_p, sink, n_seq=batch, seq_len=seq, col=col,
                             riders=(w_glu[l], w_ssm_o[l], w_attn_o[l], w_out[l]))
    o_s = _attn_lat(proj_s, cache_k[:, l].reshape(dec_batch, -1, d_kv), cache_v[:, l].reshape(dec_batch, -1, d_kv),
                    sink, n_seq=dec_batch, seq_len=dec_seq, col=col)

    x1_p, xm2_p = _mixer_out(xp, y_ssm_p, o_p, proj_p, mod_p, norm_ffn_g[l], *mixer_w, tm=MIXER_TM,
                             tiles_per_seq=max(seq // MIXER_TM, 1), col=col)
    x1_s, xm2_s = _mixer_out(xs, y_ssm_s, o_s, proj_s, mod_s, norm_ffn_g[l], *mixer_w, tm=MIXER_TM,
                             tiles_per_seq=max(dec_seq // MIXER_TM, 1), col=col)
    y_p = _ffn(x1_p, xm2_p, mod_p, w_up_b, conv_w[l], conv_b[l], w_down_b, final_norm_g, tm=FFN_TM, seq_len=seq)
    y_s = _ffn(x1_s, xm2_s, mod_s, w_up_b, conv_w[l], conv_b[l], w_down_b, final_norm_g, tm=FFN_TM, seq_len=dec_seq)

    new_k = k_raw.reshape(batch, 1, seq, N_KV_HEADS, HEAD_DIM)
    new_v = v_raw.reshape(batch, 1, seq, N_KV_HEADS, HEAD_DIM)

    def unlanes(hl):
        return hl.reshape(batch, groups, 2, SSM_STATE).transpose(0, 2, 1, 3)[:, None]
    h_fin = _unslab_lanes(h_fin, groups)
    new_re = unlanes(h_fin[:, :, :2 * SSM_STATE])
    new_im = unlanes(h_fin[:, :, 2 * SSM_STATE:])
    return (y_p.reshape(batch, seq, d), y_s.reshape(dec_batch, dec_seq, d), new_k, new_v, new_re, new_im)
```

```python
import functools
import math

import jax
import jax.numpy as jnp
import numpy as np
from jax import lax
from jax.experimental import pallas as pl
from jax.experimental.pallas import tpu as pltpu

F32 = jnp.float32
BF16 = jnp.bfloat16

GRID_W = 64
SSM_CH = 16
SSM_STATE = 64
N_HEADS = 8
N_KV_HEADS = 2
HEAD_DIM = 128
Q_PER_KV = N_HEADS // N_KV_HEADS
WINDOW = 128
BLOCK = 128
ROPE_THETA = 10000.0
EPS = 1e-6
NEG_INF = -1e30

SSM_CHUNK = 16
SSM_LANES = SSM_CHUNK * SSM_CH
SSM_SEQ_BLOCK = 16

V7X_VMEM_LIMIT_BYTES = 56 * 1024 * 1024
V7X_MXU_WIDTH = 256


def _cparams(semantics):
    return pltpu.CompilerParams(dimension_semantics=semantics, vmem_limit_bytes=V7X_VMEM_LIMIT_BYTES)


def _sigmoid(x):
    return 1.0 / (1.0 + jnp.exp(-x))


def _ride_casts(body, n_in, n_out, n_riders):
    def kernel(*refs):
        ins = refs[:n_in]
        rider_ins = refs[n_in:n_in + n_riders]
        outs = refs[n_in + n_riders:n_in + n_riders + n_out]
        rider_outs = refs[n_in + n_riders + n_out:n_in + 2 * n_riders + n_out]
        scratch = refs[n_in + 2 * n_riders + n_out:]
        for src, dst in zip(rider_ins, rider_outs):
            dst[...] = src[...].astype(BF16)
        body(*ins, *outs, *scratch)
    return kernel


def _rider_specs(weights, n_steps, step_of):
    specs, shapes = [], []
    for w in weights:
        r, c = w.shape
        assert r % (16 * n_steps) == 0
        specs.append(pl.BlockSpec((r // n_steps, c), lambda *idx: (step_of(*idx), 0)))
        shapes.append(jax.ShapeDtypeStruct((r, c), BF16))
    return specs, shapes


def _rms_modulate(x, g, scale, shift):
    ms = jnp.mean(x * x, axis=-1, keepdims=True)
    return (x * lax.rsqrt(ms + EPS) * g) * (1.0 + scale) + shift


MOD_K_CHUNK = 256


def _mod_kernel(c_ref, w_ref, b_ref, o_ref):
    c = c_ref[...]
    a = (c * _sigmoid(c)).astype(BF16)
    acc = b_ref[...]
    for k in range(0, w_ref.shape[0], MOD_K_CHUNK):
        acc = acc + jnp.dot(a[:, k:k + MOD_K_CHUNK], w_ref[k:k + MOD_K_CHUNK, :].astype(BF16),
                            preferred_element_type=F32)
    o_ref[...] = acc


PROJ_TN = 512
NORM_ROWS = 16
NORM_UNROLL = 16


def _swap32(x):
    n = x.shape[-1]
    lane = lax.broadcasted_iota(jnp.int32, x.shape, x.ndim - 1)
    return jnp.where((lane & 63) < 32, pltpu.roll(x, n - 32, x.ndim - 1), pltpu.roll(x, 32, x.ndim - 1))


def _proj_kernel(x_ref, mod_ref, g_ref, w_ref, o_ref, tail_ref, k_ref, v_ref, *rest, q_tiles):
    xm_scr = rest[-1]
    j = pl.program_id(1)

    @pl.when(j == 0)
    def _():
        gain = g_ref[...] * (1.0 + mod_ref[1:2, :])
        shift = mod_ref[0:1, :]

        def block(c, carry):
            rows = pl.ds(pl.multiple_of(c * NORM_ROWS, NORM_ROWS), NORM_ROWS)
            x = x_ref[rows, :]
            ms = jnp.mean(x * x, axis=-1, keepdims=True)
            xm_scr[rows, :] = (x * lax.rsqrt(ms + EPS) * gain + shift).astype(BF16)
            return carry

        lax.fori_loop(0, x_ref.shape[0] // NORM_ROWS, block, 0, unroll=NORM_UNROLL)

    w = w_ref[...]
    if len(rest) == 2:
        w = w.astype(BF16)
        rest[0][...] = w
    acc = jnp.dot(xm_scr[...], w, preferred_element_type=F32)
    scale = jnp.where((j >= q_tiles[0]) & (j < q_tiles[1]), HEAD_DIM ** -0.5, 1.0)
    o_ref[...] = (acc * scale).astype(BF16)
    tail_ref[...] = acc
    rows = acc.shape[0]
    for h in range(N_KV_HEADS):
        k_ref[pl.ds(h, rows, stride=N_KV_HEADS), :] = acc[:, h * HEAD_DIM:(h + 1) * HEAD_DIM]
        v_ref[pl.ds(h, rows, stride=N_KV_HEADS), :] = acc[:, (N_KV_HEADS + h) * HEAD_DIM:(N_KV_HEADS + h + 1) * HEAD_DIM]


def _proj(x, mod, norm_g, w_in, *, tm, tiles_per_seq, col):
    n, d = x.shape
    cols = w_in.shape[1]
    tn = PROJ_TN
    n_tiles = cols // tn
    d_kv = col["d_kv"]
    n_gate, n_q, n_u = col["q"] // tn, col["d_attn"] // tn, col["d_ssm"] // tn
    q_tiles = (n_gate, n_gate + n_q)
    tail0 = col["u"] // tn
    n_tail = n_tiles - tail0
    assert 2 * d_kv == tn and col["u"] == col["q"] + col["d_attn"] and col["k"] == cols - tn

    def src_tile(j):
        return jnp.where(j < n_gate, j + (n_u + n_q + 1),
                         jnp.where(j < n_gate + n_q, j - n_gate + n_u,
                                   jnp.where(j < n_gate + n_q + n_u, j - (n_gate + n_q), n_u + n_q)))
    seq_of = (lambda i: i // tiles_per_seq) if mod.shape[0] > 1 else (lambda i: 0)
    out_specs = [pl.BlockSpec((tm, tn), lambda i, j: (i, j)),
                 pl.BlockSpec((tm, tn), lambda i, j: (i, jnp.clip(j - tail0, 0, n_tail - 1))),
                 pl.BlockSpec((tm * N_KV_HEADS, HEAD_DIM), lambda i, j: (i, 0)),
                 pl.BlockSpec((tm * N_KV_HEADS, HEAD_DIM), lambda i, j: (i, 0))]
    out_shape = [jax.ShapeDtypeStruct((n, cols), BF16), jax.ShapeDtypeStruct((n, n_tail * tn), F32),
                 jax.ShapeDtypeStruct((n * N_KV_HEADS, HEAD_DIM), F32),
                 jax.ShapeDtypeStruct((n * N_KV_HEADS, HEAD_DIM), F32)]
    if w_in.dtype != BF16:
        out_specs.append(pl.BlockSpec((d, tn), lambda i, j: (0, src_tile(jnp.where(i == 0, j, n_tiles - 1)))))
        out_shape.append(jax.ShapeDtypeStruct((d, cols), BF16))
    return pl.pallas_call(
        functools.partial(_proj_kernel, q_tiles=q_tiles),
        grid=(n // tm, n_tiles),
        in_specs=[pl.BlockSpec((tm, d), lambda i, j: (i, 0)),
                  pl.BlockSpec((None, 6, d), lambda i, j: (seq_of(i), 0, 0)),
                  pl.BlockSpec((1, d), lambda i, j: (0, 0)),
                  pl.BlockSpec((d, tn), lambda i, j: (0, src_tile(j)))],
        out_specs=out_specs,
        out_shape=out_shape,
        scratch_shapes=[pltpu.VMEM((tm, d), BF16)],
        compiler_params=_cparams(("arbitrary", "arbitrary")),
        name="proj",
    )(x, mod, norm_g.reshape(1, d), w_in)


def _cmul(ar, ai, br, bi):
    return ar * br - ai * bi, ar * bi + ai * br


def _cpow(base_r, base_i, n, bits):
    res_r = jnp.ones_like(base_r)
    res_i = jnp.zeros_like(base_r)
    for bit in range(bits):
        nr, ni = _cmul(res_r, res_i, base_r, base_i)
        take = ((n >> bit) & 1) == 1
        res_r = jnp.where(take, nr, res_r)
        res_i = jnp.where(take, ni, res_i)
        if bit + 1 < bits:
            base_r, base_i = _cmul(base_r, base_i, base_r, base_i)
    return res_r, res_i


def _ssm_disc_kernel(lre_ref, lim_ref, ldt_ref, ere_ref, eim_ref, qre_ref, qim_ref):
    lre = lre_ref[...]
    lim = lim_ref[...]
    dt = jnp.exp(ldt_ref[...])
    mag = jnp.exp(lre * dt)
    e_re = mag * jnp.cos(lim * dt)
    e_im = mag * jnp.sin(lim * dt)
    n_re = e_re - 1.0
    den = lre * lre + lim * lim
    ere_ref[...] = e_re
    eim_ref[...] = e_im
    qre_ref[...] = (n_re * lre + e_im * lim) / den
    qim_ref[...] = (e_im * lre - n_re * lim) / den


def _ssm_disc(lam_re, lam_im, log_dt):
    two, g, p = lam_re.shape
    shp = jax.ShapeDtypeStruct((two * g, p), F32)
    outs = pl.pallas_call(_ssm_disc_kernel, out_shape=[shp] * 4, name="ssm_disc")(
        lam_re.reshape(two * g, p), lam_im.reshape(two * g, p), log_dt.reshape(two * g, 1))
    return [o.reshape(two, g, p) for o in outs]


SSM_OPS_GROUPS = 8


def _ssm_ops_mod_kernel(ecol_ref, rows_ref, b_ref, ct_ref, d_ref, c_ref, wm_ref, bm_ref,
                        m_ref, w_ref, v_ref, a_ref, mod_ref):
    _mod_kernel(c_ref, wm_ref, bm_ref, mod_ref)
    _ssm_ops_kernel(ecol_ref, rows_ref, b_ref, ct_ref, d_ref, m_ref, w_ref, v_ref, a_ref)


def _ssm_ops_kernel(ecol_ref, rows_ref, b_ref, ct_ref, d_ref, m_ref, w_ref, v_ref, a_ref):
    def one(gi, carry):
        _ssm_ops_group(ecol_ref.at[gi], rows_ref.at[gi], b_ref.at[gi], ct_ref.at[gi], d_ref.at[gi],
                       m_ref.at[gi], w_ref.at[gi], v_ref.at[gi], a_ref.at[gi])
        return carry

    lax.fori_loop(0, ecol_ref.shape[0], one, 0, unroll=True)


def _ssm_ops_group(ecol_ref, rows_ref, b_ref, ct_ref, d_ref, m_ref, w_ref, v_ref, a_ref):
    L, ch, p = SSM_CHUNK, SSM_CH, SSM_STATE
    lanes = SSM_LANES
    half = 2 * p
    lane_c = lax.broadcasted_iota(jnp.int32, (p, lanes), 1)
    lag = lane_c // ch

    def col(k):
        return jnp.broadcast_to(ecol_ref[:, k:k + 1], (p, lanes))

    f0 = _cpow(col(0), col(1), lag, 4)
    b0 = _cpow(col(2), col(3), (L - 1) - lag, 4)
    f1 = _cmul(f0[0], f0[1], col(0), col(1))
    b1 = _cmul(b0[0], b0[1], col(2), col(3))
    def ct(k):
        return jnp.tile(ct_ref[:, k * ch:(k + 1) * ch], (1, L))

    cf = (ct(0), ct(1))
    cb = (ct(2), ct(3))
    yf0 = _cmul(*f0, *cf)
    yb0 = _cmul(*b0, *cb)
    yf1 = _cmul(*f1, *cf)
    yb1 = _cmul(*b1, *cb)
    v_ref[...] = jnp.concatenate([yf1[0], yb1[0], -yf1[1], -yb1[1]], axis=0).astype(BF16)

    lane_r = lax.broadcasted_iota(jnp.int32, (ch, lanes), 1)
    sgn = jnp.where(lane_r < half, -1.0, 1.0)
    b1raw = b_ref[...]
    b2raw = pltpu.roll(b1raw, half, 1) * sgn
    bb1 = rows_ref[2:3, :] * b1raw + rows_ref[3:4, :] * b2raw
    bb2 = pltpu.roll(bb1, half, 1) * sgn

    zero = jnp.zeros((p, lanes), F32)
    hi = lax.Precision.HIGHEST
    r0f = jnp.dot(bb1, jnp.concatenate([yf0[0], zero, -yf0[1], zero], axis=0), precision=hi,
                  preferred_element_type=F32)
    r0b = jnp.dot(bb1, jnp.concatenate([zero, yb0[0], zero, -yb0[1]], axis=0), precision=hi,
                  preferred_element_type=F32)
    row_r = lax.broadcasted_iota(jnp.int32, (ch, lanes), 0)
    d_col = d_ref[...]
    for s in range(L):
        blk = jnp.where(lane_r == ch * s + row_r, d_col, 0.0)
        fwd = r0f if s == 0 else jnp.where(lane_r >= ch * s, pltpu.roll(r0f, ch * s, 1), 0.0)
        k = ch * (L - 1 - s)
        bwd = r0b if k == 0 else jnp.where(lane_r < lanes - k, pltpu.roll(r0b, lanes - k, 1), 0.0)
        m_ref[s * ch:(s + 1) * ch, :] = (blk + fwd + bwd).astype(BF16)

    is_f = (lane_r & (half - 1)) < p
    n_row = jnp.where(is_f, (L - 1) - row_r, row_r)
    er, ei = _cpow(jnp.broadcast_to(rows_ref[0:1, :], (L, lanes)), jnp.broadcast_to(rows_ref[1:2, :], (L, lanes)),
                   n_row, 4)
    for s in range(L):
        w_ref[s * ch:(s + 1) * ch, :] = (er[s:s + 1, :] * bb1 + ei[s:s + 1, :] * bb2).astype(BF16)

    ar, ai = rows_ref[0:1, :], rows_ref[1:2, :]
    for _ in range(4):
        ar, ai = _cmul(ar, ai, ar, ai)
    lane_1 = lax.broadcasted_iota(jnp.int32, (1, lanes), 1)
    a_ref[...] = jnp.where(lane_1 < half, ar, ai)


def _ssm_operators_and_mod(lam_re, lam_im, log_dt, b_re, b_im, c_re, c_im, ssm_d, cond, w_mod, b_mod):
    two, g, p = lam_re.shape
    ch = b_re.shape[-1]
    lanes = SSM_LANES
    assert (two, p, ch, SSM_CHUNK * ch) == (2, SSM_STATE, SSM_CH, lanes)
    e_re, e_im, q_re, q_im = _ssm_disc(lam_re, lam_im, log_dt)
    ecol = jnp.stack([e_re[0], e_im[0], e_re[1], e_im[1]], axis=-1)

    def fbfb(x):
        return jnp.concatenate([x[0], x[1], x[0], x[1]], axis=-1)
    rows = jnp.stack([fbfb(e_re), fbfb(e_im), fbfb(q_re), fbfb(q_im)], axis=1)

    def bt(x):
        return jnp.swapaxes(x, -1, -2)
    b_rows = jnp.concatenate([bt(b_re[0]), bt(b_re[1]), bt(b_im[0]), bt(b_im[1])], axis=-1)

    ct = jnp.concatenate([bt(c_re[0]), bt(c_im[0]), bt(c_re[1]), bt(c_im[1])], axis=-1)
    d_col = ssm_d.reshape(g, ch, 1)
    gb = SSM_OPS_GROUPS
    mat = pl.BlockSpec((gb, lanes, lanes), lambda i: (i, 0, 0))
    steps = g // gb
    c_rows, d_model = cond.shape
    n_mod = w_mod.shape[1]
    tn = n_mod // steps
    assert n_mod % steps == 0 and tn % 128 == 0
    outs = pl.pallas_call(
        _ssm_ops_mod_kernel,
        grid=(steps,),
        in_specs=[pl.BlockSpec((gb, p, 4), lambda i: (i, 0, 0)),
                  pl.BlockSpec((gb, 4, lanes), lambda i: (i, 0, 0)),
                  pl.BlockSpec((gb, ch, lanes), lambda i: (i, 0, 0)),
                  pl.BlockSpec((gb, p, 4 * ch), lambda i: (i, 0, 0)),
                  pl.BlockSpec((gb, ch, 1), lambda i: (i, 0, 0)),
                  pl.BlockSpec((c_rows, d_model), lambda i: (0, 0)),
                  pl.BlockSpec((d_model, tn), lambda i: (0, i)),
                  pl.BlockSpec((1, tn), lambda i: (0, i))],
        out_specs=[mat, mat, mat, pl.BlockSpec((gb, 1, lanes), lambda i: (i, 0, 0)),
                   pl.BlockSpec((c_rows, tn), lambda i: (0, i))],
        out_shape=[jax.ShapeDtypeStruct((g, lanes, lanes), BF16)] * 3 + [jax.ShapeDtypeStruct((g, 1, lanes), F32),
                                                                         jax.ShapeDtypeStruct((c_rows, n_mod), F32)],
        compiler_params=_cparams(("arbitrary",)),
        name="ssm_ops_mod",
    )(ecol, rows, b_rows, ct, d_col, cond, w_mod, b_mod.reshape(1, n_mod))
    return outs[:4], outs[4]


SSM_SLAB_GROUPS = 128 // SSM_CH


def _chunk_transpose(xs, chunk_id):
    for k in (4, 2, 1):
        keep = (chunk_id & k) == 0
        new = list(xs)
        for i in range(len(xs)):
            if i & k == 0:
                a, b = xs[i], xs[i + k]
                if 2 * SSM_CH * k == 128:
                    moved = pltpu.roll(jnp.where(keep, b, a), SSM_CH * k, 1)
                    new[i] = jnp.where(keep, a, moved)
                    new[i + k] = jnp.where(keep, moved, b)
                else:
                    new[i] = jnp.where(keep, a, pltpu.roll(b, SSM_CH * k, 1))
                    new[i + k] = jnp.where(keep, pltpu.roll(a, 128 - SSM_CH * k, 1), b)
        xs = new
    return xs


def _ssm_kernel(u_ref, m_ref, w_ref, v_ref, a_ref, h0_ref, y_ref, hfin_ref, ug_scr, s_scr, hf_scr, hb_scr, yg_scr,
                *, n_chunks, ns):
    gps = SSM_SLAB_GROUPS
    nc = ns * n_chunks
    half = 2 * SSM_STATE
    chunk_id = lax.broadcasted_iota(jnp.int32, (nc, 128), 1) // SSM_CH
    for hf in range(2):
        xs = [u_ref[pl.ds(hf * gps + i, nc, stride=SSM_CHUNK), :] for i in range(gps)]
        xs = _chunk_transpose(xs, chunk_id)
        for g in range(gps):
            ug_scr[g, :, hf * 128:(hf + 1) * 128] = xs[g].astype(BF16)
    pitch = n_chunks + 8
    for g in range(gps):
        s = jnp.dot(ug_scr[g], w_ref[g], preferred_element_type=F32)
        for q in range(ns):
            s_scr[g, q * pitch:q * pitch + n_chunks, :] = s[q * n_chunks:(q + 1) * n_chunks, :half]
            s_scr[gps + g, q * pitch:q * pitch + n_chunks, :] = s[q * n_chunks:(q + 1) * n_chunks, half:]
    is_fwd = lax.broadcasted_iota(jnp.int32, (1, half), 1) < SSM_STATE
    blocks = range(2 * gps)
    a = [a_ref[:, b * half:(b + 1) * half] for b in blocks]
    h = [h0_ref[:, b * half:(b + 1) * half] for b in blocks]

    def rows(c):
        return pl.ds(c, ns, stride=pitch)

    for k in range(n_chunks):
        kb = n_chunks - 1 - k
        new_h = list(h)
        for b in blocks:
            hf_scr[b, rows(k), :] = h[b]
            hb_scr[b, rows(kb), :] = h[b]
        for g in range(gps):
            s_re = jnp.where(is_fwd, s_scr[g, rows(k), :], s_scr[g, rows(kb), :])
            s_im = jnp.where(is_fwd, s_scr[gps + g, rows(k), :], s_scr[gps + g, rows(kb), :])
            new_h[g] = a[g] * h[g] - a[gps + g] * h[gps + g] + s_re
            new_h[gps + g] = a[g] * h[gps + g] + a[gps + g] * h[g] + s_im
        h = new_h
    for b in blocks:
        hfin_ref[:, b * half:(b + 1) * half] = h[b]

    def entering(b):
        return jnp.concatenate(
            [jnp.where(is_fwd, hf_scr[b, q * pitch:q * pitch + n_chunks, :], hb_scr[b, q * pitch:q * pitch + n_chunks, :])
             for q in range(ns)], axis=0)

    for g in range(gps):
        hin = jnp.concatenate([entering(g), entering(gps + g)], axis=1).astype(BF16)
        yg_scr[g] = (jnp.dot(ug_scr[g], m_ref[g], preferred_element_type=F32)
                     + jnp.dot(hin, v_ref[g], preferred_element_type=F32))
    for hf in range(2):
        xs = [yg_scr[g, :, hf * 128:(hf + 1) * 128] for g in range(gps)]
        xs = _chunk_transpose(xs, chunk_id)
        for i in range(gps):
            y_ref[pl.ds(hf * gps + i, nc, stride=SSM_CHUNK), :] = xs[i]


def _slab_lanes(x):
    lead = x.shape[:-2]
    g = x.shape[-2]
    gps = SSM_SLAB_GROUPS
    x = x.reshape(lead + (g // gps, gps, 2, 2 * SSM_STATE))
    x = jnp.swapaxes(x, -3, -2)
    return x.reshape(lead + (g * 4 * SSM_STATE,))


def _unslab_lanes(x, g):
    lead = x.shape[:-1]
    gps = SSM_SLAB_GROUPS
    x = x.reshape(lead + (g // gps, 2, gps, 2 * SSM_STATE))
    x = jnp.swapaxes(x, -3, -2)
    return x.reshape(lead + (g, 4 * SSM_STATE))


def _ssm(u, u_col, ops, h0, *, n_seq, seq_len, seq_block, riders=()):
    m_op, w_op, v_op, a_op = ops
    n = u.shape[0]
    g = m_op.shape[0]
    d_ssm = g * SSM_CH
    slab0 = u_col // 128
    gps = SSM_SLAB_GROUPS
    lanes = SSM_LANES
    n_chunks = seq_len // SSM_CHUNK
    nc = seq_block * n_chunks
    padded = seq_block * (n_chunks + 8)
    rows = seq_block * seq_len
    wide2 = gps * 4 * SSM_STATE
    mat = pl.BlockSpec((gps, lanes, lanes), lambda j, b: (j, 0, 0))
    n_blocks = n_seq // seq_block
    rider_specs, rider_shapes = _rider_specs(riders, (g // gps) * n_blocks, lambda j, b: j * n_blocks + b)
    outs = pl.pallas_call(
        _ride_casts(functools.partial(_ssm_kernel, n_chunks=n_chunks, ns=seq_block), 6, 2, len(riders)),
        grid=(g // gps, n_blocks),
        in_specs=[pl.BlockSpec((rows, 128), lambda j, b: (b, slab0 + j)), mat, mat, mat,
                  pl.BlockSpec((1, wide2), lambda j, b: (0, j)),
                  pl.BlockSpec((seq_block, wide2), lambda j, b: (b, j))] + rider_specs,
        out_specs=[pl.BlockSpec((rows, 128), lambda j, b: (b, j)),
                   pl.BlockSpec((seq_block, wide2), lambda j, b: (b, j))] + rider_specs,
        out_shape=[jax.ShapeDtypeStruct((n, d_ssm), F32),
                   jax.ShapeDtypeStruct((n_seq, g * 4 * SSM_STATE), F32)] + rider_shapes,
        scratch_shapes=[pltpu.VMEM((gps, nc, lanes), BF16), pltpu.VMEM((2 * gps, padded, 128), F32),
                        pltpu.VMEM((2 * gps, padded, 128), F32), pltpu.VMEM((2 * gps, padded, 128), F32),
                        pltpu.VMEM((gps, nc, lanes), F32)],
        compiler_params=_cparams(("arbitrary", "arbitrary")),
        name="ssm",
    )(u, m_op, w_op, v_op, _slab_lanes(a_op[:, 0])[None], h0, *riders)
    return outs[0], outs[1], outs[2:]


def _softmax_pv(pieces, sink):
    m = sink
    for s, _ in pieces:
        m = jnp.maximum(m, jnp.max(s, axis=-1, keepdims=True))
    out = None
    for s, v in pieces:
        pv = jnp.dot(jnp.exp(s - m).astype(BF16), v, preferred_element_type=F32)
        out = pv if out is None else out + pv
    return out[:, :HEAD_DIM] / (out[:, HEAD_DIM:] + jnp.exp(sink - m))


def _qk(q, k):
    return lax.dot_general(q, k, (((1,), (1,)), ((), ())), preferred_element_type=F32)


def _with_ones(v):
    return jnp.concatenate([v, jnp.ones_like(v)], axis=1)


ATTN_CTX_SEQS = 8


def _attn_ctx_kernel(sink_ref, q_ref, kv_ref, o_ref, *, seq_len):
    d_kv = N_KV_HEADS * HEAD_DIM
    for r0 in range(0, q_ref.shape[0], seq_len):
        rows = slice(r0, r0 + seq_len)
        for kh in range(N_KV_HEADS):
            k = kv_ref[rows, kh * HEAD_DIM:(kh + 1) * HEAD_DIM]
            v = _with_ones(kv_ref[rows, d_kv + kh * HEAD_DIM:d_kv + (kh + 1) * HEAD_DIM])
            for h in range(kh * Q_PER_KV, (kh + 1) * Q_PER_KV):
                q = q_ref[rows, h * HEAD_DIM:(h + 1) * HEAD_DIM]
                o = _softmax_pv([(_qk(q, k), v)], sink_ref[h])
                o_ref[rows, h * HEAD_DIM:(h + 1) * HEAD_DIM] = o.astype(BF16)


def _attn_ctx(proj, sink, *, n_seq, seq_len, col, riders=()):
    d_attn = col["d_attn"]
    kv_w = 2 * col["d_kv"]
    spb = ATTN_CTX_SEQS if n_seq % ATTN_CTX_SEQS == 0 else 1
    rows = spb * seq_len
    rider_specs, rider_shapes = _rider_specs(riders, n_seq // spb, lambda b: b)
    outs = pl.pallas_call(
        _ride_casts(functools.partial(_attn_ctx_kernel, seq_len=seq_len), 3, 1, len(riders)),
        grid=(n_seq // spb,),
        in_specs=[pl.BlockSpec(memory_space=pltpu.SMEM),
                  pl.BlockSpec((rows, d_attn), lambda b: (b, col["q"] // d_attn)),
                  pl.BlockSpec((rows, kv_w), lambda b: (b, col["k"] // kv_w))] + rider_specs,
        out_specs=[pl.BlockSpec((rows, d_attn), lambda b: (b, 0))] + rider_specs,
        out_shape=[jax.ShapeDtypeStruct((n_seq * seq_len, d_attn), BF16)] + rider_shapes,
        compiler_params=_cparams(("arbitrary",)),
        name="attn_ctx",
    )(sink, proj, proj, *riders)
    return outs[0], outs[1:]


def _rope(x, cos, sin):
    reps = x.shape[1] // HEAD_DIM
    xf = x.astype(F32)
    return (xf * jnp.concatenate([cos] * reps, axis=1) + _swap32(xf) * jnp.concatenate([sin] * reps, axis=1)).astype(BF16)


ATTN_LAT_QBLOCKS = 4


def _attn_lat_kernel(*refs, n_blocks, qb):
    sink_ref, q_ref = refs[:2]
    kv_refs = refs[2:4 + qb]
    ck_ref, cv_ref, cos_ref, sin_ref, o_ref = refs[4 + qb:]
    first = pl.program_id(1) * qb
    d_kv = N_KV_HEADS * HEAD_DIM

    def tables(blk):
        rows = pl.ds(pl.multiple_of(blk * BLOCK, BLOCK), BLOCK)
        return cos_ref[rows, :], sin_ref[rows, :]

    k_rot = [_rope(kv_refs[n][:, :d_kv], *tables(jnp.clip(first + n - 1, 0, n_blocks - 1))) for n in range(qb + 2)]
    r = lax.broadcasted_iota(jnp.int32, (BLOCK, 3 * BLOCK), 0)
    c = lax.broadcasted_iota(jnp.int32, (BLOCK, 3 * BLOCK), 1)
    cc = c & (BLOCK - 1)
    for sub in range(qb):
        blk = first + sub
        q_rows = slice(sub * BLOCK, (sub + 1) * BLOCK)
        q_all = _rope(q_ref[q_rows, :], *tables(blk))
        k_all = jnp.concatenate(k_rot[sub:sub + 3], axis=0)
        valid = (((c < BLOCK) & (cc >= r) & (blk > 0)) | ((c >= BLOCK) & (c < 2 * BLOCK))
                 | ((c >= 2 * BLOCK) & (cc <= r) & (blk < n_blocks - 1)))
        for kh in range(N_KV_HEADS):
            ks = slice(kh * HEAD_DIM, (kh + 1) * HEAD_DIM)
            vs = slice(d_kv + kh * HEAD_DIM, d_kv + (kh + 1) * HEAD_DIM)
            k_loc = k_all[:, ks]
            k_ctx = ck_ref[:, ks].astype(BF16)
            v_loc = _with_ones(jnp.concatenate([kv_refs[sub + n][:, vs] for n in range(3)], axis=0))
            v_ctx = _with_ones(cv_ref[:, ks].astype(BF16))
            for h in range(kh * Q_PER_KV, (kh + 1) * Q_PER_KV):
                q = q_all[:, h * HEAD_DIM:(h + 1) * HEAD_DIM]
                s_loc = jnp.where(valid, _qk(q, k_loc), NEG_INF)
                o = _softmax_pv([(s_loc, v_loc), (_qk(q, k_ctx), v_ctx)], sink_ref[h])
                o_ref[q_rows, h * HEAD_DIM:(h + 1) * HEAD_DIM] = o.astype(BF16)


def _attn_lat(proj, cache_k, cache_v, sink, *, n_seq, seq_len, col):
    d_attn = col["d_attn"]
    kv_w = 2 * col["d_kv"]
    assert WINDOW == BLOCK and seq_len % BLOCK == 0
    nb = seq_len // BLOCK
    qb = ATTN_LAT_QBLOCKS if nb % ATTN_LAT_QBLOCKS == 0 else 1
    steps = nb // qb
    kv_col = col["k"] // kv_w
    past, d_kv = cache_k.shape[1], cache_k.shape[2]

    def kv_spec(off):
        return pl.BlockSpec((BLOCK, kv_w), lambda b, i: (b * nb + jnp.clip(i * qb + off, 0, nb - 1), kv_col))
    cache_spec = pl.BlockSpec((None, past, d_kv), lambda b, i: (b, 0, 0))
    table_spec = pl.BlockSpec((seq_len, HEAD_DIM), lambda b, i: (0, 0))
    cos, sin = _rope_tables(seq_len)
    return pl.pallas_call(
        functools.partial(_attn_lat_kernel, n_blocks=nb, qb=qb),
        grid=(n_seq, steps),
        in_specs=[pl.BlockSpec(memory_space=pltpu.SMEM),
                  pl.BlockSpec((qb * BLOCK, d_attn), lambda b, i: (b * steps + i, col["q"] // d_attn))]
                 + [kv_spec(off) for off in range(-1, qb + 1)]
                 + [cache_spec, cache_spec, table_spec, table_spec],
        out_specs=pl.BlockSpec((qb * BLOCK, d_attn), lambda b, i: (b * steps + i, 0)),
        out_shape=jax.ShapeDtypeStruct((n_seq * seq_len, d_attn), BF16),
        compiler_params=_cparams(("arbitrary", "arbitrary")),
        name="attn_lat",
    )(sink, proj, *([proj] * (qb + 2)), cache_k, cache_v, cos, sin)


def _gelu_tanh(x):
    return 0.5 * x * (1.0 + jnp.tanh(math.sqrt(2.0 / math.pi) * (x + 0.044715 * (x * x * x))))


def _mixer_out_kernel(x_ref, y_ref, o_ref, gs_ref, ga_ref, mod_ref, g_ref, wglu_ref, wso_ref, wao_ref, wout_ref,
                      x1_ref, xm2_ref):
    z = _gelu_tanh(y_ref[...].astype(F32))
    z = z * _sigmoid(jnp.dot(z.astype(BF16), wglu_ref[...], preferred_element_type=F32))
    s_br = jnp.dot(z.astype(BF16), wso_ref[...], preferred_element_type=F32)
    a_br = jnp.dot(o_ref[...], wao_ref[...], preferred_element_type=F32)
    merged = _sigmoid(gs_ref[...].astype(F32)) * s_br + _sigmoid(ga_ref[...].astype(F32)) * a_br
    out = jnp.dot(merged.astype(BF16), wout_ref[...], preferred_element_type=F32)
    x1 = x_ref[...] + mod_ref[2:3, :] * out
    x1_ref[...] = x1
    xm2_ref[...] = _rms_modulate(x1, g_ref[...], mod_ref[4:5, :], mod_ref[3:4, :]).astype(BF16)


def _mixer_out(x, y, o, proj, mod, norm_g, w_glu, w_ssm_o, w_attn_o, w_out, *, tm, tiles_per_seq, col):
    n, d = x.shape
    d_ssm = y.shape[1]
    d_attn = o.shape[1]
    seq_of = (lambda i: i // tiles_per_seq) if mod.shape[0] > 1 else (lambda i: 0)

    def resident(shape):
        return pl.BlockSpec(shape, lambda i: (0, 0), pipeline_mode=pl.Buffered(1))
    return pl.pallas_call(
        _mixer_out_kernel,
        grid=(n // tm,),
        in_specs=[pl.BlockSpec((tm, d), lambda i: (i, 0)),
                  pl.BlockSpec((tm, d_ssm), lambda i: (i, 0)),
                  pl.BlockSpec((tm, d_attn), lambda i: (i, 0)),
                  pl.BlockSpec((tm, d), lambda i: (i, col["gs"] // d)),
                  pl.BlockSpec((tm, d), lambda i: (i, col["ga"] // d)),
                  pl.BlockSpec((None, 6, d), lambda i: (seq_of(i), 0, 0)),
                  pl.BlockSpec((1, d), lambda i: (0, 0)),
                  resident(w_glu.shape), resident(w_ssm_o.shape), resident(w_attn_o.shape), resident(w_out.shape)],
        out_specs=[pl.BlockSpec((tm, d), lambda i: (i, 0)), pl.BlockSpec((tm, d), lambda i: (i, 0))],
        out_shape=[jax.ShapeDtypeStruct((n, d), F32), jax.ShapeDtypeStruct((n, d), BF16)],
        compiler_params=_cparams(("arbitrary",)),
        name="mixer_out",
    )(x, y, o, proj, proj, mod, norm_g.reshape(1, d), w_glu, w_ssm_o, w_attn_o, w_out)


FFN_TM = 1024
FFN_TF = 512
FFN_TN = 256


def _ffn_kernel(xm_ref, wa_ref, wb_ref, cw_ref, cb_ref, wd_ref, x1_ref, mod_ref, g_ref, o_ref, act_scr, raw_scr,
                *, seq_len, nf, nn, d_ff):
    j = pl.program_id(1)
    tm = xm_ref.shape[0]
    tf = wa_ref.shape[1]
    tn = wd_ref.shape[1]
    ch = tf // 2
    pos = lax.broadcasted_iota(jnp.int32, (tm, 1), 0) & (seq_len - 1)
    has_prev = pos != 0
    has_next = pos != seq_len - 1

    def lanes(off):
        return slice(off, off + ch) if isinstance(off, int) else pl.ds(pl.multiple_of(off, ch), ch)

    def gate(ha, hb, col0):
        def conv(h, off):
            cols = lanes(off)
            cw = cw_ref[:, cols]
            h_prev = jnp.where(has_prev, pltpu.roll(h, 1, 0), 0.0)
            h_next = jnp.where(has_next, pltpu.roll(h, tm - 1, 0), 0.0)
            return cw[0:1] * h_prev + cw[1:2] * h + cw[2:3] * h_next + cb_ref[:, cols]

        a = conv(ha, col0)
        b = conv(hb, d_ff + col0)
        return ((a * _sigmoid(a)) * b).astype(BF16)

    def act_cols(col0):
        return lanes(tf + col0)

    @pl.when(j == 0)
    def _():
        raw_scr[...] = jnp.zeros_like(raw_scr)

    @pl.when(j < nf)
    def _():
        xm = xm_ref[...]
        ha0 = jnp.dot(xm, wa_ref[:, :ch], preferred_element_type=F32)
        hb0 = jnp.dot(xm, wb_ref[:, :ch], preferred_element_type=F32)
        carried = (j - 1) * tf + ch
        act_scr[:, act_cols(carried)] = gate(raw_scr[0], raw_scr[1], jnp.maximum(carried, 0))
        ha1 = jnp.dot(xm, wa_ref[:, ch:], preferred_element_type=F32)
        hb1 = jnp.dot(xm, wb_ref[:, ch:], preferred_element_type=F32)
        act_scr[:, act_cols(j * tf)] = gate(ha0, hb0, j * tf)
        raw_scr[0] = ha1
        raw_scr[1] = hb1

    @pl.when(j == nf)
    def _():
        last = (nf - 1) * tf + ch
        act_scr[:, act_cols(last)] = gate(raw_scr[0], raw_scr[1], last)

    @pl.when(j >= nf)
    def _():
        cols = pl.ds(pl.multiple_of((j - nf) * tn, tn), tn)
        ffn = jnp.dot(act_scr[:, tf:], wd_ref[...], preferred_element_type=F32)
        o_ref[:, cols] = x1_ref[...] + mod_ref[5:6, cols] * ffn

    @pl.when(j == nf + nn - 1)
    def _():
        x2 = o_ref[...]
        ms = jnp.mean(x2 * x2, axis=-1, keepdims=True)
        o_ref[...] = x2 * lax.rsqrt(ms + EPS) * g_ref[...]


def _ffn(x1, xm2, mod, w_up, conv_w, conv_b, w_down, final_g, *, tm, seq_len):
    n, d = x1.shape
    d_ff = w_down.shape[0]
    tf, tn = FFN_TF, FFN_TN
    nf, nn = d_ff // tf, d // tn
    assert tm % seq_len == 0 and seq_len & (seq_len - 1) == 0 and d_ff % tf == 0 and (tf // 2) % V7X_MXU_WIDTH == 0
    seqs_per_tile = tm // seq_len
    seq_of = (lambda i: i * seqs_per_tile) if mod.shape[0] > 1 else (lambda i: 0)
    assert mod.shape[0] == 1 or seqs_per_tile == 1
    cb = conv_b.reshape(1, 2 * d_ff)

    def up(j):
        return jnp.minimum(j, nf - 1)

    def down(j):
        return jnp.maximum(j - nf, 0)
    return pl.pallas_call(
        functools.partial(_ffn_kernel, seq_len=seq_len, nf=nf, nn=nn, d_ff=d_ff),
        grid=(n // tm, nf + nn),
        in_specs=[pl.BlockSpec((tm, d), lambda i, j: (i, 0), pipeline_mode=pl.Buffered(1)),
                  pl.BlockSpec((d, tf), lambda i, j: (0, up(j))),
                  pl.BlockSpec((d, tf), lambda i, j: (0, nf + up(j))),
                  pl.BlockSpec((3, 2 * d_ff), lambda i, j: (0, 0)),
                  pl.BlockSpec((1, 2 * d_ff), lambda i, j: (0, 0)),
                  pl.BlockSpec((d_ff, tn), lambda i, j: (0, down(j))),
                  pl.BlockSpec((tm, tn), lambda i, j: (i, down(j))),
                  pl.BlockSpec((None, 6, d), lambda i, j: (seq_of(i), 0, 0)),
                  pl.BlockSpec((1, d), lambda i, j: (0, 0))],
        out_specs=pl.BlockSpec((tm, d), lambda i, j: (i, 0)),
        out_shape=jax.ShapeDtypeStruct((n, d), F32),
        scratch_shapes=[pltpu.VMEM((tm, tf + d_ff), BF16), pltpu.VMEM((2, tm, tf // 2), F32)],
        compiler_params=_cparams(("arbitrary", "arbitrary")),
        name="ffn",
    )(xm2, w_up, w_up, conv_w, cb, w_down, x1, mod, final_g.reshape(1, d))


def _rope_tables(seq_len):
    rows = seq_len // GRID_W
    row = np.repeat(np.arange(rows, dtype=np.float64), GRID_W)
    colp = np.tile(np.arange(GRID_W, dtype=np.float64), rows)
    n_freq = HEAD_DIM // 4
    inv = ROPE_THETA ** (-np.arange(n_freq, dtype=np.float64) / n_freq)
    ang_r = row[:, None] * inv[None, :]
    ang_c = colp[:, None] * inv[None, :]
    cos = np.concatenate([np.cos(ang_r), np.cos(ang_r), np.cos(ang_c), np.cos(ang_c)], axis=1)
    sin = np.concatenate([-np.sin(ang_r), np.sin(ang_r), -np.sin(ang_c), np.sin(ang_c)], axis=1)
    return jnp.asarray(cos, F32), jnp.asarray(sin, F32)


PROJ_TM = 1024
MIXER_TM = 256


def kernel(x_prompt, x_sample, cache_k, cache_v, state_ssm_re, state_ssm_im, c, c_ctx, norm_mix_g, norm_ffn_g,
           w_mod, b_mod, w_in, ssm_lambda_re, ssm_lambda_im, ssm_log_dt, ssm_b_re, ssm_b_im, ssm_c_re, ssm_c_im,
           ssm_d, w_glu, attn_sink, w_ssm_o, w_attn_o, w_out, w_up, conv_w, conv_b, w_down, final_norm_g):
    batch, seq, d = x_prompt.shape
    dec_batch, dec_seq, _ = x_sample.shape
    depth = w_in.shape[0]
    assert depth == 1, "final norm is fused into the (single) layer's ffn kernel"
    d_ssm = w_glu.shape[1]
    d_attn = N_HEADS * HEAD_DIM
    d_kv = N_KV_HEADS * HEAD_DIM
    groups = d_ssm // SSM_CH
    assert w_in.shape[2] == d_ssm + d_attn + 2 * d_kv + 2 * d
    assert ssm_lambda_re.shape[2:] == (groups, SSM_STATE) and dec_batch <= 8 - 1
    l = 0

    col = {"gs": 0, "ga": d, "q": 2 * d, "u": 2 * d + d_attn, "k": 2 * d + d_attn + d_ssm,
           "d_ssm": d_ssm, "d_attn": d_attn, "d_kv": d_kv}
    xp = x_prompt.reshape(batch * seq, d)
    xs = x_sample.reshape(dec_batch * dec_seq, d)

    cond = jnp.concatenate([c_ctx[None], c, jnp.zeros((8 - 1 - dec_batch, d), F32)], axis=0)

    ssm_ops, mod = _ssm_operators_and_mod(
        ssm_lambda_re[l], ssm_lambda_im[l], ssm_log_dt[l], ssm_b_re[l], ssm_b_im[l], ssm_c_re[l], ssm_c_im[l],
        ssm_d[l], cond, w_mod[l], b_mod[l])
    mod = mod.reshape(8, 6, d)
    mod_p, mod_s = mod[0:1], mod[1:1 + dec_batch]
    proj_s, tail_s, _, _, w_in_b = _proj(xs, mod_s, norm_mix_g[l], w_in[l], tm=PROJ_TM,
                                         tiles_per_seq=max(dec_seq // PROJ_TM, 1), col=col)
    proj_p, tail_p, k_raw, v_raw = _proj(xp, mod_p, norm_mix_g[l], w_in_b, tm=PROJ_TM,
                                         tiles_per_seq=max(seq // PROJ_TM, 1), col=col)

    def lanes(s):
        return s.transpose(0, 2, 1, 3).reshape(dec_batch, groups, 2 * SSM_STATE)
    h0_p = jnp.zeros((batch, groups * 4 * SSM_STATE), F32)
    h0_s = _slab_lanes(jnp.concatenate([lanes(state_ssm_re[:, l]), lanes(state_ssm_im[:, l])], axis=-1))
    y_ssm_p, h_fin, (w_up_b,) = _ssm(tail_p, 0, ssm_ops, h0_p, n_seq=batch, seq_len=seq,
                                     seq_block=min(batch, SSM_SEQ_BLOCK), riders=(w_up[l],))
    y_ssm_s, _, (w_down_b,) = _ssm(tail_s, 0, ssm_ops, h0_s, n_seq=dec_batch, seq_len=dec_seq,
                                   seq_block=min(dec_batch, SSM_SEQ_BLOCK), riders=(w_down[l],))

    sink = attn_sink[l]
    o_p, mixer_w = _attn_ctx(proj_p, sink, n_seq=batch, seq_len=seq, col=col,
                             riders=(w_glu[l], w_ssm_o[l], w_attn_o[l], w_out[l]))
    o_s = _attn_lat(proj_s, cache_k[:, l].reshape(dec_batch, -1, d_kv), cache_v[:, l].reshape(dec_batch, -1, d_kv),
                    sink, n_seq=dec_batch, seq_len=dec_seq, col=col)

    x1_p, xm2_p = _mixer_out(xp, y_ssm_p, o_p, proj_p, mod_p, norm_ffn_g[l], *mixer_w, tm=MIXER_TM,
                             tiles_per_seq=max(seq // MIXER_TM, 1), col=col)
    x1_s, xm2_s = _mixer_out(xs, y_ssm_s, o_s, proj_s, mod_s, norm_ffn_g[l], *mixer_w, tm=MIXER_TM,
                             tiles_per_seq=max(dec_seq // MIXER_TM, 1), col=col)
    y_p = _ffn(x1_p, xm2_p, mod_p, w_up_b, conv_w[l], conv_b[l], w_down_b, final_norm_g, tm=FFN_TM, seq_len=seq)
    y_s = _ffn(x1_s, xm2_s, mod_s, w_up_b, conv_w[l], conv_b[l], w_down_b, final_norm_g, tm=FFN_TM, seq_len=dec_seq)

    new_k = k_raw.reshape(batch, 1, seq, N_KV_HEADS, HEAD_DIM)
    new_v = v_raw.reshape(batch, 1, seq, N_KV_HEADS, HEAD_DIM)

    def unlanes(hl):
        return hl.reshape(batch, groups, 2, SSM_STATE).transpose(0, 2, 1, 3)[:, None]
    h_fin = _unslab_lanes(h_fin, groups)
    new_re = unlanes(h_fin[:, :, :2 * SSM_STATE])
    new_im = unlanes(h_fin[:, :, 2 * SSM_STATE:])
    return (y_p.reshape(batch, seq, d), y_s.reshape(dec_batch, dec_seq, d), new_k, new_v, new_re, new_im)
```

```python
import functools
import math

import jax
import jax.numpy as jnp
import numpy as np
from jax import lax
from jax.experimental import pallas as pl
from jax.experimental.pallas import tpu as pltpu

F32 = jnp.float32
BF16 = jnp.bfloat16

GRID_W = 64
SSM_CH = 16
SSM_STATE = 64
N_HEADS = 8
N_KV_HEADS = 2
HEAD_DIM = 128
Q_PER_KV = N_HEADS // N_KV_HEADS
WINDOW = 128
BLOCK = 128
ROPE_THETA = 10000.0
EPS = 1e-6
NEG_INF = -1e30

SSM_CHUNK = 16
SSM_LANES = SSM_CHUNK * SSM_CH
SSM_SEQ_BLOCK = 16

V7X_VMEM_LIMIT_BYTES = 58 * 1024 * 1024
V7X_MXU_WIDTH = 256


def _cparams(semantics):
    return pltpu.CompilerParams(dimension_semantics=semantics, vmem_limit_bytes=V7X_VMEM_LIMIT_BYTES)


def _sigmoid(x):
    return 1.0 / (1.0 + jnp.exp(-x))


def _ride_casts(body, n_in, n_out, n_riders):
    def kernel(*refs):
        ins = refs[:n_in]
        rider_ins = refs[n_in:n_in + n_riders]
        outs = refs[n_in + n_riders:n_in + n_riders + n_out]
        rider_outs = refs[n_in + n_riders + n_out:n_in + 2 * n_riders + n_out]
        scratch = refs[n_in + 2 * n_riders + n_out:]
        for src, dst in zip(rider_ins, rider_outs):
            dst[...] = src[...].astype(BF16)
        body(*ins, *outs, *scratch)
    return kernel


def _rider_specs(weights, n_steps, step_of):
    specs, shapes = [], []
    for w in weights:
        r, c = w.shape
        assert r % (16 * n_steps) == 0
        specs.append(pl.BlockSpec((r // n_steps, c), lambda *idx: (step_of(*idx), 0)))
        shapes.append(jax.ShapeDtypeStruct((r, c), BF16))
    return specs, shapes


def _rms_modulate(x, g, scale, shift):
    ms = jnp.mean(x * x, axis=-1, keepdims=True)
    return (x * lax.rsqrt(ms + EPS) * g) * (1.0 + scale) + shift


MOD_K_CHUNK = 256


def _mod_kernel(c_ref, w_ref, b_ref, o_ref):
    c = c_ref[...]
    a = (c * _sigmoid(c)).astype(BF16)
    acc = b_ref[...]
    for k in range(0, w_ref.shape[0], MOD_K_CHUNK):
        acc = acc + jnp.dot(a[:, k:k + MOD_K_CHUNK], w_ref[k:k + MOD_K_CHUNK, :].astype(BF16),
                            preferred_element_type=F32)
    o_ref[...] = acc


PROJ_TN = 512
NORM_ROWS = 16
NORM_UNROLL = 16


def _swap32(x):
    n = x.shape[-1]
    lane = lax.broadcasted_iota(jnp.int32, x.shape, x.ndim - 1)
    return jnp.where((lane & 63) < 32, pltpu.roll(x, n - 32, x.ndim - 1), pltpu.roll(x, 32, x.ndim - 1))


def _proj_kernel(x_ref, mod_ref, g_ref, w_ref, o_ref, tail_ref, k_ref, v_ref, *rest, q_tiles):
    xm_scr = rest[-1]
    j = pl.program_id(1)

    @pl.when(j == 0)
    def _():
        gain = g_ref[...] * (1.0 + mod_ref[1:2, :])
        shift = mod_ref[0:1, :]

        def block(c, carry):
            rows = pl.ds(pl.multiple_of(c * NORM_ROWS, NORM_ROWS), NORM_ROWS)
            x = x_ref[rows, :]
            ms = jnp.mean(x * x, axis=-1, keepdims=True)
            xm_scr[rows, :] = (x * lax.rsqrt(ms + EPS) * gain + shift).astype(BF16)
            return carry

        lax.fori_loop(0, x_ref.shape[0] // NORM_ROWS, block, 0, unroll=NORM_UNROLL)

    w = w_ref[...]
    if len(rest) == 2:
        w = w.astype(BF16)
        rest[0][...] = w
    acc = jnp.dot(xm_scr[...], w, preferred_element_type=F32)
    scale = jnp.where((j >= q_tiles[0]) & (j < q_tiles[1]), HEAD_DIM ** -0.5, 1.0)
    o_ref[...] = (acc * scale).astype(BF16)
    tail_ref[...] = acc
    rows = acc.shape[0]
    for h in range(N_KV_HEADS):
        k_ref[pl.ds(h, rows, stride=N_KV_HEADS), :] = acc[:, h * HEAD_DIM:(h + 1) * HEAD_DIM]
        v_ref[pl.ds(h, rows, stride=N_KV_HEADS), :] = acc[:, (N_KV_HEADS + h) * HEAD_DIM:(N_KV_HEADS + h + 1) * HEAD_DIM]


def _proj(x, mod, norm_g, w_in, *, tm, tiles_per_seq, col):
    n, d = x.shape
    cols = w_in.shape[1]
    tn = PROJ_TN
    n_tiles = cols // tn
    d_kv = col["d_kv"]
    n_gate, n_q, n_u = col["q"] // tn, col["d_attn"] // tn, col["d_ssm"] // tn
    q_tiles = (n_gate, n_gate + n_q)
    tail0 = col["u"] // tn
    n_tail = n_tiles - tail0
    assert 2 * d_kv == tn and col["u"] == col["q"] + col["d_attn"] and col["k"] == cols - tn

    def src_tile(j):
        return jnp.where(j < n_gate, j + (n_u + n_q + 1),
                         jnp.where(j < n_gate + n_q, j - n_gate + n_u,
                                   jnp.where(j < n_gate + n_q + n_u, j - (n_gate + n_q), n_u + n_q)))
    seq_of = (lambda i: i // tiles_per_seq) if mod.shape[0] > 1 else (lambda i: 0)
    out_specs = [pl.BlockSpec((tm, tn), lambda i, j: (i, j)),
                 pl.BlockSpec((tm, tn), lambda i, j: (i, jnp.clip(j - tail0, 0, n_tail - 1))),
                 pl.BlockSpec((tm * N_KV_HEADS, HEAD_DIM), lambda i, j: (i, 0)),
                 pl.BlockSpec((tm * N_KV_HEADS, HEAD_DIM), lambda i, j: (i, 0))]
    out_shape = [jax.ShapeDtypeStruct((n, cols), BF16), jax.ShapeDtypeStruct((n, n_tail * tn), F32),
                 jax.ShapeDtypeStruct((n * N_KV_HEADS, HEAD_DIM), F32),
                 jax.ShapeDtypeStruct((n * N_KV_HEADS, HEAD_DIM), F32)]
    if w_in.dtype != BF16:
        out_specs.append(pl.BlockSpec((d, tn), lambda i, j: (0, src_tile(jnp.where(i == 0, j, n_tiles - 1)))))
        out_shape.append(jax.ShapeDtypeStruct((d, cols), BF16))
    return pl.pallas_call(
        functools.partial(_proj_kernel, q_tiles=q_tiles),
        grid=(n // tm, n_tiles),
        in_specs=[pl.BlockSpec((tm, d), lambda i, j: (i, 0)),
                  pl.BlockSpec((None, 6, d), lambda i, j: (seq_of(i), 0, 0)),
                  pl.BlockSpec((1, d), lambda i, j: (0, 0)),
                  pl.BlockSpec((d, tn), lambda i, j: (0, src_tile(j)))],
        out_specs=out_specs,
        out_shape=out_shape,
        scratch_shapes=[pltpu.VMEM((tm, d), BF16)],
        compiler_params=_cparams(("arbitrary", "arbitrary")),
        name="proj",
    )(x, mod, norm_g.reshape(1, d), w_in)


def _cmul(ar, ai, br, bi):
    return ar * br - ai * bi, ar * bi + ai * br


def _cpow(base_r, base_i, n, bits):
    res_r = jnp.ones_like(base_r)
    res_i = jnp.zeros_like(base_r)
    for bit in range(bits):
        nr, ni = _cmul(res_r, res_i, base_r, base_i)
        take = ((n >> bit) & 1) == 1
        res_r = jnp.where(take, nr, res_r)
        res_i = jnp.where(take, ni, res_i)
        if bit + 1 < bits:
            base_r, base_i = _cmul(base_r, base_i, base_r, base_i)
    return res_r, res_i


def _ssm_disc_kernel(lre_ref, lim_ref, ldt_ref, ere_ref, eim_ref, qre_ref, qim_ref):
    lre = lre_ref[...]
    lim = lim_ref[...]
    dt = jnp.exp(ldt_ref[...])
    mag = jnp.exp(lre * dt)
    e_re = mag * jnp.cos(lim * dt)
    e_im = mag * jnp.sin(lim * dt)
    n_re = e_re - 1.0
    den = lre * lre + lim * lim
    ere_ref[...] = e_re
    eim_ref[...] = e_im
    qre_ref[...] = (n_re * lre + e_im * lim) / den
    qim_ref[...] = (e_im * lre - n_re * lim) / den


def _ssm_disc(lam_re, lam_im, log_dt):
    two, g, p = lam_re.shape
    shp = jax.ShapeDtypeStruct((two * g, p), F32)
    outs = pl.pallas_call(_ssm_disc_kernel, out_shape=[shp] * 4, name="ssm_disc")(
        lam_re.reshape(two * g, p), lam_im.reshape(two * g, p), log_dt.reshape(two * g, 1))
    return [o.reshape(two, g, p) for o in outs]


SSM_OPS_GROUPS = 8


def _ssm_ops_mod_kernel(ecol_ref, rows_ref, b_ref, ct_ref, d_ref, c_ref, wm_ref, bm_ref,
                        m_ref, w_ref, v_ref, a_ref, mod_ref):
    _mod_kernel(c_ref, wm_ref, bm_ref, mod_ref)
    _ssm_ops_kernel(ecol_ref, rows_ref, b_ref, ct_ref, d_ref, m_ref, w_ref, v_ref, a_ref)


def _ssm_ops_kernel(ecol_ref, rows_ref, b_ref, ct_ref, d_ref, m_ref, w_ref, v_ref, a_ref):
    def one(gi, carry):
        _ssm_ops_group(ecol_ref.at[gi], rows_ref.at[gi], b_ref.at[gi], ct_ref.at[gi], d_ref.at[gi],
                       m_ref.at[gi], w_ref.at[gi], v_ref.at[gi], a_ref.at[gi])
        return carry

    lax.fori_loop(0, ecol_ref.shape[0], one, 0, unroll=True)


def _ssm_ops_group(ecol_ref, rows_ref, b_ref, ct_ref, d_ref, m_ref, w_ref, v_ref, a_ref):
    L, ch, p = SSM_CHUNK, SSM_CH, SSM_STATE
    lanes = SSM_LANES
    half = 2 * p
    lane_c = lax.broadcasted_iota(jnp.int32, (p, lanes), 1)
    lag = lane_c // ch

    def col(k):
        return jnp.broadcast_to(ecol_ref[:, k:k + 1], (p, lanes))

    f0 = _cpow(col(0), col(1), lag, 4)
    b0 = _cpow(col(2), col(3), (L - 1) - lag, 4)
    f1 = _cmul(f0[0], f0[1], col(0), col(1))
    b1 = _cmul(b0[0], b0[1], col(2), col(3))
    def ct(k):
        return jnp.tile(ct_ref[:, k * ch:(k + 1) * ch], (1, L))

    cf = (ct(0), ct(1))
    cb = (ct(2), ct(3))
    yf0 = _cmul(*f0, *cf)
    yb0 = _cmul(*b0, *cb)
    yf1 = _cmul(*f1, *cf)
    yb1 = _cmul(*b1, *cb)
    v_ref[...] = jnp.concatenate([yf1[0], yb1[0], -yf1[1], -yb1[1]], axis=0).astype(BF16)

    lane_r = lax.broadcasted_iota(jnp.int32, (ch, lanes), 1)
    sgn = jnp.where(lane_r < half, -1.0, 1.0)
    b1raw = b_ref[...]
    b2raw = pltpu.roll(b1raw, half, 1) * sgn
    bb1 = rows_ref[2:3, :] * b1raw + rows_ref[3:4, :] * b2raw
    bb2 = pltpu.roll(bb1, half, 1) * sgn

    zero = jnp.zeros((p, lanes), F32)
    hi = lax.Precision.HIGHEST
    r0f = jnp.dot(bb1, jnp.concatenate([yf0[0], zero, -yf0[1], zero], axis=0), precision=hi,
                  preferred_element_type=F32)
    r0b = jnp.dot(bb1, jnp.concatenate([zero, yb0[0], zero, -yb0[1]], axis=0), precision=hi,
                  preferred_element_type=F32)
    row_r = lax.broadcasted_iota(jnp.int32, (ch, lanes), 0)
    d_col = d_ref[...]
    for s in range(L):
        blk = jnp.where(lane_r == ch * s + row_r, d_col, 0.0)
        fwd = r0f if s == 0 else jnp.where(lane_r >= ch * s, pltpu.roll(r0f, ch * s, 1), 0.0)
        k = ch * (L - 1 - s)
        bwd = r0b if k == 0 else jnp.where(lane_r < lanes - k, pltpu.roll(r0b, lanes - k, 1), 0.0)
        m_ref[s * ch:(s + 1) * ch, :] = (blk + fwd + bwd).astype(BF16)

    is_f = (lane_r & (half - 1)) < p
    n_row = jnp.where(is_f, (L - 1) - row_r, row_r)
    er, ei = _cpow(jnp.broadcast_to(rows_ref[0:1, :], (L, lanes)), jnp.broadcast_to(rows_ref[1:2, :], (L, lanes)),
                   n_row, 4)
    for s in range(L):
        w_ref[s * ch:(s + 1) * ch, :] = (er[s:s + 1, :] * bb1 + ei[s:s + 1, :] * bb2).astype(BF16)

    ar, ai = rows_ref[0:1, :], rows_ref[1:2, :]
    for _ in range(4):
        ar, ai = _cmul(ar, ai, ar, ai)
    lane_1 = lax.broadcasted_iota(jnp.int32, (1, lanes), 1)
    a_ref[...] = jnp.where(lane_1 < half, ar, ai)


def _ssm_operators_and_mod(lam_re, lam_im, log_dt, b_re, b_im, c_re, c_im, ssm_d, cond, w_mod, b_mod):
    two, g, p = lam_re.shape
    ch = b_re.shape[-1]
    lanes = SSM_LANES
    assert (two, p, ch, SSM_CHUNK * ch) == (2, SSM_STATE, SSM_CH, lanes)
    e_re, e_im, q_re, q_im = _ssm_disc(lam_re, lam_im, log_dt)
    ecol = jnp.stack([e_re[0], e_im[0], e_re[1], e_im[1]], axis=-1)

    def fbfb(x):
        return jnp.concatenate([x[0], x[1], x[0], x[1]], axis=-1)
    rows = jnp.stack([fbfb(e_re), fbfb(e_im), fbfb(q_re), fbfb(q_im)], axis=1)

    def bt(x):
        return jnp.swapaxes(x, -1, -2)
    b_rows = jnp.concatenate([bt(b_re[0]), bt(b_re[1]), bt(b_im[0]), bt(b_im[1])], axis=-1)

    ct = jnp.concatenate([bt(c_re[0]), bt(c_im[0]), bt(c_re[1]), bt(c_im[1])], axis=-1)
    d_col = ssm_d.reshape(g, ch, 1)
    gb = SSM_OPS_GROUPS
    mat = pl.BlockSpec((gb, lanes, lanes), lambda i: (i, 0, 0))
    steps = g // gb
    c_rows, d_model = cond.shape
    n_mod = w_mod.shape[1]
    tn = n_mod // steps
    assert n_mod % steps == 0 and tn % 128 == 0
    outs = pl.pallas_call(
        _ssm_ops_mod_kernel,
        grid=(steps,),
        in_specs=[pl.BlockSpec((gb, p, 4), lambda i: (i, 0, 0)),
                  pl.BlockSpec((gb, 4, lanes), lambda i: (i, 0, 0)),
                  pl.BlockSpec((gb, ch, lanes), lambda i: (i, 0, 0)),
                  pl.BlockSpec((gb, p, 4 * ch), lambda i: (i, 0, 0)),
                  pl.BlockSpec((gb, ch, 1), lambda i: (i, 0, 0)),
                  pl.BlockSpec((c_rows, d_model), lambda i: (0, 0)),
                  pl.BlockSpec((d_model, tn), lambda i: (0, i)),
                  pl.BlockSpec((1, tn), lambda i: (0, i))],
        out_specs=[mat, mat, mat, pl.BlockSpec((gb, 1, lanes), lambda i: (i, 0, 0)),
                   pl.BlockSpec((c_rows, tn), lambda i: (0, i))],
        out_shape=[jax.ShapeDtypeStruct((g, lanes, lanes), BF16)] * 3 + [jax.ShapeDtypeStruct((g, 1, lanes), F32),
                                                                         jax.ShapeDtypeStruct((c_rows, n_mod), F32)],
        compiler_params=_cparams(("arbitrary",)),
        name="ssm_ops_mod",
    )(ecol, rows, b_rows, ct, d_col, cond, w_mod, b_mod.reshape(1, n_mod))
    return outs[:4], outs[4]


SSM_SLAB_GROUPS = 128 // SSM_CH


def _chunk_transpose(xs, chunk_id):
    for k in (4, 2, 1):
        keep = (chunk_id & k) == 0
        new = list(xs)
        for i in range(len(xs)):
            if i & k == 0:
                a, b = xs[i], xs[i + k]
                if 2 * SSM_CH * k == 128:
                    moved = pltpu.roll(jnp.where(keep, b, a), SSM_CH * k, 1)
                    new[i] = jnp.where(keep, a, moved)
                    new[i + k] = jnp.where(keep, moved, b)
                else:
                    new[i] = jnp.where(keep, a, pltpu.roll(b, SSM_CH * k, 1))
                    new[i + k] = jnp.where(keep, pltpu.roll(a, 128 - SSM_CH * k, 1), b)
        xs = new
    return xs


def _ssm_kernel(u_ref, m_ref, w_ref, v_ref, a_ref, h0_ref, y_ref, hfin_ref, ug_scr, s_scr, hf_scr, hb_scr, yg_scr,
                *, n_chunks, ns):
    gps = SSM_SLAB_GROUPS
    nc = ns * n_chunks
    half = 2 * SSM_STATE
    chunk_id = lax.broadcasted_iota(jnp.int32, (nc, 128), 1) // SSM_CH
    for hf in range(2):
        xs = [u_ref[pl.ds(hf * gps + i, nc, stride=SSM_CHUNK), :] for i in range(gps)]
        xs = _chunk_transpose(xs, chunk_id)
        for g in range(gps):
            ug_scr[g, :, hf * 128:(hf + 1) * 128] = xs[g].astype(BF16)
    pitch = n_chunks + 8
    for g in range(gps):
        s = jnp.dot(ug_scr[g], w_ref[g], preferred_element_type=F32)
        for q in range(ns):
            s_scr[g, q * pitch:q * pitch + n_chunks, :] = s[q * n_chunks:(q + 1) * n_chunks, :half]
            s_scr[gps + g, q * pitch:q * pitch + n_chunks, :] = s[q * n_chunks:(q + 1) * n_chunks, half:]
    is_fwd = lax.broadcasted_iota(jnp.int32, (1, half), 1) < SSM_STATE
    blocks = range(2 * gps)
    a = [a_ref[:, b * half:(b + 1) * half] for b in blocks]
    h = [h0_ref[:, b * half:(b + 1) * half] for b in blocks]

    def rows(c):
        return pl.ds(c, ns, stride=pitch)

    for k in range(n_chunks):
        kb = n_chunks - 1 - k
        new_h = list(h)
        for b in blocks:
            hf_scr[b, rows(k), :] = h[b]
            hb_scr[b, rows(kb), :] = h[b]
        for g in range(gps):
            s_re = jnp.where(is_fwd, s_scr[g, rows(k), :], s_scr[g, rows(kb), :])
            s_im = jnp.where(is_fwd, s_scr[gps + g, rows(k), :], s_scr[gps + g, rows(kb), :])
            new_h[g] = a[g] * h[g] - a[gps + g] * h[gps + g] + s_re
            new_h[gps + g] = a[g] * h[gps + g] + a[gps + g] * h[g] + s_im
        h = new_h
    for b in blocks:
        hfin_ref[:, b * half:(b + 1) * half] = h[b]

    def entering(b):
        return jnp.concatenate(
            [jnp.where(is_fwd, hf_scr[b, q * pitch:q * pitch + n_chunks, :], hb_scr[b, q * pitch:q * pitch + n_chunks, :])
             for q in range(ns)], axis=0)

    for g in range(gps):
        hin = jnp.concatenate([entering(g), entering(gps + g)], axis=1).astype(BF16)
        yg_scr[g] = (jnp.dot(ug_scr[g], m_ref[g], preferred_element_type=F32)
                     + jnp.dot(hin, v_ref[g], preferred_element_type=F32))
    for hf in range(2):
        xs = [yg_scr[g, :, hf * 128:(hf + 1) * 128] for g in range(gps)]
        xs = _chunk_transpose(xs, chunk_id)
        for i in range(gps):
            y_ref[pl.ds(hf * gps + i, nc, stride=SSM_CHUNK), :] = xs[i]


def _slab_lanes(x):
    lead = x.shape[:-2]
    g = x.shape[-2]
    gps = SSM_SLAB_GROUPS
    x = x.reshape(lead + (g // gps, gps, 2, 2 * SSM_STATE))
    x = jnp.swapaxes(x, -3, -2)
    return x.reshape(lead + (g * 4 * SSM_STATE,))


def _unslab_lanes(x, g):
    lead = x.shape[:-1]
    gps = SSM_SLAB_GROUPS
    x = x.reshape(lead + (g // gps, 2, gps, 2 * SSM_STATE))
    x = jnp.swapaxes(x, -3, -2)
    return x.reshape(lead + (g, 4 * SSM_STATE))


def _ssm(u, u_col, ops, h0, *, n_seq, seq_len, seq_block, riders=()):
    m_op, w_op, v_op, a_op = ops
    n = u.shape[0]
    g = m_op.shape[0]
    d_ssm = g * SSM_CH
    slab0 = u_col // 128
    gps = SSM_SLAB_GROUPS
    lanes = SSM_LANES
    n_chunks = seq_len // SSM_CHUNK
    nc = seq_block * n_chunks
    padded = seq_block * (n_chunks + 8)
    rows = seq_block * seq_len
    wide2 = gps * 4 * SSM_STATE
    mat = pl.BlockSpec((gps, lanes, lanes), lambda j, b: (j, 0, 0))
    n_blocks = n_seq // seq_block
    rider_specs, rider_shapes = _rider_specs(riders, (g // gps) * n_blocks, lambda j, b: j * n_blocks + b)
    outs = pl.pallas_call(
        _ride_casts(functools.partial(_ssm_kernel, n_chunks=n_chunks, ns=seq_block), 6, 2, len(riders)),
        grid=(g // gps, n_blocks),
        in_specs=[pl.BlockSpec((rows, 128), lambda j, b: (b, slab0 + j)), mat, mat, mat,
                  pl.BlockSpec((1, wide2), lambda j, b: (0, j)),
                  pl.BlockSpec((seq_block, wide2), lambda j, b: (b, j))] + rider_specs,
        out_specs=[pl.BlockSpec((rows, 128), lambda j, b: (b, j)),
                   pl.BlockSpec((seq_block, wide2), lambda j, b: (b, j))] + rider_specs,
        out_shape=[jax.ShapeDtypeStruct((n, d_ssm), F32),
                   jax.ShapeDtypeStruct((n_seq, g * 4 * SSM_STATE), F32)] + rider_shapes,
        scratch_shapes=[pltpu.VMEM((gps, nc, lanes), BF16), pltpu.VMEM((2 * gps, padded, 128), F32),
                        pltpu.VMEM((2 * gps, padded, 128), F32), pltpu.VMEM((2 * gps, padded, 128), F32),
                        pltpu.VMEM((gps, nc, lanes), F32)],
        compiler_params=_cparams(("arbitrary", "arbitrary")),
        name="ssm",
    )(u, m_op, w_op, v_op, _slab_lanes(a_op[:, 0])[None], h0, *riders)
    return outs[0], outs[1], outs[2:]


def _softmax_pv(pieces, sink):
    m = sink
    for s, _ in pieces:
        m = jnp.maximum(m, jnp.max(s, axis=-1, keepdims=True))
    out = None
    for s, v in pieces:
        pv = jnp.dot(jnp.exp(s - m).astype(BF16), v, preferred_element_type=F32)
        out = pv if out is None else out + pv
    return out[:, :HEAD_DIM] / (out[:, HEAD_DIM:] + jnp.exp(sink - m))


def _qk(q, k):
    return lax.dot_general(q, k, (((1,), (1,)), ((), ())), preferred_element_type=F32)


def _with_ones(v):
    return jnp.concatenate([v, jnp.ones_like(v)], axis=1)


ATTN_CTX_SEQS = 8


def _attn_ctx_kernel(sink_ref, q_ref, kv_ref, o_ref, *, seq_len):
    d_kv = N_KV_HEADS * HEAD_DIM
    for r0 in range(0, q_ref.shape[0], seq_len):
        rows = slice(r0, r0 + seq_len)
        for kh in range(N_KV_HEADS):
            k = kv_ref[rows, kh * HEAD_DIM:(kh + 1) * HEAD_DIM]
            v = _with_ones(kv_ref[rows, d_kv + kh * HEAD_DIM:d_kv + (kh + 1) * HEAD_DIM])
            for h in range(kh * Q_PER_KV, (kh + 1) * Q_PER_KV):
                q = q_ref[rows, h * HEAD_DIM:(h + 1) * HEAD_DIM]
                o = _softmax_pv([(_qk(q, k), v)], sink_ref[h])
                o_ref[rows, h * HEAD_DIM:(h + 1) * HEAD_DIM] = o.astype(BF16)


def _attn_ctx(proj, sink, *, n_seq, seq_len, col, riders=()):
    d_attn = col["d_attn"]
    kv_w = 2 * col["d_kv"]
    spb = ATTN_CTX_SEQS if n_seq % ATTN_CTX_SEQS == 0 else 1
    rows = spb * seq_len
    rider_specs, rider_shapes = _rider_specs(riders, n_seq // spb, lambda b: b)
    outs = pl.pallas_call(
        _ride_casts(functools.partial(_attn_ctx_kernel, seq_len=seq_len), 3, 1, len(riders)),
        grid=(n_seq // spb,),
        in_specs=[pl.BlockSpec(memory_space=pltpu.SMEM),
                  pl.BlockSpec((rows, d_attn), lambda b: (b, col["q"] // d_attn)),
                  pl.BlockSpec((rows, kv_w), lambda b: (b, col["k"] // kv_w))] + rider_specs,
        out_specs=[pl.BlockSpec((rows, d_attn), lambda b: (b, 0))] + rider_specs,
        out_shape=[jax.ShapeDtypeStruct((n_seq * seq_len, d_attn), BF16)] + rider_shapes,
        compiler_params=_cparams(("arbitrary",)),
        name="attn_ctx",
    )(sink, proj, proj, *riders)
    return outs[0], outs[1:]


def _rope(x, cos, sin):
    reps = x.shape[1] // HEAD_DIM
    xf = x.astype(F32)
    return (xf * jnp.concatenate([cos] * reps, axis=1) + _swap32(xf) * jnp.concatenate([sin] * reps, axis=1)).astype(BF16)


ATTN_LAT_QBLOCKS = 4


def _attn_lat_kernel(*refs, n_blocks, qb):
    sink_ref, q_ref = refs[:2]
    kv_refs = refs[2:4 + qb]
    ck_ref, cv_ref, cos_ref, sin_ref, o_ref = refs[4 + qb:]
    first = pl.program_id(1) * qb
    d_kv = N_KV_HEADS * HEAD_DIM

    def tables(blk):
        rows = pl.ds(pl.multiple_of(blk * BLOCK, BLOCK), BLOCK)
        return cos_ref[rows, :], sin_ref[rows, :]

    k_rot = [_rope(kv_refs[n][:, :d_kv], *tables(jnp.clip(first + n - 1, 0, n_blocks - 1))) for n in range(qb + 2)]
    r = lax.broadcasted_iota(jnp.int32, (BLOCK, 3 * BLOCK), 0)
    c = lax.broadcasted_iota(jnp.int32, (BLOCK, 3 * BLOCK), 1)
    cc = c & (BLOCK - 1)
    for sub in range(qb):
        blk = first + sub
        q_rows = slice(sub * BLOCK, (sub + 1) * BLOCK)
        q_all = _rope(q_ref[q_rows, :], *tables(blk))
        k_all = jnp.concatenate(k_rot[sub:sub + 3], axis=0)
        valid = (((c < BLOCK) & (cc >= r) & (blk > 0)) | ((c >= BLOCK) & (c < 2 * BLOCK))
                 | ((c >= 2 * BLOCK) & (cc <= r) & (blk < n_blocks - 1)))
        for kh in range(N_KV_HEADS):
            ks = slice(kh * HEAD_DIM, (kh + 1) * HEAD_DIM)
            vs = slice(d_kv + kh * HEAD_DIM, d_kv + (kh + 1) * HEAD_DIM)
            k_loc = k_all[:, ks]
            k_ctx = ck_ref[:, ks].astype(BF16)
            v_loc = _with_ones(jnp.concatenate([kv_refs[sub + n][:, vs] for n in range(3)], axis=0))
            v_ctx = _with_ones(cv_ref[:, ks].astype(BF16))
            for h in range(kh * Q_PER_KV, (kh + 1) * Q_PER_KV):
                q = q_all[:, h * HEAD_DIM:(h + 1) * HEAD_DIM]
                s_loc = jnp.where(valid, _qk(q, k_loc), NEG_INF)
                o = _softmax_pv([(s_loc, v_loc), (_qk(q, k_ctx), v_ctx)], sink_ref[h])
                o_ref[q_rows, h * HEAD_DIM:(h + 1) * HEAD_DIM] = o.astype(BF16)


def _attn_lat(proj, cache_k, cache_v, sink, *, n_seq, seq_len, col):
    d_attn = col["d_attn"]
    kv_w = 2 * col["d_kv"]
    assert WINDOW == BLOCK and seq_len % BLOCK == 0
    nb = seq_len // BLOCK
    qb = ATTN_LAT_QBLOCKS if nb % ATTN_LAT_QBLOCKS == 0 else 1
    steps = nb // qb
    kv_col = col["k"] // kv_w
    past, d_kv = cache_k.shape[1], cache_k.shape[2]

    def kv_spec(off):
        return pl.BlockSpec((BLOCK, kv_w), lambda b, i: (b * nb + jnp.clip(i * qb + off, 0, nb - 1), kv_col))
    cache_spec = pl.BlockSpec((None, past, d_kv), lambda b, i: (b, 0, 0))
    table_spec = pl.BlockSpec((seq_len, HEAD_DIM), lambda b, i: (0, 0))
    cos, sin = _rope_tables(seq_len)
    return pl.pallas_call(
        functools.partial(_attn_lat_kernel, n_blocks=nb, qb=qb),
        grid=(n_seq, steps),
        in_specs=[pl.BlockSpec(memory_space=pltpu.SMEM),
                  pl.BlockSpec((qb * BLOCK, d_attn), lambda b, i: (b * steps + i, col["q"] // d_attn))]
                 + [kv_spec(off) for off in range(-1, qb + 1)]
                 + [cache_spec, cache_spec, table_spec, table_spec],
        out_specs=pl.BlockSpec((qb * BLOCK, d_attn), lambda b, i: (b * steps + i, 0)),
        out_shape=jax.ShapeDtypeStruct((n_seq * seq_len, d_attn), BF16),
        compiler_params=_cparams(("arbitrary", "arbitrary")),
        name="attn_lat",
    )(sink, proj, *([proj] * (qb + 2)), cache_k, cache_v, cos, sin)


def _gelu_tanh(x):
    return 0.5 * x * (1.0 + jnp.tanh(math.sqrt(2.0 / math.pi) * (x + 0.044715 * (x * x * x))))


def _mixer_out_kernel(x_ref, y_ref, o_ref, gs_ref, ga_ref, mod_ref, g_ref, wglu_ref, wso_ref, wao_ref, wout_ref,
                      x1_ref, xm2_ref):
    z = _gelu_tanh(y_ref[...].astype(F32))
    z = z * _sigmoid(jnp.dot(z.astype(BF16), wglu_ref[...], preferred_element_type=F32))
    s_br = jnp.dot(z.astype(BF16), wso_ref[...], preferred_element_type=F32)
    a_br = jnp.dot(o_ref[...], wao_ref[...], preferred_element_type=F32)
    merged = _sigmoid(gs_ref[...].astype(F32)) * s_br + _sigmoid(ga_ref[...].astype(F32)) * a_br
    out = jnp.dot(merged.astype(BF16), wout_ref[...], preferred_element_type=F32)
    x1 = x_ref[...] + mod_ref[2:3, :] * out
    x1_ref[...] = x1
    xm2_ref[...] = _rms_modulate(x1, g_ref[...], mod_ref[4:5, :], mod_ref[3:4, :]).astype(BF16)


def _mixer_out(x, y, o, proj, mod, norm_g, w_glu, w_ssm_o, w_attn_o, w_out, *, tm, tiles_per_seq, col):
    n, d = x.shape
    d_ssm = y.shape[1]
    d_attn = o.shape[1]
    seq_of = (lambda i: i // tiles_per_seq) if mod.shape[0] > 1 else (lambda i: 0)

    def resident(shape):
        return pl.BlockSpec(shape, lambda i: (0, 0), pipeline_mode=pl.Buffered(1))
    return pl.pallas_call(
        _mixer_out_kernel,
        grid=(n // tm,),
        in_specs=[pl.BlockSpec((tm, d), lambda i: (i, 0)),
                  pl.BlockSpec((tm, d_ssm), lambda i: (i, 0)),
                  pl.BlockSpec((tm, d_attn), lambda i: (i, 0)),
                  pl.BlockSpec((tm, d), lambda i: (i, col["gs"] // d)),
                  pl.BlockSpec((tm, d), lambda i: (i, col["ga"] // d)),
                  pl.BlockSpec((None, 6, d), lambda i: (seq_of(i), 0, 0)),
                  pl.BlockSpec((1, d), lambda i: (0, 0)),
                  resident(w_glu.shape), resident(w_ssm_o.shape), resident(w_attn_o.shape), resident(w_out.shape)],
        out_specs=[pl.BlockSpec((tm, d), lambda i: (i, 0)), pl.BlockSpec((tm, d), lambda i: (i, 0))],
        out_shape=[jax.ShapeDtypeStruct((n, d), F32), jax.ShapeDtypeStruct((n, d), BF16)],
        compiler_params=_cparams(("arbitrary",)),
        name="mixer_out",
    )(x, y, o, proj, proj, mod, norm_g.reshape(1, d), w_glu, w_ssm_o, w_attn_o, w_out)


FFN_TM = 1024
FFN_TF = 512
FFN_TN = 256


def _ffn_kernel(xm_ref, wa_ref, wb_ref, cw_ref, cb_ref, wd_ref, x1_ref, mod_ref, g_ref, o_ref, act_scr, raw_scr,
                *, seq_len, nf, nn, d_ff):
    j = pl.program_id(1)
    tm = xm_ref.shape[0]
    tf = wa_ref.shape[1]
    tn = wd_ref.shape[1]
    ch = tf // 2
    pos = lax.broadcasted_iota(jnp.int32, (tm, 1), 0) & (seq_len - 1)
    has_prev = pos != 0
    has_next = pos != seq_len - 1

    def lanes(off):
        return slice(off, off + ch) if isinstance(off, int) else pl.ds(pl.multiple_of(off, ch), ch)

    def gate(ha, hb, col0):
        def conv(h, off):
            cols = lanes(off)
            cw = cw_ref[:, cols]
            h_prev = jnp.where(has_prev, pltpu.roll(h, 1, 0), 0.0)
            h_next = jnp.where(has_next, pltpu.roll(h, tm - 1, 0), 0.0)
            return cw[0:1] * h_prev + cw[1:2] * h + cw[2:3] * h_next + cb_ref[:, cols]

        a = conv(ha, col0)
        b = conv(hb, d_ff + col0)
        return ((a * _sigmoid(a)) * b).astype(BF16)

    def act_cols(col0):
        return lanes(tf + col0)

    @pl.when(j == 0)
    def _():
        raw_scr[...] = jnp.zeros_like(raw_scr)

    @pl.when(j < nf)
    def _():
        xm = xm_ref[...]
        ha0 = jnp.dot(xm, wa_ref[:, :ch], preferred_element_type=F32)
        hb0 = jnp.dot(xm, wb_ref[:, :ch], preferred_element_type=F32)
        carried = (j - 1) * tf + ch
        act_scr[:, act_cols(carried)] = gate(raw_scr[0], raw_scr[1], jnp.maximum(carried, 0))
        ha1 = jnp.dot(xm, wa_ref[:, ch:], preferred_element_type=F32)
        hb1 = jnp.dot(xm, wb_ref[:, ch:], preferred_element_type=F32)
        act_scr[:, act_cols(j * tf)] = gate(ha0, hb0, j * tf)
        raw_scr[0] = ha1
        raw_scr[1] = hb1

    @pl.when(j == nf)
    def _():
        last = (nf - 1) * tf + ch
        act_scr[:, act_cols(last)] = gate(raw_scr[0], raw_scr[1], last)

    @pl.when(j >= nf)
    def _():
        cols = pl.ds(pl.multiple_of((j - nf) * tn, tn), tn)
        ffn = jnp.dot(act_scr[:, tf:], wd_ref[...], preferred_element_type=F32)
        o_ref[:, cols] = x1_ref[...] + mod_ref[5:6, cols] * ffn

    @pl.when(j == nf + nn - 1)
    def _():
        x2 = o_ref[...]
        ms = jnp.mean(x2 * x2, axis=-1, keepdims=True)
        o_ref[...] = x2 * lax.rsqrt(ms + EPS) * g_ref[...]


def _ffn(x1, xm2, mod, w_up, conv_w, conv_b, w_down, final_g, *, tm, seq_len):
    n, d = x1.shape
    d_ff = w_down.shape[0]
    tf, tn = FFN_TF, FFN_TN
    nf, nn = d_ff // tf, d // tn
    assert tm % seq_len == 0 and seq_len & (seq_len - 1) == 0 and d_ff % tf == 0 and (tf // 2) % V7X_MXU_WIDTH == 0
    seqs_per_tile = tm // seq_len
    seq_of = (lambda i: i * seqs_per_tile) if mod.shape[0] > 1 else (lambda i: 0)
    assert mod.shape[0] == 1 or seqs_per_tile == 1
    cb = conv_b.reshape(1, 2 * d_ff)

    def up(j):
        return jnp.minimum(j, nf - 1)

    def down(j):
        return jnp.maximum(j - nf, 0)
    return pl.pallas_call(
        functools.partial(_ffn_kernel, seq_len=seq_len, nf=nf, nn=nn, d_ff=d_ff),
        grid=(n // tm, nf + nn),
        in_specs=[pl.BlockSpec((tm, d), lambda i, j: (i, 0), pipeline_mode=pl.Buffered(1)),
                  pl.BlockSpec((d, tf), lambda i, j: (0, up(j))),
                  pl.BlockSpec((d, tf), lambda i, j: (0, nf + up(j))),
                  pl.BlockSpec((3, 2 * d_ff), lambda i, j: (0, 0)),
                  pl.BlockSpec((1, 2 * d_ff), lambda i, j: (0, 0)),
                  pl.BlockSpec((d_ff, tn), lambda i, j: (0, down(j))),
                  pl.BlockSpec((tm, tn), lambda i, j: (i, down(j))),
                  pl.BlockSpec((None, 6, d), lambda i, j: (seq_of(i), 0, 0)),
                  pl.BlockSpec((1, d), lambda i, j: (0, 0))],
        out_specs=pl.BlockSpec((tm, d), lambda i, j: (i, 0)),
        out_shape=jax.ShapeDtypeStruct((n, d), F32),
        scratch_shapes=[pltpu.VMEM((tm, tf + d_ff), BF16), pltpu.VMEM((2, tm, tf // 2), F32)],
        compiler_params=_cparams(("arbitrary", "arbitrary")),
        name="ffn",
    )(xm2, w_up, w_up, conv_w, cb, w_down, x1, mod, final_g.reshape(1, d))


def _rope_tables(seq_len):
    rows = seq_len // GRID_W
    row = np.repeat(np.arange(rows, dtype=np.float64), GRID_W)
    colp = np.tile(np.arange(GRID_W, dtype=np.float64), rows)
    n_freq = HEAD_DIM // 4
    inv = ROPE_THETA ** (-np.arange(n_freq, dtype=np.float64) / n_freq)
    ang_r = row[:, None] * inv[None, :]
    ang_c = colp[:, None] * inv[None, :]
    cos = np.concatenate([np.cos(ang_r), np.cos(ang_r), np.cos(ang_c), np.cos(ang_c)], axis=1)
    sin = np.concatenate([-np.sin(ang_r), np.sin(ang_r), -np.sin(ang_c), np.sin(ang_c)], axis=1)
    return jnp.asarray(cos, F32), jnp.asarray(sin, F32)


PROJ_TM = 1024
MIXER_TM = 512


def kernel(x_prompt, x_sample, cache_k, cache_v, state_ssm_re, state_ssm_im, c, c_ctx, norm_mix_g, norm_ffn_g,
           w_mod, b_mod, w_in, ssm_lambda_re, ssm_lambda_im, ssm_log_dt, ssm_b_re, ssm_b_im, ssm_c_re, ssm_c_im,
           ssm_d, w_glu, attn_sink, w_ssm_o, w_attn_o, w_out, w_up, conv_w, conv_b, w_down, final_norm_g):
    batch, seq, d = x_prompt.shape
    dec_batch, dec_seq, _ = x_sample.shape
    depth = w_in.shape[0]
    assert depth == 1, "final norm is fused into the (single) layer's ffn kernel"
    d_ssm = w_glu.shape[1]
    d_attn = N_HEADS * HEAD_DIM
    d_kv = N_KV_HEADS * HEAD_DIM
    groups = d_ssm // SSM_CH
    assert w_in.shape[2] == d_ssm + d_attn + 2 * d_kv + 2 * d
    assert ssm_lambda_re.shape[2:] == (groups, SSM_STATE) and dec_batch <= 8 - 1
    l = 0

    col = {"gs": 0, "ga": d, "q": 2 * d, "u": 2 * d + d_attn, "k": 2 * d + d_attn + d_ssm,
           "d_ssm": d_ssm, "d_attn": d_attn, "d_kv": d_kv}
    xp = x_prompt.reshape(batch * seq, d)
    xs = x_sample.reshape(dec_batch * dec_seq, d)

    cond = jnp.concatenate([c_ctx[None], c, jnp.zeros((8 - 1 - dec_batch, d), F32)], axis=0)

    ssm_ops, mod = _ssm_operators_and_mod(
        ssm_lambda_re[l], ssm_lambda_im[l], ssm_log_dt[l], ssm_b_re[l], ssm_b_im[l], ssm_c_re[l], ssm_c_im[l],
        ssm_d[l], cond, w_mod[l], b_mod[l])
    mod = mod.reshape(8, 6, d)
    mod_p, mod_s = mod[0:1], mod[1:1 + dec_batch]
    proj_s, tail_s, _, _, w_in_b = _proj(xs, mod_s, norm_mix_g[l], w_in[l], tm=PROJ_TM,
                                         tiles_per_seq=max(dec_seq // PROJ_TM, 1), col=col)
    proj_p, tail_p, k_raw, v_raw = _proj(xp, mod_p, norm_mix_g[l], w_in_b, tm=PROJ_TM,
                                         tiles_per_seq=max(seq // PROJ_TM, 1), col=col)

    def lanes(s):
        return s.transpose(0, 2, 1, 3).reshape(dec_batch, groups, 2 * SSM_STATE)
    h0_p = jnp.zeros((batch, groups * 4 * SSM_STATE), F32)
    h0_s = _slab_lanes(jnp.concatenate([lanes(state_ssm_re[:, l]), lanes(state_ssm_im[:, l])], axis=-1))
    y_ssm_p, h_fin, (w_up_b,) = _ssm(tail_p, 0, ssm_ops, h0_p, n_seq=batch, seq_len=seq,
                                     seq_block=min(batch, SSM_SEQ_BLOCK), riders=(w_up[l],))
    y_ssm_s, _, (w_down_b,) = _ssm(tail_s, 0, ssm_ops, h0_s, n_seq=dec_batch, seq_len=dec_seq,
                                   seq_block=min(dec_batch, SSM_SEQ_BLOCK), riders=(w_down[l],))

    sink = attn_sink[l]
    o_p, mixer_w = _attn_ctx(proj_p, sink, n_seq=batch, seq_len=seq, col=col,
                             riders=(w_glu[l], w_ssm_o[l], w_attn_o[l], w_out[l]))
    o_s = _attn_lat(proj_s, cache_k[:, l].reshape(dec_batch, -1, d_kv), cache_v[:, l].reshape(dec_batch, -1, d_kv),
                    sink, n_seq=dec_batch, seq_len=dec_seq, col=col)

    x1_p, xm2_p = _mixer_out(xp, y_ssm_p, o_p, proj_p, mod_p, norm_ffn_g[l], *mixer_w, tm=MIXER_TM,
                             tiles_per_seq=max(seq // MIXER_TM, 1), col=col)
    x1_s, xm2_s = _mixer_out(xs, y_ssm_s, o_s, proj_s, mod_s, norm_ffn_g[l], *mixer_w, tm=MIXER_TM,
                             tiles_per_seq=max(dec_seq // MIXER_TM, 1), col=col)
    y_p = _ffn(x1_p, xm2_p, mod_p, w_up_b, conv_w[l], conv_b[l], w_down_b, final_norm_g, tm=FFN_TM, seq_len=seq)
    y_s = _ffn(x1_s, xm2_s, mod_s, w_up_b, conv_w[l], conv_b[l], w_down_b, final_norm_g, tm=FFN_TM, seq_len=dec_seq)

    new_k = k_raw.reshape(batch, 1, seq, N_KV_HEADS, HEAD_DIM)
    new_v = v_raw.reshape(batch, 1, seq, N_KV_HEADS, HEAD_DIM)

    def unlanes(hl):
        return hl.reshape(batch, groups, 2, SSM_STATE).transpose(0, 2, 1, 3)[:, None]
    h_fin = _unslab_lanes(h_fin, groups)
    new_re = unlanes(h_fin[:, :, :2 * SSM_STATE])
    new_im = unlanes(h_fin[:, :, 2 * SSM_STATE:])
    return (y_p.reshape(batch, seq, d), y_s.reshape(dec_batch, dec_seq, d), new_k, new_v, new_re, new_im)
```

```python
import functools
import math

import jax
import jax.numpy as jnp
import numpy as np
from jax import lax
from jax.experimental import pallas as pl
from jax.experimental.pallas import tpu as pltpu

F32 = jnp.float32
BF16 = jnp.bfloat16

GRID_W = 64
SSM_CH = 16
SSM_STATE = 64
N_HEADS = 8
N_KV_HEADS = 2
HEAD_DIM = 128
Q_PER_KV = N_HEADS // N_KV_HEADS
WINDOW = 128
BLOCK = 128
ROPE_THETA = 10000.0
EPS = 1e-6
NEG_INF = -1e30

SSM_CHUNK = 16
SSM_LANES = SSM_CHUNK * SSM_CH
SSM_SEQ_BLOCK = 16

V7X_VMEM_LIMIT_BYTES = 56 * 1024 * 1024
V7X_MXU_WIDTH = 256


def _cparams(semantics):
    return pltpu.CompilerParams(dimension_semantics=semantics, vmem_limit_bytes=V7X_VMEM_LIMIT_BYTES)


def _sigmoid(x):
    return 1.0 / (1.0 + jnp.exp(-x))


def _ride_casts(body, n_in, n_out, n_riders):
    def kernel(*refs):
        ins = refs[:n_in]
        rider_ins = refs[n_in:n_in + n_riders]
        outs = refs[n_in + n_riders:n_in + n_riders + n_out]
        rider_outs = refs[n_in + n_riders + n_out:n_in + 2 * n_riders + n_out]
        scratch = refs[n_in + 2 * n_riders + n_out:]
        for src, dst in zip(rider_ins, rider_outs):
            dst[...] = src[...].astype(BF16)
        body(*ins, *outs, *scratch)
    return kernel


def _rider_specs(weights, n_steps, step_of):
    specs, shapes = [], []
    for w in weights:
        r, c = w.shape
        assert r % (16 * n_steps) == 0
        specs.append(pl.BlockSpec((r // n_steps, c), lambda *idx: (step_of(*idx), 0)))
        shapes.append(jax.ShapeDtypeStruct((r, c), BF16))
    return specs, shapes


def _rms_modulate(x, g, scale, shift):
    ms = jnp.mean(x * x, axis=-1, keepdims=True)
    return (x * lax.rsqrt(ms + EPS) * g) * (1.0 + scale) + shift


MOD_K_CHUNK = 256


def _mod_kernel(c_ref, w_ref, b_ref, o_ref):
    c = c_ref[...]
    a = (c * _sigmoid(c)).astype(BF16)
    acc = b_ref[...]
    for k in range(0, w_ref.shape[0], MOD_K_CHUNK):
        acc = acc + jnp.dot(a[:, k:k + MOD_K_CHUNK], w_ref[k:k + MOD_K_CHUNK, :].astype(BF16),
                            preferred_element_type=F32)
    o_ref[...] = acc


PROJ_TN = 512
NORM_ROWS = 16
NORM_UNROLL = 16


def _swap32(x):
    n = x.shape[-1]
    lane = lax.broadcasted_iota(jnp.int32, x.shape, x.ndim - 1)
    return jnp.where((lane & 63) < 32, pltpu.roll(x, n - 32, x.ndim - 1), pltpu.roll(x, 32, x.ndim - 1))


def _proj_kernel(x_ref, mod_ref, g_ref, w_ref, o_ref, tail_ref, k_ref, v_ref, *rest, q_tiles):
    xm_scr = rest[-1]
    j = pl.program_id(1)

    @pl.when(j == 0)
    def _():
        gain = g_ref[...] * (1.0 + mod_ref[1:2, :])
        shift = mod_ref[0:1, :]

        def block(c, carry):
            rows = pl.ds(pl.multiple_of(c * NORM_ROWS, NORM_ROWS), NORM_ROWS)
            x = x_ref[rows, :]
            ms = jnp.mean(x * x, axis=-1, keepdims=True)
            xm_scr[rows, :] = (x * lax.rsqrt(ms + EPS) * gain + shift).astype(BF16)
            return carry

        lax.fori_loop(0, x_ref.shape[0] // NORM_ROWS, block, 0, unroll=NORM_UNROLL)

    w = w_ref[...]
    if len(rest) == 2:
        w = w.astype(BF16)
        rest[0][...] = w
    acc = jnp.dot(xm_scr[...], w, preferred_element_type=F32)
    scale = jnp.where((j >= q_tiles[0]) & (j < q_tiles[1]), HEAD_DIM ** -0.5, 1.0)
    o_ref[...] = (acc * scale).astype(BF16)
    tail_ref[...] = acc
    rows = acc.shape[0]
    for h in range(N_KV_HEADS):
        k_ref[pl.ds(h, rows, stride=N_KV_HEADS), :] = acc[:, h * HEAD_DIM:(h + 1) * HEAD_DIM]
        v_ref[pl.ds(h, rows, stride=N_KV_HEADS), :] = acc[:, (N_KV_HEADS + h) * HEAD_DIM:(N_KV_HEADS + h + 1) * HEAD_DIM]


def _proj(x, mod, norm_g, w_in, *, tm, tiles_per_seq, col):
    n, d = x.shape
    cols = w_in.shape[1]
    tn = PROJ_TN
    n_tiles = cols // tn
    d_kv = col["d_kv"]
    n_gate, n_q, n_u = col["q"] // tn, col["d_attn"] // tn, col["d_ssm"] // tn
    q_tiles = (n_gate, n_gate + n_q)
    tail0 = col["u"] // tn
    n_tail = n_tiles - tail0
    assert 2 * d_kv == tn and col["u"] == col["q"] + col["d_attn"] and col["k"] == cols - tn

    def src_tile(j):
        return jnp.where(j < n_gate, j + (n_u + n_q + 1),
                         jnp.where(j < n_gate + n_q, j - n_gate + n_u,
                                   jnp.where(j < n_gate + n_q + n_u, j - (n_gate + n_q), n_u + n_q)))
    seq_of = (lambda i: i // tiles_per_seq) if mod.shape[0] > 1 else (lambda i: 0)
    out_specs = [pl.BlockSpec((tm, tn), lambda i, j: (i, j)),
                 pl.BlockSpec((tm, tn), lambda i, j: (i, jnp.clip(j - tail0, 0, n_tail - 1))),
                 pl.BlockSpec((tm * N_KV_HEADS, HEAD_DIM), lambda i, j: (i, 0)),
                 pl.BlockSpec((tm * N_KV_HEADS, HEAD_DIM), lambda i, j: (i, 0))]
    out_shape = [jax.ShapeDtypeStruct((n, cols), BF16), jax.ShapeDtypeStruct((n, n_tail * tn), F32),
                 jax.ShapeDtypeStruct((n * N_KV_HEADS, HEAD_DIM), F32),
                 jax.ShapeDtypeStruct((n * N_KV_HEADS, HEAD_DIM), F32)]
    if w_in.dtype != BF16:
        out_specs.append(pl.BlockSpec((d, tn), lambda i, j: (0, src_tile(jnp.where(i == 0, j, n_tiles - 1)))))
        out_shape.append(jax.ShapeDtypeStruct((d, cols), BF16))
    return pl.pallas_call(
        functools.partial(_proj_kernel, q_tiles=q_tiles),
        grid=(n // tm, n_tiles),
        in_specs=[pl.BlockSpec((tm, d), lambda i, j: (i, 0)),
                  pl.BlockSpec((None, 6, d), lambda i, j: (seq_of(i), 0, 0)),
                  pl.BlockSpec((1, d), lambda i, j: (0, 0)),
                  pl.BlockSpec((d, tn), lambda i, j: (0, src_tile(j)))],
        out_specs=out_specs,
        out_shape=out_shape,
        scratch_shapes=[pltpu.VMEM((tm, d), BF16)],
        compiler_params=_cparams(("arbitrary", "arbitrary")),
        name="proj",
    )(x, mod, norm_g.reshape(1, d), w_in)


def _cmul(ar, ai, br, bi):
    return ar * br - ai * bi, ar * bi + ai * br


def _cpow(base_r, base_i, n, bits):
    res_r = jnp.ones_like(base_r)
    res_i = jnp.zeros_like(base_r)
    for bit in range(bits):
        nr, ni = _cmul(res_r, res_i, base_r, base_i)
        take = ((n >> bit) & 1) == 1
        res_r = jnp.where(take, nr, res_r)
        res_i = jnp.where(take, ni, res_i)
        if bit + 1 < bits:
            base_r, base_i = _cmul(base_r, base_i, base_r, base_i)
    return res_r, res_i


def _ssm_disc_kernel(lre_ref, lim_ref, ldt_ref, ere_ref, eim_ref, qre_ref, qim_ref):
    lre = lre_ref[...]
    lim = lim_ref[...]
    dt = jnp.exp(ldt_ref[...])
    mag = jnp.exp(lre * dt)
    e_re = mag * jnp.cos(lim * dt)
    e_im = mag * jnp.sin(lim * dt)
    n_re = e_re - 1.0
    den = lre * lre + lim * lim
    ere_ref[...] = e_re
    eim_ref[...] = e_im
    qre_ref[...] = (n_re * lre + e_im * lim) / den
    qim_ref[...] = (e_im * lre - n_re * lim) / den


def _ssm_disc(lam_re, lam_im, log_dt):
    two, g, p = lam_re.shape
    shp = jax.ShapeDtypeStruct((two * g, p), F32)
    outs = pl.pallas_call(_ssm_disc_kernel, out_shape=[shp] * 4, name="ssm_disc")(
        lam_re.reshape(two * g, p), lam_im.reshape(two * g, p), log_dt.reshape(two * g, 1))
    return [o.reshape(two, g, p) for o in outs]


SSM_OPS_GROUPS = 8


def _ssm_ops_mod_kernel(ecol_ref, rows_ref, b_ref, ct_ref, d_ref, c_ref, wm_ref, bm_ref,
                        m_ref, w_ref, v_ref, a_ref, mod_ref):
    _mod_kernel(c_ref, wm_ref, bm_ref, mod_ref)
    _ssm_ops_kernel(ecol_ref, rows_ref, b_ref, ct_ref, d_ref, m_ref, w_ref, v_ref, a_ref)


def _ssm_ops_kernel(ecol_ref, rows_ref, b_ref, ct_ref, d_ref, m_ref, w_ref, v_ref, a_ref):
    def one(gi, carry):
        _ssm_ops_group(ecol_ref.at[gi], rows_ref.at[gi], b_ref.at[gi], ct_ref.at[gi], d_ref.at[gi],
                       m_ref.at[gi], w_ref.at[gi], v_ref.at[gi], a_ref.at[gi])
        return carry

    lax.fori_loop(0, ecol_ref.shape[0], one, 0, unroll=True)


def _ssm_ops_group(ecol_ref, rows_ref, b_ref, ct_ref, d_ref, m_ref, w_ref, v_ref, a_ref):
    L, ch, p = SSM_CHUNK, SSM_CH, SSM_STATE
    lanes = SSM_LANES
    half = 2 * p
    lane_c = lax.broadcasted_iota(jnp.int32, (p, lanes), 1)
    lag = lane_c // ch

    def col(k):
        return jnp.broadcast_to(ecol_ref[:, k:k + 1], (p, lanes))

    f0 = _cpow(col(0), col(1), lag, 4)
    b0 = _cpow(col(2), col(3), (L - 1) - lag, 4)
    f1 = _cmul(f0[0], f0[1], col(0), col(1))
    b1 = _cmul(b0[0], b0[1], col(2), col(3))
    def ct(k):
        return jnp.tile(ct_ref[:, k * ch:(k + 1) * ch], (1, L))

    cf = (ct(0), ct(1))
    cb = (ct(2), ct(3))
    yf0 = _cmul(*f0, *cf)
    yb0 = _cmul(*b0, *cb)
    yf1 = _cmul(*f1, *cf)
    yb1 = _cmul(*b1, *cb)
    v_ref[...] = jnp.concatenate([yf1[0], yb1[0], -yf1[1], -yb1[1]], axis=0).astype(BF16)

    lane_r = lax.broadcasted_iota(jnp.int32, (ch, lanes), 1)
    sgn = jnp.where(lane_r < half, -1.0, 1.0)
    b1raw = b_ref[...]
    b2raw = pltpu.roll(b1raw, half, 1) * sgn
    bb1 = rows_ref[2:3, :] * b1raw + rows_ref[3:4, :] * b2raw
    bb2 = pltpu.roll(bb1, half, 1) * sgn

    zero = jnp.zeros((p, lanes), F32)
    hi = lax.Precision.HIGHEST
    r0f = jnp.dot(bb1, jnp.concatenate([yf0[0], zero, -yf0[1], zero], axis=0), precision=hi,
                  preferred_element_type=F32)
    r0b = jnp.dot(bb1, jnp.concatenate([zero, yb0[0], zero, -yb0[1]], axis=0), precision=hi,
                  preferred_element_type=F32)
    row_r = lax.broadcasted_iota(jnp.int32, (ch, lanes), 0)
    d_col = d_ref[...]
    for s in range(L):
        blk = jnp.where(lane_r == ch * s + row_r, d_col, 0.0)
        fwd = r0f if s == 0 else jnp.where(lane_r >= ch * s, pltpu.roll(r0f, ch * s, 1), 0.0)
        k = ch * (L - 1 - s)
        bwd = r0b if k == 0 else jnp.where(lane_r < lanes - k, pltpu.roll(r0b, lanes - k, 1), 0.0)
        m_ref[s * ch:(s + 1) * ch, :] = (blk + fwd + bwd).astype(BF16)

    is_f = (lane_r & (half - 1)) < p
    n_row = jnp.where(is_f, (L - 1) - row_r, row_r)
    er, ei = _cpow(jnp.broadcast_to(rows_ref[0:1, :], (L, lanes)), jnp.broadcast_to(rows_ref[1:2, :], (L, lanes)),
                   n_row, 4)
    for s in range(L):
        w_ref[s * ch:(s + 1) * ch, :] = (er[s:s + 1, :] * bb1 + ei[s:s + 1, :] * bb2).astype(BF16)

    ar, ai = rows_ref[0:1, :], rows_ref[1:2, :]
    for _ in range(4):
        ar, ai = _cmul(ar, ai, ar, ai)
    lane_1 = lax.broadcasted_iota(jnp.int32, (1, lanes), 1)
    a_ref[...] = jnp.where(lane_1 < half, ar, ai)


def _ssm_operators_and_mod(lam_re, lam_im, log_dt, b_re, b_im, c_re, c_im, ssm_d, cond, w_mod, b_mod):
    two, g, p = lam_re.shape
    ch = b_re.shape[-1]
    lanes = SSM_LANES
    assert (two, p, ch, SSM_CHUNK * ch) == (2, SSM_STATE, SSM_CH, lanes)
    e_re, e_im, q_re, q_im = _ssm_disc(lam_re, lam_im, log_dt)
    ecol = jnp.stack([e_re[0], e_im[0], e_re[1], e_im[1]], axis=-1)

    def fbfb(x):
        return jnp.concatenate([x[0], x[1], x[0], x[1]], axis=-1)
    rows = jnp.stack([fbfb(e_re), fbfb(e_im), fbfb(q_re), fbfb(q_im)], axis=1)

    def bt(x):
        return jnp.swapaxes(x, -1, -2)
    b_rows = jnp.concatenate([bt(b_re[0]), bt(b_re[1]), bt(b_im[0]), bt(b_im[1])], axis=-1)

    ct = jnp.concatenate([bt(c_re[0]), bt(c_im[0]), bt(c_re[1]), bt(c_im[1])], axis=-1)
    d_col = ssm_d.reshape(g, ch, 1)
    gb = SSM_OPS_GROUPS
    mat = pl.BlockSpec((gb, lanes, lanes), lambda i: (i, 0, 0))
    steps = g // gb
    c_rows, d_model = cond.shape
    n_mod = w_mod.shape[1]
    tn = n_mod // steps
    assert n_mod % steps == 0 and tn % 128 == 0
    outs = pl.pallas_call(
        _ssm_ops_mod_kernel,
        grid=(steps,),
        in_specs=[pl.BlockSpec((gb, p, 4), lambda i: (i, 0, 0)),
                  pl.BlockSpec((gb, 4, lanes), lambda i: (i, 0, 0)),
                  pl.BlockSpec((gb, ch, lanes), lambda i: (i, 0, 0)),
                  pl.BlockSpec((gb, p, 4 * ch), lambda i: (i, 0, 0)),
                  pl.BlockSpec((gb, ch, 1), lambda i: (i, 0, 0)),
                  pl.BlockSpec((c_rows, d_model), lambda i: (0, 0)),
                  pl.BlockSpec((d_model, tn), lambda i: (0, i)),
                  pl.BlockSpec((1, tn), lambda i: (0, i))],
        out_specs=[mat, mat, mat, pl.BlockSpec((gb, 1, lanes), lambda i: (i, 0, 0)),
                   pl.BlockSpec((c_rows, tn), lambda i: (0, i))],
        out_shape=[jax.ShapeDtypeStruct((g, lanes, lanes), BF16)] * 3 + [jax.ShapeDtypeStruct((g, 1, lanes), F32),
                                                                         jax.ShapeDtypeStruct((c_rows, n_mod), F32)],
        compiler_params=_cparams(("arbitrary",)),
        name="ssm_ops_mod",
    )(ecol, rows, b_rows, ct, d_col, cond, w_mod, b_mod.reshape(1, n_mod))
    return outs[:4], outs[4]


SSM_SLAB_GROUPS = 128 // SSM_CH


def _chunk_transpose(xs, chunk_id):
    for k in (4, 2, 1):
        keep = (chunk_id & k) == 0
        new = list(xs)
        for i in range(len(xs)):
            if i & k == 0:
                a, b = xs[i], xs[i + k]
                if 2 * SSM_CH * k == 128:
                    moved = pltpu.roll(jnp.where(keep, b, a), SSM_CH * k, 1)
                    new[i] = jnp.where(keep, a, moved)
                    new[i + k] = jnp.where(keep, moved, b)
                else:
                    new[i] = jnp.where(keep, a, pltpu.roll(b, SSM_CH * k, 1))
                    new[i + k] = jnp.where(keep, pltpu.roll(a, 128 - SSM_CH * k, 1), b)
        xs = new
    return xs


def _ssm_kernel(u_ref, m_ref, w_ref, v_ref, a_ref, h0_ref, y_ref, hfin_ref, ug_scr, s_scr, hf_scr, hb_scr, yg_scr,
                *, n_chunks, ns):
    gps = SSM_SLAB_GROUPS
    nc = ns * n_chunks
    half = 2 * SSM_STATE
    chunk_id = lax.broadcasted_iota(jnp.int32, (nc, 128), 1) // SSM_CH
    for hf in range(2):
        xs = [u_ref[pl.ds(hf * gps + i, nc, stride=SSM_CHUNK), :] for i in range(gps)]
        xs = _chunk_transpose(xs, chunk_id)
        for g in range(gps):
            ug_scr[g, :, hf * 128:(hf + 1) * 128] = xs[g].astype(BF16)
    pitch = n_chunks + 8
    for g in range(gps):
        s = jnp.dot(ug_scr[g], w_ref[g], preferred_element_type=F32)
        for q in range(ns):
            s_scr[g, q * pitch:q * pitch + n_chunks, :] = s[q * n_chunks:(q + 1) * n_chunks, :half]
            s_scr[gps + g, q * pitch:q * pitch + n_chunks, :] = s[q * n_chunks:(q + 1) * n_chunks, half:]
    is_fwd = lax.broadcasted_iota(jnp.int32, (1, half), 1) < SSM_STATE
    blocks = range(2 * gps)
    a = [a_ref[:, b * half:(b + 1) * half] for b in blocks]
    h = [h0_ref[:, b * half:(b + 1) * half] for b in blocks]

    def rows(c):
        return pl.ds(c, ns, stride=pitch)

    for k in range(n_chunks):
        kb = n_chunks - 1 - k
        new_h = list(h)
        for b in blocks:
            hf_scr[b, rows(k), :] = h[b]
            hb_scr[b, rows(kb), :] = h[b]
        for g in range(gps):
            s_re = jnp.where(is_fwd, s_scr[g, rows(k), :], s_scr[g, rows(kb), :])
            s_im = jnp.where(is_fwd, s_scr[gps + g, rows(k), :], s_scr[gps + g, rows(kb), :])
            new_h[g] = a[g] * h[g] - a[gps + g] * h[gps + g] + s_re
            new_h[gps + g] = a[g] * h[gps + g] + a[gps + g] * h[g] + s_im
        h = new_h
    for b in blocks:
        hfin_ref[:, b * half:(b + 1) * half] = h[b]

    def entering(b):
        return jnp.concatenate(
            [jnp.where(is_fwd, hf_scr[b, q * pitch:q * pitch + n_chunks, :], hb_scr[b, q * pitch:q * pitch + n_chunks, :])
             for q in range(ns)], axis=0)

    for g in range(gps):
        hin = jnp.concatenate([entering(g), entering(gps + g)], axis=1).astype(BF16)
        yg_scr[g] = (jnp.dot(ug_scr[g], m_ref[g], preferred_element_type=F32)
                     + jnp.dot(hin, v_ref[g], preferred_element_type=F32))
    for hf in range(2):
        xs = [yg_scr[g, :, hf * 128:(hf + 1) * 128] for g in range(gps)]
        xs = _chunk_transpose(xs, chunk_id)
        for i in range(gps):
            y_ref[pl.ds(hf * gps + i, nc, stride=SSM_CHUNK), :] = xs[i]


def _slab_lanes(x):
    lead = x.shape[:-2]
    g = x.shape[-2]
    gps = SSM_SLAB_GROUPS
    x = x.reshape(lead + (g // gps, gps, 2, 2 * SSM_STATE))
    x = jnp.swapaxes(x, -3, -2)
    return x.reshape(lead + (g * 4 * SSM_STATE,))


def _unslab_lanes(x, g):
    lead = x.shape[:-1]
    gps = SSM_SLAB_GROUPS
    x = x.reshape(lead + (g // gps, 2, gps, 2 * SSM_STATE))
    x = jnp.swapaxes(x, -3, -2)
    return x.reshape(lead + (g, 4 * SSM_STATE))


def _ssm(u, u_col, ops, h0, *, n_seq, seq_len, seq_block, riders=()):
    m_op, w_op, v_op, a_op = ops
    n = u.shape[0]
    g = m_op.shape[0]
    d_ssm = g * SSM_CH
    slab0 = u_col // 128
    gps = SSM_SLAB_GROUPS
    lanes = SSM_LANES
    n_chunks = seq_len // SSM_CHUNK
    nc = seq_block * n_chunks
    padded = seq_block * (n_chunks + 8)
    rows = seq_block * seq_len
    wide2 = gps * 4 * SSM_STATE
    mat = pl.BlockSpec((gps, lanes, lanes), lambda j, b: (j, 0, 0))
    n_blocks = n_seq // seq_block
    rider_specs, rider_shapes = _rider_specs(riders, (g // gps) * n_blocks, lambda j, b: j * n_blocks + b)
    outs = pl.pallas_call(
        _ride_casts(functools.partial(_ssm_kernel, n_chunks=n_chunks, ns=seq_block), 6, 2, len(riders)),
        grid=(g // gps, n_blocks),
        in_specs=[pl.BlockSpec((rows, 128), lambda j, b: (b, slab0 + j)), mat, mat, mat,
                  pl.BlockSpec((1, wide2), lambda j, b: (0, j)),
                  pl.BlockSpec((seq_block, wide2), lambda j, b: (b, j))] + rider_specs,
        out_specs=[pl.BlockSpec((rows, 128), lambda j, b: (b, j)),
                   pl.BlockSpec((seq_block, wide2), lambda j, b: (b, j))] + rider_specs,
        out_shape=[jax.ShapeDtypeStruct((n, d_ssm), F32),
                   jax.ShapeDtypeStruct((n_seq, g * 4 * SSM_STATE), F32)] + rider_shapes,
        scratch_shapes=[pltpu.VMEM((gps, nc, lanes), BF16), pltpu.VMEM((2 * gps, padded, 128), F32),
                        pltpu.VMEM((2 * gps, padded, 128), F32), pltpu.VMEM((2 * gps, padded, 128), F32),
                        pltpu.VMEM((gps, nc, lanes), F32)],
        compiler_params=_cparams(("arbitrary", "arbitrary")),
        name="ssm",
    )(u, m_op, w_op, v_op, _slab_lanes(a_op[:, 0])[None], h0, *riders)
    return outs[0], outs[1], outs[2:]


def _softmax_pv(pieces, sink):
    m = sink
    for s, _ in pieces:
        m = jnp.maximum(m, jnp.max(s, axis=-1, keepdims=True))
    out = None
    for s, v in pieces:
        pv = jnp.dot(jnp.exp(s - m).astype(BF16), v, preferred_element_type=F32)
        out = pv if out is None else out + pv
    return out[:, :HEAD_DIM] / (out[:, HEAD_DIM:] + jnp.exp(sink - m))


def _qk(q, k):
    return lax.dot_general(q, k, (((1,), (1,)), ((), ())), preferred_element_type=F32)


def _with_ones(v):
    return jnp.concatenate([v, jnp.ones_like(v)], axis=1)


ATTN_CTX_SEQS = 8


def _attn_ctx_kernel(sink_ref, q_ref, kv_ref, o_ref, *, seq_len):
    d_kv = N_KV_HEADS * HEAD_DIM
    for r0 in range(0, q_ref.shape[0], seq_len):
        rows = slice(r0, r0 + seq_len)
        for kh in range(N_KV_HEADS):
            k = kv_ref[rows, kh * HEAD_DIM:(kh + 1) * HEAD_DIM]
            v = _with_ones(kv_ref[rows, d_kv + kh * HEAD_DIM:d_kv + (kh + 1) * HEAD_DIM])
            for h in range(kh * Q_PER_KV, (kh + 1) * Q_PER_KV):
                q = q_ref[rows, h * HEAD_DIM:(h + 1) * HEAD_DIM]
                o = _softmax_pv([(_qk(q, k), v)], sink_ref[h])
                o_ref[rows, h * HEAD_DIM:(h + 1) * HEAD_DIM] = o.astype(BF16)


def _attn_ctx(proj, sink, *, n_seq, seq_len, col, riders=()):
    d_attn = col["d_attn"]
    kv_w = 2 * col["d_kv"]
    spb = ATTN_CTX_SEQS if n_seq % ATTN_CTX_SEQS == 0 else 1
    rows = spb * seq_len
    rider_specs, rider_shapes = _rider_specs(riders, n_seq // spb, lambda b: b)
    outs = pl.pallas_call(
        _ride_casts(functools.partial(_attn_ctx_kernel, seq_len=seq_len), 3, 1, len(riders)),
        grid=(n_seq // spb,),
        in_specs=[pl.BlockSpec(memory_space=pltpu.SMEM),
                  pl.BlockSpec((rows, d_attn), lambda b: (b, col["q"] // d_attn)),
                  pl.BlockSpec((rows, kv_w), lambda b: (b, col["k"] // kv_w))] + rider_specs,
        out_specs=[pl.BlockSpec((rows, d_attn), lambda b: (b, 0))] + rider_specs,
        out_shape=[jax.ShapeDtypeStruct((n_seq * seq_len, d_attn), BF16)] + rider_shapes,
        compiler_params=_cparams(("arbitrary",)),
        name="attn_ctx",
    )(sink, proj, proj, *riders)
    return outs[0], outs[1:]


def _rope(x, cos, sin):
    reps = x.shape[1] // HEAD_DIM
    xf = x.astype(F32)
    return (xf * jnp.concatenate([cos] * reps, axis=1) + _swap32(xf) * jnp.concatenate([sin] * reps, axis=1)).astype(BF16)


ATTN_LAT_QBLOCKS = 4


def _attn_lat_kernel(*refs, n_blocks, qb):
    sink_ref, q_ref = refs[:2]
    kv_refs = refs[2:4 + qb]
    ck_ref, cv_ref, cos_ref, sin_ref, o_ref = refs[4 + qb:]
    first = pl.program_id(1) * qb
    d_kv = N_KV_HEADS * HEAD_DIM

    def tables(blk):
        rows = pl.ds(pl.multiple_of(blk * BLOCK, BLOCK), BLOCK)
        return cos_ref[rows, :], sin_ref[rows, :]

    k_rot = [_rope(kv_refs[n][:, :d_kv], *tables(jnp.clip(first + n - 1, 0, n_blocks - 1))) for n in range(qb + 2)]
    r = lax.broadcasted_iota(jnp.int32, (BLOCK, 3 * BLOCK), 0)
    c = lax.broadcasted_iota(jnp.int32, (BLOCK, 3 * BLOCK), 1)
    cc = c & (BLOCK - 1)
    for sub in range(qb):
        blk = first + sub
        q_rows = slice(sub * BLOCK, (sub + 1) * BLOCK)
        q_all = _rope(q_ref[q_rows, :], *tables(blk))
        k_all = jnp.concatenate(k_rot[sub:sub + 3], axis=0)
        valid = (((c < BLOCK) & (cc >= r) & (blk > 0)) | ((c >= BLOCK) & (c < 2 * BLOCK))
                 | ((c >= 2 * BLOCK) & (cc <= r) & (blk < n_blocks - 1)))
        for kh in range(N_KV_HEADS):
            ks = slice(kh * HEAD_DIM, (kh + 1) * HEAD_DIM)
            vs = slice(d_kv + kh * HEAD_DIM, d_kv + (kh + 1) * HEAD_DIM)
            k_loc = k_all[:, ks]
            k_ctx = ck_ref[:, ks].astype(BF16)
            v_loc = _with_ones(jnp.concatenate([kv_refs[sub + n][:, vs] for n in range(3)], axis=0))
            v_ctx = _with_ones(cv_ref[:, ks].astype(BF16))
            for h in range(kh * Q_PER_KV, (kh + 1) * Q_PER_KV):
                q = q_all[:, h * HEAD_DIM:(h + 1) * HEAD_DIM]
                s_loc = jnp.where(valid, _qk(q, k_loc), NEG_INF)
                o = _softmax_pv([(s_loc, v_loc), (_qk(q, k_ctx), v_ctx)], sink_ref[h])
                o_ref[q_rows, h * HEAD_DIM:(h + 1) * HEAD_DIM] = o.astype(BF16)


def _attn_lat(proj, cache_k, cache_v, sink, *, n_seq, seq_len, col):
    d_attn = col["d_attn"]
    kv_w = 2 * col["d_kv"]
    assert WINDOW == BLOCK and seq_len % BLOCK == 0
    nb = seq_len // BLOCK
    qb = ATTN_LAT_QBLOCKS if nb % ATTN_LAT_QBLOCKS == 0 else 1
    steps = nb // qb
    kv_col = col["k"] // kv_w
    past, d_kv = cache_k.shape[1], cache_k.shape[2]

    def kv_spec(off):
        return pl.BlockSpec((BLOCK, kv_w), lambda b, i: (b * nb + jnp.clip(i * qb + off, 0, nb - 1), kv_col))
    cache_spec = pl.BlockSpec((None, past, d_kv), lambda b, i: (b, 0, 0))
    table_spec = pl.BlockSpec((seq_len, HEAD_DIM), lambda b, i: (0, 0))
    cos, sin = _rope_tables(seq_len)
    return pl.pallas_call(
        functools.partial(_attn_lat_kernel, n_blocks=nb, qb=qb),
        grid=(n_seq, steps),
        in_specs=[pl.BlockSpec(memory_space=pltpu.SMEM),
                  pl.BlockSpec((qb * BLOCK, d_attn), lambda b, i: (b * steps + i, col["q"] // d_attn))]
                 + [kv_spec(off) for off in range(-1, qb + 1)]
                 + [cache_spec, cache_spec, table_spec, table_spec],
        out_specs=pl.BlockSpec((qb * BLOCK, d_attn), lambda b, i: (b * steps + i, 0)),
        out_shape=jax.ShapeDtypeStruct((n_seq * seq_len, d_attn), BF16),
        compiler_params=_cparams(("arbitrary", "arbitrary")),
        name="attn_lat",
    )(sink, proj, *([proj] * (qb + 2)), cache_k, cache_v, cos, sin)


def _gelu_tanh(x):
    return 0.5 * x * (1.0 + jnp.tanh(math.sqrt(2.0 / math.pi) * (x + 0.044715 * (x * x * x))))


def _mixer_out_kernel(x_ref, y_ref, o_ref, gs_ref, ga_ref, mod_ref, g_ref, wglu_ref, wso_ref, wao_ref, wout_ref,
                      x1_ref, xm2_ref):
    z = _gelu_tanh(y_ref[...].astype(F32))
    z = z * _sigmoid(jnp.dot(z.astype(BF16), wglu_ref[...], preferred_element_type=F32))
    zb = z.astype(BF16)
    half = wout_ref.shape[0] // 2
    out = None
    for h in range(2):
        cols = slice(h * half, (h + 1) * half)
        s_br = jnp.dot(zb, wso_ref[:, cols], preferred_element_type=F32)
        a_br = jnp.dot(o_ref[...], wao_ref[:, cols], preferred_element_type=F32)
        merged = _sigmoid(gs_ref[:, cols].astype(F32)) * s_br + _sigmoid(ga_ref[:, cols].astype(F32)) * a_br
        part = jnp.dot(merged.astype(BF16), wout_ref[cols, :], preferred_element_type=F32)
        out = part if out is None else out + part
    x1 = x_ref[...] + mod_ref[2:3, :] * out
    x1_ref[...] = x1
    xm2_ref[...] = _rms_modulate(x1, g_ref[...], mod_ref[4:5, :], mod_ref[3:4, :]).astype(BF16)


def _mixer_out(x, y, o, proj, mod, norm_g, w_glu, w_ssm_o, w_attn_o, w_out, *, tm, tiles_per_seq, col):
    n, d = x.shape
    d_ssm = y.shape[1]
    d_attn = o.shape[1]
    seq_of = (lambda i: i // tiles_per_seq) if mod.shape[0] > 1 else (lambda i: 0)

    def resident(shape):
        return pl.BlockSpec(shape, lambda i: (0, 0), pipeline_mode=pl.Buffered(1))
    return pl.pallas_call(
        _mixer_out_kernel,
        grid=(n // tm,),
        in_specs=[pl.BlockSpec((tm, d), lambda i: (i, 0)),
                  pl.BlockSpec((tm, d_ssm), lambda i: (i, 0)),
                  pl.BlockSpec((tm, d_attn), lambda i: (i, 0)),
                  pl.BlockSpec((tm, d), lambda i: (i, col["gs"] // d)),
                  pl.BlockSpec((tm, d), lambda i: (i, col["ga"] // d)),
                  pl.BlockSpec((None, 6, d), lambda i: (seq_of(i), 0, 0)),
                  pl.BlockSpec((1, d), lambda i: (0, 0)),
                  resident(w_glu.shape), resident(w_ssm_o.shape), resident(w_attn_o.shape), resident(w_out.shape)],
        out_specs=[pl.BlockSpec((tm, d), lambda i: (i, 0)), pl.BlockSpec((tm, d), lambda i: (i, 0))],
        out_shape=[jax.ShapeDtypeStruct((n, d), F32), jax.ShapeDtypeStruct((n, d), BF16)],
        compiler_params=_cparams(("arbitrary",)),
        name="mixer_out",
    )(x, y, o, proj, proj, mod, norm_g.reshape(1, d), w_glu, w_ssm_o, w_attn_o, w_out)


FFN_TM = 1024
FFN_TF = 512
FFN_TN = 256


def _ffn_kernel(xm_ref, wa_ref, wb_ref, cw_ref, cb_ref, wd_ref, x1_ref, mod_ref, g_ref, o_ref, act_scr, raw_scr,
                *, seq_len, nf, nn, d_ff):
    j = pl.program_id(1)
    tm = xm_ref.shape[0]
    tf = wa_ref.shape[1]
    tn = wd_ref.shape[1]
    ch = tf // 2
    pos = lax.broadcasted_iota(jnp.int32, (tm, 1), 0) & (seq_len - 1)
    has_prev = pos != 0
    has_next = pos != seq_len - 1

    def lanes(off):
        return slice(off, off + ch) if isinstance(off, int) else pl.ds(pl.multiple_of(off, ch), ch)

    def gate(ha, hb, col0):
        def conv(h, off):
            cols = lanes(off)
            cw = cw_ref[:, cols]
            h_prev = jnp.where(has_prev, pltpu.roll(h, 1, 0), 0.0)
            h_next = jnp.where(has_next, pltpu.roll(h, tm - 1, 0), 0.0)
            return cw[0:1] * h_prev + cw[1:2] * h + cw[2:3] * h_next + cb_ref[:, cols]

        a = conv(ha, col0)
        b = conv(hb, d_ff + col0)
        return ((a * _sigmoid(a)) * b).astype(BF16)

    def act_cols(col0):
        return lanes(tf + col0)

    @pl.when(j == 0)
    def _():
        raw_scr[...] = jnp.zeros_like(raw_scr)

    @pl.when(j < nf)
    def _():
        xm = xm_ref[...]
        ha0 = jnp.dot(xm, wa_ref[:, :ch], preferred_element_type=F32)
        hb0 = jnp.dot(xm, wb_ref[:, :ch], preferred_element_type=F32)
        carried = (j - 1) * tf + ch
        act_scr[:, act_cols(carried)] = gate(raw_scr[0], raw_scr[1], jnp.maximum(carried, 0))
        ha1 = jnp.dot(xm, wa_ref[:, ch:], preferred_element_type=F32)
        hb1 = jnp.dot(xm, wb_ref[:, ch:], preferred_element_type=F32)
        act_scr[:, act_cols(j * tf)] = gate(ha0, hb0, j * tf)
        raw_scr[0] = ha1
        raw_scr[1] = hb1

    @pl.when(j == nf)
    def _():
        last = (nf - 1) * tf + ch
        act_scr[:, act_cols(last)] = gate(raw_scr[0], raw_scr[1], last)

    @pl.when(j >= nf)
    def _():
        cols = pl.ds(pl.multiple_of((j - nf) * tn, tn), tn)
        ffn = jnp.dot(act_scr[:, tf:], wd_ref[...], preferred_element_type=F32)
        o_ref[:, cols] = x1_ref[...] + mod_ref[5:6, cols] * ffn

    @pl.when(j == nf + nn - 1)
    def _():
        x2 = o_ref[...]
        ms = jnp.mean(x2 * x2, axis=-1, keepdims=True)
        o_ref[...] = x2 * lax.rsqrt(ms + EPS) * g_ref[...]


def _ffn(x1, xm2, mod, w_up, conv_w, conv_b, w_down, final_g, *, tm, seq_len):
    n, d = x1.shape
    d_ff = w_down.shape[0]
    tf, tn = FFN_TF, FFN_TN
    nf, nn = d_ff // tf, d // tn
    assert tm % seq_len == 0 and seq_len & (seq_len - 1) == 0 and d_ff % tf == 0 and (tf // 2) % V7X_MXU_WIDTH == 0
    seqs_per_tile = tm // seq_len
    seq_of = (lambda i: i * seqs_per_tile) if mod.shape[0] > 1 else (lambda i: 0)
    assert mod.shape[0] == 1 or seqs_per_tile == 1
    cb = conv_b.reshape(1, 2 * d_ff)

    def up(j):
        return jnp.minimum(j, nf - 1)

    def down(j):
        return jnp.maximum(j - nf, 0)
    return pl.pallas_call(
        functools.partial(_ffn_kernel, seq_len=seq_len, nf=nf, nn=nn, d_ff=d_ff),
        grid=(n // tm, nf + nn),
        in_specs=[pl.BlockSpec((tm, d), lambda i, j: (i, 0), pipeline_mode=pl.Buffered(1)),
                  pl.BlockSpec((d, tf), lambda i, j: (0, up(j))),
                  pl.BlockSpec((d, tf), lambda i, j: (0, nf + up(j))),
                  pl.BlockSpec((3, 2 * d_ff), lambda i, j: (0, 0)),
                  pl.BlockSpec((1, 2 * d_ff), lambda i, j: (0, 0)),
                  pl.BlockSpec((d_ff, tn), lambda i, j: (0, down(j))),
                  pl.BlockSpec((tm, tn), lambda i, j: (i, down(j))),
                  pl.BlockSpec((None, 6, d), lambda i, j: (seq_of(i), 0, 0)),
                  pl.BlockSpec((1, d), lambda i, j: (0, 0))],
        out_specs=pl.BlockSpec((tm, d), lambda i, j: (i, 0)),
        out_shape=jax.ShapeDtypeStruct((n, d), F32),
        scratch_shapes=[pltpu.VMEM((tm, tf + d_ff), BF16), pltpu.VMEM((2, tm, tf // 2), F32)],
        compiler_params=_cparams(("arbitrary", "arbitrary")),
        name="ffn",
    )(xm2, w_up, w_up, conv_w, cb, w_down, x1, mod, final_g.reshape(1, d))


def _rope_tables(seq_len):
    rows = seq_len // GRID_W
    row = np.repeat(np.arange(rows, dtype=np.float64), GRID_W)
    colp = np.tile(np.arange(GRID_W, dtype=np.float64), rows)
    n_freq = HEAD_DIM // 4
    inv = ROPE_THETA ** (-np.arange(n_freq, dtype=np.float64) / n_freq)
    ang_r = row[:, None] * inv[None, :]
    ang_c = colp[:, None] * inv[None, :]
    cos = np.concatenate([np.cos(ang_r), np.cos(ang_r), np.cos(ang_c), np.cos(ang_c)], axis=1)
    sin = np.concatenate([-np.sin(ang_r), np.sin(ang_r), -np.sin(ang_c), np.sin(ang_c)], axis=1)
    return jnp.asarray(cos, F32), jnp.asarray(sin, F32)


PROJ_TM = 1024
MIXER_TM = 256


def kernel(x_prompt, x_sample, cache_k, cache_v, state_ssm_re, state_ssm_im, c, c_ctx, norm_mix_g, norm_ffn_g,
           w_mod, b_mod, w_in, ssm_lambda_re, ssm_lambda_im, ssm_log_dt, ssm_b_re, ssm_b_im, ssm_c_re, ssm_c_im,
           ssm_d, w_glu, attn_sink, w_ssm_o, w_attn_o, w_out, w_up, conv_w, conv_b, w_down, final_norm_g):
    batch, seq, d = x_prompt.shape
    dec_batch, dec_seq, _ = x_sample.shape
    depth = w_in.shape[0]
    assert depth == 1, "final norm is fused into the (single) layer's ffn kernel"
    d_ssm = w_glu.shape[1]
    d_attn = N_HEADS * HEAD_DIM
    d_kv = N_KV_HEADS * HEAD_DIM
    groups = d_ssm // SSM_CH
    assert w_in.shape[2] == d_ssm + d_attn + 2 * d_kv + 2 * d
    assert ssm_lambda_re.shape[2:] == (groups, SSM_STATE) and dec_batch <= 8 - 1
    l = 0

    col = {"gs": 0, "ga": d, "q": 2 * d, "u": 2 * d + d_attn, "k": 2 * d + d_attn + d_ssm,
           "d_ssm": d_ssm, "d_attn": d_attn, "d_kv": d_kv}
    xp = x_prompt.reshape(batch * seq, d)
    xs = x_sample.reshape(dec_batch * dec_seq, d)

    cond = jnp.concatenate([c_ctx[None], c, jnp.zeros((8 - 1 - dec_batch, d), F32)], axis=0)

    ssm_ops, mod = _ssm_operators_and_mod(
        ssm_lambda_re[l], ssm_lambda_im[l], ssm_log_dt[l], ssm_b_re[l], ssm_b_im[l], ssm_c_re[l], ssm_c_im[l],
        ssm_d[l], cond, w_mod[l], b_mod[l])
    mod = mod.reshape(8, 6, d)
    mod_p, mod_s = mod[0:1], mod[1:1 + dec_batch]
    proj_s, tail_s, _, _, w_in_b = _proj(xs, mod_s, norm_mix_g[l], w_in[l], tm=PROJ_TM,
                                         tiles_per_seq=max(dec_seq // PROJ_TM, 1), col=col)
    proj_p, tail_p, k_raw, v_raw = _proj(xp, mod_p, norm_mix_g[l], w_in_b, tm=PROJ_TM,
                                         tiles_per_seq=max(seq // PROJ_TM, 1), col=col)

    def lanes(s):
        return s.transpose(0, 2, 1, 3).reshape(dec_batch, groups, 2 * SSM_STATE)
    h0_p = jnp.zeros((batch, groups * 4 * SSM_STATE), F32)
    h0_s = _slab_lanes(jnp.concatenate([lanes(state_ssm_re[:, l]), lanes(state_ssm_im[:, l])], axis=-1))
    y_ssm_p, h_fin, (w_up_b,) = _ssm(tail_p, 0, ssm_ops, h0_p, n_seq=batch, seq_len=seq,
                                     seq_block=min(batch, SSM_SEQ_BLOCK), riders=(w_up[l],))
    y_ssm_s, _, (w_down_b,) = _ssm(tail_s, 0, ssm_ops, h0_s, n_seq=dec_batch, seq_len=dec_seq,
                                   seq_block=min(dec_batch, SSM_SEQ_BLOCK), riders=(w_down[l],))

    sink = attn_sink[l]
    o_p, mixer_w = _attn_ctx(proj_p, sink, n_seq=batch, seq_len=seq, col=col,
                             riders=(w_glu[l], w_ssm_o[l], w_attn_o[l], w_out[l]))
    o_s = _attn_lat(proj_s, cache_k[:, l].reshape(dec_batch, -1, d_kv), cache_v[:, l].reshape(dec_batch, -1, d_kv),
                    sink, n_seq=dec_batch, seq_len=dec_seq, col=col)

    x1_p, xm2_p = _mixer_out(xp, y_ssm_p, o_p, proj_p, mod_p, norm_ffn_g[l], *mixer_w, tm=MIXER_TM,
                             tiles_per_seq=max(seq // MIXER_TM, 1), col=col)
    x1_s, xm2_s = _mixer_out(xs, y_ssm_s, o_s, proj_s, mod_s, norm_ffn_g[l], *mixer_w, tm=MIXER_TM,
                             tiles_per_seq=max(dec_seq // MIXER_TM, 1), col=col)
    y_p = _ffn(x1_p, xm2_p, mod_p, w_up_b, conv_w[l], conv_b[l], w_down_b, final_norm_g, tm=FFN_TM, seq_len=seq)
    y_s = _ffn(x1_s, xm2_s, mod_s, w_up_b, conv_w[l], conv_b[l], w_down_b, final_norm_g, tm=FFN_TM, seq_len=dec_seq)

    new_k = k_raw.reshape(batch, 1, seq, N_KV_HEADS, HEAD_DIM)
    new_v = v_raw.reshape(batch, 1, seq, N_KV_HEADS, HEAD_DIM)

    def unlanes(hl):
        return hl.reshape(batch, groups, 2, SSM_STATE).transpose(0, 2, 1, 3)[:, None]
    h_fin = _unslab_lanes(h_fin, groups)
    new_re = unlanes(h_fin[:, :, :2 * SSM_STATE])
    new_im = unlanes(h_fin[:, :, 2 * SSM_STATE:])
    return (y_p.reshape(batch, seq, d), y_s.reshape(dec_batch, dec_seq, d), new_k, new_v, new_re, new_im)
```
